```python
import math
import jax
import jax.numpy as jnp
from jax import lax
import numpy as np

D_MODEL = 1024
BATCH = 2
SEQ = 16384
DEPTH = 2

GRID_W = 64
CTX_LEN = 256

HG_HEADS = 6
HG_KDIM = 64
HG_VDIM = 64
RET_HEADS = 6
RET_KDIM = 64
RET_VDIM = 64
DA_HEADS = 4
DA_HEAD_DIM = 32
DA_VDIM = 2 * DA_HEAD_DIM
MIX_WIDTH = HG_HEADS * HG_VDIM + RET_HEADS * RET_VDIM + DA_HEADS * DA_VDIM
IN_SPLITS = (HG_HEADS * HG_KDIM, HG_HEADS * HG_KDIM, HG_HEADS * HG_KDIM, HG_HEADS * HG_VDIM, HG_HEADS * HG_VDIM, RET_HEADS * RET_KDIM, RET_HEADS * RET_KDIM, RET_HEADS * RET_VDIM, RET_HEADS * RET_VDIM, 2 * DA_HEADS * DA_HEAD_DIM, 2 * DA_HEADS * DA_HEAD_DIM, DA_HEADS * DA_VDIM)
IN_COLS = sum(IN_SPLITS)

CHUNK = 64
ATTN_BLOCK = 128
ROPE_BASE = 10000.0
N_EXPERTS = 32
TOP_K = 4
D_FF = D_MODEL
EXPERT_BLOCK = 128
SWIGLU_ALPHA = 1.702
SWIGLU_LIMIT = 7.0
NORM_EPS = 1e-6

kernel_name = 'hybrid_hgrn2_retention_diffattn_moe_dit'

F32 = jnp.float32


def rms_norm(x, g):
    xf = x.astype(F32)
    y = xf * lax.rsqrt(jnp.mean(xf * xf, axis=-1, keepdims=True) + NORM_EPS)
    return (y * g.astype(F32)).astype(x.dtype)


def split_heads(a, head_dim):
    return a.reshape(a.shape[:-1] + (a.shape[-1] // head_dim, head_dim))


def split_cols(p):
    bounds = []
    acc = 0
    for w in IN_SPLITS[:-1]:
        acc += w
        bounds.append(acc)
    return jnp.split(p, bounds, axis=-1)


def rope_tables(pos, dim):
    inv = 1.0 / (ROPE_BASE ** (jnp.arange(0, dim, 2, dtype=F32) / dim))
    ang = pos.astype(F32)[:, None] * inv[None, :]
    return jnp.cos(ang), jnp.sin(ang)


def apply_rope(x, cos, sin):
    half = x.shape[-1] // 2
    xf = x.astype(F32)
    x1, x2 = xf[..., :half], xf[..., half:]
    c = cos[None, :, None, :]
    s = sin[None, :, None, :]
    return jnp.concatenate([x1 * c - x2 * s, x2 * c + x1 * s], axis=-1).astype(x.dtype)


def apply_axial_rope(x, rope_row, rope_col):
    half = x.shape[-1] // 2
    return jnp.concatenate([apply_rope(x[..., :half], *rope_row), apply_rope(x[..., half:], *rope_col)], axis=-1)


def chunk_scan(q, k, v, logf, s0):
    out_dtype = v.dtype
    B, T, H, K = q.shape
    V = v.shape[-1]
    n = T // CHUNK

    def chunks(a):
        return a.astype(F32).reshape(B, n, CHUNK, H, a.shape[-1]).transpose(1, 0, 3, 2, 4)

    lower = jnp.tril(jnp.ones((CHUNK, CHUNK), dtype=bool))[:, :, None]

    def step(state, blk):
        qi, ki, vi, gi = blk
        b = jnp.cumsum(gi, axis=2)
        rel = jnp.where(lower, b[:, :, :, None, :] - b[:, :, None, :, :], -jnp.inf)
        scores = jnp.sum(qi[:, :, :, None, :] * ki[:, :, None, :, :] * jnp.exp(rel), axis=-1)
        o = jnp.einsum('bhts,bhsv->bhtv', scores, vi) + jnp.einsum('bhtk,bhkv->bhtv', qi * jnp.exp(b), state)
        b_end = b[:, :, -1:, :]
        state = jnp.exp(b_end[:, :, 0, :])[..., None] * state + jnp.einsum('bhsk,bhsv->bhkv', ki * jnp.exp(b_end - b), vi)
        return state, o

    state, o = lax.scan(step, s0.astype(F32), (chunks(q), chunks(k), chunks(v), chunks(logf)))
    o = o.transpose(1, 0, 3, 2, 4).reshape(B, T, H, V)
    return o.astype(out_dtype), state


def final_state(k, v, logf):
    cum = jnp.cumsum(logf.astype(F32), axis=1)
    w = jnp.exp(cum[:, -1:] - cum)
    return jnp.einsum('bthk,bthv->bhkv', k.astype(F32) * w, v.astype(F32))


def prefixed_scan(q, k, v, logf, qc, kc, vc, logfc, need_ctx, reverse):
    if reverse:
        q, k, v, logf, qc, kc, vc, logfc = [jnp.flip(a, axis=1) for a in (q, k, v, logf, qc, kc, vc, logfc)]
    if need_ctx:
        s_zero = jnp.zeros((qc.shape[0], qc.shape[2], qc.shape[3], vc.shape[3]), F32)
        oc, s_ctx = chunk_scan(qc, kc, vc, logfc, s_zero)
    else:
        oc, s_ctx = None, final_state(kc, vc, logfc)
    o, _ = chunk_scan(q, k, v, logf, s_ctx)
    if reverse:
        o = jnp.flip(o, axis=1)
        if oc is not None:
            oc = jnp.flip(oc, axis=1)
    return o, oc


def gated_head_norm(o, w, g):
    return rms_norm(o, w).reshape(g.shape) * jax.nn.silu(g)


def diff_attention(q, k, v, lam):
    B, T, H2, d = q.shape
    H = H2 // 2
    nb = T // ATTN_BLOCK
    qb = q.reshape(B, nb, ATTN_BLOCK, H2, d).transpose(1, 0, 2, 3, 4)
    scale = d ** -0.5

    def block(qi):
        s = jnp.einsum('bqhd,bkhd->bhqk', qi, k).astype(F32) * scale
        p = jax.nn.softmax(s, axis=-1).reshape(B, H, 2, ATTN_BLOCK, -1)
        a = p[:, :, 0] - lam * p[:, :, 1]
        return jnp.einsum('bhqk,bkhv->bqhv', a.astype(v.dtype), v)

    o = lax.map(block, qb)
    return o.transpose(1, 0, 2, 3, 4).reshape(B, T, H, v.shape[-1])


def token_mixer(h, hc, w_in, w_out, lower_bound, hg_norm_w, ret_dec, ret_norm_w, da_lam, da_subln_w, lam_init, rope_row, rope_col, rope_seq, need_ctx):
    (hq, hff, hfb, hi, hg, rq, rk, rv, rg, dq, dk, dv) = split_cols(h @ w_in)
    (hq_c, hff_c, hfb_c, hi_c, hg_c, rq_c, rk_c, rv_c, rg_c, dq_c, dk_c, dv_c) = split_cols(hc @ w_in)

    lb = lower_bound.astype(F32).reshape(HG_HEADS, HG_KDIM)

    def hg_gates(z):
        z = split_heads(z, HG_KDIM).astype(F32)
        logf = jnp.logaddexp(jnp.log(lb), jnp.log1p(-lb) + jax.nn.log_sigmoid(z))
        return (1.0 - lb) * jax.nn.sigmoid(-z), logf

    hq_h, hq_ch = split_heads(hq, HG_KDIM), split_heads(hq_c, HG_KDIM)
    hv, hv_c = split_heads(hi, HG_VDIM), split_heads(hi_c, HG_VDIM)
    kf, lff = hg_gates(hff)
    kf_c, lff_c = hg_gates(hff_c)
    kb, lfb = hg_gates(hfb)
    kb_c, lfb_c = hg_gates(hfb_c)
    hg_f, hg_fc = prefixed_scan(hq_h, kf, hv, lff, hq_ch, kf_c, hv_c, lff_c, need_ctx, False)
    hg_b, hg_bc = prefixed_scan(hq_h, kb, hv, lfb, hq_ch, kb_c, hv_c, lfb_c, need_ctx, True)
    out_a = gated_head_norm(hg_f + hg_b, hg_norm_w, hg)

    log_gamma = jnp.log1p(-jnp.exp2(-ret_dec.astype(F32)))

    def decay_like(t, direction):
        return jnp.broadcast_to(log_gamma[direction][:, None], t.shape)

    kscale = RET_KDIM ** -0.5
    r_q = apply_rope(split_heads(rq, RET_KDIM), *rope_seq)
    r_k = apply_rope(split_heads(rk, RET_KDIM), *rope_seq) * kscale
    r_v = split_heads(rv, RET_VDIM)
    r_qc = split_heads(rq_c, RET_KDIM)
    r_kc = split_heads(rk_c, RET_KDIM) * kscale
    r_vc = split_heads(rv_c, RET_VDIM)
    ret_f, ret_fc = prefixed_scan(r_q, r_k, r_v, decay_like(r_q, 0), r_qc, r_kc, r_vc, decay_like(r_qc, 0), need_ctx, False)
    ret_b, ret_bc = prefixed_scan(r_q, r_k, r_v, decay_like(r_q, 1), r_qc, r_kc, r_vc, decay_like(r_qc, 1), need_ctx, True)
    out_b = gated_head_norm(ret_f + ret_b, ret_norm_w, rg)

    lamv = da_lam.astype(F32)
    lam = jnp.exp(jnp.sum(lamv[0] * lamv[1])) - jnp.exp(jnp.sum(lamv[2] * lamv[3])) + lam_init
    d_q = apply_axial_rope(split_heads(dq, DA_HEAD_DIM), rope_row, rope_col)
    d_k = apply_axial_rope(split_heads(dk, DA_HEAD_DIM), rope_row, rope_col)
    d_v = split_heads(dv, DA_VDIM)
    d_qc, d_kc, d_vc = split_heads(dq_c, DA_HEAD_DIM), split_heads(dk_c, DA_HEAD_DIM), split_heads(dv_c, DA_VDIM)
    o_c = diff_attention(d_q, jnp.concatenate([d_kc, d_k], axis=1), jnp.concatenate([d_vc, d_v], axis=1), lam)
    out_c = (rms_norm(o_c, da_subln_w) * (1.0 - lam_init)).reshape(o_c.shape[:2] + (-1,))

    y = jnp.concatenate([out_a, out_b, out_c], axis=-1) @ w_out
    if not need_ctx:
        return y, None
    oc_c = diff_attention(d_qc, d_kc, d_vc, lam)
    yc = jnp.concatenate([gated_head_norm(hg_fc + hg_bc, hg_norm_w, hg_c), gated_head_norm(ret_fc + ret_bc, ret_norm_w, rg_c), (rms_norm(oc_c, da_subln_w) * (1.0 - lam_init)).reshape(oc_c.shape[:2] + (-1,))], axis=-1) @ w_out
    return y, yc


def clamped_swiglu(a):
    a_glu = jnp.minimum(a[..., ::2], SWIGLU_LIMIT)
    a_lin = jnp.clip(a[..., 1::2], -SWIGLU_LIMIT, SWIGLU_LIMIT)
    return a_glu * jax.nn.sigmoid(SWIGLU_ALPHA * a_glu) * (a_lin + 1.0)


def moe_ffn(h, router_w, router_b, w1, b1, w2, b2):
    N, D = h.shape
    logits = (h @ router_w + router_b).astype(F32)
    top_val, top_idx = lax.top_k(logits, TOP_K)
    gates = jax.nn.softmax(top_val, axis=-1)
    A = N * TOP_K
    flat_e = top_idx.reshape(A)
    order = jnp.argsort(flat_e)
    sorted_e = flat_e[order]
    counts = jnp.bincount(flat_e, length=N_EXPERTS)
    padded = (counts + EXPERT_BLOCK - 1) // EXPERT_BLOCK * EXPERT_BLOCK
    start = jnp.cumsum(counts) - counts
    pend = jnp.cumsum(padded)
    pstart = pend - padded
    dest = pstart[sorted_e] + jnp.arange(A) - start[sorted_e]
    n_blocks = -(-(A + N_EXPERTS * (EXPERT_BLOCK - 1)) // EXPERT_BLOCK)
    P = n_blocks * EXPERT_BLOCK
    slot = jnp.full((P,), A, dtype=jnp.int32).at[dest].set(order.astype(jnp.int32))
    tok = jnp.where(slot < A, slot // TOP_K, N)
    slot_gate = jnp.concatenate([gates.reshape(A), jnp.zeros((1,), F32)])[slot]
    block_expert = jnp.minimum(jnp.searchsorted(pend, jnp.arange(n_blocks) * EXPERT_BLOCK, side='right'), N_EXPERTS - 1)
    h_pad = jnp.concatenate([h, jnp.zeros((1, D), h.dtype)], axis=0)

    def run_block(args):
        tok_b, gate_b, e = args
        a = h_pad[tok_b] @ w1[e] + b1[e]
        y = clamped_swiglu(a) @ w2[e] + b2[e]
        return y * gate_b[:, None].astype(y.dtype)

    y = lax.map(run_block, (tok.reshape(n_blocks, EXPERT_BLOCK), slot_gate.reshape(n_blocks, EXPERT_BLOCK), block_expert))
    return jax.ops.segment_sum(y.reshape(P, D), tok, num_segments=N + 1)[:N]


def setup_inputs(seed: int = 0) -> dict:
    key = jax.random.key(seed)
    ks = jax.random.split(key, 24)
    D = D_MODEL

    def nrm(k, shape, s):
        return jax.random.normal(k, shape, F32) * s

    return {
        'x': nrm(ks[0], (BATCH, SEQ, D), 1.0),
        'c': nrm(ks[1], (BATCH, D), 1.0),
        'ctx': nrm(ks[2], (BATCH, CTX_LEN, D), 1.0),
        'c_ctx': nrm(ks[3], (D,), 1.0),
        'mod_w': nrm(ks[4], (DEPTH, D, 6 * D), 0.3 * D ** -0.5),
        'mod_b': nrm(ks[5], (DEPTH, 6 * D), 0.02),
        'norm_g': 1.0 + nrm(ks[6], (DEPTH, 4, D), 0.02),
        'w_in': nrm(ks[7], (DEPTH, D, IN_COLS), D ** -0.5),
        'hgrn_lb': nrm(ks[8], (DEPTH, HG_HEADS * HG_KDIM), 1.0),
        'hgrn_norm': 1.0 + nrm(ks[9], (DEPTH, HG_VDIM), 0.02),
        'ret_decay': 5.0 + jnp.arange(RET_HEADS, dtype=F32) + nrm(ks[10], (DEPTH, 2, RET_HEADS), 0.1),
        'ret_norm': 1.0 + nrm(ks[11], (DEPTH, RET_VDIM), 0.02),
        'da_lambda': nrm(ks[12], (DEPTH, 4, DA_HEAD_DIM), 0.1),
        'da_subln': 1.0 + nrm(ks[13], (DEPTH, DA_VDIM), 0.02),
        'w_out': nrm(ks[14], (DEPTH, MIX_WIDTH, D), MIX_WIDTH ** -0.5),
        'router_w': nrm(ks[15], (DEPTH, D, N_EXPERTS), D ** -0.5),
        'router_b': nrm(ks[16], (DEPTH, N_EXPERTS), 0.01),
        'w1': nrm(ks[17], (DEPTH, N_EXPERTS, D, 2 * D_FF), D ** -0.5),
        'b1': nrm(ks[18], (DEPTH, N_EXPERTS, 2 * D_FF), 0.02),
        'w2': nrm(ks[19], (DEPTH, N_EXPERTS, D_FF, D), D_FF ** -0.5),
        'b2': nrm(ks[20], (DEPTH, N_EXPERTS, D), 0.02),
    }


def reference(x, c, ctx, c_ctx, mod_w, mod_b, norm_g, w_in, hgrn_lb, hgrn_norm, ret_decay, ret_norm, da_lambda, da_subln, w_out, router_w, router_b, w1, b1, w2, b2):
    B, S, D = x.shape
    C = ctx.shape[1]
    rows = S // GRID_W
    pos = jnp.arange(rows * GRID_W)
    rope_row = rope_tables(pos // GRID_W, DA_HEAD_DIM // 2)
    rope_col = rope_tables(pos % GRID_W, DA_HEAD_DIM // 2)
    rope_seq = rope_tables(pos, RET_KDIM)
    lb_cum = jnp.cumsum(jax.nn.softmax(hgrn_lb.astype(F32), axis=0), axis=0)
    lower_bounds = lb_cum - lb_cum[0:1]
    xc = ctx
    for layer in range(DEPTH):
        need_ctx = layer < DEPTH - 1
        lam_init = 0.8 - 0.6 * math.exp(-0.3 * layer)
        sh1, sc1, g1, sh2, sc2, g2 = [m[:, None, :] for m in jnp.split(jax.nn.silu(c) @ mod_w[layer] + mod_b[layer], 6, axis=-1)]
        csh1, csc1, cg1, csh2, csc2, cg2 = jnp.split(jax.nn.silu(c_ctx) @ mod_w[layer] + mod_b[layer], 6, axis=-1)
        h = rms_norm(x, norm_g[layer, 0]) * (1.0 + sc1) + sh1
        hc = rms_norm(xc, norm_g[layer, 0]) * (1.0 + csc1) + csh1
        y, yc = token_mixer(h, hc, w_in[layer], w_out[layer], lower_bounds[layer], hgrn_norm[layer], ret_decay[layer], ret_norm[layer], da_lambda[layer], da_subln[layer], lam_init, rope_row, rope_col, rope_seq, need_ctx)
        x = x + g1 * rms_norm(y, norm_g[layer, 1])
        tokens = (rms_norm(x, norm_g[layer, 2]) * (1.0 + sc2) + sh2).reshape(B * S, D)
        if need_ctx:
            xc = xc + cg1 * rms_norm(yc, norm_g[layer, 1])
            tokens_c = (rms_norm(xc, norm_g[layer, 2]) * (1.0 + csc2) + csh2).reshape(B * C, D)
            tokens = jnp.concatenate([tokens, tokens_c], axis=0)
        f = moe_ffn(tokens, router_w[layer], router_b[layer], w1[layer], b1[layer], w2[layer], b2[layer])
        x = x + g2 * rms_norm(f[:B * S].reshape(B, S, D), norm_g[layer, 3])
        if need_ctx:
            xc = xc + cg2 * rms_norm(f[B * S:].reshape(B, C, D), norm_g[layer, 3])
    return x
```

```python
import functools
import math

import jax
import jax.numpy as jnp
from jax import lax
from jax.experimental import pallas as pl
from jax.experimental.pallas import tpu as pltpu

F32 = jnp.float32
BF16 = jnp.bfloat16

GRID_W = 64
HEADS = 6
HDIM = 64
REC_W = HEADS * HDIM
DA_HEADS = 4
DA_HEAD_DIM = 32
DA_MAPS = 2 * DA_HEADS
DA_W = DA_MAPS * DA_HEAD_DIM
DA_VDIM = 2 * DA_HEAD_DIM
ROPE_BASE = 10000.0
N_EXPERTS = 32
TOP_K = 4
SWIGLU_ALPHA = 1.702
SWIGLU_LIMIT = 7.0
NORM_EPS = 1e-6

SUB = 16
NEG_BIG = -1e30
VMEM_LIMIT = 56 * 1024 * 1024

_C_HQ, _C_HFF, _C_HFB, _C_HI, _C_HG = 0, 384, 768, 1152, 1536
_C_RQ, _C_RK, _C_RV, _C_RG = 1920, 2304, 2688, 3072
_C_DQ, _C_DK, _C_DV = 3456, 3712, 3968
_C_RQR, _C_RKR, _C_DQR, _C_DKR = 4224, 4608, 4992, 5248
_W_ALL = 5504


def _dot(a, b):
    return jnp.dot(a, b, preferred_element_type=F32)


def _dot_nt(a, b):
    return lax.dot_general(a, b, (((1,), (1,)), ((), ())), preferred_element_type=F32)


def _dot_tn(a, b):
    return lax.dot_general(a, b, (((0,), (0,)), ((), ())), preferred_element_type=F32)


def _sigmoid(x):
    return 1.0 / (1.0 + jnp.exp(-x))


def _rms(x, g):
    ms = jnp.mean(x * x, axis=-1, keepdims=True)
    return x * lax.rsqrt(ms + NORM_EPS) * g


def _params(*sem):
    return pltpu.CompilerParams(dimension_semantics=sem, vmem_limit_bytes=VMEM_LIMIT)


def _mod_kernel(c_ref, w_ref, b_ref, o_ref):
    c = c_ref[...]
    o_ref[0] = _dot(c * _sigmoid(c), w_ref[0]) + b_ref[0]


def _modulation(cvec, mod_w, mod_b):
    depth, d, n = mod_w.shape
    tn = 1536
    return pl.pallas_call(
        _mod_kernel,
        grid=(depth, n // tn),
        in_specs=[
            pl.BlockSpec((8, d), lambda l, j: (0, 0)),
            pl.BlockSpec((1, d, tn), lambda l, j: (l, 0, j)),
            pl.BlockSpec((1, 1, tn), lambda l, j: (l, 0, j)),
        ],
        out_specs=pl.BlockSpec((1, 8, tn), lambda l, j: (l, 0, j)),
        out_shape=jax.ShapeDtypeStruct((depth, 8, n), F32),
        compiler_params=_params("arbitrary", "arbitrary"),
        name="modulation",
    )(cvec, mod_w, mod_b.reshape(depth, 1, n))


def _inproj_kernel(x_ref, g_ref, sc_ref, sh_ref, w_ref, lbc_ref, cosr_ref, sinr_ref, cosd_ref, sind_ref,
                   hq_ref, hv_ref, hg_ref, kf_ref, lff_ref, kb_ref, lfb_ref,
                   rq_ref, rk_ref, rv_ref, rg_ref, dq_ref, dk_ref, dvx_ref):
    x = x_ref[...]
    h = _rms(x, g_ref[...]) * (1.0 + sc_ref[0]) + sh_ref[0]
    hb = h.astype(BF16)

    def proj(c0, n):
        return _dot(hb, w_ref[:, c0:c0 + n])

    hq_ref[...] = proj(_C_HQ, REC_W).astype(BF16)
    hv_ref[...] = proj(_C_HI, REC_W).astype(BF16)
    hg_ref[...] = proj(_C_HG, REC_W).astype(BF16)

    log_lb = lbc_ref[0:1, :]
    log_1m = lbc_ref[1:2, :]
    one_m = lbc_ref[2:3, :]

    def gates(z):
        log_sig = jnp.minimum(z, 0.0) - jnp.log1p(jnp.exp(-jnp.abs(z)))
        t = log_1m + log_sig
        m = jnp.maximum(log_lb, t)
        logf = m + jnp.log1p(jnp.exp(-jnp.abs(log_lb - t)))
        return one_m / (1.0 + jnp.exp(z)), logf

    k, lf = gates(proj(_C_HFF, REC_W))
    kf_ref[...] = k.astype(BF16)
    lff_ref[...] = lf
    k, lf = gates(proj(_C_HFB, REC_W))
    kb_ref[...] = k.astype(BF16)
    lfb_ref[...] = lf

    cosr = cosr_ref[...]
    sinr = sinr_ref[...]
    rq_ref[...] = (proj(_C_RQ, REC_W) * cosr + proj(_C_RQR, REC_W) * sinr).astype(BF16)
    rk_ref[...] = ((proj(_C_RK, REC_W) * cosr + proj(_C_RKR, REC_W) * sinr) * (HDIM ** -0.5)).astype(BF16)
    rv_ref[...] = proj(_C_RV, REC_W).astype(BF16)
    rg_ref[...] = proj(_C_RG, REC_W).astype(BF16)

    cosd = cosd_ref[...]
    sind = sind_ref[...]
    dq_ref[...] = ((proj(_C_DQ, DA_W) * cosd + proj(_C_DQR, DA_W) * sind) * (DA_HEAD_DIM ** -0.5)).astype(BF16)
    dk_ref[...] = (proj(_C_DK, DA_W) * cosd + proj(_C_DKR, DA_W) * sind).astype(BF16)
    dv = proj(_C_DV, DA_W).astype(BF16)
    tm = dv.shape[0]
    lane = lax.broadcasted_iota(jnp.int32, (tm, 64), 1)
    ones_col = jnp.where(lane == 0, 1.0, 0.0).astype(BF16)
    for h_ in range(DA_HEADS):
        dvx_ref[:, 128 * h_:128 * h_ + 64] = dv[:, 64 * h_:64 * h_ + 64]
        dvx_ref[:, 128 * h_ + 64:128 * h_ + 128] = ones_col


def _inproj(xf, seq, gnorm, sc, sh, w_all, lbc, cosr, sinr, cosd, sind):
    rows, d = xf.shape
    tm = min(256, seq)
    nb = seq // tm
    row_spec = lambda w: pl.BlockSpec((tm, w), lambda i: (i, 0))
    tab_spec = lambda w: pl.BlockSpec((tm, w), lambda i: (i % nb, 0))
    mod_spec = pl.BlockSpec((1, 1, d), lambda i: (i // nb, 0, 0))
    widths = [REC_W] * 11 + [DA_W, DA_W, 2 * DA_W]
    dtypes = [BF16, BF16, BF16, BF16, F32, BF16, F32, BF16, BF16, BF16, BF16, BF16, BF16, BF16]
    return pl.pallas_call(
        _inproj_kernel,
        grid=(rows // tm,),
        in_specs=[
            row_spec(d),
            pl.BlockSpec((1, d), lambda i: (0, 0)),
            mod_spec, mod_spec,
            pl.BlockSpec((d, _W_ALL), lambda i: (0, 0)),
            pl.BlockSpec((8, REC_W), lambda i: (0, 0)),
            tab_spec(REC_W), tab_spec(REC_W), tab_spec(DA_W), tab_spec(DA_W),
        ],
        out_specs=[row_spec(w) for w in widths],
        out_shape=[jax.ShapeDtypeStruct((rows, w), dt) for w, dt in zip(widths, dtypes)],
        compiler_params=_params("arbitrary"),
        name="inproj",
    )(xf, gnorm, sc, sh, w_all, lbc, cosr, sinr, cosd, sind)


def _hgrn_dir(q, k, v, lf, seg, st_ref, inter_ref, reverse):
    tt = q.shape[0]
    nblk = tt // SUB
    row = lax.broadcasted_iota(jnp.int32, (tt, tt), 0)
    col = lax.broadcasted_iota(jnp.int32, (tt, tt), 1)
    same = (row // SUB) == (col // SUB)
    tri = (col >= row) if reverse else (col <= row)
    l_all = jnp.where(same, 1.0, 0.0)
    l_cum = jnp.where(tri, l_all, 0.0).astype(BF16)
    l_all = l_all.astype(BF16)
    hi = lf.astype(BF16)
    r1 = lf - hi.astype(F32)
    mid = r1.astype(BF16)
    lo = (r1 - mid.astype(F32)).astype(BF16)
    b = _dot(l_cum, hi) + _dot(l_cum, mid) + _dot(l_cum, lo)
    btot = _dot(l_all, hi) + _dot(l_all, mid) + _dot(l_all, lo)

    p = lax.broadcasted_iota(jnp.int32, (tt, 1), 0) % SUB
    acc = _dot((q * k).astype(BF16), seg) * v
    for d in range(1, SUB):
        sh = (tt - d) if reverse else d
        ks = pltpu.roll(k, sh, 0)
        bs = pltpu.roll(b, sh, 0)
        vs = pltpu.roll(v, sh, 0)
        mask = (p <= SUB - 1 - d) if reverse else (p >= d)
        e = jnp.exp(jnp.where(mask, b - bs, NEG_BIG))
        acc = acc + _dot((q * ks * e).astype(BF16), seg) * vs

    qe = (q * jnp.exp(b)).astype(BF16)
    kd = (k * jnp.exp(btot - b)).astype(BF16)
    dec = jnp.exp(btot)
    vb = v.astype(BF16)
    order = range(nblk - 1, -1, -1) if reverse else range(nblk)
    for h in range(HEADS):
        c0 = h * HDIM
        st = st_ref[h]
        for j in order:
            r0 = j * SUB
            inter_ref[r0:r0 + SUB, c0:c0 + HDIM] = _dot_nt(qe[r0:r0 + SUB, c0:c0 + HDIM], st.astype(BF16))
            dst = _dot_tn(vb[r0:r0 + SUB, c0:c0 + HDIM], kd[r0:r0 + SUB, c0:c0 + HDIM])
            st = st * dec[r0:r0 + 1, c0:c0 + HDIM] + dst
        st_ref[h] = st
    return acc + inter_ref[...]


def _hgrn_kernel(qf_ref, vf_ref, kf_ref, lff_ref, qb_ref, vb_ref, kb_ref, lfb_ref, s0_ref, seg_ref,
                 of_ref, ob_ref, sout_ref, st_ref, inter_ref):
    i = pl.program_id(1)

    @pl.when(i == 0)
    def _():
        st_ref[...] = s0_ref[0]

    seg = seg_ref[...]
    of_ref[...] = _hgrn_dir(qf_ref[...].astype(F32), kf_ref[...].astype(F32), vf_ref[...].astype(F32),
                            lff_ref[...], seg, st_ref.at[0], inter_ref, False).astype(BF16)
    ob_ref[...] = _hgrn_dir(qb_ref[...].astype(F32), kb_ref[...].astype(F32), vb_ref[...].astype(F32),
                            lfb_ref[...], seg, st_ref.at[1], inter_ref, True).astype(BF16)

    @pl.when(i == pl.num_programs(1) - 1)
    def _():
        sout_ref[0] = st_ref[...]


def _hgrn_scan(hq, hv, kf, lff, kb, lfb, s0, seg, batch, seq):
    tt = min(256, seq)
    n = seq // tt
    fwd = pl.BlockSpec((tt, REC_W), lambda b, i: (b * n + i, 0))
    bwd = pl.BlockSpec((tt, REC_W), lambda b, i: (b * n + n - 1 - i, 0))
    st_spec = pl.BlockSpec((1, 2, HEADS, HDIM, HDIM), lambda b, i: (b, 0, 0, 0, 0))
    rows = batch * seq
    return pl.pallas_call(
        _hgrn_kernel,
        grid=(batch, n),
        in_specs=[fwd, fwd, fwd, fwd, bwd, bwd, bwd, bwd, st_spec,
                  pl.BlockSpec((REC_W, REC_W), lambda b, i: (0, 0))],
        out_specs=[fwd, bwd, st_spec],
        out_shape=[jax.ShapeDtypeStruct((rows, REC_W), BF16), jax.ShapeDtypeStruct((rows, REC_W), BF16),
                   jax.ShapeDtypeStruct(s0.shape, F32)],
        scratch_shapes=[pltpu.VMEM((2, HEADS, HDIM, HDIM), F32), pltpu.VMEM((tt, REC_W), F32)],
        compiler_params=_params("arbitrary", "arbitrary"),
        name="hgrn_scan",
    )(hq, hv, kf, lff, hq, hv, kb, lfb, s0, seg)


def _ret_kernel(lg_ref, qf_ref, kf_ref, vf_ref, qb_ref, kb_ref, vb_ref, s0_ref, lgrow_ref,
                of_ref, ob_ref, sout_ref, st_ref, dmask_ref, tab_ref, o_scr):
    i = pl.program_id(1)
    tt = qf_ref.shape[0]

    @pl.when(i == 0)
    def _():
        st_ref[...] = s0_ref[0]
        row = lax.broadcasted_iota(jnp.int32, (tt, tt), 0)
        col = lax.broadcasted_iota(jnp.int32, (tt, tt), 1)
        dist = (row - col).astype(F32)
        for h in range(HEADS):
            fw = jnp.where(dist >= 0, jnp.exp(dist * lg_ref[0, h]), 0.0)
            bw = jnp.where(dist <= 0, jnp.exp(-dist * lg_ref[1, h]), 0.0)
            dmask_ref[h] = fw + bw
        pos = lax.broadcasted_iota(jnp.int32, (tt, REC_W), 0).astype(F32)
        lgf = lgrow_ref[0:1, :]
        lgb = lgrow_ref[1:2, :]
        tab_ref[0] = jnp.exp((pos + 1.0) * lgf)
        tab_ref[1] = jnp.exp((tt - 1.0 - pos) * lgf)
        tab_ref[2] = jnp.exp((tt - pos) * lgb)
        tab_ref[3] = jnp.exp(pos * lgb)

    tile_f = jnp.exp(tt * lgrow_ref[0:1, :])
    tile_b = jnp.exp(tt * lgrow_ref[1:2, :])

    q = qf_ref[...]
    k = kf_ref[...]
    v = vf_ref[...]
    qe = (q.astype(F32) * tab_ref[0]).astype(BF16)
    ke = (k.astype(F32) * tab_ref[1]).astype(BF16)
    for h in range(HEADS):
        c = slice(h * HDIM, (h + 1) * HDIM)
        s = _dot_nt(q[:, c], k[:, c]) * dmask_ref[h]
        st = st_ref[0, h]
        o_scr[:, c] = _dot(s.astype(BF16), v[:, c]) + _dot(qe[:, c], st.astype(BF16))
        st_ref[0, h] = st * tile_f[:, c] + _dot_tn(ke[:, c], v[:, c])
    of_ref[...] = o_scr[...].astype(BF16)

    q = qb_ref[...]
    k = kb_ref[...]
    v = vb_ref[...]
    qe = (q.astype(F32) * tab_ref[2]).astype(BF16)
    ke = (k.astype(F32) * tab_ref[3]).astype(BF16)
    for h in range(HEADS):
        c = slice(h * HDIM, (h + 1) * HDIM)
        st = st_ref[1, h]
        o_scr[:, c] = _dot(qe[:, c], st.astype(BF16))
        st_ref[1, h] = st * tile_b[:, c] + _dot_tn(ke[:, c], v[:, c])
    ob_ref[...] = o_scr[...].astype(BF16)

    @pl.when(i == pl.num_programs(1) - 1)
    def _():
        sout_ref[0] = st_ref[...]


def _ret_scan(rq, rk, rv, s0, lg, lgrow, batch, seq):
    tt = min(256, seq)
    n = seq // tt
    fwd = pl.BlockSpec((tt, REC_W), lambda b, i: (b * n + i, 0))
    bwd = pl.BlockSpec((tt, REC_W), lambda b, i: (b * n + n - 1 - i, 0))
    st_spec = pl.BlockSpec((1, 2, HEADS, HDIM, HDIM), lambda b, i: (b, 0, 0, 0, 0))
    rows = batch * seq
    return pl.pallas_call(
        _ret_kernel,
        grid=(batch, n),
        in_specs=[pl.BlockSpec(memory_space=pltpu.SMEM), fwd, fwd, fwd, bwd, bwd, bwd, st_spec,
                  pl.BlockSpec((8, REC_W), lambda b, i: (0, 0))],
        out_specs=[fwd, bwd, st_spec],
        out_shape=[jax.ShapeDtypeStruct((rows, REC_W), BF16), jax.ShapeDtypeStruct((rows, REC_W), BF16),
                   jax.ShapeDtypeStruct(s0.shape, F32)],
        scratch_shapes=[pltpu.VMEM((2, HEADS, HDIM, HDIM), F32), pltpu.VMEM((HEADS, tt, tt), F32),
                        pltpu.VMEM((4, tt, REC_W), F32), pltpu.VMEM((tt, REC_W), F32)],
        compiler_params=_params("arbitrary", "arbitrary"),
        name="ret_scan",
    )(lg, rq, rk, rv, rq, rk, rv, s0, lgrow)


def _attn_kernel(lam_ref, q_ref, kc_ref, vc_ref, k_ref, v_ref, gain_ref, o_ref, qs_ref, m_ref, acc_ref,
                 *, with_latent, kchunk, out_scale):
    j = pl.program_id(2)
    tq = q_ref.shape[0]

    def consume(kblk, vblk):
        for m in range(DA_MAPS):
            hcol = (m // 2) * 128
            s = _dot_nt(qs_ref[m], kblk)
            m_prev = m_ref[m]
            m_new = jnp.maximum(m_prev, jnp.max(s, axis=-1, keepdims=True))
            alpha = jnp.exp(m_prev - m_new)
            p = jnp.exp(s - m_new).astype(BF16)
            acc_ref[m] = alpha * acc_ref[m] + _dot(p, vblk[:, hcol:hcol + 128])
            m_ref[m] = m_new

    @pl.when(j == 0)
    def _():
        q = q_ref[...]
        lane = lax.broadcasted_iota(jnp.int32, (tq, DA_W), 1) // DA_HEAD_DIM
        for m in range(DA_MAPS):
            qs_ref[m] = jnp.where(lane == m, q, jnp.zeros_like(q))
        m_ref[...] = jnp.full(m_ref.shape, NEG_BIG, F32)
        acc_ref[...] = jnp.zeros(acc_ref.shape, F32)
        consume(kc_ref[...], vc_ref[...])

    if with_latent:
        tk = k_ref.shape[0]
        for c0 in range(0, tk, kchunk):
            consume(k_ref[c0:c0 + kchunk, :], v_ref[c0:c0 + kchunk, :])

    @pl.when(j == pl.num_programs(2) - 1)
    def _():
        lam = lam_ref[0]
        for h in range(DA_HEADS):
            a1 = acc_ref[2 * h]
            a2 = acc_ref[2 * h + 1]
            o = a1[:, :DA_VDIM] / a1[:, DA_VDIM:DA_VDIM + 1] - lam * (a2[:, :DA_VDIM] / a2[:, DA_VDIM:DA_VDIM + 1])
            c = slice(h * DA_VDIM, (h + 1) * DA_VDIM)
            o_ref[:, c] = (_rms(o, gain_ref[:, c]) * out_scale).astype(BF16)


def _diff_attention(lam, q, kc, vc, k, v, gain, batch, seq_q, ctx_len, seq_k, out_scale, q_is_ctx):
    tq = min(512, seq_q)
    nq = seq_q // tq
    with_latent = not q_is_ctx
    if with_latent:
        tk = min(2048, seq_k)
        nk = seq_k // tk
        kchunk = min(512, tk)
    else:
        k, v = kc, vc
        tk, nk, kchunk = ctx_len, 1, ctx_len
    nkb = (seq_k // tk) if with_latent else 1
    kern = functools.partial(_attn_kernel, with_latent=with_latent, kchunk=kchunk, out_scale=out_scale)
    return pl.pallas_call(
        kern,
        grid=(batch, nq, nk),
        in_specs=[
            pl.BlockSpec(memory_space=pltpu.SMEM),
            pl.BlockSpec((tq, DA_W), lambda b, i, j: (b * nq + i, 0)),
            pl.BlockSpec((ctx_len, DA_W), lambda b, i, j: (b, 0)),
            pl.BlockSpec((ctx_len, 2 * DA_W), lambda b, i, j: (b, 0)),
            pl.BlockSpec((tk, DA_W), lambda b, i, j: (b * nkb + j, 0)),
            pl.BlockSpec((tk, 2 * DA_W), lambda b, i, j: (b * nkb + j, 0)),
            pl.BlockSpec((1, DA_W), lambda b, i, j: (0, 0)),
        ],
        out_specs=pl.BlockSpec((tq, DA_W), lambda b, i, j: (b * nq + i, 0)),
        out_shape=jax.ShapeDtypeStruct((batch * seq_q, DA_W), BF16),
        scratch_shapes=[pltpu.VMEM((DA_MAPS, tq, DA_W), BF16), pltpu.VMEM((DA_MAPS, tq, 1), F32),
                        pltpu.VMEM((DA_MAPS, tq, 128), F32)],
        compiler_params=_params("arbitrary", "arbitrary", "arbitrary"),
        name="diff_attention",
    )(lam, q, kc, vc, k, v, gain)


def _outproj_kernel(x_ref, hof_ref, hob_ref, hg_ref, rof_ref, rob_ref, rg_ref, da_ref, seg_ref, w_ref,
                    hn_ref, rn_ref, n1_ref, n2_ref, g1_ref, sc2_ref, sh2_ref, rw_ref, rb_ref,
                    x1_ref, tok_ref, logit_ref):
    seg = seg_ref[...]

    def gated_head_norm(o, gain, gate):
        ms = _dot((o * o).astype(BF16), seg) * (1.0 / HDIM)
        return (o * lax.rsqrt(ms + NORM_EPS) * gain * (gate * _sigmoid(gate))).astype(BF16)

    a = gated_head_norm(hof_ref[...].astype(F32) + hob_ref[...].astype(F32), hn_ref[...], hg_ref[...].astype(F32))
    b = gated_head_norm(rof_ref[...].astype(F32) + rob_ref[...].astype(F32), rn_ref[...], rg_ref[...].astype(F32))
    y = (_dot(a, w_ref[0:REC_W, :]) + _dot(b, w_ref[REC_W:2 * REC_W, :])
         + _dot(da_ref[...], w_ref[2 * REC_W:2 * REC_W + DA_W, :]))
    x1 = x_ref[...] + g1_ref[0] * _rms(y, n1_ref[...])
    x1_ref[...] = x1
    tok = _rms(x1, n2_ref[...]) * (1.0 + sc2_ref[0]) + sh2_ref[0]
    tok_ref[...] = tok.astype(BF16)
    logit_ref[...] = _dot(tok, rw_ref[...]) + rb_ref[...]


def _outproj(xf, seq, streams, seg, w_out, hn, rn, n1, n2, g1, sc2, sh2, rw, rb):
    rows, d = xf.shape
    tm = min(256, seq)
    nb = seq // tm
    row_spec = lambda w: pl.BlockSpec((tm, w), lambda i: (i, 0))
    full = lambda a: pl.BlockSpec(a.shape, lambda i: (0,) * a.ndim)
    mod_spec = pl.BlockSpec((1, 1, d), lambda i: (i // nb, 0, 0))
    hof, hob, hg, rof, rob, rg, da = streams
    return pl.pallas_call(
        _outproj_kernel,
        grid=(rows // tm,),
        in_specs=[row_spec(d)] + [row_spec(REC_W)] * 6 + [row_spec(DA_W), full(seg), full(w_out),
                  full(hn), full(rn), full(n1), full(n2), mod_spec, mod_spec, mod_spec, full(rw), full(rb)],
        out_specs=[row_spec(d), row_spec(d), row_spec(N_EXPERTS)],
        out_shape=[jax.ShapeDtypeStruct((rows, d), F32), jax.ShapeDtypeStruct((rows, d), BF16),
                   jax.ShapeDtypeStruct((rows, N_EXPERTS), F32)],
        compiler_params=_params("arbitrary"),
        name="outproj",
    )(xf, hof, hob, hg, rof, rob, rg, da, seg, w_out, hn, rn, n1, n2, g1, sc2, sh2, rw, rb)


def _ffn_kernel(be_ref, na_ref, x_ref, w1g_ref, w1l_ref, b1g_ref, b1l_ref, w2_ref, b2_ref, y_ref):
    active = pl.program_id(0) < na_ref[0]

    @pl.when(jnp.logical_not(active))
    def _():
        y_ref[...] = jnp.zeros(y_ref.shape, y_ref.dtype)

    @pl.when(active)
    def _():
        x = x_ref[...]
        glu = jnp.minimum(_dot(x, w1g_ref[0]) + b1g_ref[0], SWIGLU_LIMIT)
        lin = jnp.clip(_dot(x, w1l_ref[0]) + b1l_ref[0], -SWIGLU_LIMIT, SWIGLU_LIMIT)
        act = glu * _sigmoid(SWIGLU_ALPHA * glu) * (lin + 1.0)
        y_ref[...] = (_dot(act.astype(BF16), w2_ref[0]) + b2_ref[0]).astype(y_ref.dtype)


def _expert_ffn(block_expert, n_active, xg, w1g, w1l, b1g, b1l, w2, b2, bm):
    p, d = xg.shape
    f = w1g.shape[-1]
    nblocks = p // bm
    wspec = lambda s: pl.BlockSpec((1,) + s, lambda i, be, na: (be[i], 0, 0))
    return pl.pallas_call(
        _ffn_kernel,
        grid_spec=pltpu.PrefetchScalarGridSpec(
            num_scalar_prefetch=2,
            grid=(nblocks,),
            in_specs=[pl.BlockSpec((bm, d), lambda i, be, na: (i, 0)),
                      wspec((d, f)), wspec((d, f)), wspec((1, f)), wspec((1, f)), wspec((f, d)), wspec((1, d))],
            out_specs=pl.BlockSpec((bm, d), lambda i, be, na: (i, 0)),
        ),
        out_shape=jax.ShapeDtypeStruct((p, d), BF16),
        compiler_params=_params("arbitrary"),
        name="expert_ffn",
    )(block_expert, n_active, xg, w1g, w1l, b1g, b1l, w2, b2)


def _resid_kernel(x_ref, f_ref, n_ref, g_ref, o_ref):
    o_ref[...] = x_ref[...] + g_ref[0] * _rms(f_ref[...].astype(F32), n_ref[...])


def _ffn_residual(xf, f, seq, n3, g2):
    rows, d = xf.shape
    tm = min(512, seq)
    nb = seq // tm
    row_spec = pl.BlockSpec((tm, d), lambda i: (i, 0))
    return pl.pallas_call(
        _resid_kernel,
        grid=(rows // tm,),
        in_specs=[row_spec, row_spec, pl.BlockSpec((1, d), lambda i: (0, 0)),
                  pl.BlockSpec((1, 1, d), lambda i: (i // nb, 0, 0))],
        out_specs=row_spec,
        out_shape=jax.ShapeDtypeStruct((rows, d), F32),
        compiler_params=_params("arbitrary"),
        name="ffn_residual",
    )(xf, f, n3, g2)


def _rope_tables(pos, dim):
    inv = 1.0 / (ROPE_BASE ** (jnp.arange(0, dim, 2, dtype=F32) / dim))
    ang = pos.astype(F32)[:, None] * inv[None, :]
    return jnp.cos(ang), jnp.sin(ang)


def _rot_cols(w, head_dim, halves):
    d, n = w.shape
    g = head_dim // halves
    w4 = w.reshape(d, n // g, 2, g // 2)
    return jnp.concatenate([-w4[:, :, 1], w4[:, :, 0]], axis=-1).reshape(d, n)


def _prep_w_in(w):
    rq, rk = w[:, _C_RQ:_C_RQ + REC_W], w[:, _C_RK:_C_RK + REC_W]
    dq, dk = w[:, _C_DQ:_C_DQ + DA_W], w[:, _C_DK:_C_DK + DA_W]
    return jnp.concatenate([w, _rot_cols(rq, HDIM, 1), _rot_cols(rk, HDIM, 1),
                            _rot_cols(dq, DA_HEAD_DIM, 2), _rot_cols(dk, DA_HEAD_DIM, 2)], axis=1).astype(BF16)


def _routing(logits, bm):
    n = logits.shape[0]
    top_val, top_idx = lax.top_k(logits, TOP_K)
    gates = jax.nn.softmax(top_val, axis=-1)
    a = n * TOP_K
    flat_e = top_idx.reshape(a)
    order = jnp.argsort(flat_e)
    sorted_e = flat_e[order]
    counts = jnp.bincount(flat_e, length=N_EXPERTS)
    padded = (counts + bm - 1) // bm * bm
    start = jnp.cumsum(counts) - counts
    pend = jnp.cumsum(padded)
    pstart = pend - padded
    dest = (pstart[sorted_e] + jnp.arange(a) - start[sorted_e]).astype(jnp.int32)
    nblocks = -(-(a + N_EXPERTS * (bm - 1)) // bm)
    p = nblocks * bm
    tok_sorted = jnp.zeros((p,), jnp.int32).at[dest].set((order // TOP_K).astype(jnp.int32))
    pos = jnp.zeros((a,), jnp.int32).at[order].set(dest)
    block_expert = jnp.minimum(jnp.searchsorted(pend, jnp.arange(nblocks) * bm, side='right'),
                               N_EXPERTS - 1).astype(jnp.int32)
    n_active = (pend[-1:] // bm).astype(jnp.int32)
    return gates, tok_sorted, pos, block_expert, n_active


def kernel(x, c, ctx, c_ctx, mod_w, mod_b, norm_g, w_in, hgrn_lb, hgrn_norm, ret_decay, ret_norm, da_lambda,
           da_subln, w_out, router_w, router_b, w1, b1, w2, b2):
    B, S, D = x.shape
    C = ctx.shape[1]
    depth = mod_w.shape[0]
    bm = 256

    pos = jnp.arange(S)
    cr, sr = _rope_tables(pos // GRID_W, DA_HEAD_DIM // 2)
    cc, sc_ = _rope_tables(pos % GRID_W, DA_HEAD_DIM // 2)
    cs, ss = _rope_tables(pos, HDIM)
    cosr = jnp.tile(jnp.concatenate([cs, cs], -1), (1, HEADS))
    sinr = jnp.tile(jnp.concatenate([ss, ss], -1), (1, HEADS))
    cosd = jnp.tile(jnp.concatenate([cr, cr, cc, cc], -1), (1, DA_MAPS))
    sind = jnp.tile(jnp.concatenate([sr, sr, sc_, sc_], -1), (1, DA_MAPS))
    ones_r, zeros_r = jnp.ones((C, REC_W), F32), jnp.zeros((C, REC_W), F32)
    ones_d, zeros_d = jnp.ones((C, DA_W), F32), jnp.zeros((C, DA_W), F32)

    lb_cum = jnp.cumsum(jax.nn.softmax(hgrn_lb.astype(F32), axis=0), axis=0)
    lower = lb_cum - lb_cum[0:1]

    cvec = jnp.zeros((8, D), F32).at[:B].set(c).at[B].set(c_ctx)
    mods = _modulation(cvec, mod_w, mod_b)

    head_id = jnp.arange(REC_W) // HDIM
    seg = (head_id[:, None] == head_id[None, :]).astype(BF16)

    xf = x.reshape(B * S, D)
    xc = ctx.reshape(B * C, D)
    zero_state = jnp.zeros((B, 2, HEADS, HDIM, HDIM), F32)

    for layer in range(depth):
        need_ctx = layer < depth - 1
        lam_init = 0.8 - 0.6 * math.exp(-0.3 * layer)
        m6 = mods[layer].reshape(8, 6, D)
        lat = lambda k: m6[:B, k][:, None, :]
        cxm = lambda k: jnp.broadcast_to(m6[B, k][None, None, :], (B, 1, D))
        ng = norm_g[layer]
        lb = lower[layer]
        lbc = jnp.zeros((8, REC_W), F32).at[0].set(jnp.log(lb)).at[1].set(jnp.log1p(-lb)).at[2].set(1.0 - lb)
        w_all = _prep_w_in(w_in[layer])
        log_gamma = jnp.log1p(-jnp.exp2(-ret_decay[layer].astype(F32)))
        lgrow = jnp.zeros((8, REC_W), F32).at[:2].set(jnp.repeat(log_gamma, HDIM, axis=1))
        lamv = da_lambda[layer].astype(F32)
        lam = (jnp.exp(jnp.sum(lamv[0] * lamv[1])) - jnp.exp(jnp.sum(lamv[2] * lamv[3])) + lam_init).reshape(1)
        hn = jnp.tile(hgrn_norm[layer], HEADS)[None, :]
        rn = jnp.tile(ret_norm[layer], HEADS)[None, :]
        dn = jnp.tile(da_subln[layer], DA_HEADS)[None, :]
        w_out_b = w_out[layer].astype(BF16)
        rw = router_w[layer]
        rb = router_b[layer][None, :]

        (hq, hv, hg, kf, lff, kb, lfb, rq, rk, rv, rg, dq, dk, dvx) = _inproj(
            xf, S, ng[0:1], lat(1), lat(0), w_all, lbc, cosr, sinr, cosd, sind)
        (hq_c, hv_c, hg_c, kf_c, lff_c, kb_c, lfb_c, rq_c, rk_c, rv_c, rg_c, dq_c, dk_c, dvx_c) = _inproj(
            xc, C, ng[0:1], cxm(1), cxm(0), w_all, lbc, ones_r, zeros_r, ones_d, zeros_d)

        hof_c, hob_c, hs = _hgrn_scan(hq_c, hv_c, kf_c, lff_c, kb_c, lfb_c, zero_state, seg, B, C)
        hof, hob, _ = _hgrn_scan(hq, hv, kf, lff, kb, lfb, hs, seg, B, S)
        rof_c, rob_c, rs = _ret_scan(rq_c, rk_c, rv_c, zero_state, log_gamma, lgrow, B, C)
        rof, rob, _ = _ret_scan(rq, rk, rv, rs, log_gamma, lgrow, B, S)
        out_scale = 1.0 - lam_init
        da = _diff_attention(lam, dq, dk_c, dvx_c, dk, dvx, dn, B, S, C, S, out_scale, False)

        x1, tok, logits = _outproj(xf, S, (hof, hob, hg, rof, rob, rg, da), seg, w_out_b, hn, rn,
                                   ng[1:2], ng[2:3], lat(2), lat(4), lat(3), rw, rb)
        if need_ctx:
            da_c = _diff_attention(lam, dq_c, dk_c, dvx_c, None, None, dn, B, C, C, C, out_scale, True)
            xc1, tok_c, logits_c = _outproj(xc, C, (hof_c, hob_c, hg_c, rof_c, rob_c, rg_c, da_c), seg, w_out_b,
                                            hn, rn, ng[1:2], ng[2:3], cxm(2), cxm(4), cxm(3), rw, rb)
            tok = jnp.concatenate([tok, tok_c], axis=0)
            logits = jnp.concatenate([logits, logits_c], axis=0)

        n_tok = tok.shape[0]
        gates, tok_sorted, posn, block_expert, n_active = _routing(logits, bm)
        w1l = w1[layer]
        w1g_b = w1l[:, :, 0::2].astype(BF16)
        w1l_b = w1l[:, :, 1::2].astype(BF16)
        b1g = b1[layer][:, None, 0::2]
        b1l = b1[layer][:, None, 1::2]
        w2_b = w2[layer].astype(BF16)
        b2l = b2[layer][:, None, :]
        xg = jnp.take(tok, tok_sorted, axis=0)
        yg = _expert_ffn(block_expert, n_active, xg, w1g_b, w1l_b, b1g, b1l, w2_b, b2l, bm)
        f = jnp.sum(jnp.take(yg, posn, axis=0).reshape(n_tok, TOP_K, D).astype(F32) * gates[:, :, None], axis=1)

        xf = _ffn_residual(x1, f[:B * S], S, ng[3:4], lat(5))
        if need_ctx:
            xc = _ffn_residual(xc1, f[B * S:], C, ng[3:4], cxm(5))
    return xf.reshape(B, S, D)
```

```python
import functools
import math

import jax
import jax.numpy as jnp
from jax import lax
from jax.experimental import pallas as pl
from jax.experimental.pallas import tpu as pltpu

F32 = jnp.float32
BF16 = jnp.bfloat16

GRID_W = 64
HEADS = 6
HDIM = 64
REC_W = HEADS * HDIM
DA_HEADS = 4
DA_HEAD_DIM = 32
DA_MAPS = 2 * DA_HEADS
DA_W = DA_MAPS * DA_HEAD_DIM
DA_VDIM = 2 * DA_HEAD_DIM
DA_X = DA_HEADS * 128
_Q_SCALE = DA_HEAD_DIM ** -0.5 * math.log2(math.e)
ROPE_BASE = 10000.0
N_EXPERTS = 32
TOP_K = 4
SWIGLU_ALPHA = 1.702
SWIGLU_LIMIT = 7.0
NORM_EPS = 1e-6

SUB = 16
NEG_BIG = -1e30
VMEM_LIMIT = 56 * 1024 * 1024

_C_HQ, _C_HFF, _C_HFB, _C_HI, _C_HG = 0, 384, 768, 1152, 1536
_C_RQ, _C_RK, _C_RV, _C_RG = 1920, 2304, 2688, 3072
_C_DQ, _C_DK, _C_DV = 3456, 3712, 3968
_C_RQR, _C_RKR, _C_DQR, _C_DKR = 4224, 4608, 4992, 5248
_W_ALL = 5504


def _dot(a, b):
    return jnp.dot(a, b, preferred_element_type=F32)


def _dot_nt(a, b):
    return lax.dot_general(a, b, (((1,), (1,)), ((), ())), preferred_element_type=F32)


def _dot_tn(a, b):
    return lax.dot_general(a, b, (((0,), (0,)), ((), ())), preferred_element_type=F32)


def _sigmoid(x):
    return 1.0 / (1.0 + jnp.exp(-x))


def _rms(x, g):
    ms = jnp.mean(x * x, axis=-1, keepdims=True)
    return x * lax.rsqrt(ms + NORM_EPS) * g


def _params(*sem):
    return pltpu.CompilerParams(dimension_semantics=sem, vmem_limit_bytes=VMEM_LIMIT)


def _mod_kernel(c_ref, w_ref, b_ref, o_ref):
    c = c_ref[...]
    o_ref[0] = _dot(c * _sigmoid(c), w_ref[0]) + b_ref[0]


def _modulation(cvec, mod_w, mod_b):
    depth, d, n = mod_w.shape
    tn = 1536
    return pl.pallas_call(
        _mod_kernel,
        grid=(depth, n // tn),
        in_specs=[
            pl.BlockSpec((8, d), lambda l, j: (0, 0)),
            pl.BlockSpec((1, d, tn), lambda l, j: (l, 0, j)),
            pl.BlockSpec((1, 1, tn), lambda l, j: (l, 0, j)),
        ],
        out_specs=pl.BlockSpec((1, 8, tn), lambda l, j: (l, 0, j)),
        out_shape=jax.ShapeDtypeStruct((depth, 8, n), F32),
        compiler_params=_params("arbitrary", "arbitrary"),
        name="modulation",
    )(cvec, mod_w, mod_b.reshape(depth, 1, n))


def _inproj_kernel(x_ref, g_ref, sc_ref, sh_ref, w_ref, lbc_ref, cosr_ref, sinr_ref, cosd_ref, sind_ref,
                   hq_ref, hv_ref, hg_ref, kf_ref, lff_ref, kb_ref, lfb_ref,
                   rq_ref, rk_ref, rv_ref, rg_ref, dq_ref, dk_ref, dvx_ref):
    x = x_ref[...]
    h = _rms(x, g_ref[...]) * (1.0 + sc_ref[0]) + sh_ref[0]
    hb = h.astype(BF16)

    def proj(c0, n):
        return _dot(hb, w_ref[:, c0:c0 + n])

    hq_ref[...] = proj(_C_HQ, REC_W).astype(BF16)
    hv_ref[...] = proj(_C_HI, REC_W).astype(BF16)
    hg_ref[...] = proj(_C_HG, REC_W).astype(BF16)

    log_lb = lbc_ref[0:1, :]
    log_1m = lbc_ref[1:2, :]
    one_m = lbc_ref[2:3, :]

    def gates(z):
        log_sig = jnp.minimum(z, 0.0) - jnp.log1p(jnp.exp(-jnp.abs(z)))
        t = log_1m + log_sig
        m = jnp.maximum(log_lb, t)
        logf = m + jnp.log1p(jnp.exp(-jnp.abs(log_lb - t)))
        return one_m / (1.0 + jnp.exp(z)), logf

    k, lf = gates(proj(_C_HFF, REC_W))
    kf_ref[...] = k.astype(BF16)
    lff_ref[...] = lf
    k, lf = gates(proj(_C_HFB, REC_W))
    kb_ref[...] = k.astype(BF16)
    lfb_ref[...] = lf

    cosr = cosr_ref[...]
    sinr = sinr_ref[...]
    rq_ref[...] = (proj(_C_RQ, REC_W) * cosr + proj(_C_RQR, REC_W) * sinr).astype(BF16)
    rk_ref[...] = ((proj(_C_RK, REC_W) * cosr + proj(_C_RKR, REC_W) * sinr) * (HDIM ** -0.5)).astype(BF16)
    rv_ref[...] = proj(_C_RV, REC_W).astype(BF16)
    rg_ref[...] = proj(_C_RG, REC_W).astype(BF16)

    cosd = cosd_ref[...]
    sind = sind_ref[...]
    dq = ((proj(_C_DQ, DA_W) * cosd + proj(_C_DQR, DA_W) * sind) * _Q_SCALE).astype(BF16)
    dk = (proj(_C_DK, DA_W) * cosd + proj(_C_DKR, DA_W) * sind).astype(BF16)
    dv = proj(_C_DV, DA_W).astype(BF16)
    tm = dv.shape[0]
    lane = lax.broadcasted_iota(jnp.int32, (tm, 64), 1)
    ones_col = jnp.where(lane == 0, 1.0, 0.0).astype(BF16)
    zero_col = jnp.zeros((tm, 64), BF16)
    for h_ in range(DA_HEADS):
        lo, mid, hi_ = 128 * h_, 128 * h_ + 64, 128 * h_ + 128
        dq_ref[:, lo:mid] = dq[:, 64 * h_:64 * h_ + 64]
        dq_ref[:, mid:hi_] = zero_col
        dk_ref[:, lo:mid] = dk[:, 64 * h_:64 * h_ + 64]
        dk_ref[:, mid:hi_] = ones_col
        dvx_ref[:, lo:mid] = dv[:, 64 * h_:64 * h_ + 64]
        dvx_ref[:, mid:hi_] = ones_col


def _inproj(xf, seq, gnorm, sc, sh, w_all, lbc, cosr, sinr, cosd, sind):
    rows, d = xf.shape
    tm = min(256, seq)
    nb = seq // tm
    row_spec = lambda w: pl.BlockSpec((tm, w), lambda i: (i, 0))
    tab_spec = lambda w: pl.BlockSpec((tm, w), lambda i: (i % nb, 0))
    mod_spec = pl.BlockSpec((1, 1, d), lambda i: (i // nb, 0, 0))
    widths = [REC_W] * 11 + [DA_X, DA_X, DA_X]
    dtypes = [BF16, BF16, BF16, BF16, F32, BF16, F32, BF16, BF16, BF16, BF16, BF16, BF16, BF16]
    return pl.pallas_call(
        _inproj_kernel,
        grid=(rows // tm,),
        in_specs=[
            row_spec(d),
            pl.BlockSpec((1, d), lambda i: (0, 0)),
            mod_spec, mod_spec,
            pl.BlockSpec((d, _W_ALL), lambda i: (0, 0)),
            pl.BlockSpec((8, REC_W), lambda i: (0, 0)),
            tab_spec(REC_W), tab_spec(REC_W), tab_spec(DA_W), tab_spec(DA_W),
        ],
        out_specs=[row_spec(w) for w in widths],
        out_shape=[jax.ShapeDtypeStruct((rows, w), dt) for w, dt in zip(widths, dtypes)],
        compiler_params=_params("arbitrary"),
        name="inproj",
    )(xf, gnorm, sc, sh, w_all, lbc, cosr, sinr, cosd, sind)


def _hgrn_dir(q, k, v, lf, seg, st_ref, inter_ref, reverse):
    tt = q.shape[0]
    nblk = tt // SUB
    row = lax.broadcasted_iota(jnp.int32, (tt, tt), 0)
    col = lax.broadcasted_iota(jnp.int32, (tt, tt), 1)
    same = (row // SUB) == (col // SUB)
    tri = (col >= row) if reverse else (col <= row)
    l_all = jnp.where(same, 1.0, 0.0)
    l_cum = jnp.where(tri, l_all, 0.0).astype(BF16)
    l_all = l_all.astype(BF16)
    hi = lf.astype(BF16)
    r1 = lf - hi.astype(F32)
    mid = r1.astype(BF16)
    lo = (r1 - mid.astype(F32)).astype(BF16)
    b = _dot(l_cum, hi) + _dot(l_cum, mid) + _dot(l_cum, lo)
    btot = _dot(l_all, hi) + _dot(l_all, mid) + _dot(l_all, lo)

    p = lax.broadcasted_iota(jnp.int32, (tt, 1), 0) % SUB
    acc = _dot((q * k).astype(BF16), seg) * v
    for d in range(1, SUB):
        sh = (tt - d) if reverse else d
        ks = pltpu.roll(k, sh, 0)
        bs = pltpu.roll(b, sh, 0)
        vs = pltpu.roll(v, sh, 0)
        mask = (p <= SUB - 1 - d) if reverse else (p >= d)
        e = jnp.exp(jnp.where(mask, b - bs, NEG_BIG))
        acc = acc + _dot((q * ks * e).astype(BF16), seg) * vs

    qe = (q * jnp.exp(b)).astype(BF16)
    kd = (k * jnp.exp(btot - b)).astype(BF16)
    dec = jnp.exp(btot)
    vb = v.astype(BF16)
    order = range(nblk - 1, -1, -1) if reverse else range(nblk)
    for h in range(HEADS):
        c0 = h * HDIM
        st = st_ref[h]
        for j in order:
            r0 = j * SUB
            inter_ref[r0:r0 + SUB, c0:c0 + HDIM] = _dot_nt(qe[r0:r0 + SUB, c0:c0 + HDIM], st.astype(BF16))
            dst = _dot_tn(vb[r0:r0 + SUB, c0:c0 + HDIM], kd[r0:r0 + SUB, c0:c0 + HDIM])
            st = st * dec[r0:r0 + 1, c0:c0 + HDIM] + dst
        st_ref[h] = st
    return acc + inter_ref[...]


def _hgrn_kernel(qf_ref, vf_ref, kf_ref, lff_ref, qb_ref, vb_ref, kb_ref, lfb_ref, s0_ref, seg_ref,
                 of_ref, ob_ref, sout_ref, st_ref, inter_ref):
    i = pl.program_id(1)

    @pl.when(i == 0)
    def _():
        st_ref[...] = s0_ref[0]

    seg = seg_ref[...]
    of_ref[...] = _hgrn_dir(qf_ref[...].astype(F32), kf_ref[...].astype(F32), vf_ref[...].astype(F32),
                            lff_ref[...], seg, st_ref.at[0], inter_ref, False).astype(BF16)
    ob_ref[...] = _hgrn_dir(qb_ref[...].astype(F32), kb_ref[...].astype(F32), vb_ref[...].astype(F32),
                            lfb_ref[...], seg, st_ref.at[1], inter_ref, True).astype(BF16)

    @pl.when(i == pl.num_programs(1) - 1)
    def _():
        sout_ref[0] = st_ref[...]


def _hgrn_scan(hq, hv, kf, lff, kb, lfb, s0, seg, batch, seq):
    tt = min(256, seq)
    n = seq // tt
    fwd = pl.BlockSpec((tt, REC_W), lambda b, i: (b * n + i, 0))
    bwd = pl.BlockSpec((tt, REC_W), lambda b, i: (b * n + n - 1 - i, 0))
    st_spec = pl.BlockSpec((1, 2, HEADS, HDIM, HDIM), lambda b, i: (b, 0, 0, 0, 0))
    rows = batch * seq
    return pl.pallas_call(
        _hgrn_kernel,
        grid=(batch, n),
        in_specs=[fwd, fwd, fwd, fwd, bwd, bwd, bwd, bwd, st_spec,
                  pl.BlockSpec((REC_W, REC_W), lambda b, i: (0, 0))],
        out_specs=[fwd, bwd, st_spec],
        out_shape=[jax.ShapeDtypeStruct((rows, REC_W), BF16), jax.ShapeDtypeStruct((rows, REC_W), BF16),
                   jax.ShapeDtypeStruct(s0.shape, F32)],
        scratch_shapes=[pltpu.VMEM((2, HEADS, HDIM, HDIM), F32), pltpu.VMEM((tt, REC_W), F32)],
        compiler_params=_params("arbitrary", "arbitrary"),
        name="hgrn_scan",
    )(hq, hv, kf, lff, hq, hv, kb, lfb, s0, seg)


def _ret_kernel(lg_ref, qf_ref, kf_ref, vf_ref, qb_ref, kb_ref, vb_ref, s0_ref, lgrow_ref,
                of_ref, ob_ref, sout_ref, st_ref, dmask_ref, tab_ref, o_scr):
    i = pl.program_id(1)
    tt = qf_ref.shape[0]

    @pl.when(i == 0)
    def _():
        st_ref[...] = s0_ref[0]
        row = lax.broadcasted_iota(jnp.int32, (tt, tt), 0)
        col = lax.broadcasted_iota(jnp.int32, (tt, tt), 1)
        dist = (row - col).astype(F32)
        for h in range(HEADS):
            fw = jnp.where(dist >= 0, jnp.exp(dist * lg_ref[0, h]), 0.0)
            bw = jnp.where(dist <= 0, jnp.exp(-dist * lg_ref[1, h]), 0.0)
            dmask_ref[h] = fw + bw
        pos = lax.broadcasted_iota(jnp.int32, (tt, REC_W), 0).astype(F32)
        lgf = lgrow_ref[0:1, :]
        lgb = lgrow_ref[1:2, :]
        tab_ref[0] = jnp.exp((pos + 1.0) * lgf)
        tab_ref[1] = jnp.exp((tt - 1.0 - pos) * lgf)
        tab_ref[2] = jnp.exp((tt - pos) * lgb)
        tab_ref[3] = jnp.exp(pos * lgb)

    tile_f = jnp.exp(tt * lgrow_ref[0:1, :])
    tile_b = jnp.exp(tt * lgrow_ref[1:2, :])

    q = qf_ref[...]
    k = kf_ref[...]
    v = vf_ref[...]
    qe = (q.astype(F32) * tab_ref[0]).astype(BF16)
    ke = (k.astype(F32) * tab_ref[1]).astype(BF16)
    for h in range(HEADS):
        c = slice(h * HDIM, (h + 1) * HDIM)
        s = _dot_nt(q[:, c], k[:, c]) * dmask_ref[h]
        st = st_ref[0, h]
        o_scr[:, c] = _dot(s.astype(BF16), v[:, c]) + _dot(qe[:, c], st.astype(BF16))
        st_ref[0, h] = st * tile_f[:, c] + _dot_tn(ke[:, c], v[:, c])
    of_ref[...] = o_scr[...].astype(BF16)

    q = qb_ref[...]
    k = kb_ref[...]
    v = vb_ref[...]
    qe = (q.astype(F32) * tab_ref[2]).astype(BF16)
    ke = (k.astype(F32) * tab_ref[3]).astype(BF16)
    for h in range(HEADS):
        c = slice(h * HDIM, (h + 1) * HDIM)
        st = st_ref[1, h]
        o_scr[:, c] = _dot(qe[:, c], st.astype(BF16))
        st_ref[1, h] = st * tile_b[:, c] + _dot_tn(ke[:, c], v[:, c])
    ob_ref[...] = o_scr[...].astype(BF16)

    @pl.when(i == pl.num_programs(1) - 1)
    def _():
        sout_ref[0] = st_ref[...]


def _ret_scan(rq, rk, rv, s0, lg, lgrow, batch, seq):
    tt = min(256, seq)
    n = seq // tt
    fwd = pl.BlockSpec((tt, REC_W), lambda b, i: (b * n + i, 0))
    bwd = pl.BlockSpec((tt, REC_W), lambda b, i: (b * n + n - 1 - i, 0))
    st_spec = pl.BlockSpec((1, 2, HEADS, HDIM, HDIM), lambda b, i: (b, 0, 0, 0, 0))
    rows = batch * seq
    return pl.pallas_call(
        _ret_kernel,
        grid=(batch, n),
        in_specs=[pl.BlockSpec(memory_space=pltpu.SMEM), fwd, fwd, fwd, bwd, bwd, bwd, st_spec,
                  pl.BlockSpec((8, REC_W), lambda b, i: (0, 0))],
        out_specs=[fwd, bwd, st_spec],
        out_shape=[jax.ShapeDtypeStruct((rows, REC_W), BF16), jax.ShapeDtypeStruct((rows, REC_W), BF16),
                   jax.ShapeDtypeStruct(s0.shape, F32)],
        scratch_shapes=[pltpu.VMEM((2, HEADS, HDIM, HDIM), F32), pltpu.VMEM((HEADS, tt, tt), F32),
                        pltpu.VMEM((4, tt, REC_W), F32), pltpu.VMEM((tt, REC_W), F32)],
        compiler_params=_params("arbitrary", "arbitrary"),
        name="ret_scan",
    )(lg, rq, rk, rv, rq, rk, rv, s0, lgrow)


def _attn_kernel(lam_ref, q_ref, kc_ref, vc_ref, k_ref, v_ref, gain_ref, o_ref, *, with_latent, kchunk, unroll,
                 out_scale):
    tq = q_ref.shape[0]
    q = q_ref[...]
    lane = lax.broadcasted_iota(jnp.int32, (tq, 128), 1)
    zero = jnp.zeros_like(q)
    qs = jnp.concatenate([jnp.where(lane < DA_HEAD_DIM, q, zero),
                          jnp.where(lane >= DA_HEAD_DIM, q, zero)], axis=0)
    nchunks = (k_ref.shape[0] // kchunk) if with_latent else 0

    def latent(ref, c):
        return ref[pl.ds(pl.multiple_of(c * kchunk, kchunk), kchunk), :]

    def step(kblk, vblk, carry):
        m, acc = carry
        s = _dot_nt(qs, kblk)
        mc = s[:, 0:128]
        for c0 in range(128, s.shape[1], 128):
            mc = jnp.maximum(mc, s[:, c0:c0 + 128])
        m_new = jnp.maximum(m, jnp.max(mc, axis=-1, keepdims=True))
        p = jnp.exp2((s - m_new).astype(BF16))
        return m_new, jnp.exp2(m - m_new) * acc + _dot(p, vblk)

    carry = step(kc_ref[...], vc_ref[...], (jnp.full((2 * tq, 1), NEG_BIG, F32), jnp.zeros((2 * tq, 128), F32)))
    if with_latent:
        carry = lax.fori_loop(0, nchunks, lambda c, cr: step(latent(k_ref, c), latent(v_ref, c), cr), carry,
                              unroll=unroll)
    acc = carry[1]

    a1 = acc[:tq]
    a2 = acc[tq:]
    o = a1 / a1[:, DA_VDIM:DA_VDIM + 1] - lam_ref[0] * (a2 / a2[:, DA_VDIM:DA_VDIM + 1])
    o = jnp.where(lane < DA_VDIM, o, 0.0)
    ms = jnp.sum(o * o, axis=-1, keepdims=True) * (1.0 / DA_VDIM)
    o_ref[...] = (o * lax.rsqrt(ms + NORM_EPS) * gain_ref[...] * out_scale).astype(BF16)


def _diff_attention(lam, q, kc, vc, k, v, gain, batch, seq_q, ctx_len, seq_k, out_scale, q_is_ctx):
    tq = min(256, seq_q)
    nq = seq_q // tq
    with_latent = not q_is_ctx
    if with_latent:
        kchunk = min(2048, seq_k)
    else:
        k, v = kc, vc
        seq_k, kchunk = ctx_len, ctx_len
    unroll = 1
    kern = functools.partial(_attn_kernel, with_latent=with_latent, kchunk=kchunk, unroll=unroll,
                             out_scale=out_scale)
    return pl.pallas_call(
        kern,
        grid=(batch, DA_HEADS, nq),
        in_specs=[
            pl.BlockSpec(memory_space=pltpu.SMEM),
            pl.BlockSpec((tq, 128), lambda b, h, i: (b * nq + i, h)),
            pl.BlockSpec((ctx_len, 128), lambda b, h, i: (b, h)),
            pl.BlockSpec((ctx_len, 128), lambda b, h, i: (b, h)),
            pl.BlockSpec((seq_k, 128), lambda b, h, i: (b, h)),
            pl.BlockSpec((seq_k, 128), lambda b, h, i: (b, h)),
            pl.BlockSpec((1, 128), lambda b, h, i: (0, h)),
        ],
        out_specs=pl.BlockSpec((tq, 128), lambda b, h, i: (b * nq + i, h)),
        out_shape=jax.ShapeDtypeStruct((batch * seq_q, DA_X), BF16),
        compiler_params=_params("arbitrary", "arbitrary", "arbitrary"),
        name="diff_attention",
    )(lam, q, kc, vc, k, v, gain)


def _outproj_kernel(x_ref, hof_ref, hob_ref, hg_ref, rof_ref, rob_ref, rg_ref, da_ref, seg_ref, w_ref,
                    hn_ref, rn_ref, n1_ref, n2_ref, g1_ref, sc2_ref, sh2_ref, rw_ref, rb_ref,
                    x1_ref, tok_ref, logit_ref):
    seg = seg_ref[...]

    def gated_head_norm(o, gain, gate):
        ms = _dot((o * o).astype(BF16), seg) * (1.0 / HDIM)
        return (o * lax.rsqrt(ms + NORM_EPS) * gain * (gate * _sigmoid(gate))).astype(BF16)

    a = gated_head_norm(hof_ref[...].astype(F32) + hob_ref[...].astype(F32), hn_ref[...], hg_ref[...].astype(F32))
    b = gated_head_norm(rof_ref[...].astype(F32) + rob_ref[...].astype(F32), rn_ref[...], rg_ref[...].astype(F32))
    y = (_dot(a, w_ref[0:REC_W, :]) + _dot(b, w_ref[REC_W:2 * REC_W, :])
         + _dot(da_ref[...], w_ref[2 * REC_W:2 * REC_W + DA_X, :]))
    x1 = x_ref[...] + g1_ref[0] * _rms(y, n1_ref[...])
    x1_ref[...] = x1
    tok = _rms(x1, n2_ref[...]) * (1.0 + sc2_ref[0]) + sh2_ref[0]
    tok_ref[...] = tok.astype(BF16)
    logit_ref[...] = _dot(tok, rw_ref[...]) + rb_ref[...]


def _outproj(xf, seq, streams, seg, w_out, hn, rn, n1, n2, g1, sc2, sh2, rw, rb):
    rows, d = xf.shape
    tm = min(256, seq)
    nb = seq // tm
    row_spec = lambda w: pl.BlockSpec((tm, w), lambda i: (i, 0))
    full = lambda a: pl.BlockSpec(a.shape, lambda i: (0,) * a.ndim)
    mod_spec = pl.BlockSpec((1, 1, d), lambda i: (i // nb, 0, 0))
    hof, hob, hg, rof, rob, rg, da = streams
    return pl.pallas_call(
        _outproj_kernel,
        grid=(rows // tm,),
        in_specs=[row_spec(d)] + [row_spec(REC_W)] * 6 + [row_spec(DA_X), full(seg), full(w_out),
                  full(hn), full(rn), full(n1), full(n2), mod_spec, mod_spec, mod_spec, full(rw), full(rb)],
        out_specs=[row_spec(d), row_spec(d), row_spec(N_EXPERTS)],
        out_shape=[jax.ShapeDtypeStruct((rows, d), F32), jax.ShapeDtypeStruct((rows, d), BF16),
                   jax.ShapeDtypeStruct((rows, N_EXPERTS), F32)],
        compiler_params=_params("arbitrary"),
        name="outproj",
    )(xf, hof, hob, hg, rof, rob, rg, da, seg, w_out, hn, rn, n1, n2, g1, sc2, sh2, rw, rb)


def _w1_split_kernel(w_ref, pe_ref, po_ref, g_ref, l_ref):
    n = w_ref.shape[2]
    for c in range(n // 256):
        wb = w_ref[0, :, 256 * c:256 * (c + 1)].astype(BF16)
        g_ref[0, :, 128 * c:128 * (c + 1)] = _dot(wb, pe_ref[...]).astype(BF16)
        l_ref[0, :, 128 * c:128 * (c + 1)] = _dot(wb, po_ref[...]).astype(BF16)


def _w1_split(w1):
    e, d, n = w1.shape
    tk = 512
    src = jnp.arange(256)[:, None]
    dst = jnp.arange(128)[None, :]
    pe = (src == 2 * dst).astype(BF16)
    po = (src == 2 * dst + 1).astype(BF16)
    sel = pl.BlockSpec((256, 128), lambda i, j: (0, 0))
    out = pl.BlockSpec((1, tk, n // 2), lambda i, j: (i, j, 0))
    return pl.pallas_call(
        _w1_split_kernel,
        grid=(e, d // tk),
        in_specs=[pl.BlockSpec((1, tk, n), lambda i, j: (i, j, 0)), sel, sel],
        out_specs=[out, out],
        out_shape=[jax.ShapeDtypeStruct((e, d, n // 2), BF16)] * 2,
        compiler_params=_params("arbitrary", "arbitrary"),
        name="w1_split",
    )(w1, pe, po)


def _ffn_kernel(be_ref, na_ref, x_ref, w1g_ref, w1l_ref, b1g_ref, b1l_ref, w2_ref, b2_ref, y_ref):
    active = pl.program_id(0) < na_ref[0]

    @pl.when(jnp.logical_not(active))
    def _():
        y_ref[...] = jnp.zeros(y_ref.shape, y_ref.dtype)

    @pl.when(active)
    def _():
        x = x_ref[...]
        glu = jnp.minimum(_dot(x, w1g_ref[0]) + b1g_ref[0], SWIGLU_LIMIT)
        lin = jnp.clip(_dot(x, w1l_ref[0]) + b1l_ref[0], -SWIGLU_LIMIT, SWIGLU_LIMIT)
        act = glu * _sigmoid(SWIGLU_ALPHA * glu) * (lin + 1.0)
        y_ref[...] = (_dot(act.astype(BF16), w2_ref[0]) + b2_ref[0]).astype(y_ref.dtype)


def _expert_ffn(block_expert, n_active, xg, w1g, w1l, b1g, b1l, w2, b2, bm):
    p, d = xg.shape
    f = w1g.shape[-1]
    nblocks = p // bm
    wspec = lambda s: pl.BlockSpec((1,) + s, lambda i, be, na: (be[i], 0, 0))
    return pl.pallas_call(
        _ffn_kernel,
        grid_spec=pltpu.PrefetchScalarGridSpec(
            num_scalar_prefetch=2,
            grid=(nblocks,),
            in_specs=[pl.BlockSpec((bm, d), lambda i, be, na: (i, 0)),
                      wspec((d, f)), wspec((d, f)), wspec((1, f)), wspec((1, f)), wspec((f, d)), wspec((1, d))],
            out_specs=pl.BlockSpec((bm, d), lambda i, be, na: (i, 0)),
        ),
        out_shape=jax.ShapeDtypeStruct((p, d), BF16),
        compiler_params=_params("arbitrary"),
        name="expert_ffn",
    )(block_expert, n_active, xg, w1g, w1l, b1g, b1l, w2, b2)


def _resid_kernel(x_ref, f_ref, n_ref, g_ref, o_ref):
    o_ref[...] = x_ref[...] + g_ref[0] * _rms(f_ref[...].astype(F32), n_ref[...])


def _ffn_residual(xf, f, seq, n3, g2):
    rows, d = xf.shape
    tm = min(512, seq)
    nb = seq // tm
    row_spec = pl.BlockSpec((tm, d), lambda i: (i, 0))
    return pl.pallas_call(
        _resid_kernel,
        grid=(rows // tm,),
        in_specs=[row_spec, row_spec, pl.BlockSpec((1, d), lambda i: (0, 0)),
                  pl.BlockSpec((1, 1, d), lambda i: (i // nb, 0, 0))],
        out_specs=row_spec,
        out_shape=jax.ShapeDtypeStruct((rows, d), F32),
        compiler_params=_params("arbitrary"),
        name="ffn_residual",
    )(xf, f, n3, g2)


def _rope_tables(pos, dim):
    inv = 1.0 / (ROPE_BASE ** (jnp.arange(0, dim, 2, dtype=F32) / dim))
    ang = pos.astype(F32)[:, None] * inv[None, :]
    return jnp.cos(ang), jnp.sin(ang)


def _rot_cols(w, head_dim, halves):
    d, n = w.shape
    g = head_dim // halves
    w4 = w.reshape(d, n // g, 2, g // 2)
    return jnp.concatenate([-w4[:, :, 1], w4[:, :, 0]], axis=-1).reshape(d, n)


def _prep_w_in(w):
    rq, rk = w[:, _C_RQ:_C_RQ + REC_W], w[:, _C_RK:_C_RK + REC_W]
    dq, dk = w[:, _C_DQ:_C_DQ + DA_W], w[:, _C_DK:_C_DK + DA_W]
    return jnp.concatenate([w, _rot_cols(rq, HDIM, 1), _rot_cols(rk, HDIM, 1),
                            _rot_cols(dq, DA_HEAD_DIM, 2), _rot_cols(dk, DA_HEAD_DIM, 2)], axis=1).astype(BF16)


def _routing(logits, bm):
    n = logits.shape[0]
    top_val, top_idx = lax.top_k(logits, TOP_K)
    gates = jax.nn.softmax(top_val, axis=-1)
    a = n * TOP_K
    flat_e = top_idx.reshape(a).astype(jnp.int32)
    sorted_e, order = lax.sort_key_val(flat_e, jnp.arange(a, dtype=jnp.int32))
    bounds = jnp.searchsorted(sorted_e, jnp.arange(N_EXPERTS + 1, dtype=jnp.int32), side='left').astype(jnp.int32)
    start, counts = bounds[:-1], bounds[1:] - bounds[:-1]
    padded = (counts + bm - 1) // bm * bm
    pend = jnp.cumsum(padded)
    shift = pend - padded - start
    dest = jnp.arange(a, dtype=jnp.int32) + shift[sorted_e]
    _, pos = lax.sort_key_val(order, dest)
    nblocks = -(-(a + N_EXPERTS * (bm - 1)) // bm)
    block_expert = jnp.minimum(jnp.searchsorted(pend, jnp.arange(nblocks, dtype=jnp.int32) * bm, side='right'),
                               N_EXPERTS - 1).astype(jnp.int32)
    row_e = jnp.repeat(block_expert, bm)
    rank = jnp.arange(nblocks * bm, dtype=jnp.int32) - shift[row_e]
    valid = rank < bounds[row_e + 1]
    tok_sorted = jnp.where(valid, order[jnp.clip(rank, 0, a - 1)] // TOP_K, 0)
    n_active = (pend[-1:] // bm).astype(jnp.int32)
    return gates, tok_sorted, pos, block_expert, n_active


def kernel(x, c, ctx, c_ctx, mod_w, mod_b, norm_g, w_in, hgrn_lb, hgrn_norm, ret_decay, ret_norm, da_lambda,
           da_subln, w_out, router_w, router_b, w1, b1, w2, b2):
    B, S, D = x.shape
    C = ctx.shape[1]
    depth = mod_w.shape[0]
    bm = 256

    pos = jnp.arange(S)
    cr, sr = _rope_tables(pos // GRID_W, DA_HEAD_DIM // 2)
    cc, sc_ = _rope_tables(pos % GRID_W, DA_HEAD_DIM // 2)
    cs, ss = _rope_tables(pos, HDIM)
    cosr = jnp.tile(jnp.concatenate([cs, cs], -1), (1, HEADS))
    sinr = jnp.tile(jnp.concatenate([ss, ss], -1), (1, HEADS))
    cosd = jnp.tile(jnp.concatenate([cr, cr, cc, cc], -1), (1, DA_MAPS))
    sind = jnp.tile(jnp.concatenate([sr, sr, sc_, sc_], -1), (1, DA_MAPS))
    ones_r, zeros_r = jnp.ones((C, REC_W), F32), jnp.zeros((C, REC_W), F32)
    ones_d, zeros_d = jnp.ones((C, DA_W), F32), jnp.zeros((C, DA_W), F32)

    lb_cum = jnp.cumsum(jax.nn.softmax(hgrn_lb.astype(F32), axis=0), axis=0)
    lower = lb_cum - lb_cum[0:1]

    cvec = jnp.zeros((8, D), F32).at[:B].set(c).at[B].set(c_ctx)
    mods = _modulation(cvec, mod_w, mod_b)

    head_id = jnp.arange(REC_W) // HDIM
    seg = (head_id[:, None] == head_id[None, :]).astype(BF16)

    xf = x.reshape(B * S, D)
    xc = ctx.reshape(B * C, D)
    zero_state = jnp.zeros((B, 2, HEADS, HDIM, HDIM), F32)

    for layer in range(depth):
        need_ctx = layer < depth - 1
        lam_init = 0.8 - 0.6 * math.exp(-0.3 * layer)
        m6 = mods[layer].reshape(8, 6, D)
        lat = lambda k: m6[:B, k][:, None, :]
        cxm = lambda k: jnp.broadcast_to(m6[B, k][None, None, :], (B, 1, D))
        ng = norm_g[layer]
        lb = lower[layer]
        lbc = jnp.zeros((8, REC_W), F32).at[0].set(jnp.log(lb)).at[1].set(jnp.log1p(-lb)).at[2].set(1.0 - lb)
        w_all = _prep_w_in(w_in[layer])
        log_gamma = jnp.log1p(-jnp.exp2(-ret_decay[layer].astype(F32)))
        lgrow = jnp.zeros((8, REC_W), F32).at[:2].set(jnp.repeat(log_gamma, HDIM, axis=1))
        lamv = da_lambda[layer].astype(F32)
        lam = (jnp.exp(jnp.sum(lamv[0] * lamv[1])) - jnp.exp(jnp.sum(lamv[2] * lamv[3])) + lam_init).reshape(1)
        hn = jnp.tile(hgrn_norm[layer], HEADS)[None, :]
        rn = jnp.tile(ret_norm[layer], HEADS)[None, :]
        dn = jnp.tile(jnp.concatenate([da_subln[layer], jnp.zeros((128 - DA_VDIM,), F32)]), DA_HEADS)[None, :]
        wo = w_out[layer]
        wo_da = jnp.pad(wo[2 * REC_W:].reshape(DA_HEADS, DA_VDIM, D), ((0, 0), (0, 128 - DA_VDIM), (0, 0)))
        w_out_b = jnp.concatenate([wo[:2 * REC_W], wo_da.reshape(DA_X, D)], axis=0).astype(BF16)
        rw = router_w[layer]
        rb = router_b[layer][None, :]

        (hq, hv, hg, kf, lff, kb, lfb, rq, rk, rv, rg, dq, dk, dvx) = _inproj(
            xf, S, ng[0:1], lat(1), lat(0), w_all, lbc, cosr, sinr, cosd, sind)
        (hq_c, hv_c, hg_c, kf_c, lff_c, kb_c, lfb_c, rq_c, rk_c, rv_c, rg_c, dq_c, dk_c, dvx_c) = _inproj(
            xc, C, ng[0:1], cxm(1), cxm(0), w_all, lbc, ones_r, zeros_r, ones_d, zeros_d)

        hof_c, hob_c, hs = _hgrn_scan(hq_c, hv_c, kf_c, lff_c, kb_c, lfb_c, zero_state, seg, B, C)
        hof, hob, _ = _hgrn_scan(hq, hv, kf, lff, kb, lfb, hs, seg, B, S)
        rof_c, rob_c, rs = _ret_scan(rq_c, rk_c, rv_c, zero_state, log_gamma, lgrow, B, C)
        rof, rob, _ = _ret_scan(rq, rk, rv, rs, log_gamma, lgrow, B, S)
        out_scale = 1.0 - lam_init
        da = _diff_attention(lam, dq, dk_c, dvx_c, dk, dvx, dn, B, S, C, S, out_scale, False)

        x1, tok, logits = _outproj(xf, S, (hof, hob, hg, rof, rob, rg, da), seg, w_out_b, hn, rn,
                                   ng[1:2], ng[2:3], lat(2), lat(4), lat(3), rw, rb)
        if need_ctx:
            da_c = _diff_attention(lam, dq_c, dk_c, dvx_c, None, None, dn, B, C, C, C, out_scale, True)
            xc1, tok_c, logits_c = _outproj(xc, C, (hof_c, hob_c, hg_c, rof_c, rob_c, rg_c, da_c), seg, w_out_b,
                                            hn, rn, ng[1:2], ng[2:3], cxm(2), cxm(4), cxm(3), rw, rb)
            tok = jnp.concatenate([tok, tok_c], axis=0)
            logits = jnp.concatenate([logits, logits_c], axis=0)

        n_tok = tok.shape[0]
        gates, tok_sorted, posn, block_expert, n_active = _routing(logits, bm)
        w1g_b, w1l_b = _w1_split(w1[layer])
        b1g = b1[layer][:, None, 0::2]
        b1l = b1[layer][:, None, 1::2]
        w2_b = w2[layer].astype(BF16)
        b2l = b2[layer][:, None, :]
        xg = jnp.take(tok, tok_sorted, axis=0)
        yg = _expert_ffn(block_expert, n_active, xg, w1g_b, w1l_b, b1g, b1l, w2_b, b2l, bm)
        f = jnp.sum(jnp.take(yg, posn, axis=0).reshape(n_tok, TOP_K, D).astype(F32) * gates[:, :, None], axis=1)

        xf = _ffn_residual(x1, f[:B * S], S, ng[3:4], lat(5))
        if need_ctx:
            xc = _ffn_residual(xc1, f[B * S:], C, ng[3:4], cxm(5))
    return xf.reshape(B, S, D)
```

```python
import functools
import math

import jax
import jax.numpy as jnp
from jax import lax
from jax.experimental import pallas as pl
from jax.experimental.pallas import tpu as pltpu

F32 = jnp.float32
BF16 = jnp.bfloat16

GRID_W = 64
HEADS = 6
HDIM = 64
REC_W = HEADS * HDIM
DA_HEADS = 4
DA_HEAD_DIM = 32
DA_MAPS = 2 * DA_HEADS
DA_W = DA_MAPS * DA_HEAD_DIM
DA_VDIM = 2 * DA_HEAD_DIM
DA_X = DA_HEADS * 128
_Q_SCALE = DA_HEAD_DIM ** -0.5 * math.log2(math.e)
ROPE_BASE = 10000.0
N_EXPERTS = 32
TOP_K = 4
SWIGLU_ALPHA = 1.702
SWIGLU_LIMIT = 7.0
NORM_EPS = 1e-6

SUB = 16
NEG_BIG = -1e30
VMEM_LIMIT = 56 * 1024 * 1024

_C_HQ, _C_HFF, _C_HFB, _C_HI, _C_HG = 0, 384, 768, 1152, 1536
_C_RQ, _C_RK, _C_RV, _C_RG = 1920, 2304, 2688, 3072
_C_DQ, _C_DK, _C_DV = 3456, 3712, 3968
_C_RQR, _C_RKR, _C_DQR, _C_DKR = 4224, 4608, 4992, 5248
_W_ALL = 5504


def _dot(a, b):
    return jnp.dot(a, b, preferred_element_type=F32)


def _dot_nt(a, b):
    return lax.dot_general(a, b, (((1,), (1,)), ((), ())), preferred_element_type=F32)


def _dot_tn(a, b):
    return lax.dot_general(a, b, (((0,), (0,)), ((), ())), preferred_element_type=F32)


def _sigmoid(x):
    return 1.0 / (1.0 + jnp.exp(-x))


def _rms(x, g):
    ms = jnp.mean(x * x, axis=-1, keepdims=True)
    return x * lax.rsqrt(ms + NORM_EPS) * g


def _params(*sem):
    return pltpu.CompilerParams(dimension_semantics=sem, vmem_limit_bytes=VMEM_LIMIT)


def _mod_kernel(c_ref, w_ref, b_ref, o_ref):
    c = c_ref[...]
    o_ref[0] = _dot(c * _sigmoid(c), w_ref[0]) + b_ref[0]


def _modulation(cvec, mod_w, mod_b):
    depth, d, n = mod_w.shape
    tn = 1536
    return pl.pallas_call(
        _mod_kernel,
        grid=(depth, n // tn),
        in_specs=[
            pl.BlockSpec((8, d), lambda l, j: (0, 0)),
            pl.BlockSpec((1, d, tn), lambda l, j: (l, 0, j)),
            pl.BlockSpec((1, 1, tn), lambda l, j: (l, 0, j)),
        ],
        out_specs=pl.BlockSpec((1, 8, tn), lambda l, j: (l, 0, j)),
        out_shape=jax.ShapeDtypeStruct((depth, 8, n), F32),
        compiler_params=_params("arbitrary", "arbitrary"),
        name="modulation",
    )(cvec, mod_w, mod_b.reshape(depth, 1, n))


def _inproj_kernel(x_ref, g_ref, sc_ref, sh_ref, w_ref, lbc_ref, cosr_ref, sinr_ref, cosd_ref, sind_ref,
                   hq_ref, hv_ref, hg_ref, kf_ref, lff_ref, kb_ref, lfb_ref,
                   rq_ref, rk_ref, rv_ref, rg_ref, dq_ref, dk_ref, dvx_ref):
    x = x_ref[...]
    h = _rms(x, g_ref[...]) * (1.0 + sc_ref[0]) + sh_ref[0]
    hb = h.astype(BF16)

    def proj(c0, n):
        return _dot(hb, w_ref[:, c0:c0 + n])

    hq_ref[...] = proj(_C_HQ, REC_W).astype(BF16)
    hv_ref[...] = proj(_C_HI, REC_W).astype(BF16)
    hg_ref[...] = proj(_C_HG, REC_W).astype(BF16)

    log_lb = lbc_ref[0:1, :]
    log_1m = lbc_ref[1:2, :]
    one_m = lbc_ref[2:3, :]

    def gates(z):
        log_sig = jnp.minimum(z, 0.0) - jnp.log1p(jnp.exp(-jnp.abs(z)))
        t = log_1m + log_sig
        m = jnp.maximum(log_lb, t)
        logf = m + jnp.log1p(jnp.exp(-jnp.abs(log_lb - t)))
        return one_m / (1.0 + jnp.exp(z)), logf

    k, lf = gates(proj(_C_HFF, REC_W))
    kf_ref[...] = k.astype(BF16)
    lff_ref[...] = lf
    k, lf = gates(proj(_C_HFB, REC_W))
    kb_ref[...] = k.astype(BF16)
    lfb_ref[...] = lf

    cosr = cosr_ref[...]
    sinr = sinr_ref[...]
    rq_ref[...] = (proj(_C_RQ, REC_W) * cosr + proj(_C_RQR, REC_W) * sinr).astype(BF16)
    rk_ref[...] = ((proj(_C_RK, REC_W) * cosr + proj(_C_RKR, REC_W) * sinr) * (HDIM ** -0.5)).astype(BF16)
    rv_ref[...] = proj(_C_RV, REC_W).astype(BF16)
    rg_ref[...] = proj(_C_RG, REC_W).astype(BF16)

    cosd = cosd_ref[...]
    sind = sind_ref[...]
    dq = ((proj(_C_DQ, DA_W) * cosd + proj(_C_DQR, DA_W) * sind) * _Q_SCALE).astype(BF16)
    dk = (proj(_C_DK, DA_W) * cosd + proj(_C_DKR, DA_W) * sind).astype(BF16)
    dv = proj(_C_DV, DA_W).astype(BF16)
    tm = dv.shape[0]
    lane = lax.broadcasted_iota(jnp.int32, (tm, 64), 1)
    ones_col = jnp.where(lane == 0, 1.0, 0.0).astype(BF16)
    zero_col = jnp.zeros((tm, 64), BF16)
    for h_ in range(DA_HEADS):
        lo, mid, hi_ = 128 * h_, 128 * h_ + 64, 128 * h_ + 128
        dq_ref[:, lo:mid] = dq[:, 64 * h_:64 * h_ + 64]
        dq_ref[:, mid:hi_] = zero_col
        dk_ref[:, lo:mid] = dk[:, 64 * h_:64 * h_ + 64]
        dk_ref[:, mid:hi_] = ones_col
        dvx_ref[:, lo:mid] = dv[:, 64 * h_:64 * h_ + 64]
        dvx_ref[:, mid:hi_] = ones_col


def _inproj(xf, seq, gnorm, sc, sh, w_all, lbc, cosr, sinr, cosd, sind):
    rows, d = xf.shape
    tm = min(256, seq)
    nb = seq // tm
    row_spec = lambda w: pl.BlockSpec((tm, w), lambda i: (i, 0))
    tab_spec = lambda w: pl.BlockSpec((tm, w), lambda i: (i % nb, 0))
    mod_spec = pl.BlockSpec((1, 1, d), lambda i: (i // nb, 0, 0))
    widths = [REC_W] * 11 + [DA_X, DA_X, DA_X]
    dtypes = [BF16, BF16, BF16, BF16, F32, BF16, F32, BF16, BF16, BF16, BF16, BF16, BF16, BF16]
    return pl.pallas_call(
        _inproj_kernel,
        grid=(rows // tm,),
        in_specs=[
            row_spec(d),
            pl.BlockSpec((1, d), lambda i: (0, 0)),
            mod_spec, mod_spec,
            pl.BlockSpec((d, _W_ALL), lambda i: (0, 0)),
            pl.BlockSpec((8, REC_W), lambda i: (0, 0)),
            tab_spec(REC_W), tab_spec(REC_W), tab_spec(DA_W), tab_spec(DA_W),
        ],
        out_specs=[row_spec(w) for w in widths],
        out_shape=[jax.ShapeDtypeStruct((rows, w), dt) for w, dt in zip(widths, dtypes)],
        compiler_params=_params("arbitrary"),
        name="inproj",
    )(xf, gnorm, sc, sh, w_all, lbc, cosr, sinr, cosd, sind)


def _hgrn_dir(q, k, v, lf, seg, st_ref, inter_ref, intra_ref, reverse):
    tt = q.shape[0]
    nblk = tt // SUB
    row = lax.broadcasted_iota(jnp.int32, (tt, tt), 0)
    col = lax.broadcasted_iota(jnp.int32, (tt, tt), 1)
    same = (row // SUB) == (col // SUB)
    tri = (col >= row) if reverse else (col <= row)
    l_all = jnp.where(same, 1.0, 0.0)
    l_cum = jnp.where(tri, l_all, 0.0).astype(BF16)
    l_all = l_all.astype(BF16)
    hi = lf.astype(BF16)
    r1 = lf - hi.astype(F32)
    mid = r1.astype(BF16)
    lo = (r1 - mid.astype(F32)).astype(BF16)
    b = _dot(l_cum, hi) + _dot(l_cum, mid) + _dot(l_cum, lo)
    btot = _dot(l_all, hi) + _dot(l_all, mid) + _dot(l_all, lo)

    half = lax.broadcasted_iota(jnp.int32, (nblk, 8, 128), 1)
    seg2 = seg[0:128, 0:128]
    for g in range(REC_W // 128):
        c = slice(128 * g, 128 * (g + 1))

        def halves(a):
            a4 = a[:, c].reshape(nblk, 2, 8, 128)
            return (a4[:, 1], a4[:, 0]) if reverse else (a4[:, 0], a4[:, 1])

        (q0, q1), (k0, k1), (v0, v1), (b0, b1) = halves(q), halves(k), halves(v), halves(b)

        def pairs(qa, ba, kb_, bb_, vb_, r, masked):
            sh = ((8 - r) % 8) if reverse else r
            if sh:
                kb_, bb_, vb_ = pltpu.roll(kb_, sh, 1), pltpu.roll(bb_, sh, 1), pltpu.roll(vb_, sh, 1)
            diff = ba - bb_
            if masked and r:
                diff = jnp.where((half <= 7 - r) if reverse else (half >= r), diff, NEG_BIG)
            term = (qa * kb_ * jnp.exp(diff)).reshape(nblk * 8, 128).astype(BF16)
            return _dot(term, seg2).reshape(nblk, 8, 128) * vb_

        o0 = pairs(q0, b0, k0, b0, v0, 0, True)
        o1 = pairs(q1, b1, k1, b1, v1, 0, True) + pairs(q1, b1, k0, b0, v0, 0, False)
        for r in range(1, 8):
            o0 = o0 + pairs(q0, b0, k0, b0, v0, r, True)
            o1 = o1 + pairs(q1, b1, k1, b1, v1, r, True) + pairs(q1, b1, k0, b0, v0, r, False)
        first, second = (o1, o0) if reverse else (o0, o1)
        intra_ref[:, 0, :, c] = first
        intra_ref[:, 1, :, c] = second

    qe = (q * jnp.exp(b)).astype(BF16)
    kd = (k * jnp.exp(btot - b)).astype(BF16)
    dec = jnp.exp(btot)
    v_t = v.T
    tok_blk = lax.broadcasted_iota(jnp.int32, (128, tt), 1) // SUB
    hr = lax.broadcasted_iota(jnp.int32, (128, 128), 0) // HDIM
    hc = lax.broadcasted_iota(jnp.int32, (128, 128), 1) // HDIM
    same_head = hr == hc
    order = range(nblk - 1, -1, -1) if reverse else range(nblk)
    for g in range(REC_W // 128):
        c = slice(128 * g, 128 * (g + 1))
        v_tg = v_t[c, :]
        v_stack = jnp.concatenate([jnp.where(tok_blk == j, v_tg, 0.0).astype(BF16) for j in range(nblk)], axis=0)
        upd = _dot(v_stack, kd[:, c])
        st = st_ref[g]
        for j in order:
            r0 = j * SUB
            inter_ref[r0:r0 + SUB, c] = _dot_nt(qe[r0:r0 + SUB, c], st.astype(BF16))
            st = st * dec[r0:r0 + 1, c] + jnp.where(same_head, upd[128 * j:128 * (j + 1)], 0.0)
        st_ref[g] = st
    return intra_ref[...].reshape(tt, REC_W) + inter_ref[...]


def _hgrn_kernel(qf_ref, vf_ref, kf_ref, lff_ref, qb_ref, vb_ref, kb_ref, lfb_ref, s0_ref, seg_ref,
                 of_ref, ob_ref, sout_ref, st_ref, inter_ref, intra_ref):
    i = pl.program_id(1)

    @pl.when(i == 0)
    def _():
        st_ref[...] = s0_ref[0]

    seg = seg_ref[...]
    of_ref[...] = _hgrn_dir(qf_ref[...].astype(F32), kf_ref[...].astype(F32), vf_ref[...].astype(F32),
                            lff_ref[...], seg, st_ref.at[0], inter_ref, intra_ref, False).astype(BF16)
    ob_ref[...] = _hgrn_dir(qb_ref[...].astype(F32), kb_ref[...].astype(F32), vb_ref[...].astype(F32),
                            lfb_ref[...], seg, st_ref.at[1], inter_ref, intra_ref, True).astype(BF16)

    @pl.when(i == pl.num_programs(1) - 1)
    def _():
        sout_ref[0] = st_ref[...]


def _hgrn_scan(hq, hv, kf, lff, kb, lfb, s0, seg, batch, seq):
    tt = min(256, seq)
    n = seq // tt
    fwd = pl.BlockSpec((tt, REC_W), lambda b, i: (b * n + i, 0))
    bwd = pl.BlockSpec((tt, REC_W), lambda b, i: (b * n + n - 1 - i, 0))
    st_spec = pl.BlockSpec((1, 2, REC_W // 128, 128, 128), lambda b, i: (b, 0, 0, 0, 0))
    rows = batch * seq
    return pl.pallas_call(
        _hgrn_kernel,
        grid=(batch, n),
        in_specs=[fwd, fwd, fwd, fwd, bwd, bwd, bwd, bwd, st_spec,
                  pl.BlockSpec((REC_W, REC_W), lambda b, i: (0, 0))],
        out_specs=[fwd, bwd, st_spec],
        out_shape=[jax.ShapeDtypeStruct((rows, REC_W), BF16), jax.ShapeDtypeStruct((rows, REC_W), BF16),
                   jax.ShapeDtypeStruct(s0.shape, F32)],
        scratch_shapes=[pltpu.VMEM((2, REC_W // 128, 128, 128), F32), pltpu.VMEM((tt, REC_W), F32),
                        pltpu.VMEM((tt // SUB, 2, 8, REC_W), F32)],
        compiler_params=_params("arbitrary", "arbitrary"),
        name="hgrn_scan",
    )(hq, hv, kf, lff, hq, hv, kb, lfb, s0, seg)


def _ret_kernel(lg_ref, qf_ref, kf_ref, vf_ref, qb_ref, kb_ref, vb_ref, s0_ref, lgrow_ref,
                of_ref, ob_ref, sout_ref, st_ref, dmask_ref, tab_ref, o_scr):
    i = pl.program_id(1)
    tt = qf_ref.shape[0]

    @pl.when(i == 0)
    def _():
        st_ref[...] = s0_ref[0]
        row = lax.broadcasted_iota(jnp.int32, (tt, tt), 0)
        col = lax.broadcasted_iota(jnp.int32, (tt, tt), 1)
        dist = (row - col).astype(F32)
        for h in range(HEADS):
            fw = jnp.where(dist >= 0, jnp.exp(dist * lg_ref[0, h]), 0.0)
            bw = jnp.where(dist <= 0, jnp.exp(-dist * lg_ref[1, h]), 0.0)
            dmask_ref[h] = fw + bw
        pos = lax.broadcasted_iota(jnp.int32, (tt, REC_W), 0).astype(F32)
        lgf = lgrow_ref[0:1, :]
        lgb = lgrow_ref[1:2, :]
        tab_ref[0] = jnp.exp((pos + 1.0) * lgf)
        tab_ref[1] = jnp.exp((tt - 1.0 - pos) * lgf)
        tab_ref[2] = jnp.exp((tt - pos) * lgb)
        tab_ref[3] = jnp.exp(pos * lgb)

    tile_f = jnp.exp(tt * lgrow_ref[0:1, :])
    tile_b = jnp.exp(tt * lgrow_ref[1:2, :])

    q = qf_ref[...]
    k = kf_ref[...]
    v = vf_ref[...]
    qe = (q.astype(F32) * tab_ref[0]).astype(BF16)
    ke = (k.astype(F32) * tab_ref[1]).astype(BF16)
    for h in range(HEADS):
        c = slice(h * HDIM, (h + 1) * HDIM)
        s = _dot_nt(q[:, c], k[:, c]) * dmask_ref[h]
        st = st_ref[0, h]
        o_scr[:, c] = _dot(s.astype(BF16), v[:, c]) + _dot(qe[:, c], st.astype(BF16))
        st_ref[0, h] = st * tile_f[:, c] + _dot_tn(ke[:, c], v[:, c])
    of_ref[...] = o_scr[...].astype(BF16)

    q = qb_ref[...]
    k = kb_ref[...]
    v = vb_ref[...]
    qe = (q.astype(F32) * tab_ref[2]).astype(BF16)
    ke = (k.astype(F32) * tab_ref[3]).astype(BF16)
    for h in range(HEADS):
        c = slice(h * HDIM, (h + 1) * HDIM)
        st = st_ref[1, h]
        o_scr[:, c] = _dot(qe[:, c], st.astype(BF16))
        st_ref[1, h] = st * tile_b[:, c] + _dot_tn(ke[:, c], v[:, c])
    ob_ref[...] = o_scr[...].astype(BF16)

    @pl.when(i == pl.num_programs(1) - 1)
    def _():
        sout_ref[0] = st_ref[...]


def _ret_scan(rq, rk, rv, s0, lg, lgrow, batch, seq):
    tt = min(256, seq)
    n = seq // tt
    fwd = pl.BlockSpec((tt, REC_W), lambda b, i: (b * n + i, 0))
    bwd = pl.BlockSpec((tt, REC_W), lambda b, i: (b * n + n - 1 - i, 0))
    st_spec = pl.BlockSpec((1, 2, HEADS, HDIM, HDIM), lambda b, i: (b, 0, 0, 0, 0))
    rows = batch * seq
    return pl.pallas_call(
        _ret_kernel,
        grid=(batch, n),
        in_specs=[pl.BlockSpec(memory_space=pltpu.SMEM), fwd, fwd, fwd, bwd, bwd, bwd, st_spec,
                  pl.BlockSpec((8, REC_W), lambda b, i: (0, 0))],
        out_specs=[fwd, bwd, st_spec],
        out_shape=[jax.ShapeDtypeStruct((rows, REC_W), BF16), jax.ShapeDtypeStruct((rows, REC_W), BF16),
                   jax.ShapeDtypeStruct(s0.shape, F32)],
        scratch_shapes=[pltpu.VMEM((2, HEADS, HDIM, HDIM), F32), pltpu.VMEM((HEADS, tt, tt), F32),
                        pltpu.VMEM((4, tt, REC_W), F32), pltpu.VMEM((tt, REC_W), F32)],
        compiler_params=_params("arbitrary", "arbitrary"),
        name="ret_scan",
    )(lg, rq, rk, rv, rq, rk, rv, s0, lgrow)


def _attn_kernel(lam_ref, q_ref, kc_ref, vc_ref, k_ref, v_ref, gain_ref, o_ref, *, with_latent, kchunk, unroll,
                 out_scale):
    tq = q_ref.shape[0]
    q = q_ref[...]
    lane = lax.broadcasted_iota(jnp.int32, (tq, 128), 1)
    zero = jnp.zeros_like(q)
    qs = jnp.concatenate([jnp.where(lane < DA_HEAD_DIM, q, zero),
                          jnp.where(lane >= DA_HEAD_DIM, q, zero)], axis=0)
    nchunks = (k_ref.shape[0] // kchunk) if with_latent else 0

    def latent(ref, c):
        return ref[pl.ds(pl.multiple_of(c * kchunk, kchunk), kchunk), :]

    def step(kblk, vblk, carry):
        m, acc = carry
        s = _dot_nt(qs, kblk)
        mc = s[:, 0:128]
        for c0 in range(128, s.shape[1], 128):
            mc = jnp.maximum(mc, s[:, c0:c0 + 128])
        m_new = jnp.maximum(m, jnp.max(mc, axis=-1, keepdims=True))
        p = jnp.exp2((s - m_new).astype(BF16))
        return m_new, jnp.exp2(m - m_new) * acc + _dot(p, vblk)

    carry = step(kc_ref[...], vc_ref[...], (jnp.full((2 * tq, 1), NEG_BIG, F32), jnp.zeros((2 * tq, 128), F32)))
    if with_latent:
        carry = lax.fori_loop(0, nchunks, lambda c, cr: step(latent(k_ref, c), latent(v_ref, c), cr), carry,
                              unroll=unroll)
    acc = carry[1]

    a1 = acc[:tq]
    a2 = acc[tq:]
    o = a1 / a1[:, DA_VDIM:DA_VDIM + 1] - lam_ref[0] * (a2 / a2[:, DA_VDIM:DA_VDIM + 1])
    o = jnp.where(lane < DA_VDIM, o, 0.0)
    ms = jnp.sum(o * o, axis=-1, keepdims=True) * (1.0 / DA_VDIM)
    o_ref[...] = (o * lax.rsqrt(ms + NORM_EPS) * gain_ref[...] * out_scale).astype(BF16)


def _diff_attention(lam, q, kc, vc, k, v, gain, batch, seq_q, ctx_len, seq_k, out_scale, q_is_ctx):
    tq = min(256, seq_q)
    nq = seq_q // tq
    with_latent = not q_is_ctx
    if with_latent:
        kchunk = min(2048, seq_k)
    else:
        k, v = kc, vc
        seq_k, kchunk = ctx_len, ctx_len
    unroll = 1
    kern = functools.partial(_attn_kernel, with_latent=with_latent, kchunk=kchunk, unroll=unroll,
                             out_scale=out_scale)
    return pl.pallas_call(
        kern,
        grid=(batch, DA_HEADS, nq),
        in_specs=[
            pl.BlockSpec(memory_space=pltpu.SMEM),
            pl.BlockSpec((tq, 128), lambda b, h, i: (b * nq + i, h)),
            pl.BlockSpec((ctx_len, 128), lambda b, h, i: (b, h)),
            pl.BlockSpec((ctx_len, 128), lambda b, h, i: (b, h)),
            pl.BlockSpec((seq_k, 128), lambda b, h, i: (b, h)),
            pl.BlockSpec((seq_k, 128), lambda b, h, i: (b, h)),
            pl.BlockSpec((1, 128), lambda b, h, i: (0, h)),
        ],
        out_specs=pl.BlockSpec((tq, 128), lambda b, h, i: (b * nq + i, h)),
        out_shape=jax.ShapeDtypeStruct((batch * seq_q, DA_X), BF16),
        compiler_params=_params("arbitrary", "arbitrary", "arbitrary"),
        name="diff_attention",
    )(lam, q, kc, vc, k, v, gain)


def _outproj_kernel(x_ref, hof_ref, hob_ref, hg_ref, rof_ref, rob_ref, rg_ref, da_ref, seg_ref, w_ref,
                    hn_ref, rn_ref, n1_ref, n2_ref, g1_ref, sc2_ref, sh2_ref, rw_ref, rb_ref,
                    x1_ref, tok_ref, logit_ref):
    seg = seg_ref[...]

    def gated_head_norm(o, gain, gate):
        ms = _dot((o * o).astype(BF16), seg) * (1.0 / HDIM)
        return (o * lax.rsqrt(ms + NORM_EPS) * gain * (gate * _sigmoid(gate))).astype(BF16)

    a = gated_head_norm(hof_ref[...].astype(F32) + hob_ref[...].astype(F32), hn_ref[...], hg_ref[...].astype(F32))
    b = gated_head_norm(rof_ref[...].astype(F32) + rob_ref[...].astype(F32), rn_ref[...], rg_ref[...].astype(F32))
    y = (_dot(a, w_ref[0:REC_W, :]) + _dot(b, w_ref[REC_W:2 * REC_W, :])
         + _dot(da_ref[...], w_ref[2 * REC_W:2 * REC_W + DA_X, :]))
    x1 = x_ref[...] + g1_ref[0] * _rms(y, n1_ref[...])
    x1_ref[...] = x1
    tok = _rms(x1, n2_ref[...]) * (1.0 + sc2_ref[0]) + sh2_ref[0]
    tok_ref[...] = tok.astype(BF16)
    logit_ref[...] = _dot(tok, rw_ref[...]) + rb_ref[...]


def _outproj(xf, seq, streams, seg, w_out, hn, rn, n1, n2, g1, sc2, sh2, rw, rb):
    rows, d = xf.shape
    tm = min(256, seq)
    nb = seq // tm
    row_spec = lambda w: pl.BlockSpec((tm, w), lambda i: (i, 0))
    full = lambda a: pl.BlockSpec(a.shape, lambda i: (0,) * a.ndim)
    mod_spec = pl.BlockSpec((1, 1, d), lambda i: (i // nb, 0, 0))
    hof, hob, hg, rof, rob, rg, da = streams
    return pl.pallas_call(
        _outproj_kernel,
        grid=(rows // tm,),
        in_specs=[row_spec(d)] + [row_spec(REC_W)] * 6 + [row_spec(DA_X), full(seg), full(w_out),
                  full(hn), full(rn), full(n1), full(n2), mod_spec, mod_spec, mod_spec, full(rw), full(rb)],
        out_specs=[row_spec(d), row_spec(d), row_spec(N_EXPERTS)],
        out_shape=[jax.ShapeDtypeStruct((rows, d), F32), jax.ShapeDtypeStruct((rows, d), BF16),
                   jax.ShapeDtypeStruct((rows, N_EXPERTS), F32)],
        compiler_params=_params("arbitrary"),
        name="outproj",
    )(xf, hof, hob, hg, rof, rob, rg, da, seg, w_out, hn, rn, n1, n2, g1, sc2, sh2, rw, rb)


def _expert_prep_kernel(w1_ref, w2_ref, pe_ref, po_ref, g_ref, l_ref, w2b_ref):
    n = w1_ref.shape[2]
    for c in range(n // 256):
        wb = w1_ref[0, :, 256 * c:256 * (c + 1)].astype(BF16)
        g_ref[0, :, 128 * c:128 * (c + 1)] = _dot(wb, pe_ref[...]).astype(BF16)
        l_ref[0, :, 128 * c:128 * (c + 1)] = _dot(wb, po_ref[...]).astype(BF16)
    w2b_ref[...] = w2_ref[...].astype(BF16)


def _expert_prep(w1, w2):
    e, d, n = w1.shape
    tk = 512
    src = jnp.arange(256)[:, None]
    dst = jnp.arange(128)[None, :]
    pe = (src == 2 * dst).astype(BF16)
    po = (src == 2 * dst + 1).astype(BF16)
    sel = pl.BlockSpec((256, 128), lambda i, j: (0, 0))
    out = pl.BlockSpec((1, tk, n // 2), lambda i, j: (i, j, 0))
    w2_spec = pl.BlockSpec((1, tk, w2.shape[2]), lambda i, j: (i, j, 0))
    assert w2.shape[1] == d
    return pl.pallas_call(
        _expert_prep_kernel,
        grid=(e, d // tk),
        in_specs=[pl.BlockSpec((1, tk, n), lambda i, j: (i, j, 0)), w2_spec, sel, sel],
        out_specs=[out, out, w2_spec],
        out_shape=[jax.ShapeDtypeStruct((e, d, n // 2), BF16)] * 2 + [jax.ShapeDtypeStruct(w2.shape, BF16)],
        compiler_params=_params("arbitrary", "arbitrary"),
        name="expert_prep",
    )(w1, w2, pe, po)


def _ffn_kernel(be_ref, na_ref, x_ref, w1g_ref, w1l_ref, b1g_ref, b1l_ref, w2_ref, b2_ref, y_ref):
    active = pl.program_id(0) < na_ref[0]

    @pl.when(jnp.logical_not(active))
    def _():
        y_ref[...] = jnp.zeros(y_ref.shape, y_ref.dtype)

    @pl.when(active)
    def _():
        x = x_ref[...]
        glu = jnp.minimum(_dot(x, w1g_ref[0]) + b1g_ref[0], SWIGLU_LIMIT)
        lin = jnp.clip(_dot(x, w1l_ref[0]) + b1l_ref[0], -SWIGLU_LIMIT, SWIGLU_LIMIT)
        act = glu * _sigmoid(SWIGLU_ALPHA * glu) * (lin + 1.0)
        y_ref[...] = (_dot(act.astype(BF16), w2_ref[0]) + b2_ref[0]).astype(y_ref.dtype)


def _expert_ffn(block_expert, n_active, xg, w1g, w1l, b1g, b1l, w2, b2, bm):
    p, d = xg.shape
    f = w1g.shape[-1]
    nblocks = p // bm
    wspec = lambda s: pl.BlockSpec((1,) + s, lambda i, be, na: (be[i], 0, 0))
    return pl.pallas_call(
        _ffn_kernel,
        grid_spec=pltpu.PrefetchScalarGridSpec(
            num_scalar_prefetch=2,
            grid=(nblocks,),
            in_specs=[pl.BlockSpec((bm, d), lambda i, be, na: (i, 0)),
                      wspec((d, f)), wspec((d, f)), wspec((1, f)), wspec((1, f)), wspec((f, d)), wspec((1, d))],
            out_specs=pl.BlockSpec((bm, d), lambda i, be, na: (i, 0)),
        ),
        out_shape=jax.ShapeDtypeStruct((p, d), BF16),
        compiler_params=_params("arbitrary"),
        name="expert_ffn",
    )(block_expert, n_active, xg, w1g, w1l, b1g, b1l, w2, b2)


def _resid_kernel(x_ref, y0_ref, y1_ref, y2_ref, y3_ref, gate_ref, n_ref, g_ref, o_ref):
    gates = gate_ref[...]
    f = y0_ref[...].astype(F32) * gates[:, 0:1]
    for k, y_ref in enumerate((y1_ref, y2_ref, y3_ref), start=1):
        f = f + y_ref[...].astype(F32) * gates[:, k:k + 1]
    o_ref[...] = x_ref[...] + g_ref[0] * _rms(f, n_ref[...])


def _ffn_residual(xf, ys, gates, seq, n3, g2):
    rows, d = xf.shape
    tm = min(512, seq)
    nb = seq // tm
    row_spec = pl.BlockSpec((tm, d), lambda i: (i, 0))
    return pl.pallas_call(
        _resid_kernel,
        grid=(rows // tm,),
        in_specs=[row_spec] * 5 + [pl.BlockSpec((tm, TOP_K), lambda i: (i, 0)), pl.BlockSpec((1, d), lambda i: (0, 0)),
                                   pl.BlockSpec((1, 1, d), lambda i: (i // nb, 0, 0))],
        out_specs=row_spec,
        out_shape=jax.ShapeDtypeStruct((rows, d), F32),
        compiler_params=_params("arbitrary"),
        name="ffn_residual",
    )(xf, *ys, gates, n3, g2)


def _rope_tables(pos, dim):
    inv = 1.0 / (ROPE_BASE ** (jnp.arange(0, dim, 2, dtype=F32) / dim))
    ang = pos.astype(F32)[:, None] * inv[None, :]
    return jnp.cos(ang), jnp.sin(ang)


def _rot_cols(w, head_dim, halves):
    d, n = w.shape
    g = head_dim // halves
    w4 = w.reshape(d, n // g, 2, g // 2)
    return jnp.concatenate([-w4[:, :, 1], w4[:, :, 0]], axis=-1).reshape(d, n)


def _prep_w_in(w):
    rq, rk = w[:, _C_RQ:_C_RQ + REC_W], w[:, _C_RK:_C_RK + REC_W]
    dq, dk = w[:, _C_DQ:_C_DQ + DA_W], w[:, _C_DK:_C_DK + DA_W]
    return jnp.concatenate([w, _rot_cols(rq, HDIM, 1), _rot_cols(rk, HDIM, 1),
                            _rot_cols(dq, DA_HEAD_DIM, 2), _rot_cols(dk, DA_HEAD_DIM, 2)], axis=1).astype(BF16)


def _routing(logits, bm):
    n = logits.shape[0]
    top_val, top_idx = lax.top_k(logits, TOP_K)
    gates = jax.nn.softmax(top_val, axis=-1)
    a = n * TOP_K
    flat_e = top_idx.reshape(a).astype(jnp.int32)
    sorted_e, order = lax.sort_key_val(flat_e, jnp.arange(a, dtype=jnp.int32))
    experts = jnp.arange(N_EXPERTS + 1, dtype=jnp.int32)
    bounds = jnp.sum((flat_e[None, :] < experts[:, None]).astype(jnp.int32), axis=1)
    start, counts = bounds[:-1], bounds[1:] - bounds[:-1]
    padded = (counts + bm - 1) // bm * bm
    pend = jnp.cumsum(padded)
    shift = pend - padded - start
    dest = jnp.arange(a, dtype=jnp.int32) + shift[sorted_e]
    _, pos = lax.sort_key_val(order, dest)
    nblocks = -(-(a + N_EXPERTS * (bm - 1)) // bm)
    block_row = jnp.arange(nblocks, dtype=jnp.int32) * bm
    block_expert = jnp.minimum(jnp.sum((pend[None, :] <= block_row[:, None]).astype(jnp.int32), axis=1),
                               N_EXPERTS - 1)
    row_e = jnp.repeat(block_expert, bm)
    rank = jnp.arange(nblocks * bm, dtype=jnp.int32) - shift[row_e]
    valid = rank < bounds[row_e + 1]
    tok_sorted = jnp.where(valid, order[jnp.clip(rank, 0, a - 1)] // TOP_K, 0)
    n_active = (pend[-1:] // bm).astype(jnp.int32)
    return gates, tok_sorted, pos, block_expert, n_active


def kernel(x, c, ctx, c_ctx, mod_w, mod_b, norm_g, w_in, hgrn_lb, hgrn_norm, ret_decay, ret_norm, da_lambda,
           da_subln, w_out, router_w, router_b, w1, b1, w2, b2):
    B, S, D = x.shape
    C = ctx.shape[1]
    depth = mod_w.shape[0]
    bm = 256

    pos = jnp.arange(S)
    cr, sr = _rope_tables(pos // GRID_W, DA_HEAD_DIM // 2)
    cc, sc_ = _rope_tables(pos % GRID_W, DA_HEAD_DIM // 2)
    cs, ss = _rope_tables(pos, HDIM)
    cosr = jnp.tile(jnp.concatenate([cs, cs], -1), (1, HEADS))
    sinr = jnp.tile(jnp.concatenate([ss, ss], -1), (1, HEADS))
    cosd = jnp.tile(jnp.concatenate([cr, cr, cc, cc], -1), (1, DA_MAPS))
    sind = jnp.tile(jnp.concatenate([sr, sr, sc_, sc_], -1), (1, DA_MAPS))
    ones_r, zeros_r = jnp.ones((C, REC_W), F32), jnp.zeros((C, REC_W), F32)
    ones_d, zeros_d = jnp.ones((C, DA_W), F32), jnp.zeros((C, DA_W), F32)

    lb_cum = jnp.cumsum(jax.nn.softmax(hgrn_lb.astype(F32), axis=0), axis=0)
    lower = lb_cum - lb_cum[0:1]

    cvec = jnp.zeros((8, D), F32).at[:B].set(c).at[B].set(c_ctx)
    mods = _modulation(cvec, mod_w, mod_b)

    head_id = jnp.arange(REC_W) // HDIM
    seg = (head_id[:, None] == head_id[None, :]).astype(BF16)

    xf = x.reshape(B * S, D)
    xc = ctx.reshape(B * C, D)
    zero_state = jnp.zeros((B, 2, HEADS, HDIM, HDIM), F32)

    for layer in range(depth):
        need_ctx = layer < depth - 1
        lam_init = 0.8 - 0.6 * math.exp(-0.3 * layer)
        m6 = mods[layer].reshape(8, 6, D)
        lat = lambda k: m6[:B, k][:, None, :]
        cxm = lambda k: jnp.broadcast_to(m6[B, k][None, None, :], (B, 1, D))
        ng = norm_g[layer]
        lb = lower[layer]
        lbc = jnp.zeros((8, REC_W), F32).at[0].set(jnp.log(lb)).at[1].set(jnp.log1p(-lb)).at[2].set(1.0 - lb)
        w_all = _prep_w_in(w_in[layer])
        log_gamma = jnp.log1p(-jnp.exp2(-ret_decay[layer].astype(F32)))
        lgrow = jnp.zeros((8, REC_W), F32).at[:2].set(jnp.repeat(log_gamma, HDIM, axis=1))
        lamv = da_lambda[layer].astype(F32)
        lam = (jnp.exp(jnp.sum(lamv[0] * lamv[1])) - jnp.exp(jnp.sum(lamv[2] * lamv[3])) + lam_init).reshape(1)
        hn = jnp.tile(hgrn_norm[layer], HEADS)[None, :]
        rn = jnp.tile(ret_norm[layer], HEADS)[None, :]
        dn = jnp.tile(jnp.concatenate([da_subln[layer], jnp.zeros((128 - DA_VDIM,), F32)]), DA_HEADS)[None, :]
        wo = w_out[layer]
        wo_da = jnp.pad(wo[2 * REC_W:].reshape(DA_HEADS, DA_VDIM, D), ((0, 0), (0, 128 - DA_VDIM), (0, 0)))
        w_out_b = jnp.concatenate([wo[:2 * REC_W], wo_da.reshape(DA_X, D)], axis=0).astype(BF16)
        rw = router_w[layer]
        rb = router_b[layer][None, :]

        (hq, hv, hg, kf, lff, kb, lfb, rq, rk, rv, rg, dq, dk, dvx) = _inproj(
            xf, S, ng[0:1], lat(1), lat(0), w_all, lbc, cosr, sinr, cosd, sind)
        (hq_c, hv_c, hg_c, kf_c, lff_c, kb_c, lfb_c, rq_c, rk_c, rv_c, rg_c, dq_c, dk_c, dvx_c) = _inproj(
            xc, C, ng[0:1], cxm(1), cxm(0), w_all, lbc, ones_r, zeros_r, ones_d, zeros_d)

        hof_c, hob_c, hs = _hgrn_scan(hq_c, hv_c, kf_c, lff_c, kb_c, lfb_c,
                                      jnp.zeros((B, 2, REC_W // 128, 128, 128), F32), seg, B, C)
        hof, hob, _ = _hgrn_scan(hq, hv, kf, lff, kb, lfb, hs, seg, B, S)
        rof_c, rob_c, rs = _ret_scan(rq_c, rk_c, rv_c, zero_state, log_gamma, lgrow, B, C)
        rof, rob, _ = _ret_scan(rq, rk, rv, rs, log_gamma, lgrow, B, S)
        out_scale = 1.0 - lam_init
        da = _diff_attention(lam, dq, dk_c, dvx_c, dk, dvx, dn, B, S, C, S, out_scale, False)

        x1, tok, logits = _outproj(xf, S, (hof, hob, hg, rof, rob, rg, da), seg, w_out_b, hn, rn,
                                   ng[1:2], ng[2:3], lat(2), lat(4), lat(3), rw, rb)
        if need_ctx:
            da_c = _diff_attention(lam, dq_c, dk_c, dvx_c, None, None, dn, B, C, C, C, out_scale, True)
            xc1, tok_c, logits_c = _outproj(xc, C, (hof_c, hob_c, hg_c, rof_c, rob_c, rg_c, da_c), seg, w_out_b,
                                            hn, rn, ng[1:2], ng[2:3], cxm(2), cxm(4), cxm(3), rw, rb)
            tok = jnp.concatenate([tok, tok_c], axis=0)
            logits = jnp.concatenate([logits, logits_c], axis=0)

        n_tok = tok.shape[0]
        gates, tok_sorted, posn, block_expert, n_active = _routing(logits, bm)
        w1g_b, w1l_b, w2_b = _expert_prep(w1[layer], w2[layer])
        b1g = b1[layer][:, None, 0::2]
        b1l = b1[layer][:, None, 1::2]
        b2l = b2[layer][:, None, :]
        xg = jnp.take(tok, tok_sorted, axis=0)
        yg = _expert_ffn(block_expert, n_active, xg, w1g_b, w1l_b, b1g, b1l, w2_b, b2l, bm)
        pos2 = posn.reshape(n_tok, TOP_K)

        def expert_rows(lo, hi):
            return [jnp.take(yg, pos2[lo:hi, k], axis=0) for k in range(TOP_K)]

        xf = _ffn_residual(x1, expert_rows(0, B * S), gates[:B * S], S, ng[3:4], lat(5))
        if need_ctx:
            xc = _ffn_residual(xc1, expert_rows(B * S, n_tok), gates[B * S:], C, ng[3:4], cxm(5))
    return xf.reshape(B, S, D)
```

```python
import functools
import math

import jax
import jax.numpy as jnp
from jax import lax
from jax.experimental import pallas as pl
from jax.experimental.pallas import tpu as pltpu

F32 = jnp.float32
BF16 = jnp.bfloat16

GRID_W = 64
HEADS = 6
HDIM = 64
REC_W = HEADS * HDIM
DA_HEADS = 4
DA_HEAD_DIM = 32
DA_MAPS = 2 * DA_HEADS
DA_W = DA_MAPS * DA_HEAD_DIM
DA_VDIM = 2 * DA_HEAD_DIM
DA_X = DA_HEADS * 128
_Q_SCALE = DA_HEAD_DIM ** -0.5 * math.log2(math.e)
ROPE_BASE = 10000.0
N_EXPERTS = 32
TOP_K = 4
SWIGLU_ALPHA = 1.702
SWIGLU_LIMIT = 7.0
NORM_EPS = 1e-6

SUB = 16
NEG_BIG = -1e30
VMEM_LIMIT = 56 * 1024 * 1024

_C_HQ, _C_HFF, _C_HFB, _C_HI, _C_HG = 0, 384, 768, 1152, 1536
_C_RQ, _C_RK, _C_RV, _C_RG = 1920, 2304, 2688, 3072
_C_DQ, _C_DK, _C_DV = 3456, 3712, 3968
_C_RQR, _C_RKR, _C_DQR, _C_DKR = 4224, 4608, 4992, 5248
_W_ALL = 5504


def _dot(a, b):
    return jnp.dot(a, b, preferred_element_type=F32)


def _dot_nt(a, b):
    return lax.dot_general(a, b, (((1,), (1,)), ((), ())), preferred_element_type=F32)


def _dot_tn(a, b):
    return lax.dot_general(a, b, (((0,), (0,)), ((), ())), preferred_element_type=F32)


def _sigmoid(x):
    return 1.0 / (1.0 + jnp.exp(-x))


def _rms(x, g):
    ms = jnp.mean(x * x, axis=-1, keepdims=True)
    return x * lax.rsqrt(ms + NORM_EPS) * g


def _params(*sem):
    return pltpu.CompilerParams(dimension_semantics=sem, vmem_limit_bytes=VMEM_LIMIT)


def _mod_kernel(c_ref, w_ref, b_ref, o_ref):
    c = c_ref[...]
    o_ref[0] = _dot(c * _sigmoid(c), w_ref[0]) + b_ref[0]


def _modulation(cvec, mod_w, mod_b):
    depth, d, n = mod_w.shape
    tn = 1536
    return pl.pallas_call(
        _mod_kernel,
        grid=(depth, n // tn),
        in_specs=[
            pl.BlockSpec((8, d), lambda l, j: (0, 0)),
            pl.BlockSpec((1, d, tn), lambda l, j: (l, 0, j)),
            pl.BlockSpec((1, 1, tn), lambda l, j: (l, 0, j)),
        ],
        out_specs=pl.BlockSpec((1, 8, tn), lambda l, j: (l, 0, j)),
        out_shape=jax.ShapeDtypeStruct((depth, 8, n), F32),
        compiler_params=_params("arbitrary", "arbitrary"),
        name="modulation",
    )(cvec, mod_w, mod_b.reshape(depth, 1, n))


def _inproj_kernel(x_ref, g_ref, sc_ref, sh_ref, w_ref, lbc_ref, cosr_ref, sinr_ref, cosd_ref, sind_ref,
                   hq_ref, hv_ref, hg_ref, kf_ref, lff_ref, kb_ref, lfb_ref,
                   rq_ref, rk_ref, rv_ref, rg_ref, dq_ref, dk_ref, dvt_ref):
    x = x_ref[...]
    h = _rms(x, g_ref[...]) * (1.0 + sc_ref[0]) + sh_ref[0]
    hb = h.astype(BF16)

    def proj(c0, n):
        return _dot(hb, w_ref[:, c0:c0 + n])

    hq_ref[...] = proj(_C_HQ, REC_W).astype(BF16)
    hv_ref[...] = proj(_C_HI, REC_W).astype(BF16)
    hg_ref[...] = proj(_C_HG, REC_W).astype(BF16)

    log_lb = lbc_ref[0:1, :]
    log_1m = lbc_ref[1:2, :]
    one_m = lbc_ref[2:3, :]

    def gates(z):
        log_sig = jnp.minimum(z, 0.0) - jnp.log1p(jnp.exp(-jnp.abs(z)))
        t = log_1m + log_sig
        m = jnp.maximum(log_lb, t)
        logf = m + jnp.log1p(jnp.exp(-jnp.abs(log_lb - t)))
        return one_m / (1.0 + jnp.exp(z)), logf

    k, lf = gates(proj(_C_HFF, REC_W))
    kf_ref[...] = k.astype(BF16)
    lff_ref[...] = lf
    k, lf = gates(proj(_C_HFB, REC_W))
    kb_ref[...] = k.astype(BF16)
    lfb_ref[...] = lf

    cosr = cosr_ref[...]
    sinr = sinr_ref[...]
    rq_ref[...] = (proj(_C_RQ, REC_W) * cosr + proj(_C_RQR, REC_W) * sinr).astype(BF16)
    rk_ref[...] = ((proj(_C_RK, REC_W) * cosr + proj(_C_RKR, REC_W) * sinr) * (HDIM ** -0.5)).astype(BF16)
    rv_ref[...] = proj(_C_RV, REC_W).astype(BF16)
    rg_ref[...] = proj(_C_RG, REC_W).astype(BF16)

    cosd = cosd_ref[...]
    sind = sind_ref[...]
    dq = ((proj(_C_DQ, DA_W) * cosd + proj(_C_DQR, DA_W) * sind) * _Q_SCALE).astype(BF16)
    dk = (proj(_C_DK, DA_W) * cosd + proj(_C_DKR, DA_W) * sind).astype(BF16)
    dv_t = proj(_C_DV, DA_W).T.astype(BF16)
    tm = dq.shape[0]
    zero_col = jnp.zeros((tm, 64), BF16)
    sub = lax.broadcasted_iota(jnp.int32, (64, tm), 0)
    ones_row = jnp.where(sub == 0, 1.0, 0.0).astype(BF16)
    for h_ in range(DA_HEADS):
        lo, mid, hi_ = 128 * h_, 128 * h_ + 64, 128 * h_ + 128
        dq_ref[:, lo:mid] = dq[:, 64 * h_:64 * h_ + 64]
        dq_ref[:, mid:hi_] = zero_col
        dk_ref[:, lo:mid] = dk[:, 64 * h_:64 * h_ + 64]
        dk_ref[:, mid:hi_] = zero_col
        dvt_ref[0, lo:mid, :] = dv_t[64 * h_:64 * h_ + 64, :]
        dvt_ref[0, mid:hi_, :] = ones_row


def _inproj(xf, seq, gnorm, sc, sh, w_all, lbc, cosr, sinr, cosd, sind):
    rows, d = xf.shape
    tm = min(256, seq)
    nb = seq // tm
    row_spec = lambda w: pl.BlockSpec((tm, w), lambda i: (i, 0))
    tab_spec = lambda w: pl.BlockSpec((tm, w), lambda i: (i % nb, 0))
    mod_spec = pl.BlockSpec((1, 1, d), lambda i: (i // nb, 0, 0))
    widths = [REC_W] * 11 + [DA_X, DA_X]
    dtypes = [BF16, BF16, BF16, BF16, F32, BF16, F32, BF16, BF16, BF16, BF16, BF16, BF16]
    out_specs = [row_spec(w) for w in widths] + [pl.BlockSpec((1, DA_X, tm), lambda i: (i // nb, 0, i % nb))]
    out_shape = ([jax.ShapeDtypeStruct((rows, w), dt) for w, dt in zip(widths, dtypes)]
                 + [jax.ShapeDtypeStruct((rows // seq, DA_X, seq), BF16)])
    return pl.pallas_call(
        _inproj_kernel,
        grid=(rows // tm,),
        in_specs=[
            row_spec(d),
            pl.BlockSpec((1, d), lambda i: (0, 0)),
            mod_spec, mod_spec,
            pl.BlockSpec((d, _W_ALL), lambda i: (0, 0)),
            pl.BlockSpec((8, REC_W), lambda i: (0, 0)),
            tab_spec(REC_W), tab_spec(REC_W), tab_spec(DA_W), tab_spec(DA_W),
        ],
        out_specs=out_specs,
        out_shape=out_shape,
        compiler_params=_params("arbitrary"),
        name="inproj",
    )(xf, gnorm, sc, sh, w_all, lbc, cosr, sinr, cosd, sind)


def _hgrn_dir(q, k, v, lf, seg, st_ref, inter_ref, intra_ref, reverse):
    tt = q.shape[0]
    nblk = tt // SUB
    row = lax.broadcasted_iota(jnp.int32, (tt, tt), 0)
    col = lax.broadcasted_iota(jnp.int32, (tt, tt), 1)
    same = (row // SUB) == (col // SUB)
    tri = (col >= row) if reverse else (col <= row)
    l_all = jnp.where(same, 1.0, 0.0)
    l_cum = jnp.where(tri, l_all, 0.0).astype(BF16)
    l_all = l_all.astype(BF16)
    hi = lf.astype(BF16)
    r1 = lf - hi.astype(F32)
    mid = r1.astype(BF16)
    lo = (r1 - mid.astype(F32)).astype(BF16)
    b = _dot(l_cum, hi) + _dot(l_cum, mid) + _dot(l_cum, lo)
    btot = _dot(l_all, hi) + _dot(l_all, mid) + _dot(l_all, lo)

    half = lax.broadcasted_iota(jnp.int32, (nblk, 8, 128), 1)
    seg2 = seg[0:128, 0:128]
    for g in range(REC_W // 128):
        c = slice(128 * g, 128 * (g + 1))

        def halves(a):
            a4 = a[:, c].reshape(nblk, 2, 8, 128)
            return (a4[:, 1], a4[:, 0]) if reverse else (a4[:, 0], a4[:, 1])

        (q0, q1), (k0, k1), (v0, v1), (b0, b1) = halves(q), halves(k), halves(v), halves(b)

        def pairs(qa, ba, kb_, bb_, vb_, r, masked):
            sh = ((8 - r) % 8) if reverse else r
            if sh:
                kb_, bb_, vb_ = pltpu.roll(kb_, sh, 1), pltpu.roll(bb_, sh, 1), pltpu.roll(vb_, sh, 1)
            diff = ba - bb_
            if masked and r:
                diff = jnp.where((half <= 7 - r) if reverse else (half >= r), diff, NEG_BIG)
            term = (qa * kb_ * jnp.exp(diff)).reshape(nblk * 8, 128).astype(BF16)
            return _dot(term, seg2).reshape(nblk, 8, 128) * vb_

        o0 = pairs(q0, b0, k0, b0, v0, 0, True)
        o1 = pairs(q1, b1, k1, b1, v1, 0, True) + pairs(q1, b1, k0, b0, v0, 0, False)
        for r in range(1, 8):
            o0 = o0 + pairs(q0, b0, k0, b0, v0, r, True)
            o1 = o1 + pairs(q1, b1, k1, b1, v1, r, True) + pairs(q1, b1, k0, b0, v0, r, False)
        first, second = (o1, o0) if reverse else (o0, o1)
        intra_ref[:, 0, :, c] = first
        intra_ref[:, 1, :, c] = second

    qe = (q * jnp.exp(b)).astype(BF16)
    kd = (k * jnp.exp(btot - b)).astype(BF16)
    dec = jnp.exp(btot)
    v_t = v.T
    tok_blk = lax.broadcasted_iota(jnp.int32, (128, tt), 1) // SUB
    hr = lax.broadcasted_iota(jnp.int32, (128, 128), 0) // HDIM
    hc = lax.broadcasted_iota(jnp.int32, (128, 128), 1) // HDIM
    same_head = hr == hc
    order = range(nblk - 1, -1, -1) if reverse else range(nblk)
    for g in range(REC_W // 128):
        c = slice(128 * g, 128 * (g + 1))
        v_tg = v_t[c, :]
        v_stack = jnp.concatenate([jnp.where(tok_blk == j, v_tg, 0.0).astype(BF16) for j in range(nblk)], axis=0)
        upd = _dot(v_stack, kd[:, c])
        st = st_ref[g]
        for j in order:
            r0 = j * SUB
            inter_ref[r0:r0 + SUB, c] = _dot_nt(qe[r0:r0 + SUB, c], st.astype(BF16))
            st = st * dec[r0:r0 + 1, c] + jnp.where(same_head, upd[128 * j:128 * (j + 1)], 0.0)
        st_ref[g] = st
    return intra_ref[...].reshape(tt, REC_W) + inter_ref[...]


def _hgrn_kernel(qf_ref, vf_ref, kf_ref, lff_ref, qb_ref, vb_ref, kb_ref, lfb_ref, s0_ref, seg_ref,
                 of_ref, ob_ref, sout_ref, st_ref, inter_ref, intra_ref):
    i = pl.program_id(1)

    @pl.when(i == 0)
    def _():
        st_ref[...] = s0_ref[0]

    seg = seg_ref[...]
    of_ref[...] = _hgrn_dir(qf_ref[...].astype(F32), kf_ref[...].astype(F32), vf_ref[...].astype(F32),
                            lff_ref[...], seg, st_ref.at[0], inter_ref, intra_ref, False).astype(BF16)
    ob_ref[...] = _hgrn_dir(qb_ref[...].astype(F32), kb_ref[...].astype(F32), vb_ref[...].astype(F32),
                            lfb_ref[...], seg, st_ref.at[1], inter_ref, intra_ref, True).astype(BF16)

    @pl.when(i == pl.num_programs(1) - 1)
    def _():
        sout_ref[0] = st_ref[...]


def _hgrn_scan(hq, hv, kf, lff, kb, lfb, s0, seg, batch, seq):
    tt = min(256, seq)
    n = seq // tt
    fwd = pl.BlockSpec((tt, REC_W), lambda b, i: (b * n + i, 0))
    bwd = pl.BlockSpec((tt, REC_W), lambda b, i: (b * n + n - 1 - i, 0))
    st_spec = pl.BlockSpec((1, 2, REC_W // 128, 128, 128), lambda b, i: (b, 0, 0, 0, 0))
    rows = batch * seq
    return pl.pallas_call(
        _hgrn_kernel,
        grid=(batch, n),
        in_specs=[fwd, fwd, fwd, fwd, bwd, bwd, bwd, bwd, st_spec,
                  pl.BlockSpec((REC_W, REC_W), lambda b, i: (0, 0))],
        out_specs=[fwd, bwd, st_spec],
        out_shape=[jax.ShapeDtypeStruct((rows, REC_W), BF16), jax.ShapeDtypeStruct((rows, REC_W), BF16),
                   jax.ShapeDtypeStruct(s0.shape, F32)],
        scratch_shapes=[pltpu.VMEM((2, REC_W // 128, 128, 128), F32), pltpu.VMEM((tt, REC_W), F32),
                        pltpu.VMEM((tt // SUB, 2, 8, REC_W), F32)],
        compiler_params=_params("arbitrary", "arbitrary"),
        name="hgrn_scan",
    )(hq, hv, kf, lff, hq, hv, kb, lfb, s0, seg)


def _ret_kernel(lg_ref, qf_ref, kf_ref, vf_ref, qb_ref, kb_ref, vb_ref, s0_ref, lgrow_ref,
                of_ref, ob_ref, sout_ref, st_ref, dmask_ref, tab_ref, o_scr):
    i = pl.program_id(1)
    tt = qf_ref.shape[0]

    @pl.when(i == 0)
    def _():
        st_ref[...] = s0_ref[0]
        row = lax.broadcasted_iota(jnp.int32, (tt, tt), 0)
        col = lax.broadcasted_iota(jnp.int32, (tt, tt), 1)
        dist = (row - col).astype(F32)
        for h in range(HEADS):
            fw = jnp.where(dist >= 0, jnp.exp(dist * lg_ref[0, h]), 0.0)
            bw = jnp.where(dist <= 0, jnp.exp(-dist * lg_ref[1, h]), 0.0)
            dmask_ref[h] = fw + bw
        pos = lax.broadcasted_iota(jnp.int32, (tt, REC_W), 0).astype(F32)
        lgf = lgrow_ref[0:1, :]
        lgb = lgrow_ref[1:2, :]
        tab_ref[0] = jnp.exp((pos + 1.0) * lgf)
        tab_ref[1] = jnp.exp((tt - 1.0 - pos) * lgf)
        tab_ref[2] = jnp.exp((tt - pos) * lgb)
        tab_ref[3] = jnp.exp(pos * lgb)

    tile_f = jnp.exp(tt * lgrow_ref[0:1, :])
    tile_b = jnp.exp(tt * lgrow_ref[1:2, :])

    q = qf_ref[...]
    k = kf_ref[...]
    v = vf_ref[...]
    qe = (q.astype(F32) * tab_ref[0]).astype(BF16)
    ke = (k.astype(F32) * tab_ref[1]).astype(BF16)
    for h in range(HEADS):
        c = slice(h * HDIM, (h + 1) * HDIM)
        s = _dot_nt(q[:, c], k[:, c]) * dmask_ref[h]
        st = st_ref[0, h]
        o_scr[:, c] = _dot(s.astype(BF16), v[:, c]) + _dot(qe[:, c], st.astype(BF16))
        st_ref[0, h] = st * tile_f[:, c] + _dot_tn(ke[:, c], v[:, c])
    of_ref[...] = o_scr[...].astype(BF16)

    q = qb_ref[...]
    k = kb_ref[...]
    v = vb_ref[...]
    qe = (q.astype(F32) * tab_ref[2]).astype(BF16)
    ke = (k.astype(F32) * tab_ref[3]).astype(BF16)
    for h in range(HEADS):
        c = slice(h * HDIM, (h + 1) * HDIM)
        st = st_ref[1, h]
        o_scr[:, c] = _dot(qe[:, c], st.astype(BF16))
        st_ref[1, h] = st * tile_b[:, c] + _dot_tn(ke[:, c], v[:, c])
    ob_ref[...] = o_scr[...].astype(BF16)

    @pl.when(i == pl.num_programs(1) - 1)
    def _():
        sout_ref[0] = st_ref[...]


def _ret_scan(rq, rk, rv, s0, lg, lgrow, batch, seq):
    tt = min(256, seq)
    n = seq // tt
    fwd = pl.BlockSpec((tt, REC_W), lambda b, i: (b * n + i, 0))
    bwd = pl.BlockSpec((tt, REC_W), lambda b, i: (b * n + n - 1 - i, 0))
    st_spec = pl.BlockSpec((1, 2, HEADS, HDIM, HDIM), lambda b, i: (b, 0, 0, 0, 0))
    rows = batch * seq
    return pl.pallas_call(
        _ret_kernel,
        grid=(batch, n),
        in_specs=[pl.BlockSpec(memory_space=pltpu.SMEM), fwd, fwd, fwd, bwd, bwd, bwd, st_spec,
                  pl.BlockSpec((8, REC_W), lambda b, i: (0, 0))],
        out_specs=[fwd, bwd, st_spec],
        out_shape=[jax.ShapeDtypeStruct((rows, REC_W), BF16), jax.ShapeDtypeStruct((rows, REC_W), BF16),
                   jax.ShapeDtypeStruct(s0.shape, F32)],
        scratch_shapes=[pltpu.VMEM((2, HEADS, HDIM, HDIM), F32), pltpu.VMEM((HEADS, tt, tt), F32),
                        pltpu.VMEM((4, tt, REC_W), F32), pltpu.VMEM((tt, REC_W), F32)],
        compiler_params=_params("arbitrary", "arbitrary"),
        name="ret_scan",
    )(lg, rq, rk, rv, rq, rk, rv, s0, lgrow)


def _attn_kernel(lam_ref, q_ref, kc_ref, vct_ref, k_ref, vt_ref, gain_ref, o_ref, s_ref, *, with_latent, kchunk,
                 out_scale):
    tq = q_ref.shape[0]
    q = q_ref[...]
    lane = lax.broadcasted_iota(jnp.int32, (tq, 128), 1)
    zero = jnp.zeros_like(q)
    qs = jnp.concatenate([jnp.where(lane < DA_HEAD_DIM, q, zero),
                          jnp.where(lane >= DA_HEAD_DIM, q, zero)], axis=0)
    nchunks = (k_ref.shape[0] // kchunk) if with_latent else 0

    def scores(kblk):
        return _dot_nt(kblk, qs)

    def absorb(s, vtblk, m, acc):
        n = s.shape[0]
        part = jnp.max(s.reshape(n // 128, 128, 2 * tq), axis=0) if n > 128 else s
        m_new = jnp.maximum(m, jnp.max(part, axis=0, keepdims=True))
        p = jnp.exp2((s - m_new).astype(BF16))
        return m_new, jnp.exp2(m - m_new) * acc + _dot(vtblk, p)

    def latent_k(c):
        return k_ref[pl.ds(pl.multiple_of(c * kchunk, kchunk), kchunk), :]

    def latent_vt(c):
        return vt_ref[0, :, pl.ds(pl.multiple_of(c * kchunk, kchunk), kchunk)]

    m, acc = absorb(scores(kc_ref[...]), vct_ref[0], jnp.full((1, 2 * tq), NEG_BIG, F32),
                    jnp.zeros((128, 2 * tq), F32))
    if with_latent:
        def pair(c, carry, last):
            m_, acc_ = carry
            s_ref[1] = scores(latent_k(c + 1))
            m_, acc_ = absorb(s_ref[0], latent_vt(c), m_, acc_)
            if not last:
                s_ref[0] = scores(latent_k(c + 2))
            return absorb(s_ref[1], latent_vt(c + 1), m_, acc_)

        s_ref[0] = scores(latent_k(0))
        if nchunks > 2:
            m, acc = lax.fori_loop(0, nchunks // 2 - 1, lambda j, cr: pair(2 * j, cr, False), (m, acc))
        m, acc = pair(nchunks - 2, (m, acc), True)

    a1 = acc[:, :tq]
    a2 = acc[:, tq:]
    o = a1 / a1[DA_VDIM:DA_VDIM + 1, :] - lam_ref[0] * (a2 / a2[DA_VDIM:DA_VDIM + 1, :])
    row = lax.broadcasted_iota(jnp.int32, (128, tq), 0)
    o = jnp.where(row < DA_VDIM, o, 0.0)
    ms = jnp.sum(o * o, axis=0, keepdims=True) * (1.0 / DA_VDIM)
    o_ref[...] = (o * lax.rsqrt(ms + NORM_EPS) * gain_ref[...] * out_scale).T.astype(BF16)


def _diff_attention(lam, q, kc, vc, k, v, gain, batch, seq_q, ctx_len, seq_k, out_scale, q_is_ctx):
    tq = min(256, seq_q)
    nq = seq_q // tq
    with_latent = not q_is_ctx
    if with_latent:
        kchunk = min(2048, seq_k // 2)
        assert seq_k % (2 * kchunk) == 0
    else:
        k, v = kc, vc
        seq_k, kchunk = ctx_len, ctx_len
    kern = functools.partial(_attn_kernel, with_latent=with_latent, kchunk=kchunk, out_scale=out_scale)
    return pl.pallas_call(
        kern,
        grid=(batch, DA_HEADS, nq),
        in_specs=[
            pl.BlockSpec(memory_space=pltpu.SMEM),
            pl.BlockSpec((tq, 128), lambda b, h, i: (b * nq + i, h)),
            pl.BlockSpec((ctx_len, 128), lambda b, h, i: (b, h)),
            pl.BlockSpec((1, 128, ctx_len), lambda b, h, i: (b, h, 0)),
            pl.BlockSpec((seq_k, 128), lambda b, h, i: (b, h)),
            pl.BlockSpec((1, 128, seq_k), lambda b, h, i: (b, h, 0)),
            pl.BlockSpec((128, 1), lambda b, h, i: (h, 0)),
        ],
        out_specs=pl.BlockSpec((tq, 128), lambda b, h, i: (b * nq + i, h)),
        out_shape=jax.ShapeDtypeStruct((batch * seq_q, DA_X), BF16),
        scratch_shapes=[pltpu.VMEM((2, kchunk, 2 * tq), F32)],
        compiler_params=_params("arbitrary", "arbitrary", "arbitrary"),
        name="diff_attention",
    )(lam, q, kc, vc, k, v, gain)


def _outproj_kernel(x_ref, hof_ref, hob_ref, hg_ref, rof_ref, rob_ref, rg_ref, da_ref, seg_ref, w_ref,
                    hn_ref, rn_ref, n1_ref, n2_ref, g1_ref, sc2_ref, sh2_ref, rw_ref, rb_ref,
                    x1_ref, tok_ref, logit_ref):
    seg = seg_ref[...]

    def gated_head_norm(o, gain, gate):
        ms = _dot((o * o).astype(BF16), seg) * (1.0 / HDIM)
        return (o * lax.rsqrt(ms + NORM_EPS) * gain * (gate * _sigmoid(gate))).astype(BF16)

    a = gated_head_norm(hof_ref[...].astype(F32) + hob_ref[...].astype(F32), hn_ref[...], hg_ref[...].astype(F32))
    b = gated_head_norm(rof_ref[...].astype(F32) + rob_ref[...].astype(F32), rn_ref[...], rg_ref[...].astype(F32))
    y = (_dot(a, w_ref[0:REC_W, :]) + _dot(b, w_ref[REC_W:2 * REC_W, :])
         + _dot(da_ref[...], w_ref[2 * REC_W:2 * REC_W + DA_X, :]))
    x1 = x_ref[...] + g1_ref[0] * _rms(y, n1_ref[...])
    x1_ref[...] = x1
    tok = _rms(x1, n2_ref[...]) * (1.0 + sc2_ref[0]) + sh2_ref[0]
    tok_ref[...] = tok.astype(BF16)
    logit_ref[...] = _dot(tok, rw_ref[...]) + rb_ref[...]


def _outproj(xf, seq, streams, seg, w_out, hn, rn, n1, n2, g1, sc2, sh2, rw, rb):
    rows, d = xf.shape
    tm = min(256, seq)
    nb = seq // tm
    row_spec = lambda w: pl.BlockSpec((tm, w), lambda i: (i, 0))
    full = lambda a: pl.BlockSpec(a.shape, lambda i: (0,) * a.ndim)
    mod_spec = pl.BlockSpec((1, 1, d), lambda i: (i // nb, 0, 0))
    hof, hob, hg, rof, rob, rg, da = streams
    return pl.pallas_call(
        _outproj_kernel,
        grid=(rows // tm,),
        in_specs=[row_spec(d)] + [row_spec(REC_W)] * 6 + [row_spec(DA_X), full(seg), full(w_out),
                  full(hn), full(rn), full(n1), full(n2), mod_spec, mod_spec, mod_spec, full(rw), full(rb)],
        out_specs=[row_spec(d), row_spec(d), row_spec(N_EXPERTS)],
        out_shape=[jax.ShapeDtypeStruct((rows, d), F32), jax.ShapeDtypeStruct((rows, d), BF16),
                   jax.ShapeDtypeStruct((rows, N_EXPERTS), F32)],
        compiler_params=_params("arbitrary"),
        name="outproj",
    )(xf, hof, hob, hg, rof, rob, rg, da, seg, w_out, hn, rn, n1, n2, g1, sc2, sh2, rw, rb)


def _expert_prep_kernel(w1_ref, w2_ref, pe_ref, po_ref, g_ref, l_ref, w2b_ref):
    n = w1_ref.shape[3]
    for c in range(n // 256):
        wb = w1_ref[0, 0, :, 256 * c:256 * (c + 1)].astype(BF16)
        g_ref[0, :, 128 * c:128 * (c + 1)] = _dot(wb, pe_ref[...]).astype(BF16)
        l_ref[0, :, 128 * c:128 * (c + 1)] = _dot(wb, po_ref[...]).astype(BF16)
    w2b_ref[...] = w2_ref[0].astype(BF16)


def _expert_prep(w1, w2, layer):
    _, e, d, n = w1.shape
    tk = 512
    src = jnp.arange(256)[:, None]
    dst = jnp.arange(128)[None, :]
    pe = (src == 2 * dst).astype(BF16)
    po = (src == 2 * dst + 1).astype(BF16)
    sel = pl.BlockSpec((256, 128), lambda i, j: (0, 0))
    out = pl.BlockSpec((1, tk, n // 2), lambda i, j: (i, j, 0))
    w2_out = pl.BlockSpec((1, tk, w2.shape[3]), lambda i, j: (i, j, 0))
    assert w2.shape[2] == d
    return pl.pallas_call(
        _expert_prep_kernel,
        grid=(e, d // tk),
        in_specs=[pl.BlockSpec((1, 1, tk, n), lambda i, j: (layer, i, j, 0)),
                  pl.BlockSpec((1, 1, tk, w2.shape[3]), lambda i, j: (layer, i, j, 0)), sel, sel],
        out_specs=[out, out, w2_out],
        out_shape=[jax.ShapeDtypeStruct((e, d, n // 2), BF16)] * 2 + [jax.ShapeDtypeStruct(w2.shape[1:], BF16)],
        compiler_params=_params("arbitrary", "arbitrary"),
        name="expert_prep",
    )(w1, w2, pe, po)


def _ffn_kernel(be_ref, na_ref, x_ref, w1g_ref, w1l_ref, b1g_ref, b1l_ref, w2_ref, b2_ref, y_ref):
    active = pl.program_id(0) < na_ref[0]

    @pl.when(jnp.logical_not(active))
    def _():
        y_ref[...] = jnp.zeros(y_ref.shape, y_ref.dtype)

    @pl.when(active)
    def _():
        x = x_ref[...]
        glu = jnp.minimum(_dot(x, w1g_ref[0]) + b1g_ref[0], SWIGLU_LIMIT)
        lin = jnp.clip(_dot(x, w1l_ref[0]) + b1l_ref[0], -SWIGLU_LIMIT, SWIGLU_LIMIT)
        act = glu * _sigmoid(SWIGLU_ALPHA * glu) * (lin + 1.0)
        y_ref[...] = (_dot(act.astype(BF16), w2_ref[0]) + b2_ref[0]).astype(y_ref.dtype)


def _expert_ffn(block_expert, n_active, xg, w1g, w1l, b1g, b1l, w2, b2, bm):
    p, d = xg.shape
    f = w1g.shape[-1]
    nblocks = p // bm
    wspec = lambda s: pl.BlockSpec((1,) + s, lambda i, be, na: (be[i], 0, 0))
    return pl.pallas_call(
        _ffn_kernel,
        grid_spec=pltpu.PrefetchScalarGridSpec(
            num_scalar_prefetch=2,
            grid=(nblocks,),
            in_specs=[pl.BlockSpec((bm, d), lambda i, be, na: (i, 0)),
                      wspec((d, f)), wspec((d, f)), wspec((1, f)), wspec((1, f)), wspec((f, d)), wspec((1, d))],
            out_specs=pl.BlockSpec((bm, d), lambda i, be, na: (i, 0)),
        ),
        out_shape=jax.ShapeDtypeStruct((p, d), BF16),
        compiler_params=_params("arbitrary"),
        name="expert_ffn",
    )(block_expert, n_active, xg, w1g, w1l, b1g, b1l, w2, b2)


def _resid_kernel(x_ref, y0_ref, y1_ref, y2_ref, y3_ref, gate_ref, n_ref, g_ref, o_ref):
    gates = gate_ref[...]
    f = y0_ref[...].astype(F32) * gates[:, 0:1]
    for k, y_ref in enumerate((y1_ref, y2_ref, y3_ref), start=1):
        f = f + y_ref[...].astype(F32) * gates[:, k:k + 1]
    o_ref[...] = x_ref[...] + g_ref[0] * _rms(f, n_ref[...])


def _ffn_residual(xf, ys, gates, seq, n3, g2):
    rows, d = xf.shape
    tm = min(512, seq)
    nb = seq // tm
    row_spec = pl.BlockSpec((tm, d), lambda i: (i, 0))
    return pl.pallas_call(
        _resid_kernel,
        grid=(rows // tm,),
        in_specs=[row_spec] * 5 + [pl.BlockSpec((tm, TOP_K), lambda i: (i, 0)), pl.BlockSpec((1, d), lambda i: (0, 0)),
                                   pl.BlockSpec((1, 1, d), lambda i: (i // nb, 0, 0))],
        out_specs=row_spec,
        out_shape=jax.ShapeDtypeStruct((rows, d), F32),
        compiler_params=_params("arbitrary"),
        name="ffn_residual",
    )(xf, *ys, gates, n3, g2)


def _rope_tables(pos, dim):
    inv = 1.0 / (ROPE_BASE ** (jnp.arange(0, dim, 2, dtype=F32) / dim))
    ang = pos.astype(F32)[:, None] * inv[None, :]
    return jnp.cos(ang), jnp.sin(ang)


def _rot_cols(w, head_dim, halves):
    d, n = w.shape
    g = head_dim // halves
    w4 = w.reshape(d, n // g, 2, g // 2)
    return jnp.concatenate([-w4[:, :, 1], w4[:, :, 0]], axis=-1).reshape(d, n)


def _prep_w_in(w):
    rq, rk = w[:, _C_RQ:_C_RQ + REC_W], w[:, _C_RK:_C_RK + REC_W]
    dq, dk = w[:, _C_DQ:_C_DQ + DA_W], w[:, _C_DK:_C_DK + DA_W]
    return jnp.concatenate([w, _rot_cols(rq, HDIM, 1), _rot_cols(rk, HDIM, 1),
                            _rot_cols(dq, DA_HEAD_DIM, 2), _rot_cols(dk, DA_HEAD_DIM, 2)], axis=1).astype(BF16)


def _routing(logits, bm):
    n = logits.shape[0]
    top_val, top_idx = lax.top_k(logits, TOP_K)
    gates = jax.nn.softmax(top_val, axis=-1)
    a = n * TOP_K
    flat_e = top_idx.reshape(a).astype(jnp.int32)
    sorted_e, order = lax.sort_key_val(flat_e, jnp.arange(a, dtype=jnp.int32))
    experts = jnp.arange(N_EXPERTS + 1, dtype=jnp.int32)
    bounds = jnp.sum((flat_e[None, :] < experts[:, None]).astype(jnp.int32), axis=1)
    start, counts = bounds[:-1], bounds[1:] - bounds[:-1]
    padded = (counts + bm - 1) // bm * bm
    pend = jnp.cumsum(padded)
    shift = pend - padded - start
    dest = jnp.arange(a, dtype=jnp.int32) + shift[sorted_e]
    _, pos = lax.sort_key_val(order, dest)
    nblocks = -(-(a + N_EXPERTS * (bm - 1)) // bm)
    block_row = jnp.arange(nblocks, dtype=jnp.int32) * bm
    block_expert = jnp.minimum(jnp.sum((pend[None, :] <= block_row[:, None]).astype(jnp.int32), axis=1),
                               N_EXPERTS - 1)
    row_e = jnp.repeat(block_expert, bm)
    rank = jnp.arange(nblocks * bm, dtype=jnp.int32) - shift[row_e]
    valid = rank < bounds[row_e + 1]
    tok_sorted = jnp.where(valid, order[jnp.clip(rank, 0, a - 1)] // TOP_K, 0)
    n_active = (pend[-1:] // bm).astype(jnp.int32)
    return gates, tok_sorted, pos, block_expert, n_active


def kernel(x, c, ctx, c_ctx, mod_w, mod_b, norm_g, w_in, hgrn_lb, hgrn_norm, ret_decay, ret_norm, da_lambda,
           da_subln, w_out, router_w, router_b, w1, b1, w2, b2):
    B, S, D = x.shape
    C = ctx.shape[1]
    depth = mod_w.shape[0]
    bm = 256

    pos = jnp.arange(S)
    cr, sr = _rope_tables(pos // GRID_W, DA_HEAD_DIM // 2)
    cc, sc_ = _rope_tables(pos % GRID_W, DA_HEAD_DIM // 2)
    cs, ss = _rope_tables(pos, HDIM)
    cosr = jnp.tile(jnp.concatenate([cs, cs], -1), (1, HEADS))
    sinr = jnp.tile(jnp.concatenate([ss, ss], -1), (1, HEADS))
    cosd = jnp.tile(jnp.concatenate([cr, cr, cc, cc], -1), (1, DA_MAPS))
    sind = jnp.tile(jnp.concatenate([sr, sr, sc_, sc_], -1), (1, DA_MAPS))
    ones_r, zeros_r = jnp.ones((C, REC_W), F32), jnp.zeros((C, REC_W), F32)
    ones_d, zeros_d = jnp.ones((C, DA_W), F32), jnp.zeros((C, DA_W), F32)

    lb_cum = jnp.cumsum(jax.nn.softmax(hgrn_lb.astype(F32), axis=0), axis=0)
    lower = lb_cum - lb_cum[0:1]

    cvec = jnp.zeros((8, D), F32).at[:B].set(c).at[B].set(c_ctx)
    mods = _modulation(cvec, mod_w, mod_b)

    head_id = jnp.arange(REC_W) // HDIM
    seg = (head_id[:, None] == head_id[None, :]).astype(BF16)

    xf = x.reshape(B * S, D)
    xc = ctx.reshape(B * C, D)
    zero_state = jnp.zeros((B, 2, HEADS, HDIM, HDIM), F32)

    for layer in range(depth):
        need_ctx = layer < depth - 1
        lam_init = 0.8 - 0.6 * math.exp(-0.3 * layer)
        m6 = mods[layer].reshape(8, 6, D)
        lat = lambda k: m6[:B, k][:, None, :]
        cxm = lambda k: jnp.broadcast_to(m6[B, k][None, None, :], (B, 1, D))
        ng = norm_g[layer]
        lb = lower[layer]
        lbc = jnp.zeros((8, REC_W), F32).at[0].set(jnp.log(lb)).at[1].set(jnp.log1p(-lb)).at[2].set(1.0 - lb)
        w_all = _prep_w_in(w_in[layer])
        log_gamma = jnp.log1p(-jnp.exp2(-ret_decay[layer].astype(F32)))
        lgrow = jnp.zeros((8, REC_W), F32).at[:2].set(jnp.repeat(log_gamma, HDIM, axis=1))
        lamv = da_lambda[layer].astype(F32)
        lam = (jnp.exp(jnp.sum(lamv[0] * lamv[1])) - jnp.exp(jnp.sum(lamv[2] * lamv[3])) + lam_init).reshape(1)
        hn = jnp.tile(hgrn_norm[layer], HEADS)[None, :]
        rn = jnp.tile(ret_norm[layer], HEADS)[None, :]
        dn = jnp.tile(jnp.concatenate([da_subln[layer], jnp.zeros((128 - DA_VDIM,), F32)]), DA_HEADS)[:, None]
        wo = w_out[layer]
        wo_da = jnp.pad(wo[2 * REC_W:].reshape(DA_HEADS, DA_VDIM, D), ((0, 0), (0, 128 - DA_VDIM), (0, 0)))
        w_out_b = jnp.concatenate([wo[:2 * REC_W], wo_da.reshape(DA_X, D)], axis=0).astype(BF16)
        rw = router_w[layer]
        rb = router_b[layer][None, :]

        (hq, hv, hg, kf, lff, kb, lfb, rq, rk, rv, rg, dq, dk, dvx) = _inproj(
            xf, S, ng[0:1], lat(1), lat(0), w_all, lbc, cosr, sinr, cosd, sind)
        (hq_c, hv_c, hg_c, kf_c, lff_c, kb_c, lfb_c, rq_c, rk_c, rv_c, rg_c, dq_c, dk_c, dvx_c) = _inproj(
            xc, C, ng[0:1], cxm(1), cxm(0), w_all, lbc, ones_r, zeros_r, ones_d, zeros_d)

        hof_c, hob_c, hs = _hgrn_scan(hq_c, hv_c, kf_c, lff_c, kb_c, lfb_c,
                                      jnp.zeros((B, 2, REC_W // 128, 128, 128), F32), seg, B, C)
        hof, hob, _ = _hgrn_scan(hq, hv, kf, lff, kb, lfb, hs, seg, B, S)
        rof_c, rob_c, rs = _ret_scan(rq_c, rk_c, rv_c, zero_state, log_gamma, lgrow, B, C)
        rof, rob, _ = _ret_scan(rq, rk, rv, rs, log_gamma, lgrow, B, S)
        out_scale = 1.0 - lam_init
        da = _diff_attention(lam, dq, dk_c, dvx_c, dk, dvx, dn, B, S, C, S, out_scale, False)

        x1, tok, logits = _outproj(xf, S, (hof, hob, hg, rof, rob, rg, da), seg, w_out_b, hn, rn,
                                   ng[1:2], ng[2:3], lat(2), lat(4), lat(3), rw, rb)
        if need_ctx:
            da_c = _diff_attention(lam, dq_c, dk_c, dvx_c, None, None, dn, B, C, C, C, out_scale, True)
            xc1, tok_c, logits_c = _outproj(xc, C, (hof_c, hob_c, hg_c, rof_c, rob_c, rg_c, da_c), seg, w_out_b,
                                            hn, rn, ng[1:2], ng[2:3], cxm(2), cxm(4), cxm(3), rw, rb)
            tok = jnp.concatenate([tok, tok_c], axis=0)
            logits = jnp.concatenate([logits, logits_c], axis=0)

        n_tok = tok.shape[0]
        gates, tok_sorted, posn, block_expert, n_active = _routing(logits, bm)
        w1g_b, w1l_b, w2_b = _expert_prep(w1, w2, layer)
        b1g = b1[layer][:, None, 0::2]
        b1l = b1[layer][:, None, 1::2]
        b2l = b2[layer][:, None, :]
        xg = jnp.take(tok, tok_sorted, axis=0)
        yg = _expert_ffn(block_expert, n_active, xg, w1g_b, w1l_b, b1g, b1l, w2_b, b2l, bm)
        pos2 = posn.reshape(n_tok, TOP_K)

        def expert_rows(lo, hi):
            return [jnp.take(yg, pos2[lo:hi, k], axis=0) for k in range(TOP_K)]

        xf = _ffn_residual(x1, expert_rows(0, B * S), gates[:B * S], S, ng[3:4], lat(5))
        if need_ctx:
            xc = _ffn_residual(xc1, expert_rows(B * S, n_tok), gates[B * S:], C, ng[3:4], cxm(5))
    return xf.reshape(B, S, D)
```

```python
import functools
import math

import jax
import jax.numpy as jnp
from jax import lax
from jax.experimental import pallas as pl
from jax.experimental.pallas import tpu as pltpu

F32 = jnp.float32
BF16 = jnp.bfloat16

GRID_W = 64
HEADS = 6
HDIM = 64
REC_W = HEADS * HDIM
DA_HEADS = 4
DA_HEAD_DIM = 32
DA_MAPS = 2 * DA_HEADS
DA_W = DA_MAPS * DA_HEAD_DIM
DA_VDIM = 2 * DA_HEAD_DIM
DA_X = DA_HEADS * 128
_Q_SCALE = DA_HEAD_DIM ** -0.5 * math.log2(math.e)
ROPE_BASE = 10000.0
N_EXPERTS = 32
TOP_K = 4
SWIGLU_ALPHA = 1.702
SWIGLU_LIMIT = 7.0
NORM_EPS = 1e-6

SUB = 16
NEG_BIG = -1e30
VMEM_LIMIT = 56 * 1024 * 1024

_C_HQ, _C_HFF, _C_HFB, _C_HI, _C_HG = 0, 384, 768, 1152, 1536
_C_RQ, _C_RK, _C_RV, _C_RG = 1920, 2304, 2688, 3072
_C_DQ, _C_DK, _C_DV = 3456, 3712, 3968
_C_RQR, _C_RKR, _C_DQR, _C_DKR = 4224, 4608, 4992, 5248
_W_ALL = 5504


def _dot(a, b):
    return jnp.dot(a, b, preferred_element_type=F32)


def _dot_nt(a, b):
    return lax.dot_general(a, b, (((1,), (1,)), ((), ())), preferred_element_type=F32)


def _dot_tn(a, b):
    return lax.dot_general(a, b, (((0,), (0,)), ((), ())), preferred_element_type=F32)


def _sigmoid(x):
    return 1.0 / (1.0 + jnp.exp(-x))


def _rms(x, g):
    ms = jnp.mean(x * x, axis=-1, keepdims=True)
    return x * lax.rsqrt(ms + NORM_EPS) * g


def _params(*sem):
    return pltpu.CompilerParams(dimension_semantics=sem, vmem_limit_bytes=VMEM_LIMIT)


def _mod_kernel(c_ref, w_ref, b_ref, o_ref):
    c = c_ref[...]
    o_ref[0] = _dot(c * _sigmoid(c), w_ref[0]) + b_ref[0]


def _modulation(cvec, mod_w, mod_b):
    depth, d, n = mod_w.shape
    tn = 1536
    return pl.pallas_call(
        _mod_kernel,
        grid=(depth, n // tn),
        in_specs=[
            pl.BlockSpec((8, d), lambda l, j: (0, 0)),
            pl.BlockSpec((1, d, tn), lambda l, j: (l, 0, j)),
            pl.BlockSpec((1, 1, tn), lambda l, j: (l, 0, j)),
        ],
        out_specs=pl.BlockSpec((1, 8, tn), lambda l, j: (l, 0, j)),
        out_shape=jax.ShapeDtypeStruct((depth, 8, n), F32),
        compiler_params=_params("arbitrary", "arbitrary"),
        name="modulation",
    )(cvec, mod_w, mod_b.reshape(depth, 1, n))


def _inproj_kernel(x_ref, g_ref, sc_ref, sh_ref, w_ref, lbc_ref, cosr_ref, sinr_ref, cosd_ref, sind_ref,
                   hq_ref, hv_ref, hg_ref, kf_ref, lff_ref, kb_ref, lfb_ref,
                   rq_ref, rk_ref, rv_ref, rg_ref, dq_ref, dk_ref, dvt_ref):
    x = x_ref[...]
    h = _rms(x, g_ref[...]) * (1.0 + sc_ref[0]) + sh_ref[0]
    hb = h.astype(BF16)

    def proj(c0, n):
        return _dot(hb, w_ref[:, c0:c0 + n])

    hq_ref[...] = proj(_C_HQ, REC_W).astype(BF16)
    hv_ref[...] = proj(_C_HI, REC_W).astype(BF16)
    hg_ref[...] = proj(_C_HG, REC_W).astype(BF16)

    log_lb = lbc_ref[0:1, :]
    log_1m = lbc_ref[1:2, :]
    one_m = lbc_ref[2:3, :]

    def gates(z):
        log_sig = jnp.minimum(z, 0.0) - jnp.log1p(jnp.exp(-jnp.abs(z)))
        t = log_1m + log_sig
        m = jnp.maximum(log_lb, t)
        logf = m + jnp.log1p(jnp.exp(-jnp.abs(log_lb - t)))
        return one_m / (1.0 + jnp.exp(z)), logf

    k, lf = gates(proj(_C_HFF, REC_W))
    kf_ref[...] = k.astype(BF16)
    lff_ref[...] = lf
    k, lf = gates(proj(_C_HFB, REC_W))
    kb_ref[...] = k.astype(BF16)
    lfb_ref[...] = lf

    cosr = jnp.concatenate([cosr_ref[...]] * (REC_W // 128), axis=1)
    sinr = jnp.concatenate([sinr_ref[...]] * (REC_W // 128), axis=1)
    rq_ref[...] = (proj(_C_RQ, REC_W) * cosr + proj(_C_RQR, REC_W) * sinr).astype(BF16)
    rk_ref[...] = ((proj(_C_RK, REC_W) * cosr + proj(_C_RKR, REC_W) * sinr) * (HDIM ** -0.5)).astype(BF16)
    rv_ref[...] = proj(_C_RV, REC_W).astype(BF16)
    rg_ref[...] = proj(_C_RG, REC_W).astype(BF16)

    cosd = jnp.concatenate([cosd_ref[...]] * (DA_W // 128), axis=1)
    sind = jnp.concatenate([sind_ref[...]] * (DA_W // 128), axis=1)
    dq = ((proj(_C_DQ, DA_W) * cosd + proj(_C_DQR, DA_W) * sind) * _Q_SCALE).astype(BF16)
    dk = (proj(_C_DK, DA_W) * cosd + proj(_C_DKR, DA_W) * sind).astype(BF16)
    dv_t = proj(_C_DV, DA_W).T.astype(BF16)
    tm = dq.shape[0]
    zero_col = jnp.zeros((tm, 64), BF16)
    sub = lax.broadcasted_iota(jnp.int32, (64, tm), 0)
    ones_row = jnp.where(sub == 0, 1.0, 0.0).astype(BF16)
    for h_ in range(DA_HEADS):
        lo, mid, hi_ = 128 * h_, 128 * h_ + 64, 128 * h_ + 128
        dq_ref[:, lo:mid] = dq[:, 64 * h_:64 * h_ + 64]
        dq_ref[:, mid:hi_] = zero_col
        dk_ref[:, lo:mid] = dk[:, 64 * h_:64 * h_ + 64]
        dk_ref[:, mid:hi_] = zero_col
        dvt_ref[0, lo:mid, :] = dv_t[64 * h_:64 * h_ + 64, :]
        dvt_ref[0, mid:hi_, :] = ones_row


def _inproj(xf, seq, gnorm, sc, sh, w_all, lbc, cosr, sinr, cosd, sind):
    rows, d = xf.shape
    tm = min(256, seq)
    nb = seq // tm
    row_spec = lambda w: pl.BlockSpec((tm, w), lambda i: (i, 0))
    tab_spec = lambda w: pl.BlockSpec((tm, w), lambda i: (i % nb, 0))
    mod_spec = pl.BlockSpec((1, 1, d), lambda i: (i // nb, 0, 0))
    widths = [REC_W] * 11 + [DA_X, DA_X]
    dtypes = [BF16, BF16, BF16, BF16, F32, BF16, F32, BF16, BF16, BF16, BF16, BF16, BF16]
    out_specs = [row_spec(w) for w in widths] + [pl.BlockSpec((1, DA_X, tm), lambda i: (i // nb, 0, i % nb))]
    out_shape = ([jax.ShapeDtypeStruct((rows, w), dt) for w, dt in zip(widths, dtypes)]
                 + [jax.ShapeDtypeStruct((rows // seq, DA_X, seq), BF16)])
    return pl.pallas_call(
        _inproj_kernel,
        grid=(rows // tm,),
        in_specs=[
            row_spec(d),
            pl.BlockSpec((1, d), lambda i: (0, 0)),
            mod_spec, mod_spec,
            pl.BlockSpec((d, _W_ALL), lambda i: (0, 0)),
            pl.BlockSpec((8, REC_W), lambda i: (0, 0)),
            tab_spec(128), tab_spec(128), tab_spec(128), tab_spec(128),
        ],
        out_specs=out_specs,
        out_shape=out_shape,
        compiler_params=_params("arbitrary"),
        name="inproj",
    )(xf, gnorm, sc, sh, w_all, lbc, cosr, sinr, cosd, sind)


def _hgrn_dir(q, k, v, lf, seg, st_ref, inter_ref, intra_ref, reverse):
    tt = q.shape[0]
    nblk = tt // SUB
    row = lax.broadcasted_iota(jnp.int32, (tt, tt), 0)
    col = lax.broadcasted_iota(jnp.int32, (tt, tt), 1)
    same = (row // SUB) == (col // SUB)
    tri = (col >= row) if reverse else (col <= row)
    l_all = jnp.where(same, 1.0, 0.0)
    l_cum = jnp.where(tri, l_all, 0.0).astype(BF16)
    l_all = l_all.astype(BF16)
    hi = lf.astype(BF16)
    r1 = lf - hi.astype(F32)
    mid = r1.astype(BF16)
    lo = (r1 - mid.astype(F32)).astype(BF16)
    b = _dot(l_cum, hi) + _dot(l_cum, mid) + _dot(l_cum, lo)
    btot = _dot(l_all, hi) + _dot(l_all, mid) + _dot(l_all, lo)

    half = lax.broadcasted_iota(jnp.int32, (nblk, 8, 128), 1)
    seg2 = seg[0:128, 0:128]
    for g in range(REC_W // 128):
        c = slice(128 * g, 128 * (g + 1))

        def halves(a):
            a4 = a[:, c].reshape(nblk, 2, 8, 128)
            return (a4[:, 1], a4[:, 0]) if reverse else (a4[:, 0], a4[:, 1])

        (q0, q1), (k0, k1), (v0, v1), (b0, b1) = halves(q), halves(k), halves(v), halves(b)

        def pairs(qa, ba, kb_, bb_, vb_, r, masked):
            sh = ((8 - r) % 8) if reverse else r
            if sh:
                kb_, bb_, vb_ = pltpu.roll(kb_, sh, 1), pltpu.roll(bb_, sh, 1), pltpu.roll(vb_, sh, 1)
            diff = ba - bb_
            if masked and r:
                diff = jnp.where((half <= 7 - r) if reverse else (half >= r), diff, NEG_BIG)
            term = (qa * kb_ * jnp.exp(diff)).reshape(nblk * 8, 128).astype(BF16)
            return _dot(term, seg2).reshape(nblk, 8, 128) * vb_

        o0 = pairs(q0, b0, k0, b0, v0, 0, True)
        o1 = pairs(q1, b1, k1, b1, v1, 0, True) + pairs(q1, b1, k0, b0, v0, 0, False)
        for r in range(1, 8):
            o0 = o0 + pairs(q0, b0, k0, b0, v0, r, True)
            o1 = o1 + pairs(q1, b1, k1, b1, v1, r, True) + pairs(q1, b1, k0, b0, v0, r, False)
        first, second = (o1, o0) if reverse else (o0, o1)
        intra_ref[:, 0, :, c] = first
        intra_ref[:, 1, :, c] = second

    qe = (q * jnp.exp(b)).astype(BF16)
    kd = (k * jnp.exp(btot - b)).astype(BF16)
    dec = jnp.exp(btot)
    v_t = v.T
    tok_blk = lax.broadcasted_iota(jnp.int32, (128, tt), 1) // SUB
    hr = lax.broadcasted_iota(jnp.int32, (128, 128), 0) // HDIM
    hc = lax.broadcasted_iota(jnp.int32, (128, 128), 1) // HDIM
    same_head = hr == hc
    order = range(nblk - 1, -1, -1) if reverse else range(nblk)
    for g in range(REC_W // 128):
        c = slice(128 * g, 128 * (g + 1))
        v_tg = v_t[c, :]
        v_stack = jnp.concatenate([jnp.where(tok_blk == j, v_tg, 0.0).astype(BF16) for j in range(nblk)], axis=0)
        upd = _dot(v_stack, kd[:, c])
        st = st_ref[g]
        for j in order:
            r0 = j * SUB
            inter_ref[r0:r0 + SUB, c] = _dot_nt(qe[r0:r0 + SUB, c], st.astype(BF16))
            st = st * dec[r0:r0 + 1, c] + jnp.where(same_head, upd[128 * j:128 * (j + 1)], 0.0)
        st_ref[g] = st
    return intra_ref[...].reshape(tt, REC_W) + inter_ref[...]


def _hgrn_kernel(qf_ref, vf_ref, kf_ref, lff_ref, qb_ref, vb_ref, kb_ref, lfb_ref, s0_ref, seg_ref,
                 of_ref, ob_ref, sout_ref, st_ref, inter_ref, intra_ref):
    i = pl.program_id(1)

    @pl.when(i == 0)
    def _():
        st_ref[...] = s0_ref[0]

    seg = seg_ref[...]
    of_ref[...] = _hgrn_dir(qf_ref[...].astype(F32), kf_ref[...].astype(F32), vf_ref[...].astype(F32),
                            lff_ref[...], seg, st_ref.at[0], inter_ref, intra_ref, False).astype(BF16)
    ob_ref[...] = _hgrn_dir(qb_ref[...].astype(F32), kb_ref[...].astype(F32), vb_ref[...].astype(F32),
                            lfb_ref[...], seg, st_ref.at[1], inter_ref, intra_ref, True).astype(BF16)

    @pl.when(i == pl.num_programs(1) - 1)
    def _():
        sout_ref[0] = st_ref[...]


def _hgrn_scan(hq, hv, kf, lff, kb, lfb, s0, seg, batch, seq):
    tt = min(256, seq)
    n = seq // tt
    fwd = pl.BlockSpec((tt, REC_W), lambda b, i: (b * n + i, 0))
    bwd = pl.BlockSpec((tt, REC_W), lambda b, i: (b * n + n - 1 - i, 0))
    st_spec = pl.BlockSpec((1, 2, REC_W // 128, 128, 128), lambda b, i: (b, 0, 0, 0, 0))
    rows = batch * seq
    return pl.pallas_call(
        _hgrn_kernel,
        grid=(batch, n),
        in_specs=[fwd, fwd, fwd, fwd, bwd, bwd, bwd, bwd, st_spec,
                  pl.BlockSpec((REC_W, REC_W), lambda b, i: (0, 0))],
        out_specs=[fwd, bwd, st_spec],
        out_shape=[jax.ShapeDtypeStruct((rows, REC_W), BF16), jax.ShapeDtypeStruct((rows, REC_W), BF16),
                   jax.ShapeDtypeStruct(s0.shape, F32)],
        scratch_shapes=[pltpu.VMEM((2, REC_W // 128, 128, 128), F32), pltpu.VMEM((tt, REC_W), F32),
                        pltpu.VMEM((tt // SUB, 2, 8, REC_W), F32)],
        compiler_params=_params("arbitrary", "arbitrary"),
        name="hgrn_scan",
    )(hq, hv, kf, lff, hq, hv, kb, lfb, s0, seg)


def _ret_kernel(lg_ref, qf_ref, kf_ref, vf_ref, qb_ref, kb_ref, vb_ref, s0_ref, lgrow_ref,
                of_ref, ob_ref, sout_ref, st_ref, dmask_ref, tab_ref, o_scr):
    i = pl.program_id(1)
    tt = qf_ref.shape[0]

    @pl.when(i == 0)
    def _():
        st_ref[...] = s0_ref[0]
        row = lax.broadcasted_iota(jnp.int32, (tt, tt), 0)
        col = lax.broadcasted_iota(jnp.int32, (tt, tt), 1)
        dist = (row - col).astype(F32)
        for h in range(HEADS):
            fw = jnp.where(dist >= 0, jnp.exp(dist * lg_ref[0, h]), 0.0)
            bw = jnp.where(dist <= 0, jnp.exp(-dist * lg_ref[1, h]), 0.0)
            dmask_ref[h] = fw + bw
        pos = lax.broadcasted_iota(jnp.int32, (tt, REC_W), 0).astype(F32)
        lgf = lgrow_ref[0:1, :]
        lgb = lgrow_ref[1:2, :]
        tab_ref[0] = jnp.exp((pos + 1.0) * lgf)
        tab_ref[1] = jnp.exp((tt - 1.0 - pos) * lgf)
        tab_ref[2] = jnp.exp((tt - pos) * lgb)
        tab_ref[3] = jnp.exp(pos * lgb)

    tile_f = jnp.exp(tt * lgrow_ref[0:1, :])
    tile_b = jnp.exp(tt * lgrow_ref[1:2, :])

    q = qf_ref[...]
    k = kf_ref[...]
    v = vf_ref[...]
    qe = (q.astype(F32) * tab_ref[0]).astype(BF16)
    ke = (k.astype(F32) * tab_ref[1]).astype(BF16)
    for h in range(HEADS):
        c = slice(h * HDIM, (h + 1) * HDIM)
        s = _dot_nt(q[:, c], k[:, c]) * dmask_ref[h]
        st = st_ref[0, h]
        o_scr[:, c] = _dot(s.astype(BF16), v[:, c]) + _dot(qe[:, c], st.astype(BF16))
        st_ref[0, h] = st * tile_f[:, c] + _dot_tn(ke[:, c], v[:, c])
    of_ref[...] = o_scr[...].astype(BF16)

    q = qb_ref[...]
    k = kb_ref[...]
    v = vb_ref[...]
    qe = (q.astype(F32) * tab_ref[2]).astype(BF16)
    ke = (k.astype(F32) * tab_ref[3]).astype(BF16)
    for h in range(HEADS):
        c = slice(h * HDIM, (h + 1) * HDIM)
        st = st_ref[1, h]
        o_scr[:, c] = _dot(qe[:, c], st.astype(BF16))
        st_ref[1, h] = st * tile_b[:, c] + _dot_tn(ke[:, c], v[:, c])
    ob_ref[...] = o_scr[...].astype(BF16)

    @pl.when(i == pl.num_programs(1) - 1)
    def _():
        sout_ref[0] = st_ref[...]


def _ret_scan(rq, rk, rv, s0, lg, lgrow, batch, seq):
    tt = min(256, seq)
    n = seq // tt
    fwd = pl.BlockSpec((tt, REC_W), lambda b, i: (b * n + i, 0))
    bwd = pl.BlockSpec((tt, REC_W), lambda b, i: (b * n + n - 1 - i, 0))
    st_spec = pl.BlockSpec((1, 2, HEADS, HDIM, HDIM), lambda b, i: (b, 0, 0, 0, 0))
    rows = batch * seq
    return pl.pallas_call(
        _ret_kernel,
        grid=(batch, n),
        in_specs=[pl.BlockSpec(memory_space=pltpu.SMEM), fwd, fwd, fwd, bwd, bwd, bwd, st_spec,
                  pl.BlockSpec((8, REC_W), lambda b, i: (0, 0))],
        out_specs=[fwd, bwd, st_spec],
        out_shape=[jax.ShapeDtypeStruct((rows, REC_W), BF16), jax.ShapeDtypeStruct((rows, REC_W), BF16),
                   jax.ShapeDtypeStruct(s0.shape, F32)],
        scratch_shapes=[pltpu.VMEM((2, HEADS, HDIM, HDIM), F32), pltpu.VMEM((HEADS, tt, tt), F32),
                        pltpu.VMEM((4, tt, REC_W), F32), pltpu.VMEM((tt, REC_W), F32)],
        compiler_params=_params("arbitrary", "arbitrary"),
        name="ret_scan",
    )(lg, rq, rk, rv, rq, rk, rv, s0, lgrow)


def _attn_kernel(lam_ref, q_ref, kc_ref, vct_ref, k_ref, vt_ref, gain_ref, o_ref, s_ref, *, with_latent, kchunk,
                 out_scale):
    tq = q_ref.shape[0]
    q = q_ref[...]
    lane = lax.broadcasted_iota(jnp.int32, (tq, 128), 1)
    zero = jnp.zeros_like(q)
    qs = jnp.concatenate([jnp.where(lane < DA_HEAD_DIM, q, zero),
                          jnp.where(lane >= DA_HEAD_DIM, q, zero)], axis=0)
    nchunks = (k_ref.shape[0] // kchunk) if with_latent else 0

    def scores(kblk):
        return _dot_nt(kblk, qs)

    def absorb(s, vtblk, m, acc):
        n = s.shape[0]
        part = jnp.max(s.reshape(n // 128, 128, 2 * tq), axis=0) if n > 128 else s
        m_new = jnp.maximum(m, jnp.max(part, axis=0, keepdims=True))
        p = jnp.exp2((s - m_new).astype(BF16))
        return m_new, jnp.exp2(m - m_new) * acc + _dot(vtblk, p)

    def latent_k(c):
        return k_ref[pl.ds(pl.multiple_of(c * kchunk, kchunk), kchunk), :]

    def latent_vt(c):
        return vt_ref[0, :, pl.ds(pl.multiple_of(c * kchunk, kchunk), kchunk)]

    m, acc = absorb(scores(kc_ref[...]), vct_ref[0], jnp.full((1, 2 * tq), NEG_BIG, F32),
                    jnp.zeros((128, 2 * tq), F32))
    if with_latent:
        def pair(c, carry, last):
            m_, acc_ = carry
            s_ref[1] = scores(latent_k(c + 1))
            m_, acc_ = absorb(s_ref[0], latent_vt(c), m_, acc_)
            if not last:
                s_ref[0] = scores(latent_k(c + 2))
            return absorb(s_ref[1], latent_vt(c + 1), m_, acc_)

        s_ref[0] = scores(latent_k(0))
        if nchunks > 2:
            m, acc = lax.fori_loop(0, nchunks // 2 - 1, lambda j, cr: pair(2 * j, cr, False), (m, acc))
        m, acc = pair(nchunks - 2, (m, acc), True)

    a1 = acc[:, :tq]
    a2 = acc[:, tq:]
    o = a1 / a1[DA_VDIM:DA_VDIM + 1, :] - lam_ref[0] * (a2 / a2[DA_VDIM:DA_VDIM + 1, :])
    row = lax.broadcasted_iota(jnp.int32, (128, tq), 0)
    o = jnp.where(row < DA_VDIM, o, 0.0)
    ms = jnp.sum(o * o, axis=0, keepdims=True) * (1.0 / DA_VDIM)
    o_ref[...] = (o * lax.rsqrt(ms + NORM_EPS) * gain_ref[...] * out_scale).T.astype(BF16)


def _diff_attention(lam, q, kc, vc, k, v, gain, batch, seq_q, ctx_len, seq_k, out_scale, q_is_ctx):
    tq = min(256, seq_q)
    nq = seq_q // tq
    with_latent = not q_is_ctx
    if with_latent:
        kchunk = min(2048, seq_k // 2)
        assert seq_k % (2 * kchunk) == 0
    else:
        k, v = kc, vc
        seq_k, kchunk = ctx_len, ctx_len
    kern = functools.partial(_attn_kernel, with_latent=with_latent, kchunk=kchunk, out_scale=out_scale)
    return pl.pallas_call(
        kern,
        grid=(batch, DA_HEADS, nq),
        in_specs=[
            pl.BlockSpec(memory_space=pltpu.SMEM),
            pl.BlockSpec((tq, 128), lambda b, h, i: (b * nq + i, h)),
            pl.BlockSpec((ctx_len, 128), lambda b, h, i: (b, h)),
            pl.BlockSpec((1, 128, ctx_len), lambda b, h, i: (b, h, 0)),
            pl.BlockSpec((seq_k, 128), lambda b, h, i: (b, h)),
            pl.BlockSpec((1, 128, seq_k), lambda b, h, i: (b, h, 0)),
            pl.BlockSpec((128, 1), lambda b, h, i: (h, 0)),
        ],
        out_specs=pl.BlockSpec((tq, 128), lambda b, h, i: (b * nq + i, h)),
        out_shape=jax.ShapeDtypeStruct((batch * seq_q, DA_X), BF16),
        scratch_shapes=[pltpu.VMEM((2, kchunk, 2 * tq), F32)],
        compiler_params=_params("arbitrary", "arbitrary", "arbitrary"),
        name="diff_attention",
    )(lam, q, kc, vc, k, v, gain)


def _outproj_kernel(x_ref, hof_ref, hob_ref, hg_ref, rof_ref, rob_ref, rg_ref, da_ref, seg_ref, w_ref,
                    hn_ref, rn_ref, n1_ref, n2_ref, g1_ref, sc2_ref, sh2_ref, rw_ref, rb_ref,
                    x1_ref, tok_ref, idx_ref, gate_ref):
    seg = seg_ref[...]

    def gated_head_norm(o, gain, gate):
        ms = _dot((o * o).astype(BF16), seg) * (1.0 / HDIM)
        return (o * lax.rsqrt(ms + NORM_EPS) * gain * (gate * _sigmoid(gate))).astype(BF16)

    a = gated_head_norm(hof_ref[...].astype(F32) + hob_ref[...].astype(F32), hn_ref[...], hg_ref[...].astype(F32))
    b = gated_head_norm(rof_ref[...].astype(F32) + rob_ref[...].astype(F32), rn_ref[...], rg_ref[...].astype(F32))
    y = (_dot(a, w_ref[0:REC_W, :]) + _dot(b, w_ref[REC_W:2 * REC_W, :])
         + _dot(da_ref[...], w_ref[2 * REC_W:2 * REC_W + DA_X, :]))
    x1 = x_ref[...] + g1_ref[0] * _rms(y, n1_ref[...])
    x1_ref[...] = x1
    tok = _rms(x1, n2_ref[...]) * (1.0 + sc2_ref[0]) + sh2_ref[0]
    tok_ref[...] = tok.astype(BF16)
    logits = _dot_nt(rw_ref[...], tok) + rb_ref[...]
    eid = lax.broadcasted_iota(jnp.int32, logits.shape, 0).astype(F32)
    vals, ids = [], []
    for _ in range(TOP_K):
        best = jnp.max(logits, axis=0, keepdims=True)
        first = jnp.min(jnp.where(logits == best, eid, float(N_EXPERTS)), axis=0, keepdims=True)
        vals.append(best)
        ids.append(first)
        logits = jnp.where(eid == first, -jnp.inf, logits)
    ex = [jnp.exp(v - vals[0]) for v in vals]
    total = ex[0] + ex[1] + ex[2] + ex[3]
    idx_ref[...] = jnp.concatenate(ids, axis=0).astype(jnp.int32)
    gate_ref[...] = jnp.concatenate(ex, axis=0) / total


def _outproj(xf, seq, streams, seg, w_out, hn, rn, n1, n2, g1, sc2, sh2, rw, rb):
    rows, d = xf.shape
    tm = min(256, seq)
    nb = seq // tm
    row_spec = lambda w: pl.BlockSpec((tm, w), lambda i: (i, 0))
    full = lambda a: pl.BlockSpec(a.shape, lambda i: (0,) * a.ndim)
    mod_spec = pl.BlockSpec((1, 1, d), lambda i: (i // nb, 0, 0))
    hof, hob, hg, rof, rob, rg, da = streams
    return pl.pallas_call(
        _outproj_kernel,
        grid=(rows // tm,),
        in_specs=[row_spec(d)] + [row_spec(REC_W)] * 6 + [row_spec(DA_X), full(seg), full(w_out),
                  full(hn), full(rn), full(n1), full(n2), mod_spec, mod_spec, mod_spec, full(rw), full(rb)],
        out_specs=[row_spec(d), row_spec(d), pl.BlockSpec((TOP_K, tm), lambda i: (0, i)),
                   pl.BlockSpec((TOP_K, tm), lambda i: (0, i))],
        out_shape=[jax.ShapeDtypeStruct((rows, d), F32), jax.ShapeDtypeStruct((rows, d), BF16),
                   jax.ShapeDtypeStruct((TOP_K, rows), jnp.int32), jax.ShapeDtypeStruct((TOP_K, rows), F32)],
        compiler_params=_params("arbitrary"),
        name="outproj",
    )(xf, hof, hob, hg, rof, rob, rg, da, seg, w_out, hn, rn, n1, n2, g1, sc2, sh2, rw, rb)


def _ffn_kernel(be_ref, na_ref, x_ref, w1_ref, w2_ref, b1g_ref, b1l_ref, b2_ref, pe_ref, po_ref, y_ref,
                w1g_s, w1l_s, w2_s):
    i = pl.program_id(0)
    active = i < na_ref[0]
    fresh = jnp.logical_or(i == 0, be_ref[i] != be_ref[jnp.maximum(i - 1, 0)])

    @pl.when(fresh)
    def _():
        n = w1_ref.shape[3]
        for c in range(n // 256):
            wb = w1_ref[0, 0, :, 256 * c:256 * (c + 1)].astype(BF16)
            w1g_s[:, 128 * c:128 * (c + 1)] = _dot(wb, pe_ref[...]).astype(BF16)
            w1l_s[:, 128 * c:128 * (c + 1)] = _dot(wb, po_ref[...]).astype(BF16)
        w2_s[...] = w2_ref[0, 0].astype(BF16)

    @pl.when(jnp.logical_not(active))
    def _():
        y_ref[...] = jnp.zeros(y_ref.shape, y_ref.dtype)

    @pl.when(active)
    def _():
        x = x_ref[...]
        glu = jnp.minimum(_dot(x, w1g_s[...]) + b1g_ref[0], SWIGLU_LIMIT)
        lin = jnp.clip(_dot(x, w1l_s[...]) + b1l_ref[0], -SWIGLU_LIMIT, SWIGLU_LIMIT)
        act = glu * _sigmoid(SWIGLU_ALPHA * glu) * (lin + 1.0)
        y_ref[...] = (_dot(act.astype(BF16), w2_s[...]) + b2_ref[0]).astype(y_ref.dtype)


def _expert_ffn(block_expert, n_active, xg, w1, w2, b1g, b1l, b2, layer, bm):
    p, d = xg.shape
    f = w2.shape[2]
    nblocks = p // bm
    src = jnp.arange(256)[:, None]
    dst = jnp.arange(128)[None, :]
    pe = (src == 2 * dst).astype(BF16)
    po = (src == 2 * dst + 1).astype(BF16)
    wspec = lambda s: pl.BlockSpec((1, 1) + s, lambda i, be, na: (layer, be[i], 0, 0))
    bspec = lambda s: pl.BlockSpec((1,) + s, lambda i, be, na: (be[i], 0, 0))
    sel = pl.BlockSpec((256, 128), lambda i, be, na: (0, 0))
    return pl.pallas_call(
        _ffn_kernel,
        grid_spec=pltpu.PrefetchScalarGridSpec(
            num_scalar_prefetch=2,
            grid=(nblocks,),
            in_specs=[pl.BlockSpec((bm, d), lambda i, be, na: (i, 0)),
                      wspec((d, 2 * f)), wspec((f, d)), bspec((1, f)), bspec((1, f)), bspec((1, d)), sel, sel],
            out_specs=pl.BlockSpec((bm, d), lambda i, be, na: (i, 0)),
            scratch_shapes=[pltpu.VMEM((d, f), BF16), pltpu.VMEM((d, f), BF16), pltpu.VMEM((f, d), BF16)],
        ),
        out_shape=jax.ShapeDtypeStruct((p, d), BF16),
        compiler_params=_params("arbitrary"),
        name="expert_ffn",
    )(block_expert, n_active, xg, w1, w2, b1g, b1l, b2, pe, po)


def _resid_kernel(x_ref, y0_ref, y1_ref, y2_ref, y3_ref, gate_ref, n_ref, g_ref, o_ref):
    gates = gate_ref[...]
    f = y0_ref[...].astype(F32) * gates[:, 0:1]
    for k, y_ref in enumerate((y1_ref, y2_ref, y3_ref), start=1):
        f = f + y_ref[...].astype(F32) * gates[:, k:k + 1]
    o_ref[...] = x_ref[...] + g_ref[0] * _rms(f, n_ref[...])


def _ffn_residual(xf, ys, gates, seq, n3, g2):
    rows, d = xf.shape
    tm = min(512, seq)
    nb = seq // tm
    row_spec = pl.BlockSpec((tm, d), lambda i: (i, 0))
    return pl.pallas_call(
        _resid_kernel,
        grid=(rows // tm,),
        in_specs=[row_spec] * 5 + [pl.BlockSpec((tm, TOP_K), lambda i: (i, 0)), pl.BlockSpec((1, d), lambda i: (0, 0)),
                                   pl.BlockSpec((1, 1, d), lambda i: (i // nb, 0, 0))],
        out_specs=row_spec,
        out_shape=jax.ShapeDtypeStruct((rows, d), F32),
        compiler_params=_params("arbitrary"),
        name="ffn_residual",
    )(xf, *ys, gates, n3, g2)


def _rope_tables(pos, dim):
    inv = 1.0 / (ROPE_BASE ** (jnp.arange(0, dim, 2, dtype=F32) / dim))
    ang = pos.astype(F32)[:, None] * inv[None, :]
    return jnp.cos(ang), jnp.sin(ang)


def _rot_cols(w, head_dim, halves):
    d, n = w.shape
    g = head_dim // halves
    w4 = w.reshape(d, n // g, 2, g // 2)
    return jnp.concatenate([-w4[:, :, 1], w4[:, :, 0]], axis=-1).reshape(d, n)


def _prep_w_in(w):
    rq, rk = w[:, _C_RQ:_C_RQ + REC_W], w[:, _C_RK:_C_RK + REC_W]
    dq, dk = w[:, _C_DQ:_C_DQ + DA_W], w[:, _C_DK:_C_DK + DA_W]
    return jnp.concatenate([w, _rot_cols(rq, HDIM, 1), _rot_cols(rk, HDIM, 1),
                            _rot_cols(dq, DA_HEAD_DIM, 2), _rot_cols(dk, DA_HEAD_DIM, 2)], axis=1).astype(BF16)


def _routing(top_idx_t, bm):
    n = top_idx_t.shape[1]
    a = n * TOP_K
    flat_e = top_idx_t.T.reshape(a)
    sorted_e, order = lax.sort_key_val(flat_e, jnp.arange(a, dtype=jnp.int32))
    experts = jnp.arange(N_EXPERTS + 1, dtype=jnp.int32)
    bounds = jnp.sum((flat_e[None, :] < experts[:, None]).astype(jnp.int32), axis=1)
    start, counts = bounds[:-1], bounds[1:] - bounds[:-1]
    padded = (counts + bm - 1) // bm * bm
    pend = jnp.cumsum(padded)
    shift = pend - padded - start
    dest = jnp.arange(a, dtype=jnp.int32) + shift[sorted_e]
    _, pos = lax.sort_key_val(order, dest)
    nblocks = -(-(a + N_EXPERTS * (bm - 1)) // bm)
    block_row = jnp.arange(nblocks, dtype=jnp.int32) * bm
    block_expert = jnp.minimum(jnp.sum((pend[None, :] <= block_row[:, None]).astype(jnp.int32), axis=1),
                               N_EXPERTS - 1)
    row_e = jnp.repeat(block_expert, bm)
    rank = jnp.arange(nblocks * bm, dtype=jnp.int32) - shift[row_e]
    valid = rank < bounds[row_e + 1]
    tok_sorted = jnp.where(valid, order[jnp.clip(rank, 0, a - 1)] // TOP_K, 0)
    n_active = (pend[-1:] // bm).astype(jnp.int32)
    return tok_sorted, pos, block_expert, n_active


def kernel(x, c, ctx, c_ctx, mod_w, mod_b, norm_g, w_in, hgrn_lb, hgrn_norm, ret_decay, ret_norm, da_lambda,
           da_subln, w_out, router_w, router_b, w1, b1, w2, b2):
    B, S, D = x.shape
    C = ctx.shape[1]
    depth = mod_w.shape[0]
    bm = 256

    pos = jnp.arange(S)
    cr, sr = _rope_tables(pos // GRID_W, DA_HEAD_DIM // 2)
    cc, sc_ = _rope_tables(pos % GRID_W, DA_HEAD_DIM // 2)
    cs, ss = _rope_tables(pos, HDIM)
    cosr = jnp.tile(jnp.concatenate([cs, cs], -1), (1, 128 // HDIM))
    sinr = jnp.tile(jnp.concatenate([ss, ss], -1), (1, 128 // HDIM))
    cosd = jnp.tile(jnp.concatenate([cr, cr, cc, cc], -1), (1, 128 // DA_HEAD_DIM))
    sind = jnp.tile(jnp.concatenate([sr, sr, sc_, sc_], -1), (1, 128 // DA_HEAD_DIM))
    ones_r, zeros_r = jnp.ones((C, 128), F32), jnp.zeros((C, 128), F32)
    ones_d, zeros_d = ones_r, zeros_r

    lb_cum = jnp.cumsum(jax.nn.softmax(hgrn_lb.astype(F32), axis=0), axis=0)
    lower = lb_cum - lb_cum[0:1]

    cvec = jnp.zeros((8, D), F32).at[:B].set(c).at[B].set(c_ctx)
    mods = _modulation(cvec, mod_w, mod_b)

    head_id = jnp.arange(REC_W) // HDIM
    seg = (head_id[:, None] == head_id[None, :]).astype(BF16)

    xf = x.reshape(B * S, D)
    xc = ctx.reshape(B * C, D)
    zero_state = jnp.zeros((B, 2, HEADS, HDIM, HDIM), F32)

    for layer in range(depth):
        need_ctx = layer < depth - 1
        lam_init = 0.8 - 0.6 * math.exp(-0.3 * layer)
        m6 = mods[layer].reshape(8, 6, D)
        lat = lambda k: m6[:B, k][:, None, :]
        cxm = lambda k: jnp.broadcast_to(m6[B, k][None, None, :], (B, 1, D))
        ng = norm_g[layer]
        lb = lower[layer]
        lbc = jnp.zeros((8, REC_W), F32).at[0].set(jnp.log(lb)).at[1].set(jnp.log1p(-lb)).at[2].set(1.0 - lb)
        w_all = _prep_w_in(w_in[layer])
        log_gamma = jnp.log1p(-jnp.exp2(-ret_decay[layer].astype(F32)))
        lgrow = jnp.zeros((8, REC_W), F32).at[:2].set(jnp.repeat(log_gamma, HDIM, axis=1))
        lamv = da_lambda[layer].astype(F32)
        lam = (jnp.exp(jnp.sum(lamv[0] * lamv[1])) - jnp.exp(jnp.sum(lamv[2] * lamv[3])) + lam_init).reshape(1)
        hn = jnp.tile(hgrn_norm[layer], HEADS)[None, :]
        rn = jnp.tile(ret_norm[layer], HEADS)[None, :]
        dn = jnp.tile(jnp.concatenate([da_subln[layer], jnp.zeros((128 - DA_VDIM,), F32)]), DA_HEADS)[:, None]
        wo = w_out[layer]
        wo_da = jnp.pad(wo[2 * REC_W:].reshape(DA_HEADS, DA_VDIM, D), ((0, 0), (0, 128 - DA_VDIM), (0, 0)))
        w_out_b = jnp.concatenate([wo[:2 * REC_W], wo_da.reshape(DA_X, D)], axis=0).astype(BF16)
        rw = router_w[layer].T
        rb = router_b[layer][:, None]

        (hq, hv, hg, kf, lff, kb, lfb, rq, rk, rv, rg, dq, dk, dvx) = _inproj(
            xf, S, ng[0:1], lat(1), lat(0), w_all, lbc, cosr, sinr, cosd, sind)
        (hq_c, hv_c, hg_c, kf_c, lff_c, kb_c, lfb_c, rq_c, rk_c, rv_c, rg_c, dq_c, dk_c, dvx_c) = _inproj(
            xc, C, ng[0:1], cxm(1), cxm(0), w_all, lbc, ones_r, zeros_r, ones_d, zeros_d)

        hof_c, hob_c, hs = _hgrn_scan(hq_c, hv_c, kf_c, lff_c, kb_c, lfb_c,
                                      jnp.zeros((B, 2, REC_W // 128, 128, 128), F32), seg, B, C)
        hof, hob, _ = _hgrn_scan(hq, hv, kf, lff, kb, lfb, hs, seg, B, S)
        rof_c, rob_c, rs = _ret_scan(rq_c, rk_c, rv_c, zero_state, log_gamma, lgrow, B, C)
        rof, rob, _ = _ret_scan(rq, rk, rv, rs, log_gamma, lgrow, B, S)
        out_scale = 1.0 - lam_init
        da = _diff_attention(lam, dq, dk_c, dvx_c, dk, dvx, dn, B, S, C, S, out_scale, False)

        x1, tok, top_idx_t, gates_t = _outproj(xf, S, (hof, hob, hg, rof, rob, rg, da), seg, w_out_b, hn, rn,
                                               ng[1:2], ng[2:3], lat(2), lat(4), lat(3), rw, rb)
        if need_ctx:
            da_c = _diff_attention(lam, dq_c, dk_c, dvx_c, None, None, dn, B, C, C, C, out_scale, True)
            xc1, tok_c, top_idx_c, gates_c = _outproj(xc, C, (hof_c, hob_c, hg_c, rof_c, rob_c, rg_c, da_c), seg,
                                                      w_out_b, hn, rn, ng[1:2], ng[2:3], cxm(2), cxm(4), cxm(3),
                                                      rw, rb)
            tok = jnp.concatenate([tok, tok_c], axis=0)
            top_idx_t = jnp.concatenate([top_idx_t, top_idx_c], axis=1)
            gates_t = jnp.concatenate([gates_t, gates_c], axis=1)

        n_tok = tok.shape[0]
        gates = gates_t.T
        tok_sorted, posn, block_expert, n_active = _routing(top_idx_t, bm)
        b1g = b1[layer][:, None, 0::2]
        b1l = b1[layer][:, None, 1::2]
        b2l = b2[layer][:, None, :]
        xg = jnp.take(tok, tok_sorted, axis=0)
        yg = _expert_ffn(block_expert, n_active, xg, w1, w2, b1g, b1l, b2l, layer, bm)
        pos2 = posn.reshape(n_tok, TOP_K)

        def expert_rows(lo, hi):
            return [jnp.take(yg, pos2[lo:hi, k], axis=0) for k in range(TOP_K)]

        xf = _ffn_residual(x1, expert_rows(0, B * S), gates[:B * S], S, ng[3:4], lat(5))
        if need_ctx:
            xc = _ffn_residual(xc1, expert_rows(B * S, n_tok), gates[B * S:], C, ng[3:4], cxm(5))
    return xf.reshape(B, S, D)
```

```python
import functools
import math

import jax
import jax.numpy as jnp
from jax import lax
from jax.experimental import pallas as pl
from jax.experimental.pallas import tpu as pltpu

F32 = jnp.float32
BF16 = jnp.bfloat16

GRID_W = 64
HEADS = 6
HDIM = 64
REC_W = HEADS * HDIM
DA_HEADS = 4
DA_HEAD_DIM = 32
DA_MAPS = 2 * DA_HEADS
DA_W = DA_MAPS * DA_HEAD_DIM
DA_VDIM = 2 * DA_HEAD_DIM
DA_X = DA_HEADS * 128
_Q_SCALE = DA_HEAD_DIM ** -0.5 * math.log2(math.e)
ROPE_BASE = 10000.0
N_EXPERTS = 32
TOP_K = 4
SWIGLU_ALPHA = 1.702
SWIGLU_LIMIT = 7.0
NORM_EPS = 1e-6

SUB = 16
NEG_BIG = -1e30
VMEM_LIMIT = 56 * 1024 * 1024

_C_HQ, _C_HFF, _C_HFB, _C_HI, _C_HG = 0, 384, 768, 1152, 1536
_C_RQ, _C_RK, _C_RV, _C_RG = 1920, 2304, 2688, 3072
_C_DQ, _C_DK, _C_DV = 3456, 3712, 3968
_C_RQR, _C_RKR, _C_DQR, _C_DKR = 4224, 4608, 4992, 5248
_W_ALL = 5504


def _dot(a, b):
    return jnp.dot(a, b, preferred_element_type=F32)


def _dot_nt(a, b):
    return lax.dot_general(a, b, (((1,), (1,)), ((), ())), preferred_element_type=F32)


def _dot_tn(a, b):
    return lax.dot_general(a, b, (((0,), (0,)), ((), ())), preferred_element_type=F32)


def _sigmoid(x):
    return 1.0 / (1.0 + jnp.exp(-x))


def _rms(x, g):
    ms = jnp.mean(x * x, axis=-1, keepdims=True)
    return x * lax.rsqrt(ms + NORM_EPS) * g


def _params(*sem):
    return pltpu.CompilerParams(dimension_semantics=sem, vmem_limit_bytes=VMEM_LIMIT)


def _mod_kernel(c_ref, w_ref, b_ref, o_ref):
    c = c_ref[...]
    o_ref[0] = _dot(c * _sigmoid(c), w_ref[0]) + b_ref[0]


def _modulation(cvec, mod_w, mod_b):
    depth, d, n = mod_w.shape
    tn = 1536
    return pl.pallas_call(
        _mod_kernel,
        grid=(depth, n // tn),
        in_specs=[
            pl.BlockSpec((8, d), lambda l, j: (0, 0)),
            pl.BlockSpec((1, d, tn), lambda l, j: (l, 0, j)),
            pl.BlockSpec((1, 1, tn), lambda l, j: (l, 0, j)),
        ],
        out_specs=pl.BlockSpec((1, 8, tn), lambda l, j: (l, 0, j)),
        out_shape=jax.ShapeDtypeStruct((depth, 8, n), F32),
        compiler_params=_params("arbitrary", "arbitrary"),
        name="modulation",
    )(cvec, mod_w, mod_b.reshape(depth, 1, n))


def _inproj_kernel(x_ref, g_ref, sc_ref, sh_ref, w_ref, lbc_ref, cosr_ref, sinr_ref, cosd_ref, sind_ref,
                   hq_ref, hv_ref, hg_ref, kf_ref, lff_ref, kb_ref, lfb_ref,
                   rq_ref, rk_ref, rv_ref, rg_ref, dq_ref, dk_ref, dvt_ref):
    x = x_ref[...]
    h = _rms(x, g_ref[...]) * (1.0 + sc_ref[0]) + sh_ref[0]
    hb = h.astype(BF16)

    def proj(c0, n):
        return _dot(hb, w_ref[:, c0:c0 + n])

    hq_ref[...] = proj(_C_HQ, REC_W).astype(BF16)
    hv_ref[...] = proj(_C_HI, REC_W).astype(BF16)
    hg_ref[...] = proj(_C_HG, REC_W).astype(BF16)

    log_lb = lbc_ref[0:1, :]
    log_1m = lbc_ref[1:2, :]
    one_m = lbc_ref[2:3, :]

    def gates(z):
        log_sig = jnp.minimum(z, 0.0) - jnp.log1p(jnp.exp(-jnp.abs(z)))
        t = log_1m + log_sig
        m = jnp.maximum(log_lb, t)
        logf = m + jnp.log1p(jnp.exp(-jnp.abs(log_lb - t)))
        return one_m / (1.0 + jnp.exp(z)), logf

    k, lf = gates(proj(_C_HFF, REC_W))
    kf_ref[...] = k.astype(BF16)
    lff_ref[...] = lf
    k, lf = gates(proj(_C_HFB, REC_W))
    kb_ref[...] = k.astype(BF16)
    lfb_ref[...] = lf

    cosr = jnp.concatenate([cosr_ref[...]] * (REC_W // 128), axis=1)
    sinr = jnp.concatenate([sinr_ref[...]] * (REC_W // 128), axis=1)
    rq_ref[...] = (proj(_C_RQ, REC_W) * cosr + proj(_C_RQR, REC_W) * sinr).astype(BF16)
    rk_ref[...] = ((proj(_C_RK, REC_W) * cosr + proj(_C_RKR, REC_W) * sinr) * (HDIM ** -0.5)).astype(BF16)
    rv_ref[...] = proj(_C_RV, REC_W).astype(BF16)
    rg_ref[...] = proj(_C_RG, REC_W).astype(BF16)

    cosd = jnp.concatenate([cosd_ref[...]] * (DA_W // 128), axis=1)
    sind = jnp.concatenate([sind_ref[...]] * (DA_W // 128), axis=1)
    dq = ((proj(_C_DQ, DA_W) * cosd + proj(_C_DQR, DA_W) * sind) * _Q_SCALE).astype(BF16)
    dk = (proj(_C_DK, DA_W) * cosd + proj(_C_DKR, DA_W) * sind).astype(BF16)
    dv_t = proj(_C_DV, DA_W).T.astype(BF16)
    tm = dq.shape[0]
    zero_col = jnp.zeros((tm, 64), BF16)
    sub = lax.broadcasted_iota(jnp.int32, (64, tm), 0)
    ones_row = jnp.where(sub == 0, 1.0, 0.0).astype(BF16)
    for h_ in range(DA_HEADS):
        lo, mid, hi_ = 128 * h_, 128 * h_ + 64, 128 * h_ + 128
        dq_ref[:, lo:mid] = dq[:, 64 * h_:64 * h_ + 64]
        dq_ref[:, mid:hi_] = zero_col
        dk_ref[:, lo:mid] = dk[:, 64 * h_:64 * h_ + 64]
        dk_ref[:, mid:hi_] = zero_col
        dvt_ref[0, lo:mid, :] = dv_t[64 * h_:64 * h_ + 64, :]
        dvt_ref[0, mid:hi_, :] = ones_row


def _inproj(xf, seq, gnorm, sc, sh, w_all, lbc, cosr, sinr, cosd, sind):
    rows, d = xf.shape
    tm = min(256, seq)
    nb = seq // tm
    row_spec = lambda w: pl.BlockSpec((tm, w), lambda i: (i, 0))
    tab_spec = lambda w: pl.BlockSpec((tm, w), lambda i: (i % nb, 0))
    mod_spec = pl.BlockSpec((1, 1, d), lambda i: (i // nb, 0, 0))
    widths = [REC_W] * 11 + [DA_X, DA_X]
    dtypes = [BF16, BF16, BF16, BF16, F32, BF16, F32, BF16, BF16, BF16, BF16, BF16, BF16]
    out_specs = [row_spec(w) for w in widths] + [pl.BlockSpec((1, DA_X, tm), lambda i: (i // nb, 0, i % nb))]
    out_shape = ([jax.ShapeDtypeStruct((rows, w), dt) for w, dt in zip(widths, dtypes)]
                 + [jax.ShapeDtypeStruct((rows // seq, DA_X, seq), BF16)])
    return pl.pallas_call(
        _inproj_kernel,
        grid=(rows // tm,),
        in_specs=[
            row_spec(d),
            pl.BlockSpec((1, d), lambda i: (0, 0)),
            mod_spec, mod_spec,
            pl.BlockSpec((d, _W_ALL), lambda i: (0, 0)),
            pl.BlockSpec((8, REC_W), lambda i: (0, 0)),
            tab_spec(128), tab_spec(128), tab_spec(128), tab_spec(128),
        ],
        out_specs=out_specs,
        out_shape=out_shape,
        compiler_params=_params("arbitrary"),
        name="inproj",
    )(xf, gnorm, sc, sh, w_all, lbc, cosr, sinr, cosd, sind)


def _hgrn_dir(q, k, v, lf, seg, st_ref, inter_ref, intra_ref, reverse):
    tt = q.shape[0]
    nblk = tt // SUB
    row = lax.broadcasted_iota(jnp.int32, (tt, tt), 0)
    col = lax.broadcasted_iota(jnp.int32, (tt, tt), 1)
    same = (row // SUB) == (col // SUB)
    tri = (col >= row) if reverse else (col <= row)
    l_all = jnp.where(same, 1.0, 0.0)
    l_cum = jnp.where(tri, l_all, 0.0).astype(BF16)
    l_all = l_all.astype(BF16)
    hi = lf.astype(BF16)
    r1 = lf - hi.astype(F32)
    mid = r1.astype(BF16)
    lo = (r1 - mid.astype(F32)).astype(BF16)
    b = _dot(l_cum, hi) + _dot(l_cum, mid) + _dot(l_cum, lo)
    btot = _dot(l_all, hi) + _dot(l_all, mid) + _dot(l_all, lo)

    half = lax.broadcasted_iota(jnp.int32, (nblk, 8, 128), 1)
    seg2 = seg[0:128, 0:128]
    for g in range(REC_W // 128):
        c = slice(128 * g, 128 * (g + 1))

        def halves(a):
            a4 = a[:, c].reshape(nblk, 2, 8, 128)
            return (a4[:, 1], a4[:, 0]) if reverse else (a4[:, 0], a4[:, 1])

        (q0, q1), (k0, k1), (v0, v1), (b0, b1) = halves(q), halves(k), halves(v), halves(b)

        def pairs(qa, ba, kb_, bb_, vb_, r, masked):
            sh = ((8 - r) % 8) if reverse else r
            if sh:
                kb_, bb_, vb_ = pltpu.roll(kb_, sh, 1), pltpu.roll(bb_, sh, 1), pltpu.roll(vb_, sh, 1)
            diff = ba - bb_
            if masked and r:
                diff = jnp.where((half <= 7 - r) if reverse else (half >= r), diff, NEG_BIG)
            term = (qa * kb_ * jnp.exp(diff)).reshape(nblk * 8, 128).astype(BF16)
            return _dot(term, seg2).reshape(nblk, 8, 128) * vb_

        o0 = pairs(q0, b0, k0, b0, v0, 0, True)
        o1 = pairs(q1, b1, k1, b1, v1, 0, True) + pairs(q1, b1, k0, b0, v0, 0, False)
        for r in range(1, 8):
            o0 = o0 + pairs(q0, b0, k0, b0, v0, r, True)
            o1 = o1 + pairs(q1, b1, k1, b1, v1, r, True) + pairs(q1, b1, k0, b0, v0, r, False)
        first, second = (o1, o0) if reverse else (o0, o1)
        intra_ref[:, 0, :, c] = first
        intra_ref[:, 1, :, c] = second

    qe = (q * jnp.exp(b)).astype(BF16)
    kd = (k * jnp.exp(btot - b)).astype(BF16)
    dec = jnp.exp(btot)
    v_t = v.T
    tok_blk = lax.broadcasted_iota(jnp.int32, (128, tt), 1) // SUB
    hr = lax.broadcasted_iota(jnp.int32, (128, 128), 0) // HDIM
    hc = lax.broadcasted_iota(jnp.int32, (128, 128), 1) // HDIM
    same_head = hr == hc
    order = range(nblk - 1, -1, -1) if reverse else range(nblk)
    for g in range(REC_W // 128):
        c = slice(128 * g, 128 * (g + 1))
        v_tg = v_t[c, :]
        v_stack = jnp.concatenate([jnp.where(tok_blk == j, v_tg, 0.0).astype(BF16) for j in range(nblk)], axis=0)
        upd = _dot(v_stack, kd[:, c])
        st = st_ref[g]
        for j in order:
            r0 = j * SUB
            inter_ref[r0:r0 + SUB, c] = _dot_nt(qe[r0:r0 + SUB, c], st.astype(BF16))
            st = st * dec[r0:r0 + 1, c] + jnp.where(same_head, upd[128 * j:128 * (j + 1)], 0.0)
        st_ref[g] = st
    return intra_ref[...].reshape(tt, REC_W) + inter_ref[...]


def _hgrn_kernel(qf_ref, vf_ref, kf_ref, lff_ref, qb_ref, vb_ref, kb_ref, lfb_ref, s0_ref, seg_ref,
                 of_ref, ob_ref, sout_ref, st_ref, inter_ref, intra_ref):
    i = pl.program_id(1)

    @pl.when(i == 0)
    def _():
        st_ref[...] = s0_ref[0]

    seg = seg_ref[...]
    of_ref[...] = _hgrn_dir(qf_ref[...].astype(F32), kf_ref[...].astype(F32), vf_ref[...].astype(F32),
                            lff_ref[...], seg, st_ref.at[0], inter_ref, intra_ref, False).astype(BF16)
    ob_ref[...] = _hgrn_dir(qb_ref[...].astype(F32), kb_ref[...].astype(F32), vb_ref[...].astype(F32),
                            lfb_ref[...], seg, st_ref.at[1], inter_ref, intra_ref, True).astype(BF16)

    @pl.when(i == pl.num_programs(1) - 1)
    def _():
        sout_ref[0] = st_ref[...]


def _hgrn_scan(hq, hv, kf, lff, kb, lfb, s0, seg, batch, seq):
    tt = min(256, seq)
    n = seq // tt
    fwd = pl.BlockSpec((tt, REC_W), lambda b, i: (b * n + i, 0))
    bwd = pl.BlockSpec((tt, REC_W), lambda b, i: (b * n + n - 1 - i, 0))
    st_spec = pl.BlockSpec((1, 2, REC_W // 128, 128, 128), lambda b, i: (b, 0, 0, 0, 0))
    rows = batch * seq
    return pl.pallas_call(
        _hgrn_kernel,
        grid=(batch, n),
        in_specs=[fwd, fwd, fwd, fwd, bwd, bwd, bwd, bwd, st_spec,
                  pl.BlockSpec((REC_W, REC_W), lambda b, i: (0, 0))],
        out_specs=[fwd, bwd, st_spec],
        out_shape=[jax.ShapeDtypeStruct((rows, REC_W), BF16), jax.ShapeDtypeStruct((rows, REC_W), BF16),
                   jax.ShapeDtypeStruct(s0.shape, F32)],
        scratch_shapes=[pltpu.VMEM((2, REC_W // 128, 128, 128), F32), pltpu.VMEM((tt, REC_W), F32),
                        pltpu.VMEM((tt // SUB, 2, 8, REC_W), F32)],
        compiler_params=_params("arbitrary", "arbitrary"),
        name="hgrn_scan",
    )(hq, hv, kf, lff, hq, hv, kb, lfb, s0, seg)


def _ret_kernel(lg_ref, qf_ref, kf_ref, vf_ref, qb_ref, kb_ref, vb_ref, s0_ref, lgrow_ref,
                of_ref, ob_ref, sout_ref, st_ref, dmask_ref, tab_ref, o_scr):
    i = pl.program_id(1)
    tt = qf_ref.shape[0]

    @pl.when(i == 0)
    def _():
        st_ref[...] = s0_ref[0]
        row = lax.broadcasted_iota(jnp.int32, (tt, tt), 0)
        col = lax.broadcasted_iota(jnp.int32, (tt, tt), 1)
        dist = (row - col).astype(F32)
        for h in range(HEADS):
            fw = jnp.where(dist >= 0, jnp.exp(dist * lg_ref[0, h]), 0.0)
            bw = jnp.where(dist <= 0, jnp.exp(-dist * lg_ref[1, h]), 0.0)
            dmask_ref[h] = fw + bw
        pos = lax.broadcasted_iota(jnp.int32, (tt, REC_W), 0).astype(F32)
        lgf = lgrow_ref[0:1, :]
        lgb = lgrow_ref[1:2, :]
        tab_ref[0] = jnp.exp((pos + 1.0) * lgf)
        tab_ref[1] = jnp.exp((tt - 1.0 - pos) * lgf)
        tab_ref[2] = jnp.exp((tt - pos) * lgb)
        tab_ref[3] = jnp.exp(pos * lgb)

    tile_f = jnp.exp(tt * lgrow_ref[0:1, :])
    tile_b = jnp.exp(tt * lgrow_ref[1:2, :])

    q = qf_ref[...]
    k = kf_ref[...]
    v = vf_ref[...]
    qe = (q.astype(F32) * tab_ref[0]).astype(BF16)
    ke = (k.astype(F32) * tab_ref[1]).astype(BF16)
    for h in range(HEADS):
        c = slice(h * HDIM, (h + 1) * HDIM)
        s = _dot_nt(q[:, c], k[:, c]) * dmask_ref[h]
        st = st_ref[0, h]
        o_scr[:, c] = _dot(s.astype(BF16), v[:, c]) + _dot(qe[:, c], st.astype(BF16))
        st_ref[0, h] = st * tile_f[:, c] + _dot_tn(ke[:, c], v[:, c])
    of_ref[...] = o_scr[...].astype(BF16)

    q = qb_ref[...]
    k = kb_ref[...]
    v = vb_ref[...]
    qe = (q.astype(F32) * tab_ref[2]).astype(BF16)
    ke = (k.astype(F32) * tab_ref[3]).astype(BF16)
    for h in range(HEADS):
        c = slice(h * HDIM, (h + 1) * HDIM)
        st = st_ref[1, h]
        o_scr[:, c] = _dot(qe[:, c], st.astype(BF16))
        st_ref[1, h] = st * tile_b[:, c] + _dot_tn(ke[:, c], v[:, c])
    ob_ref[...] = o_scr[...].astype(BF16)

    @pl.when(i == pl.num_programs(1) - 1)
    def _():
        sout_ref[0] = st_ref[...]


def _ret_scan(rq, rk, rv, s0, lg, lgrow, batch, seq):
    tt = min(256, seq)
    n = seq // tt
    fwd = pl.BlockSpec((tt, REC_W), lambda b, i: (b * n + i, 0))
    bwd = pl.BlockSpec((tt, REC_W), lambda b, i: (b * n + n - 1 - i, 0))
    st_spec = pl.BlockSpec((1, 2, HEADS, HDIM, HDIM), lambda b, i: (b, 0, 0, 0, 0))
    rows = batch * seq
    return pl.pallas_call(
        _ret_kernel,
        grid=(batch, n),
        in_specs=[pl.BlockSpec(memory_space=pltpu.SMEM), fwd, fwd, fwd, bwd, bwd, bwd, st_spec,
                  pl.BlockSpec((8, REC_W), lambda b, i: (0, 0))],
        out_specs=[fwd, bwd, st_spec],
        out_shape=[jax.ShapeDtypeStruct((rows, REC_W), BF16), jax.ShapeDtypeStruct((rows, REC_W), BF16),
                   jax.ShapeDtypeStruct(s0.shape, F32)],
        scratch_shapes=[pltpu.VMEM((2, HEADS, HDIM, HDIM), F32), pltpu.VMEM((HEADS, tt, tt), F32),
                        pltpu.VMEM((4, tt, REC_W), F32), pltpu.VMEM((tt, REC_W), F32)],
        compiler_params=_params("arbitrary", "arbitrary"),
        name="ret_scan",
    )(lg, rq, rk, rv, rq, rk, rv, s0, lgrow)


def _attn_kernel(lam_ref, q_ref, kc_ref, vct_ref, k_ref, vt_ref, gain_ref, o_ref, s_ref, *, with_latent, kchunk,
                 piece, out_scale):
    tq = q_ref.shape[0]
    q = q_ref[...]
    lane = lax.broadcasted_iota(jnp.int32, (tq, 128), 1)
    zero = jnp.zeros_like(q)
    qs = jnp.concatenate([jnp.where(lane < DA_HEAD_DIM, q, zero),
                          jnp.where(lane >= DA_HEAD_DIM, q, zero)], axis=0)
    nchunks = (k_ref.shape[0] // kchunk) if with_latent else 0

    def scores(kblk):
        return _dot_nt(kblk, qs)

    def absorb(s, vtblk, m, acc):
        n = s.shape[0]
        part = jnp.max(s.reshape(n // 128, 128, 2 * tq), axis=0) if n > 128 else s
        m_new = jnp.maximum(m, jnp.max(part, axis=0, keepdims=True))
        p = jnp.exp2((s - m_new).astype(BF16))
        return m_new, jnp.exp2(m - m_new) * acc + _dot(vtblk, p)

    def latent_k(c):
        return k_ref[pl.ds(pl.multiple_of(c * kchunk, kchunk), kchunk), :]

    def latent_vt(c):
        return vt_ref[0, :, pl.ds(pl.multiple_of(c * kchunk, kchunk), kchunk)]

    m, acc = absorb(scores(kc_ref[...]), vct_ref[0], jnp.full((1, 2 * tq), NEG_BIG, F32),
                    jnp.zeros((128, 2 * tq), F32))
    if with_latent:
        pk = min(piece, kchunk)
        npieces = kchunk // pk

        def col_max(x):
            if pk > 128:
                x = jnp.max(x.reshape(pk // 128, 128, 2 * tq), axis=0)
            return jnp.max(x.reshape(x.shape[0] // 8, 8, 2 * tq), axis=0)

        def fused(cur, c, m_, acc_, nxt):
            part = jnp.full((8, 2 * tq), NEG_BIG, F32)
            for t in range(npieces):
                if nxt is not None:
                    r0 = pl.multiple_of((c + 1) * kchunk + t * pk, pk)
                    s_new = scores(k_ref[pl.ds(r0, pk), :])
                    s_ref[nxt, t * pk:(t + 1) * pk, :] = s_new
                    part = jnp.maximum(part, col_max(s_new))
                c0 = pl.multiple_of(c * kchunk + t * pk, pk)
                p = jnp.exp2((s_ref[cur, t * pk:(t + 1) * pk, :] - m_).astype(BF16))
                acc_ = acc_ + _dot(vt_ref[0, :, pl.ds(c0, pk)], p)
            if nxt is None:
                return m_, acc_
            m_new = jnp.maximum(m_, jnp.max(part, axis=0, keepdims=True))
            return m_new, acc_ * jnp.exp2(m_ - m_new)

        s0 = scores(latent_k(0))
        s_ref[0] = s0
        m0 = jnp.maximum(m, jnp.max(jnp.max(s0.reshape(kchunk // 128, 128, 2 * tq), axis=0), axis=0, keepdims=True))
        acc = acc * jnp.exp2(m - m0)
        m = m0

        def pair(c, carry, last):
            m_, acc_ = fused(0, c, *carry, 1)
            return fused(1, c + 1, m_, acc_, None if last else 0)

        if nchunks > 2:
            m, acc = lax.fori_loop(0, nchunks // 2 - 1, lambda j, cr: pair(2 * j, cr, False), (m, acc))
        m, acc = pair(nchunks - 2, (m, acc), True)

    a1 = acc[:, :tq]
    a2 = acc[:, tq:]
    o = a1 / a1[DA_VDIM:DA_VDIM + 1, :] - lam_ref[0] * (a2 / a2[DA_VDIM:DA_VDIM + 1, :])
    row = lax.broadcasted_iota(jnp.int32, (128, tq), 0)
    o = jnp.where(row < DA_VDIM, o, 0.0)
    ms = jnp.sum(o * o, axis=0, keepdims=True) * (1.0 / DA_VDIM)
    o_ref[...] = (o * lax.rsqrt(ms + NORM_EPS) * gain_ref[...] * out_scale).T.astype(BF16)


def _diff_attention(lam, q, kc, vc, k, v, gain, batch, seq_q, ctx_len, seq_k, out_scale, q_is_ctx):
    tq = min(256, seq_q)
    nq = seq_q // tq
    with_latent = not q_is_ctx
    if with_latent:
        kchunk = min(2048, seq_k // 2)
        assert seq_k % (2 * kchunk) == 0
    else:
        k, v = kc, vc
        seq_k, kchunk = ctx_len, ctx_len
    kern = functools.partial(_attn_kernel, with_latent=with_latent, kchunk=kchunk, piece=512, out_scale=out_scale)
    return pl.pallas_call(
        kern,
        grid=(batch, DA_HEADS, nq),
        in_specs=[
            pl.BlockSpec(memory_space=pltpu.SMEM),
            pl.BlockSpec((tq, 128), lambda b, h, i: (b * nq + i, h)),
            pl.BlockSpec((ctx_len, 128), lambda b, h, i: (b, h)),
            pl.BlockSpec((1, 128, ctx_len), lambda b, h, i: (b, h, 0)),
            pl.BlockSpec((seq_k, 128), lambda b, h, i: (b, h)),
            pl.BlockSpec((1, 128, seq_k), lambda b, h, i: (b, h, 0)),
            pl.BlockSpec((128, 1), lambda b, h, i: (h, 0)),
        ],
        out_specs=pl.BlockSpec((tq, 128), lambda b, h, i: (b * nq + i, h)),
        out_shape=jax.ShapeDtypeStruct((batch * seq_q, DA_X), BF16),
        scratch_shapes=[pltpu.VMEM((2, kchunk, 2 * tq), F32)],
        compiler_params=_params("arbitrary", "arbitrary", "arbitrary"),
        name="diff_attention",
    )(lam, q, kc, vc, k, v, gain)


def _outproj_kernel(x_ref, hof_ref, hob_ref, hg_ref, rof_ref, rob_ref, rg_ref, da_ref, seg_ref, w_ref,
                    hn_ref, rn_ref, n1_ref, n2_ref, g1_ref, sc2_ref, sh2_ref, rw_ref, rb_ref,
                    x1_ref, tok_ref, idx_ref, gate_ref):
    seg = seg_ref[...]

    def gated_head_norm(o, gain, gate):
        ms = _dot((o * o).astype(BF16), seg) * (1.0 / HDIM)
        return (o * lax.rsqrt(ms + NORM_EPS) * gain * (gate * _sigmoid(gate))).astype(BF16)

    a = gated_head_norm(hof_ref[...].astype(F32) + hob_ref[...].astype(F32), hn_ref[...], hg_ref[...].astype(F32))
    b = gated_head_norm(rof_ref[...].astype(F32) + rob_ref[...].astype(F32), rn_ref[...], rg_ref[...].astype(F32))
    y = (_dot(a, w_ref[0:REC_W, :]) + _dot(b, w_ref[REC_W:2 * REC_W, :])
         + _dot(da_ref[...], w_ref[2 * REC_W:2 * REC_W + DA_X, :]))
    x1 = x_ref[...] + g1_ref[0] * _rms(y, n1_ref[...])
    x1_ref[...] = x1
    tok = _rms(x1, n2_ref[...]) * (1.0 + sc2_ref[0]) + sh2_ref[0]
    tok_ref[...] = tok.astype(BF16)
    logits = _dot_nt(rw_ref[...], tok) + rb_ref[...]
    eid = lax.broadcasted_iota(jnp.int32, logits.shape, 0).astype(F32)
    vals, ids = [], []
    for _ in range(TOP_K):
        best = jnp.max(logits, axis=0, keepdims=True)
        first = jnp.min(jnp.where(logits == best, eid, float(N_EXPERTS)), axis=0, keepdims=True)
        vals.append(best)
        ids.append(first)
        logits = jnp.where(eid == first, -jnp.inf, logits)
    ex = [jnp.exp(v - vals[0]) for v in vals]
    total = ex[0] + ex[1] + ex[2] + ex[3]
    idx_ref[...] = jnp.concatenate(ids, axis=0).astype(jnp.int32)
    gate_ref[...] = jnp.concatenate(ex, axis=0) / total


def _outproj(xf, seq, streams, seg, w_out, hn, rn, n1, n2, g1, sc2, sh2, rw, rb):
    rows, d = xf.shape
    tm = min(256, seq)
    nb = seq // tm
    row_spec = lambda w: pl.BlockSpec((tm, w), lambda i: (i, 0))
    full = lambda a: pl.BlockSpec(a.shape, lambda i: (0,) * a.ndim)
    mod_spec = pl.BlockSpec((1, 1, d), lambda i: (i // nb, 0, 0))
    hof, hob, hg, rof, rob, rg, da = streams
    return pl.pallas_call(
        _outproj_kernel,
        grid=(rows // tm,),
        in_specs=[row_spec(d)] + [row_spec(REC_W)] * 6 + [row_spec(DA_X), full(seg), full(w_out),
                  full(hn), full(rn), full(n1), full(n2), mod_spec, mod_spec, mod_spec, full(rw), full(rb)],
        out_specs=[row_spec(d), row_spec(d), pl.BlockSpec((TOP_K, tm), lambda i: (0, i)),
                   pl.BlockSpec((TOP_K, tm), lambda i: (0, i))],
        out_shape=[jax.ShapeDtypeStruct((rows, d), F32), jax.ShapeDtypeStruct((rows, d), BF16),
                   jax.ShapeDtypeStruct((TOP_K, rows), jnp.int32), jax.ShapeDtypeStruct((TOP_K, rows), F32)],
        compiler_params=_params("arbitrary"),
        name="outproj",
    )(xf, hof, hob, hg, rof, rob, rg, da, seg, w_out, hn, rn, n1, n2, g1, sc2, sh2, rw, rb)


def _ffn_kernel(be_ref, na_ref, x_ref, w1_ref, w2_ref, b1g_ref, b1l_ref, b2_ref, pe_ref, po_ref, yprev_ref, y_ref,
                w1g_s, w1l_s, w2_s, *, blk0):
    del yprev_ref
    i = pl.program_id(0)
    blk = i + blk0
    active = blk < na_ref[0]
    fresh = jnp.logical_or(i == 0, be_ref[blk] != be_ref[jnp.maximum(blk - 1, 0)])

    @pl.when(fresh)
    def _():
        n = w1_ref.shape[3]
        for c in range(n // 256):
            wb = w1_ref[0, 0, :, 256 * c:256 * (c + 1)].astype(BF16)
            w1g_s[:, 128 * c:128 * (c + 1)] = _dot(wb, pe_ref[...]).astype(BF16)
            w1l_s[:, 128 * c:128 * (c + 1)] = _dot(wb, po_ref[...]).astype(BF16)
        w2_s[...] = w2_ref[0, 0].astype(BF16)

    @pl.when(jnp.logical_not(active))
    def _():
        y_ref[...] = jnp.zeros(y_ref.shape, y_ref.dtype)

    @pl.when(active)
    def _():
        x = x_ref[...]
        glu = jnp.minimum(_dot(x, w1g_s[...]) + b1g_ref[0], SWIGLU_LIMIT)
        lin = jnp.clip(_dot(x, w1l_s[...]) + b1l_ref[0], -SWIGLU_LIMIT, SWIGLU_LIMIT)
        act = glu * _sigmoid(SWIGLU_ALPHA * glu) * (lin + 1.0)
        y_ref[...] = (_dot(act.astype(BF16), w2_s[...]) + b2_ref[0]).astype(y_ref.dtype)


def _expert_ffn(block_expert, n_active, tok, tok_sorted, w1, w2, b1g, b1l, b2, layer, bm, nparts=4):
    p = tok_sorted.shape[0]
    d = tok.shape[1]
    f = w2.shape[2]
    nblocks = p // bm
    src = jnp.arange(256)[:, None]
    dst = jnp.arange(128)[None, :]
    pe = (src == 2 * dst).astype(BF16)
    po = (src == 2 * dst + 1).astype(BF16)
    sel = pl.BlockSpec((256, 128), lambda i, be, na: (0, 0))
    bounds = [nblocks * j // nparts for j in range(nparts + 1)]
    yg = jnp.zeros((8, 128), BF16)
    for j in range(nparts):
        b0, nb = bounds[j], bounds[j + 1] - bounds[j]
        xg = jnp.take(tok, tok_sorted[b0 * bm:(b0 + nb) * bm], axis=0)
        wspec = lambda s, b0=b0: pl.BlockSpec((1, 1) + s, lambda i, be, na: (layer, be[i + b0], 0, 0))
        bspec = lambda s, b0=b0: pl.BlockSpec((1,) + s, lambda i, be, na: (be[i + b0], 0, 0))
        yg = pl.pallas_call(
            functools.partial(_ffn_kernel, blk0=b0),
            grid_spec=pltpu.PrefetchScalarGridSpec(
                num_scalar_prefetch=2,
                grid=(nb,),
                in_specs=[pl.BlockSpec((bm, d), lambda i, be, na: (i, 0)),
                          wspec((d, 2 * f)), wspec((f, d)), bspec((1, f)), bspec((1, f)), bspec((1, d)), sel, sel,
                          pl.BlockSpec(memory_space=pl.ANY)],
                out_specs=pl.BlockSpec((bm, d), lambda i, be, na, b0=b0: (i + b0, 0)),
                scratch_shapes=[pltpu.VMEM((d, f), BF16), pltpu.VMEM((d, f), BF16), pltpu.VMEM((f, d), BF16)],
            ),
            out_shape=jax.ShapeDtypeStruct((p, d), BF16),
            input_output_aliases={10: 0} if j else {},
            compiler_params=_params("arbitrary"),
            name="expert_ffn",
        )(block_expert, n_active, xg, w1, w2, b1g, b1l, b2, pe, po, yg)
    return yg


def _resid_kernel(x_ref, y0_ref, y1_ref, y2_ref, y3_ref, gate_ref, n_ref, g_ref, o_ref):
    gates = gate_ref[...]
    f = y0_ref[...].astype(F32) * gates[:, 0:1]
    for k, y_ref in enumerate((y1_ref, y2_ref, y3_ref), start=1):
        f = f + y_ref[...].astype(F32) * gates[:, k:k + 1]
    o_ref[...] = x_ref[...] + g_ref[0] * _rms(f, n_ref[...])


def _ffn_residual(xf, ys, gates, seq, n3, g2):
    rows, d = xf.shape
    tm = min(512, seq)
    nb = seq // tm
    row_spec = pl.BlockSpec((tm, d), lambda i: (i, 0))
    return pl.pallas_call(
        _resid_kernel,
        grid=(rows // tm,),
        in_specs=[row_spec] * 5 + [pl.BlockSpec((tm, TOP_K), lambda i: (i, 0)), pl.BlockSpec((1, d), lambda i: (0, 0)),
                                   pl.BlockSpec((1, 1, d), lambda i: (i // nb, 0, 0))],
        out_specs=row_spec,
        out_shape=jax.ShapeDtypeStruct((rows, d), F32),
        compiler_params=_params("arbitrary"),
        name="ffn_residual",
    )(xf, *ys, gates, n3, g2)


def _rope_tables(pos, dim):
    inv = 1.0 / (ROPE_BASE ** (jnp.arange(0, dim, 2, dtype=F32) / dim))
    ang = pos.astype(F32)[:, None] * inv[None, :]
    return jnp.cos(ang), jnp.sin(ang)


def _rot_cols(w, head_dim, halves):
    d, n = w.shape
    g = head_dim // halves
    w4 = w.reshape(d, n // g, 2, g // 2)
    return jnp.concatenate([-w4[:, :, 1], w4[:, :, 0]], axis=-1).reshape(d, n)


def _prep_w_in(w):
    rq, rk = w[:, _C_RQ:_C_RQ + REC_W], w[:, _C_RK:_C_RK + REC_W]
    dq, dk = w[:, _C_DQ:_C_DQ + DA_W], w[:, _C_DK:_C_DK + DA_W]
    return jnp.concatenate([w, _rot_cols(rq, HDIM, 1), _rot_cols(rk, HDIM, 1),
                            _rot_cols(dq, DA_HEAD_DIM, 2), _rot_cols(dk, DA_HEAD_DIM, 2)], axis=1).astype(BF16)


def _routing(top_idx_t, bm):
    n = top_idx_t.shape[1]
    a = n * TOP_K
    flat_e = top_idx_t.T.reshape(a)
    sorted_e, order = lax.sort_key_val(flat_e, jnp.arange(a, dtype=jnp.int32))
    experts = jnp.arange(N_EXPERTS + 1, dtype=jnp.int32)
    bounds = jnp.sum((flat_e[None, :] < experts[:, None]).astype(jnp.int32), axis=1)
    start, counts = bounds[:-1], bounds[1:] - bounds[:-1]
    padded = (counts + bm - 1) // bm * bm
    pend = jnp.cumsum(padded)
    shift = pend - padded - start
    dest = jnp.arange(a, dtype=jnp.int32) + shift[sorted_e]
    _, pos = lax.sort_key_val(order, dest)
    nblocks = -(-(a + N_EXPERTS * (bm - 1)) // bm)
    block_row = jnp.arange(nblocks, dtype=jnp.int32) * bm
    block_expert = jnp.minimum(jnp.sum((pend[None, :] <= block_row[:, None]).astype(jnp.int32), axis=1),
                               N_EXPERTS - 1)
    row_e = jnp.repeat(block_expert, bm)
    rank = jnp.arange(nblocks * bm, dtype=jnp.int32) - shift[row_e]
    valid = rank < bounds[row_e + 1]
    tok_sorted = jnp.where(valid, order[jnp.clip(rank, 0, a - 1)] // TOP_K, 0)
    n_active = (pend[-1:] // bm).astype(jnp.int32)
    return tok_sorted, pos, block_expert, n_active


def kernel(x, c, ctx, c_ctx, mod_w, mod_b, norm_g, w_in, hgrn_lb, hgrn_norm, ret_decay, ret_norm, da_lambda,
           da_subln, w_out, router_w, router_b, w1, b1, w2, b2):
    B, S, D = x.shape
    C = ctx.shape[1]
    depth = mod_w.shape[0]
    bm = 256

    pos = jnp.arange(S)
    cr, sr = _rope_tables(pos // GRID_W, DA_HEAD_DIM // 2)
    cc, sc_ = _rope_tables(pos % GRID_W, DA_HEAD_DIM // 2)
    cs, ss = _rope_tables(pos, HDIM)
    cosr = jnp.tile(jnp.concatenate([cs, cs], -1), (1, 128 // HDIM))
    sinr = jnp.tile(jnp.concatenate([ss, ss], -1), (1, 128 // HDIM))
    cosd = jnp.tile(jnp.concatenate([cr, cr, cc, cc], -1), (1, 128 // DA_HEAD_DIM))
    sind = jnp.tile(jnp.concatenate([sr, sr, sc_, sc_], -1), (1, 128 // DA_HEAD_DIM))
    ones_r, zeros_r = jnp.ones((C, 128), F32), jnp.zeros((C, 128), F32)
    ones_d, zeros_d = ones_r, zeros_r

    lb_cum = jnp.cumsum(jax.nn.softmax(hgrn_lb.astype(F32), axis=0), axis=0)
    lower = lb_cum - lb_cum[0:1]

    cvec = jnp.zeros((8, D), F32).at[:B].set(c).at[B].set(c_ctx)
    mods = _modulation(cvec, mod_w, mod_b)

    head_id = jnp.arange(REC_W) // HDIM
    seg = (head_id[:, None] == head_id[None, :]).astype(BF16)

    xf = x.reshape(B * S, D)
    xc = ctx.reshape(B * C, D)
    zero_state = jnp.zeros((B, 2, HEADS, HDIM, HDIM), F32)

    for layer in range(depth):
        need_ctx = layer < depth - 1
        lam_init = 0.8 - 0.6 * math.exp(-0.3 * layer)
        m6 = mods[layer].reshape(8, 6, D)
        lat = lambda k: m6[:B, k][:, None, :]
        cxm = lambda k: jnp.broadcast_to(m6[B, k][None, None, :], (B, 1, D))
        ng = norm_g[layer]
        lb = lower[layer]
        lbc = jnp.zeros((8, REC_W), F32).at[0].set(jnp.log(lb)).at[1].set(jnp.log1p(-lb)).at[2].set(1.0 - lb)
        w_all = _prep_w_in(w_in[layer])
        log_gamma = jnp.log1p(-jnp.exp2(-ret_decay[layer].astype(F32)))
        lgrow = jnp.zeros((8, REC_W), F32).at[:2].set(jnp.repeat(log_gamma, HDIM, axis=1))
        lamv = da_lambda[layer].astype(F32)
        lam = (jnp.exp(jnp.sum(lamv[0] * lamv[1])) - jnp.exp(jnp.sum(lamv[2] * lamv[3])) + lam_init).reshape(1)
        hn = jnp.tile(hgrn_norm[layer], HEADS)[None, :]
        rn = jnp.tile(ret_norm[layer], HEADS)[None, :]
        dn = jnp.tile(jnp.concatenate([da_subln[layer], jnp.zeros((128 - DA_VDIM,), F32)]), DA_HEADS)[:, None]
        wo = w_out[layer]
        wo_da = jnp.pad(wo[2 * REC_W:].reshape(DA_HEADS, DA_VDIM, D), ((0, 0), (0, 128 - DA_VDIM), (0, 0)))
        w_out_b = jnp.concatenate([wo[:2 * REC_W], wo_da.reshape(DA_X, D)], axis=0).astype(BF16)
        rw = router_w[layer].T
        rb = router_b[layer][:, None]

        (hq, hv, hg, kf, lff, kb, lfb, rq, rk, rv, rg, dq, dk, dvx) = _inproj(
            xf, S, ng[0:1], lat(1), lat(0), w_all, lbc, cosr, sinr, cosd, sind)
        (hq_c, hv_c, hg_c, kf_c, lff_c, kb_c, lfb_c, rq_c, rk_c, rv_c, rg_c, dq_c, dk_c, dvx_c) = _inproj(
            xc, C, ng[0:1], cxm(1), cxm(0), w_all, lbc, ones_r, zeros_r, ones_d, zeros_d)

        hof_c, hob_c, hs = _hgrn_scan(hq_c, hv_c, kf_c, lff_c, kb_c, lfb_c,
                                      jnp.zeros((B, 2, REC_W // 128, 128, 128), F32), seg, B, C)
        hof, hob, _ = _hgrn_scan(hq, hv, kf, lff, kb, lfb, hs, seg, B, S)
        rof_c, rob_c, rs = _ret_scan(rq_c, rk_c, rv_c, zero_state, log_gamma, lgrow, B, C)
        rof, rob, _ = _ret_scan(rq, rk, rv, rs, log_gamma, lgrow, B, S)
        out_scale = 1.0 - lam_init
        da = _diff_attention(lam, dq, dk_c, dvx_c, dk, dvx, dn, B, S, C, S, out_scale, False)

        x1, tok, top_idx_t, gates_t = _outproj(xf, S, (hof, hob, hg, rof, rob, rg, da), seg, w_out_b, hn, rn,
                                               ng[1:2], ng[2:3], lat(2), lat(4), lat(3), rw, rb)
        if need_ctx:
            da_c = _diff_attention(lam, dq_c, dk_c, dvx_c, None, None, dn, B, C, C, C, out_scale, True)
            xc1, tok_c, top_idx_c, gates_c = _outproj(xc, C, (hof_c, hob_c, hg_c, rof_c, rob_c, rg_c, da_c), seg,
                                                      w_out_b, hn, rn, ng[1:2], ng[2:3], cxm(2), cxm(4), cxm(3),
                                                      rw, rb)
            tok = jnp.concatenate([tok, tok_c], axis=0)
            top_idx_t = jnp.concatenate([top_idx_t, top_idx_c], axis=1)
            gates_t = jnp.concatenate([gates_t, gates_c], axis=1)

        n_tok = tok.shape[0]
        gates = gates_t.T
        tok_sorted, posn, block_expert, n_active = _routing(top_idx_t, bm)
        b1g = b1[layer][:, None, 0::2]
        b1l = b1[layer][:, None, 1::2]
        b2l = b2[layer][:, None, :]
        yg = _expert_ffn(block_expert, n_active, tok, tok_sorted, w1, w2, b1g, b1l, b2l, layer, bm)
        pos2 = posn.reshape(n_tok, TOP_K)

        def expert_rows(lo, hi):
            return [jnp.take(yg, pos2[lo:hi, k], axis=0) for k in range(TOP_K)]

        xf = _ffn_residual(x1, expert_rows(0, B * S), gates[:B * S], S, ng[3:4], lat(5))
        if need_ctx:
            xc = _ffn_residual(xc1, expert_rows(B * S, n_tok), gates[B * S:], C, ng[3:4], cxm(5))
    return xf.reshape(B, S, D)
```

```python
import functools
import math

import jax
import jax.numpy as jnp
from jax import lax
from jax.experimental import pallas as pl
from jax.experimental.pallas import tpu as pltpu

F32 = jnp.float32
BF16 = jnp.bfloat16

GRID_W = 64
HEADS = 6
HDIM = 64
REC_W = HEADS * HDIM
DA_HEADS = 4
DA_HEAD_DIM = 32
DA_MAPS = 2 * DA_HEADS
DA_W = DA_MAPS * DA_HEAD_DIM
DA_VDIM = 2 * DA_HEAD_DIM
DA_X = DA_HEADS * 128
_Q_SCALE = DA_HEAD_DIM ** -0.5 * math.log2(math.e)
ROPE_BASE = 10000.0
N_EXPERTS = 32
TOP_K = 4
SWIGLU_ALPHA = 1.702
SWIGLU_LIMIT = 7.0
NORM_EPS = 1e-6

SUB = 16
NEG_BIG = -1e30
VMEM_LIMIT = 56 * 1024 * 1024

_C_HQ, _C_HFF, _C_HFB, _C_HI, _C_HG = 0, 384, 768, 1152, 1536
_C_RQ, _C_RK, _C_RV, _C_RG = 1920, 2304, 2688, 3072
_C_DQ, _C_DK, _C_DV = 3456, 3712, 3968
_C_RQR, _C_RKR, _C_DQR, _C_DKR = 4224, 4608, 4992, 5248
_W_ALL = 5504


def _dot(a, b):
    return jnp.dot(a, b, preferred_element_type=F32)


def _dot_nt(a, b):
    return lax.dot_general(a, b, (((1,), (1,)), ((), ())), preferred_element_type=F32)


def _dot_tn(a, b):
    return lax.dot_general(a, b, (((0,), (0,)), ((), ())), preferred_element_type=F32)


def _sigmoid(x):
    return 1.0 / (1.0 + jnp.exp(-x))


def _rms(x, g):
    ms = jnp.mean(x * x, axis=-1, keepdims=True)
    return x * lax.rsqrt(ms + NORM_EPS) * g


def _params(*sem):
    return pltpu.CompilerParams(dimension_semantics=sem, vmem_limit_bytes=VMEM_LIMIT)


def _mod_kernel(c_ref, w_ref, b_ref, o_ref):
    c = c_ref[...]
    o_ref[0] = _dot(c * _sigmoid(c), w_ref[0]) + b_ref[0]


def _modulation(cvec, mod_w, mod_b):
    depth, d, n = mod_w.shape
    tn = 1536
    return pl.pallas_call(
        _mod_kernel,
        grid=(depth, n // tn),
        in_specs=[
            pl.BlockSpec((8, d), lambda l, j: (0, 0)),
            pl.BlockSpec((1, d, tn), lambda l, j: (l, 0, j)),
            pl.BlockSpec((1, 1, tn), lambda l, j: (l, 0, j)),
        ],
        out_specs=pl.BlockSpec((1, 8, tn), lambda l, j: (l, 0, j)),
        out_shape=jax.ShapeDtypeStruct((depth, 8, n), F32),
        compiler_params=_params("arbitrary", "arbitrary"),
        name="modulation",
    )(cvec, mod_w, mod_b.reshape(depth, 1, n))


def _inproj_kernel(x_ref, g_ref, sc_ref, sh_ref, w_ref, lbc_ref, cosr_ref, sinr_ref, cosd_ref, sind_ref,
                   hq_ref, hv_ref, hg_ref, kf_ref, lff_ref, kb_ref, lfb_ref,
                   rq_ref, rk_ref, rv_ref, rg_ref, dq_ref, dk_ref, dvt_ref):
    x = x_ref[...]
    h = _rms(x, g_ref[...]) * (1.0 + sc_ref[0]) + sh_ref[0]
    hb = h.astype(BF16)

    def proj(c0, n):
        return _dot(hb, w_ref[:, c0:c0 + n])

    hq_ref[...] = proj(_C_HQ, REC_W).astype(BF16)
    hv_ref[...] = proj(_C_HI, REC_W).astype(BF16)
    hg_ref[...] = proj(_C_HG, REC_W).astype(BF16)

    log_lb = lbc_ref[0:1, :]
    log_1m = lbc_ref[1:2, :]
    one_m = lbc_ref[2:3, :]

    def gates(z):
        log_sig = jnp.minimum(z, 0.0) - jnp.log1p(jnp.exp(-jnp.abs(z)))
        t = log_1m + log_sig
        m = jnp.maximum(log_lb, t)
        logf = m + jnp.log1p(jnp.exp(-jnp.abs(log_lb - t)))
        return one_m / (1.0 + jnp.exp(z)), logf

    k, lf = gates(proj(_C_HFF, REC_W))
    kf_ref[...] = k.astype(BF16)
    lff_ref[...] = lf
    k, lf = gates(proj(_C_HFB, REC_W))
    kb_ref[...] = k.astype(BF16)
    lfb_ref[...] = lf

    cosr = jnp.concatenate([cosr_ref[...]] * (REC_W // 128), axis=1)
    sinr = jnp.concatenate([sinr_ref[...]] * (REC_W // 128), axis=1)
    rq_ref[...] = (proj(_C_RQ, REC_W) * cosr + proj(_C_RQR, REC_W) * sinr).astype(BF16)
    rk_ref[...] = ((proj(_C_RK, REC_W) * cosr + proj(_C_RKR, REC_W) * sinr) * (HDIM ** -0.5)).astype(BF16)
    rv_ref[...] = proj(_C_RV, REC_W).astype(BF16)
    rg_ref[...] = proj(_C_RG, REC_W).astype(BF16)

    cosd = jnp.concatenate([cosd_ref[...]] * (DA_W // 128), axis=1)
    sind = jnp.concatenate([sind_ref[...]] * (DA_W // 128), axis=1)
    dq = ((proj(_C_DQ, DA_W) * cosd + proj(_C_DQR, DA_W) * sind) * _Q_SCALE).astype(BF16)
    dk = (proj(_C_DK, DA_W) * cosd + proj(_C_DKR, DA_W) * sind).astype(BF16)
    dv_t = proj(_C_DV, DA_W).T.astype(BF16)
    tm = dq.shape[0]
    zero_col = jnp.zeros((tm, 64), BF16)
    sub = lax.broadcasted_iota(jnp.int32, (64, tm), 0)
    ones_row = jnp.where(sub == 0, 1.0, 0.0).astype(BF16)
    for h_ in range(DA_HEADS):
        lo, mid, hi_ = 128 * h_, 128 * h_ + 64, 128 * h_ + 128
        dq_ref[:, lo:mid] = dq[:, 64 * h_:64 * h_ + 64]
        dq_ref[:, mid:hi_] = zero_col
        dk_ref[:, lo:mid] = dk[:, 64 * h_:64 * h_ + 64]
        dk_ref[:, mid:hi_] = zero_col
        dvt_ref[0, lo:mid, :] = dv_t[64 * h_:64 * h_ + 64, :]
        dvt_ref[0, mid:hi_, :] = ones_row


def _inproj(xf, seq, gnorm, sc, sh, w_all, lbc, cosr, sinr, cosd, sind):
    rows, d = xf.shape
    tm = min(256, seq)
    nb = seq // tm
    row_spec = lambda w: pl.BlockSpec((tm, w), lambda i: (i, 0))
    tab_spec = lambda w: pl.BlockSpec((tm, w), lambda i: (i % nb, 0))
    mod_spec = pl.BlockSpec((1, 1, d), lambda i: (i // nb, 0, 0))
    widths = [REC_W] * 11 + [DA_X, DA_X]
    dtypes = [BF16, BF16, BF16, BF16, F32, BF16, F32, BF16, BF16, BF16, BF16, BF16, BF16]
    out_specs = [row_spec(w) for w in widths] + [pl.BlockSpec((1, DA_X, tm), lambda i: (i // nb, 0, i % nb))]
    out_shape = ([jax.ShapeDtypeStruct((rows, w), dt) for w, dt in zip(widths, dtypes)]
                 + [jax.ShapeDtypeStruct((rows // seq, DA_X, seq), BF16)])
    return pl.pallas_call(
        _inproj_kernel,
        grid=(rows // tm,),
        in_specs=[
            row_spec(d),
            pl.BlockSpec((1, d), lambda i: (0, 0)),
            mod_spec, mod_spec,
            pl.BlockSpec((d, _W_ALL), lambda i: (0, 0)),
            pl.BlockSpec((8, REC_W), lambda i: (0, 0)),
            tab_spec(128), tab_spec(128), tab_spec(128), tab_spec(128),
        ],
        out_specs=out_specs,
        out_shape=out_shape,
        compiler_params=_params("arbitrary"),
        name="inproj",
    )(xf, gnorm, sc, sh, w_all, lbc, cosr, sinr, cosd, sind)


def _hgrn_dir(q, k, v, lf, seg, st_ref, inter_ref, intra_ref, reverse):
    tt = q.shape[0]
    nblk = tt // SUB
    row = lax.broadcasted_iota(jnp.int32, (tt, tt), 0)
    col = lax.broadcasted_iota(jnp.int32, (tt, tt), 1)
    same = (row // SUB) == (col // SUB)
    tri = (col >= row) if reverse else (col <= row)
    l_all = jnp.where(same, 1.0, 0.0)
    l_cum = jnp.where(tri, l_all, 0.0).astype(BF16)
    l_all = l_all.astype(BF16)
    hi = lf.astype(BF16)
    r1 = lf - hi.astype(F32)
    mid = r1.astype(BF16)
    lo = (r1 - mid.astype(F32)).astype(BF16)
    b = _dot(l_cum, hi) + _dot(l_cum, mid) + _dot(l_cum, lo)
    btot = _dot(l_all, hi) + _dot(l_all, mid) + _dot(l_all, lo)

    half = lax.broadcasted_iota(jnp.int32, (nblk, 8, 128), 1)
    seg2 = seg[0:128, 0:128]
    for g in range(REC_W // 128):
        c = slice(128 * g, 128 * (g + 1))

        def halves(a):
            a4 = a[:, c].reshape(nblk, 2, 8, 128)
            return (a4[:, 1], a4[:, 0]) if reverse else (a4[:, 0], a4[:, 1])

        (q0, q1), (k0, k1), (v0, v1), (b0, b1) = halves(q), halves(k), halves(v), halves(b)

        def pairs(qa, ba, kb_, bb_, vb_, r, masked):
            sh = ((8 - r) % 8) if reverse else r
            if sh:
                kb_, bb_, vb_ = pltpu.roll(kb_, sh, 1), pltpu.roll(bb_, sh, 1), pltpu.roll(vb_, sh, 1)
            diff = ba - bb_
            if masked and r:
                diff = jnp.where((half <= 7 - r) if reverse else (half >= r), diff, NEG_BIG)
            term = (qa * kb_ * jnp.exp(diff)).reshape(nblk * 8, 128).astype(BF16)
            return _dot(term, seg2).reshape(nblk, 8, 128) * vb_

        o0 = pairs(q0, b0, k0, b0, v0, 0, True)
        o1 = pairs(q1, b1, k1, b1, v1, 0, True) + pairs(q1, b1, k0, b0, v0, 0, False)
        for r in range(1, 8):
            o0 = o0 + pairs(q0, b0, k0, b0, v0, r, True)
            o1 = o1 + pairs(q1, b1, k1, b1, v1, r, True) + pairs(q1, b1, k0, b0, v0, r, False)
        first, second = (o1, o0) if reverse else (o0, o1)
        intra_ref[:, 0, :, c] = first
        intra_ref[:, 1, :, c] = second

    qe = (q * jnp.exp(b)).astype(BF16)
    kd = (k * jnp.exp(btot - b)).astype(BF16)
    dec = jnp.exp(btot)
    v_t = v.T
    tok_blk = lax.broadcasted_iota(jnp.int32, (128, tt), 1) // SUB
    hr = lax.broadcasted_iota(jnp.int32, (128, 128), 0) // HDIM
    hc = lax.broadcasted_iota(jnp.int32, (128, 128), 1) // HDIM
    same_head = hr == hc
    order = range(nblk - 1, -1, -1) if reverse else range(nblk)
    for g in range(REC_W // 128):
        c = slice(128 * g, 128 * (g + 1))
        v_tg = v_t[c, :]
        v_stack = jnp.concatenate([jnp.where(tok_blk == j, v_tg, 0.0).astype(BF16) for j in range(nblk)], axis=0)
        upd = _dot(v_stack, kd[:, c])
        st = st_ref[g]
        for j in order:
            r0 = j * SUB
            inter_ref[r0:r0 + SUB, c] = _dot_nt(qe[r0:r0 + SUB, c], st.astype(BF16))
            st = st * dec[r0:r0 + 1, c] + jnp.where(same_head, upd[128 * j:128 * (j + 1)], 0.0)
        st_ref[g] = st
    return intra_ref[...].reshape(tt, REC_W) + inter_ref[...]


def _hgrn_kernel(qf_ref, vf_ref, kf_ref, lff_ref, qb_ref, vb_ref, kb_ref, lfb_ref, s0_ref, seg_ref,
                 of_ref, ob_ref, sout_ref, st_ref, inter_ref, intra_ref):
    i = pl.program_id(1)

    @pl.when(i == 0)
    def _():
        st_ref[...] = s0_ref[0]

    seg = seg_ref[...]
    of_ref[...] = _hgrn_dir(qf_ref[...].astype(F32), kf_ref[...].astype(F32), vf_ref[...].astype(F32),
                            lff_ref[...], seg, st_ref.at[0], inter_ref, intra_ref, False).astype(BF16)
    ob_ref[...] = _hgrn_dir(qb_ref[...].astype(F32), kb_ref[...].astype(F32), vb_ref[...].astype(F32),
                            lfb_ref[...], seg, st_ref.at[1], inter_ref, intra_ref, True).astype(BF16)

    @pl.when(i == pl.num_programs(1) - 1)
    def _():
        sout_ref[0] = st_ref[...]


def _hgrn_scan(hq, hv, kf, lff, kb, lfb, s0, seg, batch, seq):
    tt = min(256, seq)
    n = seq // tt
    fwd = pl.BlockSpec((tt, REC_W), lambda b, i: (b * n + i, 0))
    bwd = pl.BlockSpec((tt, REC_W), lambda b, i: (b * n + n - 1 - i, 0))
    st_spec = pl.BlockSpec((1, 2, REC_W // 128, 128, 128), lambda b, i: (b, 0, 0, 0, 0))
    rows = batch * seq
    return pl.pallas_call(
        _hgrn_kernel,
        grid=(batch, n),
        in_specs=[fwd, fwd, fwd, fwd, bwd, bwd, bwd, bwd, st_spec,
                  pl.BlockSpec((REC_W, REC_W), lambda b, i: (0, 0))],
        out_specs=[fwd, bwd, st_spec],
        out_shape=[jax.ShapeDtypeStruct((rows, REC_W), BF16), jax.ShapeDtypeStruct((rows, REC_W), BF16),
                   jax.ShapeDtypeStruct(s0.shape, F32)],
        scratch_shapes=[pltpu.VMEM((2, REC_W // 128, 128, 128), F32), pltpu.VMEM((tt, REC_W), F32),
                        pltpu.VMEM((tt // SUB, 2, 8, REC_W), F32)],
        compiler_params=_params("arbitrary", "arbitrary"),
        name="hgrn_scan",
    )(hq, hv, kf, lff, hq, hv, kb, lfb, s0, seg)


def _ret_kernel(lg_ref, qf_ref, kf_ref, vf_ref, qb_ref, kb_ref, vb_ref, s0_ref, lgrow_ref,
                of_ref, ob_ref, sout_ref, st_ref, dmask_ref, tab_ref, o_scr):
    i = pl.program_id(1)
    tt = qf_ref.shape[0]

    @pl.when(i == 0)
    def _():
        st_ref[...] = s0_ref[0]
        row = lax.broadcasted_iota(jnp.int32, (tt, tt), 0)
        col = lax.broadcasted_iota(jnp.int32, (tt, tt), 1)
        dist = (row - col).astype(F32)
        for h in range(HEADS):
            fw = jnp.where(dist >= 0, jnp.exp(dist * lg_ref[0, h]), 0.0)
            bw = jnp.where(dist <= 0, jnp.exp(-dist * lg_ref[1, h]), 0.0)
            dmask_ref[h] = fw + bw
        pos = lax.broadcasted_iota(jnp.int32, (tt, REC_W), 0).astype(F32)
        lgf = lgrow_ref[0:1, :]
        lgb = lgrow_ref[1:2, :]
        tab_ref[0] = jnp.exp((pos + 1.0) * lgf)
        tab_ref[1] = jnp.exp((tt - 1.0 - pos) * lgf)
        tab_ref[2] = jnp.exp((tt - pos) * lgb)
        tab_ref[3] = jnp.exp(pos * lgb)

    tile_f = jnp.exp(tt * lgrow_ref[0:1, :])
    tile_b = jnp.exp(tt * lgrow_ref[1:2, :])

    q = qf_ref[...]
    k = kf_ref[...]
    v = vf_ref[...]
    qe = (q.astype(F32) * tab_ref[0]).astype(BF16)
    ke = (k.astype(F32) * tab_ref[1]).astype(BF16)
    for h in range(HEADS):
        c = slice(h * HDIM, (h + 1) * HDIM)
        s = _dot_nt(q[:, c], k[:, c]) * dmask_ref[h]
        st = st_ref[0, h]
        o_scr[:, c] = _dot(s.astype(BF16), v[:, c]) + _dot(qe[:, c], st.astype(BF16))
        st_ref[0, h] = st * tile_f[:, c] + _dot_tn(ke[:, c], v[:, c])
    of_ref[...] = o_scr[...].astype(BF16)

    q = qb_ref[...]
    k = kb_ref[...]
    v = vb_ref[...]
    qe = (q.astype(F32) * tab_ref[2]).astype(BF16)
    ke = (k.astype(F32) * tab_ref[3]).astype(BF16)
    for h in range(HEADS):
        c = slice(h * HDIM, (h + 1) * HDIM)
        st = st_ref[1, h]
        o_scr[:, c] = _dot(qe[:, c], st.astype(BF16))
        st_ref[1, h] = st * tile_b[:, c] + _dot_tn(ke[:, c], v[:, c])
    ob_ref[...] = o_scr[...].astype(BF16)

    @pl.when(i == pl.num_programs(1) - 1)
    def _():
        sout_ref[0] = st_ref[...]


def _ret_scan(rq, rk, rv, s0, lg, lgrow, batch, seq):
    tt = min(256, seq)
    n = seq // tt
    fwd = pl.BlockSpec((tt, REC_W), lambda b, i: (b * n + i, 0))
    bwd = pl.BlockSpec((tt, REC_W), lambda b, i: (b * n + n - 1 - i, 0))
    st_spec = pl.BlockSpec((1, 2, HEADS, HDIM, HDIM), lambda b, i: (b, 0, 0, 0, 0))
    rows = batch * seq
    return pl.pallas_call(
        _ret_kernel,
        grid=(batch, n),
        in_specs=[pl.BlockSpec(memory_space=pltpu.SMEM), fwd, fwd, fwd, bwd, bwd, bwd, st_spec,
                  pl.BlockSpec((8, REC_W), lambda b, i: (0, 0))],
        out_specs=[fwd, bwd, st_spec],
        out_shape=[jax.ShapeDtypeStruct((rows, REC_W), BF16), jax.ShapeDtypeStruct((rows, REC_W), BF16),
                   jax.ShapeDtypeStruct(s0.shape, F32)],
        scratch_shapes=[pltpu.VMEM((2, HEADS, HDIM, HDIM), F32), pltpu.VMEM((HEADS, tt, tt), F32),
                        pltpu.VMEM((4, tt, REC_W), F32), pltpu.VMEM((tt, REC_W), F32)],
        compiler_params=_params("arbitrary", "arbitrary"),
        name="ret_scan",
    )(lg, rq, rk, rv, rq, rk, rv, s0, lgrow)


def _attn_kernel(lam_ref, q_ref, kc_ref, vct_ref, k_ref, vt_ref, gain_ref, o_ref, s_ref, *, with_latent, kchunk,
                 piece, out_scale):
    tq = q_ref.shape[0]
    q = q_ref[...]
    lane = lax.broadcasted_iota(jnp.int32, (tq, 128), 1)
    zero = jnp.zeros_like(q)
    qs = jnp.concatenate([jnp.where(lane < DA_HEAD_DIM, q, zero),
                          jnp.where(lane >= DA_HEAD_DIM, q, zero)], axis=0)
    nchunks = (k_ref.shape[0] // kchunk) if with_latent else 0

    def scores(kblk):
        return _dot_nt(kblk, qs)

    def absorb(s, vtblk, m, acc):
        n = s.shape[0]
        part = jnp.max(s.reshape(n // 128, 128, 2 * tq), axis=0) if n > 128 else s
        m_new = jnp.maximum(m, jnp.max(part, axis=0, keepdims=True))
        p = jnp.exp2((s - m_new).astype(BF16))
        return m_new, jnp.exp2(m - m_new) * acc + _dot(vtblk, p)

    def latent_k(c):
        return k_ref[pl.ds(pl.multiple_of(c * kchunk, kchunk), kchunk), :]

    def latent_vt(c):
        return vt_ref[0, :, pl.ds(pl.multiple_of(c * kchunk, kchunk), kchunk)]

    m, acc = absorb(scores(kc_ref[...]), vct_ref[0], jnp.full((1, 2 * tq), NEG_BIG, F32),
                    jnp.zeros((128, 2 * tq), F32))
    if with_latent:
        pk = min(piece, kchunk)
        npieces = kchunk // pk

        def col_max(x):
            if pk > 128:
                x = jnp.max(x.reshape(pk // 128, 128, 2 * tq), axis=0)
            return jnp.max(x.reshape(x.shape[0] // 8, 8, 2 * tq), axis=0)

        def fused(cur, c, m_, acc_, nxt):
            part = jnp.full((8, 2 * tq), NEG_BIG, F32)
            for t in range(npieces):
                if nxt is not None:
                    r0 = pl.multiple_of((c + 1) * kchunk + t * pk, pk)
                    s_new = scores(k_ref[pl.ds(r0, pk), :])
                    s_ref[nxt, t * pk:(t + 1) * pk, :] = s_new
                    part = jnp.maximum(part, col_max(s_new))
                c0 = pl.multiple_of(c * kchunk + t * pk, pk)
                p = jnp.exp2((s_ref[cur, t * pk:(t + 1) * pk, :] - m_).astype(BF16))
                acc_ = acc_ + _dot(vt_ref[0, :, pl.ds(c0, pk)], p)
            if nxt is None:
                return m_, acc_
            m_new = jnp.maximum(m_, jnp.max(part, axis=0, keepdims=True))
            return m_new, acc_ * jnp.exp2(m_ - m_new)

        s0 = scores(latent_k(0))
        s_ref[0] = s0
        m0 = jnp.maximum(m, jnp.max(jnp.max(s0.reshape(kchunk // 128, 128, 2 * tq), axis=0), axis=0, keepdims=True))
        acc = acc * jnp.exp2(m - m0)
        m = m0

        def pair(c, carry, last):
            m_, acc_ = fused(0, c, *carry, 1)
            return fused(1, c + 1, m_, acc_, None if last else 0)

        if nchunks > 2:
            m, acc = lax.fori_loop(0, nchunks // 2 - 1, lambda j, cr: pair(2 * j, cr, False), (m, acc))
        m, acc = pair(nchunks - 2, (m, acc), True)

    a1 = acc[:, :tq]
    a2 = acc[:, tq:]
    o = a1 / a1[DA_VDIM:DA_VDIM + 1, :] - lam_ref[0] * (a2 / a2[DA_VDIM:DA_VDIM + 1, :])
    row = lax.broadcasted_iota(jnp.int32, (128, tq), 0)
    o = jnp.where(row < DA_VDIM, o, 0.0)
    ms = jnp.sum(o * o, axis=0, keepdims=True) * (1.0 / DA_VDIM)
    o_ref[...] = (o * lax.rsqrt(ms + NORM_EPS) * gain_ref[...] * out_scale).T.astype(BF16)


def _diff_attention(lam, q, kc, vc, k, v, gain, batch, seq_q, ctx_len, seq_k, out_scale, q_is_ctx):
    tq = min(512, seq_q)
    nq = seq_q // tq
    with_latent = not q_is_ctx
    if with_latent:
        kchunk = min(2048, seq_k // 2)
        assert seq_k % (2 * kchunk) == 0
    else:
        k, v = kc, vc
        seq_k, kchunk = ctx_len, ctx_len
    kern = functools.partial(_attn_kernel, with_latent=with_latent, kchunk=kchunk, piece=512, out_scale=out_scale)
    return pl.pallas_call(
        kern,
        grid=(batch, DA_HEADS, nq),
        in_specs=[
            pl.BlockSpec(memory_space=pltpu.SMEM),
            pl.BlockSpec((tq, 128), lambda b, h, i: (b * nq + i, h)),
            pl.BlockSpec((ctx_len, 128), lambda b, h, i: (b, h)),
            pl.BlockSpec((1, 128, ctx_len), lambda b, h, i: (b, h, 0)),
            pl.BlockSpec((seq_k, 128), lambda b, h, i: (b, h)),
            pl.BlockSpec((1, 128, seq_k), lambda b, h, i: (b, h, 0)),
            pl.BlockSpec((128, 1), lambda b, h, i: (h, 0)),
        ],
        out_specs=pl.BlockSpec((tq, 128), lambda b, h, i: (b * nq + i, h)),
        out_shape=jax.ShapeDtypeStruct((batch * seq_q, DA_X), BF16),
        scratch_shapes=[pltpu.VMEM((2, kchunk, 2 * tq), F32)],
        compiler_params=_params("arbitrary", "arbitrary", "arbitrary"),
        name="diff_attention",
    )(lam, q, kc, vc, k, v, gain)


def _outproj_kernel(x_ref, hof_ref, hob_ref, hg_ref, rof_ref, rob_ref, rg_ref, da_ref, seg_ref, w_ref,
                    hn_ref, rn_ref, n1_ref, n2_ref, g1_ref, sc2_ref, sh2_ref, rw_ref, rb_ref,
                    x1_ref, tok_ref, idx_ref, gate_ref):
    seg = seg_ref[...]

    def gated_head_norm(o, gain, gate):
        ms = _dot((o * o).astype(BF16), seg) * (1.0 / HDIM)
        return (o * lax.rsqrt(ms + NORM_EPS) * gain * (gate * _sigmoid(gate))).astype(BF16)

    a = gated_head_norm(hof_ref[...].astype(F32) + hob_ref[...].astype(F32), hn_ref[...], hg_ref[...].astype(F32))
    b = gated_head_norm(rof_ref[...].astype(F32) + rob_ref[...].astype(F32), rn_ref[...], rg_ref[...].astype(F32))
    y = (_dot(a, w_ref[0:REC_W, :]) + _dot(b, w_ref[REC_W:2 * REC_W, :])
         + _dot(da_ref[...], w_ref[2 * REC_W:2 * REC_W + DA_X, :]))
    x1 = x_ref[...] + g1_ref[0] * _rms(y, n1_ref[...])
    x1_ref[...] = x1
    tok = _rms(x1, n2_ref[...]) * (1.0 + sc2_ref[0]) + sh2_ref[0]
    tok_ref[...] = tok.astype(BF16)
    logits = _dot_nt(rw_ref[...], tok) + rb_ref[...]
    eid = lax.broadcasted_iota(jnp.int32, logits.shape, 0).astype(F32)
    vals, ids = [], []
    for _ in range(TOP_K):
        best = jnp.max(logits, axis=0, keepdims=True)
        first = jnp.min(jnp.where(logits == best, eid, float(N_EXPERTS)), axis=0, keepdims=True)
        vals.append(best)
        ids.append(first)
        logits = jnp.where(eid == first, -jnp.inf, logits)
    ex = [jnp.exp(v - vals[0]) for v in vals]
    total = ex[0] + ex[1] + ex[2] + ex[3]
    idx_ref[...] = jnp.concatenate(ids, axis=0).astype(jnp.int32)
    gate_ref[...] = jnp.concatenate(ex, axis=0) / total


def _outproj(xf, seq, streams, seg, w_out, hn, rn, n1, n2, g1, sc2, sh2, rw, rb):
    rows, d = xf.shape
    tm = min(256, seq)
    nb = seq // tm
    row_spec = lambda w: pl.BlockSpec((tm, w), lambda i: (i, 0))
    full = lambda a: pl.BlockSpec(a.shape, lambda i: (0,) * a.ndim)
    mod_spec = pl.BlockSpec((1, 1, d), lambda i: (i // nb, 0, 0))
    hof, hob, hg, rof, rob, rg, da = streams
    return pl.pallas_call(
        _outproj_kernel,
        grid=(rows // tm,),
        in_specs=[row_spec(d)] + [row_spec(REC_W)] * 6 + [row_spec(DA_X), full(seg), full(w_out),
                  full(hn), full(rn), full(n1), full(n2), mod_spec, mod_spec, mod_spec, full(rw), full(rb)],
        out_specs=[row_spec(d), row_spec(d), pl.BlockSpec((TOP_K, tm), lambda i: (0, i)),
                   pl.BlockSpec((TOP_K, tm), lambda i: (0, i))],
        out_shape=[jax.ShapeDtypeStruct((rows, d), F32), jax.ShapeDtypeStruct((rows, d), BF16),
                   jax.ShapeDtypeStruct((TOP_K, rows), jnp.int32), jax.ShapeDtypeStruct((TOP_K, rows), F32)],
        compiler_params=_params("arbitrary"),
        name="outproj",
    )(xf, hof, hob, hg, rof, rob, rg, da, seg, w_out, hn, rn, n1, n2, g1, sc2, sh2, rw, rb)


def _ffn_kernel(be_ref, na_ref, x_ref, w1_ref, w2_ref, b1g_ref, b1l_ref, b2_ref, pe_ref, po_ref, yprev_ref, y_ref,
                w1g_s, w1l_s, w2_s, *, blk0):
    del yprev_ref
    i = pl.program_id(0)
    blk = i + blk0
    active = blk < na_ref[0]
    fresh = jnp.logical_or(i == 0, be_ref[blk] != be_ref[jnp.maximum(blk - 1, 0)])

    @pl.when(fresh)
    def _():
        n = w1_ref.shape[3]
        for c in range(n // 256):
            wb = w1_ref[0, 0, :, 256 * c:256 * (c + 1)].astype(BF16)
            w1g_s[:, 128 * c:128 * (c + 1)] = _dot(wb, pe_ref[...]).astype(BF16)
            w1l_s[:, 128 * c:128 * (c + 1)] = _dot(wb, po_ref[...]).astype(BF16)
        w2_s[...] = w2_ref[0, 0].astype(BF16)

    @pl.when(jnp.logical_not(active))
    def _():
        y_ref[...] = jnp.zeros(y_ref.shape, y_ref.dtype)

    @pl.when(active)
    def _():
        x = x_ref[...]
        glu = jnp.minimum(_dot(x, w1g_s[...]) + b1g_ref[0], SWIGLU_LIMIT)
        lin = jnp.clip(_dot(x, w1l_s[...]) + b1l_ref[0], -SWIGLU_LIMIT, SWIGLU_LIMIT)
        act = glu * _sigmoid(SWIGLU_ALPHA * glu) * (lin + 1.0)
        y_ref[...] = (_dot(act.astype(BF16), w2_s[...]) + b2_ref[0]).astype(y_ref.dtype)


def _expert_ffn(block_expert, n_active, tok, tok_sorted, w1, w2, b1g, b1l, b2, layer, bm, nparts=4):
    p = tok_sorted.shape[0]
    d = tok.shape[1]
    f = w2.shape[2]
    nblocks = p // bm
    src = jnp.arange(256)[:, None]
    dst = jnp.arange(128)[None, :]
    pe = (src == 2 * dst).astype(BF16)
    po = (src == 2 * dst + 1).astype(BF16)
    sel = pl.BlockSpec((256, 128), lambda i, be, na: (0, 0))
    bounds = [nblocks * j // nparts for j in range(nparts + 1)]
    yg = jnp.zeros((8, 128), BF16)
    for j in range(nparts):
        b0, nb = bounds[j], bounds[j + 1] - bounds[j]
        xg = tok.at[tok_sorted[b0 * bm:(b0 + nb) * bm]].get(mode="promise_in_bounds")
        wspec = lambda s, b0=b0: pl.BlockSpec((1, 1) + s, lambda i, be, na: (layer, be[i + b0], 0, 0))
        bspec = lambda s, b0=b0: pl.BlockSpec((1,) + s, lambda i, be, na: (be[i + b0], 0, 0))
        yg = pl.pallas_call(
            functools.partial(_ffn_kernel, blk0=b0),
            grid_spec=pltpu.PrefetchScalarGridSpec(
                num_scalar_prefetch=2,
                grid=(nb,),
                in_specs=[pl.BlockSpec((bm, d), lambda i, be, na: (i, 0)),
                          wspec((d, 2 * f)), wspec((f, d)), bspec((1, f)), bspec((1, f)), bspec((1, d)), sel, sel,
                          pl.BlockSpec(memory_space=pl.ANY)],
                out_specs=pl.BlockSpec((bm, d), lambda i, be, na, b0=b0: (i + b0, 0)),
                scratch_shapes=[pltpu.VMEM((d, f), BF16), pltpu.VMEM((d, f), BF16), pltpu.VMEM((f, d), BF16)],
            ),
            out_shape=jax.ShapeDtypeStruct((p, d), BF16),
            input_output_aliases={10: 0} if j else {},
            compiler_params=_params("arbitrary"),
            name="expert_ffn",
        )(block_expert, n_active, xg, w1, w2, b1g, b1l, b2, pe, po, yg)
    return yg


def _resid_kernel(x_ref, y0_ref, y1_ref, y2_ref, y3_ref, gate_ref, n_ref, g_ref, o_ref):
    gates = gate_ref[...]
    f = y0_ref[...].astype(F32) * gates[:, 0:1]
    for k, y_ref in enumerate((y1_ref, y2_ref, y3_ref), start=1):
        f = f + y_ref[...].astype(F32) * gates[:, k:k + 1]
    o_ref[...] = x_ref[...] + g_ref[0] * _rms(f, n_ref[...])


def _ffn_residual(xf, ys, gates, seq, n3, g2):
    rows, d = xf.shape
    tm = min(512, seq)
    nb = seq // tm
    row_spec = pl.BlockSpec((tm, d), lambda i: (i, 0))
    return pl.pallas_call(
        _resid_kernel,
        grid=(rows // tm,),
        in_specs=[row_spec] * 5 + [pl.BlockSpec((tm, TOP_K), lambda i: (i, 0)), pl.BlockSpec((1, d), lambda i: (0, 0)),
                                   pl.BlockSpec((1, 1, d), lambda i: (i // nb, 0, 0))],
        out_specs=row_spec,
        out_shape=jax.ShapeDtypeStruct((rows, d), F32),
        compiler_params=_params("arbitrary"),
        name="ffn_residual",
    )(xf, *ys, gates, n3, g2)


def _rope_tables(pos, dim):
    inv = 1.0 / (ROPE_BASE ** (jnp.arange(0, dim, 2, dtype=F32) / dim))
    ang = pos.astype(F32)[:, None] * inv[None, :]
    return jnp.cos(ang), jnp.sin(ang)


def _rot_cols(w, head_dim, halves):
    d, n = w.shape
    g = head_dim // halves
    w4 = w.reshape(d, n // g, 2, g // 2)
    return jnp.concatenate([-w4[:, :, 1], w4[:, :, 0]], axis=-1).reshape(d, n)


def _prep_w_in(w):
    rq, rk = w[:, _C_RQ:_C_RQ + REC_W], w[:, _C_RK:_C_RK + REC_W]
    dq, dk = w[:, _C_DQ:_C_DQ + DA_W], w[:, _C_DK:_C_DK + DA_W]
    return jnp.concatenate([w, _rot_cols(rq, HDIM, 1), _rot_cols(rk, HDIM, 1),
                            _rot_cols(dq, DA_HEAD_DIM, 2), _rot_cols(dk, DA_HEAD_DIM, 2)], axis=1).astype(BF16)


def _routing(top_idx_t, bm):
    n = top_idx_t.shape[1]
    a = n * TOP_K
    flat_e = top_idx_t.T.reshape(a)
    sorted_e, order = lax.sort_key_val(flat_e, jnp.arange(a, dtype=jnp.int32))
    experts = jnp.arange(N_EXPERTS + 1, dtype=jnp.int32)
    bounds = jnp.sum((flat_e[None, :] < experts[:, None]).astype(jnp.int32), axis=1)
    start, counts = bounds[:-1], bounds[1:] - bounds[:-1]
    padded = (counts + bm - 1) // bm * bm
    pend = jnp.cumsum(padded)
    shift = pend - padded - start
    dest = jnp.arange(a, dtype=jnp.int32) + shift[sorted_e]
    _, pos = lax.sort_key_val(order, dest)
    nblocks = -(-(a + N_EXPERTS * (bm - 1)) // bm)
    block_row = jnp.arange(nblocks, dtype=jnp.int32) * bm
    block_expert = jnp.minimum(jnp.sum((pend[None, :] <= block_row[:, None]).astype(jnp.int32), axis=1),
                               N_EXPERTS - 1)
    row_e = jnp.repeat(block_expert, bm)
    rank = jnp.arange(nblocks * bm, dtype=jnp.int32) - shift[row_e]
    valid = rank < bounds[row_e + 1]
    tok_sorted = jnp.where(valid, order[jnp.clip(rank, 0, a - 1)] // TOP_K, 0)
    n_active = (pend[-1:] // bm).astype(jnp.int32)
    return tok_sorted, pos, block_expert, n_active


def kernel(x, c, ctx, c_ctx, mod_w, mod_b, norm_g, w_in, hgrn_lb, hgrn_norm, ret_decay, ret_norm, da_lambda,
           da_subln, w_out, router_w, router_b, w1, b1, w2, b2):
    B, S, D = x.shape
    C = ctx.shape[1]
    depth = mod_w.shape[0]
    bm = 256

    pos = jnp.arange(S)
    cr, sr = _rope_tables(pos // GRID_W, DA_HEAD_DIM // 2)
    cc, sc_ = _rope_tables(pos % GRID_W, DA_HEAD_DIM // 2)
    cs, ss = _rope_tables(pos, HDIM)
    cosr = jnp.tile(jnp.concatenate([cs, cs], -1), (1, 128 // HDIM))
    sinr = jnp.tile(jnp.concatenate([ss, ss], -1), (1, 128 // HDIM))
    cosd = jnp.tile(jnp.concatenate([cr, cr, cc, cc], -1), (1, 128 // DA_HEAD_DIM))
    sind = jnp.tile(jnp.concatenate([sr, sr, sc_, sc_], -1), (1, 128 // DA_HEAD_DIM))
    ones_r, zeros_r = jnp.ones((C, 128), F32), jnp.zeros((C, 128), F32)
    ones_d, zeros_d = ones_r, zeros_r

    lb_cum = jnp.cumsum(jax.nn.softmax(hgrn_lb.astype(F32), axis=0), axis=0)
    lower = lb_cum - lb_cum[0:1]

    cvec = jnp.zeros((8, D), F32).at[:B].set(c).at[B].set(c_ctx)
    mods = _modulation(cvec, mod_w, mod_b)

    head_id = jnp.arange(REC_W) // HDIM
    seg = (head_id[:, None] == head_id[None, :]).astype(BF16)

    xf = x.reshape(B * S, D)
    xc = ctx.reshape(B * C, D)
    zero_state = jnp.zeros((B, 2, HEADS, HDIM, HDIM), F32)

    for layer in range(depth):
        need_ctx = layer < depth - 1
        lam_init = 0.8 - 0.6 * math.exp(-0.3 * layer)
        m6 = mods[layer].reshape(8, 6, D)
        lat = lambda k: m6[:B, k][:, None, :]
        cxm = lambda k: jnp.broadcast_to(m6[B, k][None, None, :], (B, 1, D))
        ng = norm_g[layer]
        lb = lower[layer]
        lbc = jnp.zeros((8, REC_W), F32).at[0].set(jnp.log(lb)).at[1].set(jnp.log1p(-lb)).at[2].set(1.0 - lb)
        w_all = _prep_w_in(w_in[layer])
        log_gamma = jnp.log1p(-jnp.exp2(-ret_decay[layer].astype(F32)))
        lgrow = jnp.zeros((8, REC_W), F32).at[:2].set(jnp.repeat(log_gamma, HDIM, axis=1))
        lamv = da_lambda[layer].astype(F32)
        lam = (jnp.exp(jnp.sum(lamv[0] * lamv[1])) - jnp.exp(jnp.sum(lamv[2] * lamv[3])) + lam_init).reshape(1)
        hn = jnp.tile(hgrn_norm[layer], HEADS)[None, :]
        rn = jnp.tile(ret_norm[layer], HEADS)[None, :]
        dn = jnp.tile(jnp.concatenate([da_subln[layer], jnp.zeros((128 - DA_VDIM,), F32)]), DA_HEADS)[:, None]
        wo = w_out[layer]
        wo_da = jnp.pad(wo[2 * REC_W:].reshape(DA_HEADS, DA_VDIM, D), ((0, 0), (0, 128 - DA_VDIM), (0, 0)))
        w_out_b = jnp.concatenate([wo[:2 * REC_W], wo_da.reshape(DA_X, D)], axis=0).astype(BF16)
        rw = router_w[layer].T
        rb = router_b[layer][:, None]

        (hq, hv, hg, kf, lff, kb, lfb, rq, rk, rv, rg, dq, dk, dvx) = _inproj(
            xf, S, ng[0:1], lat(1), lat(0), w_all, lbc, cosr, sinr, cosd, sind)
        (hq_c, hv_c, hg_c, kf_c, lff_c, kb_c, lfb_c, rq_c, rk_c, rv_c, rg_c, dq_c, dk_c, dvx_c) = _inproj(
            xc, C, ng[0:1], cxm(1), cxm(0), w_all, lbc, ones_r, zeros_r, ones_d, zeros_d)

        hof_c, hob_c, hs = _hgrn_scan(hq_c, hv_c, kf_c, lff_c, kb_c, lfb_c,
                                      jnp.zeros((B, 2, REC_W // 128, 128, 128), F32), seg, B, C)
        hof, hob, _ = _hgrn_scan(hq, hv, kf, lff, kb, lfb, hs, seg, B, S)
        rof_c, rob_c, rs = _ret_scan(rq_c, rk_c, rv_c, zero_state, log_gamma, lgrow, B, C)
        rof, rob, _ = _ret_scan(rq, rk, rv, rs, log_gamma, lgrow, B, S)
        out_scale = 1.0 - lam_init
        da = _diff_attention(lam, dq, dk_c, dvx_c, dk, dvx, dn, B, S, C, S, out_scale, False)

        x1, tok, top_idx_t, gates_t = _outproj(xf, S, (hof, hob, hg, rof, rob, rg, da), seg, w_out_b, hn, rn,
                                               ng[1:2], ng[2:3], lat(2), lat(4), lat(3), rw, rb)
        if need_ctx:
            da_c = _diff_attention(lam, dq_c, dk_c, dvx_c, None, None, dn, B, C, C, C, out_scale, True)
            xc1, tok_c, top_idx_c, gates_c = _outproj(xc, C, (hof_c, hob_c, hg_c, rof_c, rob_c, rg_c, da_c), seg,
                                                      w_out_b, hn, rn, ng[1:2], ng[2:3], cxm(2), cxm(4), cxm(3),
                                                      rw, rb)
            tok = jnp.concatenate([tok, tok_c], axis=0)
            top_idx_t = jnp.concatenate([top_idx_t, top_idx_c], axis=1)
            gates_t = jnp.concatenate([gates_t, gates_c], axis=1)

        n_tok = tok.shape[0]
        gates = gates_t.T
        tok_sorted, posn, block_expert, n_active = _routing(top_idx_t, bm)
        b1g = b1[layer][:, None, 0::2]
        b1l = b1[layer][:, None, 1::2]
        b2l = b2[layer][:, None, :]
        yg = _expert_ffn(block_expert, n_active, tok, tok_sorted, w1, w2, b1g, b1l, b2l, layer, bm)
        pos2 = posn.reshape(n_tok, TOP_K)

        def expert_rows(lo, hi):
            return [yg.at[pos2[lo:hi, k]].get(mode="promise_in_bounds") for k in range(TOP_K)]

        xf = _ffn_residual(x1, expert_rows(0, B * S), gates[:B * S], S, ng[3:4], lat(5))
        if need_ctx:
            xc = _ffn_residual(xc1, expert_rows(B * S, n_tok), gates[B * S:], C, ng[3:4], cxm(5))
    return xf.reshape(B, S, D)
```

```python
import functools
import math

import jax
import jax.numpy as jnp
from jax import lax
from jax.experimental import pallas as pl
from jax.experimental.pallas import tpu as pltpu

F32 = jnp.float32
BF16 = jnp.bfloat16

GRID_W = 64
HEADS = 6
HDIM = 64
REC_W = HEADS * HDIM
DA_HEADS = 4
DA_HEAD_DIM = 32
DA_MAPS = 2 * DA_HEADS
DA_W = DA_MAPS * DA_HEAD_DIM
DA_VDIM = 2 * DA_HEAD_DIM
DA_X = DA_HEADS * 128
_Q_SCALE = DA_HEAD_DIM ** -0.5 * math.log2(math.e)
_BOUND_MARGIN = 1.01
_MAX_EXP2_SPAN = 100.0
ROPE_BASE = 10000.0
N_EXPERTS = 32
TOP_K = 4
SWIGLU_ALPHA = 1.702
SWIGLU_LIMIT = 7.0
NORM_EPS = 1e-6

SUB = 16
NEG_BIG = -1e30
VMEM_LIMIT = 56 * 1024 * 1024

_C_HQ, _C_HFF, _C_HFB, _C_HI, _C_HG = 0, 384, 768, 1152, 1536
_C_RQ, _C_RK, _C_RV, _C_RG = 1920, 2304, 2688, 3072
_C_DQ, _C_DK, _C_DV = 3456, 3712, 3968
_C_RQR, _C_RKR, _C_DQR, _C_DKR = 4224, 4608, 4992, 5248
_W_ALL = 5504


def _dot(a, b):
    return jnp.dot(a, b, preferred_element_type=F32)


def _dot_nt(a, b):
    return lax.dot_general(a, b, (((1,), (1,)), ((), ())), preferred_element_type=F32)


def _dot_tn(a, b):
    return lax.dot_general(a, b, (((0,), (0,)), ((), ())), preferred_element_type=F32)


def _sigmoid(x):
    return 1.0 / (1.0 + jnp.exp(-x))


def _rms(x, g):
    ms = jnp.mean(x * x, axis=-1, keepdims=True)
    return x * lax.rsqrt(ms + NORM_EPS) * g


def _params(*sem):
    return pltpu.CompilerParams(dimension_semantics=sem, vmem_limit_bytes=VMEM_LIMIT)


def _mod_kernel(c_ref, w_ref, b_ref, o_ref):
    c = c_ref[...]
    o_ref[0] = _dot(c * _sigmoid(c), w_ref[0]) + b_ref[0]


def _modulation(cvec, mod_w, mod_b):
    depth, d, n = mod_w.shape
    tn = 1536
    return pl.pallas_call(
        _mod_kernel,
        grid=(depth, n // tn),
        in_specs=[
            pl.BlockSpec((8, d), lambda l, j: (0, 0)),
            pl.BlockSpec((1, d, tn), lambda l, j: (l, 0, j)),
            pl.BlockSpec((1, 1, tn), lambda l, j: (l, 0, j)),
        ],
        out_specs=pl.BlockSpec((1, 8, tn), lambda l, j: (l, 0, j)),
        out_shape=jax.ShapeDtypeStruct((depth, 8, n), F32),
        compiler_params=_params("arbitrary", "arbitrary"),
        name="modulation",
    )(cvec, mod_w, mod_b.reshape(depth, 1, n))


def _inproj_kernel(x_ref, g_ref, sc_ref, sh_ref, w_ref, lbc_ref, cosr_ref, sinr_ref, cosd_ref, sind_ref,
                   hq_ref, hv_ref, hg_ref, kf_ref, lff_ref, kb_ref, lfb_ref,
                   rq_ref, rk_ref, rv_ref, rg_ref, dq_ref, dk_ref, dvt_ref):
    x = x_ref[...]
    h = _rms(x, g_ref[...]) * (1.0 + sc_ref[0]) + sh_ref[0]
    hb = h.astype(BF16)

    def proj(c0, n):
        return _dot(hb, w_ref[:, c0:c0 + n])

    hq_ref[...] = proj(_C_HQ, REC_W).astype(BF16)
    hv_ref[...] = proj(_C_HI, REC_W).astype(BF16)
    hg_ref[...] = proj(_C_HG, REC_W).astype(BF16)

    log_lb = lbc_ref[0:1, :]
    log_1m = lbc_ref[1:2, :]
    one_m = lbc_ref[2:3, :]

    def gates(z):
        log_sig = jnp.minimum(z, 0.0) - jnp.log1p(jnp.exp(-jnp.abs(z)))
        t = log_1m + log_sig
        m = jnp.maximum(log_lb, t)
        logf = m + jnp.log1p(jnp.exp(-jnp.abs(log_lb - t)))
        return one_m / (1.0 + jnp.exp(z)), logf

    k, lf = gates(proj(_C_HFF, REC_W))
    kf_ref[...] = k.astype(BF16)
    lff_ref[...] = lf
    k, lf = gates(proj(_C_HFB, REC_W))
    kb_ref[...] = k.astype(BF16)
    lfb_ref[...] = lf

    cosr = jnp.concatenate([cosr_ref[...]] * (REC_W // 128), axis=1)
    sinr = jnp.concatenate([sinr_ref[...]] * (REC_W // 128), axis=1)
    rq_ref[...] = (proj(_C_RQ, REC_W) * cosr + proj(_C_RQR, REC_W) * sinr).astype(BF16)
    rk_ref[...] = ((proj(_C_RK, REC_W) * cosr + proj(_C_RKR, REC_W) * sinr) * (HDIM ** -0.5)).astype(BF16)
    rv_ref[...] = proj(_C_RV, REC_W).astype(BF16)
    rg_ref[...] = proj(_C_RG, REC_W).astype(BF16)

    cosd = jnp.concatenate([cosd_ref[...]] * (DA_W // 128), axis=1)
    sind = jnp.concatenate([sind_ref[...]] * (DA_W // 128), axis=1)
    dq = ((proj(_C_DQ, DA_W) * cosd + proj(_C_DQR, DA_W) * sind) * _Q_SCALE).astype(BF16)
    dk = (proj(_C_DK, DA_W) * cosd + proj(_C_DKR, DA_W) * sind).astype(BF16)
    dv_t = proj(_C_DV, DA_W).T.astype(BF16)
    tm = dq.shape[0]
    zero_col = jnp.zeros((tm, 64), BF16)
    ones_col = jnp.where(lax.broadcasted_iota(jnp.int32, (tm, 64), 1) == 0, 1.0, 0.0).astype(BF16)
    sub = lax.broadcasted_iota(jnp.int32, (64, tm), 0)
    ones_row = jnp.where(sub == 0, 1.0, 0.0).astype(BF16)
    for h_ in range(DA_HEADS):
        lo, mid, hi_ = 128 * h_, 128 * h_ + 64, 128 * h_ + 128
        dq_ref[:, lo:mid] = dq[:, 64 * h_:64 * h_ + 64]
        dq_ref[:, mid:hi_] = zero_col
        dk_ref[:, lo:mid] = dk[:, 64 * h_:64 * h_ + 64]
        dk_ref[:, mid:hi_] = ones_col
        dvt_ref[0, lo:mid, :] = dv_t[64 * h_:64 * h_ + 64, :]
        dvt_ref[0, mid:hi_, :] = ones_row


def _inproj(xf, seq, gnorm, sc, sh, w_all, lbc, cosr, sinr, cosd, sind):
    rows, d = xf.shape
    tm = min(256, seq)
    nb = seq // tm
    row_spec = lambda w: pl.BlockSpec((tm, w), lambda i: (i, 0))
    tab_spec = lambda w: pl.BlockSpec((tm, w), lambda i: (i % nb, 0))
    mod_spec = pl.BlockSpec((1, 1, d), lambda i: (i // nb, 0, 0))
    widths = [REC_W] * 11 + [DA_X, DA_X]
    dtypes = [BF16, BF16, BF16, BF16, F32, BF16, F32, BF16, BF16, BF16, BF16, BF16, BF16]
    out_specs = [row_spec(w) for w in widths] + [pl.BlockSpec((1, DA_X, tm), lambda i: (i // nb, 0, i % nb))]
    out_shape = ([jax.ShapeDtypeStruct((rows, w), dt) for w, dt in zip(widths, dtypes)]
                 + [jax.ShapeDtypeStruct((rows // seq, DA_X, seq), BF16)])
    return pl.pallas_call(
        _inproj_kernel,
        grid=(rows // tm,),
        in_specs=[
            row_spec(d),
            pl.BlockSpec((1, d), lambda i: (0, 0)),
            mod_spec, mod_spec,
            pl.BlockSpec((d, _W_ALL), lambda i: (0, 0)),
            pl.BlockSpec((8, REC_W), lambda i: (0, 0)),
            tab_spec(128), tab_spec(128), tab_spec(128), tab_spec(128),
        ],
        out_specs=out_specs,
        out_shape=out_shape,
        compiler_params=_params("arbitrary"),
        name="inproj",
    )(xf, gnorm, sc, sh, w_all, lbc, cosr, sinr, cosd, sind)


def _hgrn_dir(q, k, v, lf, seg, st_ref, inter_ref, intra_ref, reverse):
    tt = q.shape[0]
    nblk = tt // SUB
    row = lax.broadcasted_iota(jnp.int32, (tt, tt), 0)
    col = lax.broadcasted_iota(jnp.int32, (tt, tt), 1)
    same = (row // SUB) == (col // SUB)
    tri = (col >= row) if reverse else (col <= row)
    l_all = jnp.where(same, 1.0, 0.0)
    l_cum = jnp.where(tri, l_all, 0.0).astype(BF16)
    l_all = l_all.astype(BF16)
    hi = lf.astype(BF16)
    r1 = lf - hi.astype(F32)
    mid = r1.astype(BF16)
    lo = (r1 - mid.astype(F32)).astype(BF16)
    b = _dot(l_cum, hi) + _dot(l_cum, mid) + _dot(l_cum, lo)
    btot = _dot(l_all, hi) + _dot(l_all, mid) + _dot(l_all, lo)

    half = lax.broadcasted_iota(jnp.int32, (nblk, 8, 128), 1)
    seg2 = seg[0:128, 0:128]
    for g in range(REC_W // 128):
        c = slice(128 * g, 128 * (g + 1))

        def halves(a):
            a4 = a[:, c].reshape(nblk, 2, 8, 128)
            return (a4[:, 1], a4[:, 0]) if reverse else (a4[:, 0], a4[:, 1])

        (q0, q1), (k0, k1), (v0, v1), (b0, b1) = halves(q), halves(k), halves(v), halves(b)

        def pairs(qa, ba, kb_, bb_, vb_, r, masked):
            sh = ((8 - r) % 8) if reverse else r
            if sh:
                kb_, bb_, vb_ = pltpu.roll(kb_, sh, 1), pltpu.roll(bb_, sh, 1), pltpu.roll(vb_, sh, 1)
            diff = ba - bb_
            if masked and r:
                diff = jnp.where((half <= 7 - r) if reverse else (half >= r), diff, NEG_BIG)
            term = (qa * kb_ * jnp.exp(diff)).reshape(nblk * 8, 128).astype(BF16)
            return _dot(term, seg2).reshape(nblk, 8, 128) * vb_

        o0 = pairs(q0, b0, k0, b0, v0, 0, True)
        o1 = pairs(q1, b1, k1, b1, v1, 0, True) + pairs(q1, b1, k0, b0, v0, 0, False)
        for r in range(1, 8):
            o0 = o0 + pairs(q0, b0, k0, b0, v0, r, True)
            o1 = o1 + pairs(q1, b1, k1, b1, v1, r, True) + pairs(q1, b1, k0, b0, v0, r, False)
        first, second = (o1, o0) if reverse else (o0, o1)
        intra_ref[:, 0, :, c] = first
        intra_ref[:, 1, :, c] = second

    qe = (q * jnp.exp(b)).astype(BF16)
    kd = (k * jnp.exp(btot - b)).astype(BF16)
    dec = jnp.exp(btot)
    v_t = v.T
    tok_blk = lax.broadcasted_iota(jnp.int32, (128, tt), 1) // SUB
    hr = lax.broadcasted_iota(jnp.int32, (128, 128), 0) // HDIM
    hc = lax.broadcasted_iota(jnp.int32, (128, 128), 1) // HDIM
    same_head = hr == hc
    order = range(nblk - 1, -1, -1) if reverse else range(nblk)
    for g in range(REC_W // 128):
        c = slice(128 * g, 128 * (g + 1))
        v_tg = v_t[c, :]
        v_stack = jnp.concatenate([jnp.where(tok_blk == j, v_tg, 0.0).astype(BF16) for j in range(nblk)], axis=0)
        upd = _dot(v_stack, kd[:, c])
        st = st_ref[g]
        for j in order:
            r0 = j * SUB
            inter_ref[r0:r0 + SUB, c] = _dot_nt(qe[r0:r0 + SUB, c], st.astype(BF16))
            st = st * dec[r0:r0 + 1, c] + jnp.where(same_head, upd[128 * j:128 * (j + 1)], 0.0)
        st_ref[g] = st
    return intra_ref[...].reshape(tt, REC_W) + inter_ref[...]


def _hgrn_kernel(qf_ref, vf_ref, kf_ref, lff_ref, qb_ref, vb_ref, kb_ref, lfb_ref, s0_ref, seg_ref,
                 of_ref, ob_ref, sout_ref, st_ref, inter_ref, intra_ref):
    i = pl.program_id(1)

    @pl.when(i == 0)
    def _():
        st_ref[...] = s0_ref[0]

    seg = seg_ref[...]
    of_ref[...] = _hgrn_dir(qf_ref[...].astype(F32), kf_ref[...].astype(F32), vf_ref[...].astype(F32),
                            lff_ref[...], seg, st_ref.at[0], inter_ref, intra_ref, False).astype(BF16)
    ob_ref[...] = _hgrn_dir(qb_ref[...].astype(F32), kb_ref[...].astype(F32), vb_ref[...].astype(F32),
                            lfb_ref[...], seg, st_ref.at[1], inter_ref, intra_ref, True).astype(BF16)

    @pl.when(i == pl.num_programs(1) - 1)
    def _():
        sout_ref[0] = st_ref[...]


def _hgrn_scan(hq, hv, kf, lff, kb, lfb, s0, seg, batch, seq):
    tt = min(256, seq)
    n = seq // tt
    fwd = pl.BlockSpec((tt, REC_W), lambda b, i: (b * n + i, 0))
    bwd = pl.BlockSpec((tt, REC_W), lambda b, i: (b * n + n - 1 - i, 0))
    st_spec = pl.BlockSpec((1, 2, REC_W // 128, 128, 128), lambda b, i: (b, 0, 0, 0, 0))
    rows = batch * seq
    return pl.pallas_call(
        _hgrn_kernel,
        grid=(batch, n),
        in_specs=[fwd, fwd, fwd, fwd, bwd, bwd, bwd, bwd, st_spec,
                  pl.BlockSpec((REC_W, REC_W), lambda b, i: (0, 0))],
        out_specs=[fwd, bwd, st_spec],
        out_shape=[jax.ShapeDtypeStruct((rows, REC_W), BF16), jax.ShapeDtypeStruct((rows, REC_W), BF16),
                   jax.ShapeDtypeStruct(s0.shape, F32)],
        scratch_shapes=[pltpu.VMEM((2, REC_W // 128, 128, 128), F32), pltpu.VMEM((tt, REC_W), F32),
                        pltpu.VMEM((tt // SUB, 2, 8, REC_W), F32)],
        compiler_params=_params("arbitrary", "arbitrary"),
        name="hgrn_scan",
    )(hq, hv, kf, lff, hq, hv, kb, lfb, s0, seg)


def _ret_kernel(lg_ref, qf_ref, kf_ref, vf_ref, qb_ref, kb_ref, vb_ref, s0_ref, lgrow_ref,
                of_ref, ob_ref, sout_ref, st_ref, dmask_ref, tab_ref, o_scr):
    i = pl.program_id(1)
    tt = qf_ref.shape[0]

    @pl.when(i == 0)
    def _():
        st_ref[...] = s0_ref[0]
        row = lax.broadcasted_iota(jnp.int32, (tt, tt), 0)
        col = lax.broadcasted_iota(jnp.int32, (tt, tt), 1)
        dist = (row - col).astype(F32)
        for h in range(HEADS):
            fw = jnp.where(dist >= 0, jnp.exp(dist * lg_ref[0, h]), 0.0)
            bw = jnp.where(dist <= 0, jnp.exp(-dist * lg_ref[1, h]), 0.0)
            dmask_ref[h] = fw + bw
        pos = lax.broadcasted_iota(jnp.int32, (tt, REC_W), 0).astype(F32)
        lgf = lgrow_ref[0:1, :]
        lgb = lgrow_ref[1:2, :]
        tab_ref[0] = jnp.exp((pos + 1.0) * lgf)
        tab_ref[1] = jnp.exp((tt - 1.0 - pos) * lgf)
        tab_ref[2] = jnp.exp((tt - pos) * lgb)
        tab_ref[3] = jnp.exp(pos * lgb)

    tile_f = jnp.exp(tt * lgrow_ref[0:1, :])
    tile_b = jnp.exp(tt * lgrow_ref[1:2, :])

    q = qf_ref[...]
    k = kf_ref[...]
    v = vf_ref[...]
    qe = (q.astype(F32) * tab_ref[0]).astype(BF16)
    ke = (k.astype(F32) * tab_ref[1]).astype(BF16)
    for h in range(HEADS):
        c = slice(h * HDIM, (h + 1) * HDIM)
        s = _dot_nt(q[:, c], k[:, c]) * dmask_ref[h]
        st = st_ref[0, h]
        o_scr[:, c] = _dot(s.astype(BF16), v[:, c]) + _dot(qe[:, c], st.astype(BF16))
        st_ref[0, h] = st * tile_f[:, c] + _dot_tn(ke[:, c], v[:, c])
    of_ref[...] = o_scr[...].astype(BF16)

    q = qb_ref[...]
    k = kb_ref[...]
    v = vb_ref[...]
    qe = (q.astype(F32) * tab_ref[2]).astype(BF16)
    ke = (k.astype(F32) * tab_ref[3]).astype(BF16)
    for h in range(HEADS):
        c = slice(h * HDIM, (h + 1) * HDIM)
        st = st_ref[1, h]
        o_scr[:, c] = _dot(qe[:, c], st.astype(BF16))
        st_ref[1, h] = st * tile_b[:, c] + _dot_tn(ke[:, c], v[:, c])
    ob_ref[...] = o_scr[...].astype(BF16)

    @pl.when(i == pl.num_programs(1) - 1)
    def _():
        sout_ref[0] = st_ref[...]


def _ret_scan(rq, rk, rv, s0, lg, lgrow, batch, seq):
    tt = min(256, seq)
    n = seq // tt
    fwd = pl.BlockSpec((tt, REC_W), lambda b, i: (b * n + i, 0))
    bwd = pl.BlockSpec((tt, REC_W), lambda b, i: (b * n + n - 1 - i, 0))
    st_spec = pl.BlockSpec((1, 2, HEADS, HDIM, HDIM), lambda b, i: (b, 0, 0, 0, 0))
    rows = batch * seq
    return pl.pallas_call(
        _ret_kernel,
        grid=(batch, n),
        in_specs=[pl.BlockSpec(memory_space=pltpu.SMEM), fwd, fwd, fwd, bwd, bwd, bwd, st_spec,
                  pl.BlockSpec((8, REC_W), lambda b, i: (0, 0))],
        out_specs=[fwd, bwd, st_spec],
        out_shape=[jax.ShapeDtypeStruct((rows, REC_W), BF16), jax.ShapeDtypeStruct((rows, REC_W), BF16),
                   jax.ShapeDtypeStruct(s0.shape, F32)],
        scratch_shapes=[pltpu.VMEM((2, HEADS, HDIM, HDIM), F32), pltpu.VMEM((HEADS, tt, tt), F32),
                        pltpu.VMEM((4, tt, REC_W), F32), pltpu.VMEM((tt, REC_W), F32)],
        compiler_params=_params("arbitrary", "arbitrary"),
        name="ret_scan",
    )(lg, rq, rk, rv, rq, rk, rv, s0, lgrow)


def _attn_kernel(lam_ref, q_ref, kc_ref, vct_ref, k_ref, vt_ref, gain_ref, o_ref, s_ref, acc_ref, kmax_ref, *,
                 with_latent, kchunk, piece, fast_chunk, fast_unroll, out_scale):
    tq = q_ref.shape[0]
    q = q_ref[...]
    lane = lax.broadcasted_iota(jnp.int32, (tq, 128), 1)
    zero = jnp.zeros_like(q)
    qs = jnp.concatenate([jnp.where(lane < DA_HEAD_DIM, q, zero),
                          jnp.where(lane >= DA_HEAD_DIM, q, zero)], axis=0)
    nchunks = (k_ref.shape[0] // kchunk) if with_latent else 0

    def scores(kblk):
        return _dot_nt(kblk, qs)

    def absorb(s, vtblk, m, acc):
        n = s.shape[0]
        part = jnp.max(s.reshape(n // 128, 128, 2 * tq), axis=0) if n > 128 else s
        m_new = jnp.maximum(m, jnp.max(part, axis=0, keepdims=True))
        p = jnp.exp2((s - m_new).astype(BF16))
        return m_new, jnp.exp2(m - m_new) * acc + _dot(vtblk, p)

    def latent_k(c):
        return k_ref[pl.ds(pl.multiple_of(c * kchunk, kchunk), kchunk), :]

    def latent_vt(c):
        return vt_ref[0, :, pl.ds(pl.multiple_of(c * kchunk, kchunk), kchunk)]

    sel_r = lax.broadcasted_iota(jnp.int32, (128, 128), 0) // DA_HEAD_DIM
    sel_c = lax.broadcasted_iota(jnp.int32, (128, 128), 1)
    sel = jnp.where(sel_r == sel_c, 1.0, 0.0)

    def key_norm2(kblk):
        kf = kblk.astype(F32)
        return jnp.max(_dot(kf * kf, sel), axis=0, keepdims=True)

    @pl.when(pl.program_id(2) == 0)
    def _():
        km = key_norm2(kc_ref[...])
        if with_latent:
            km = lax.fori_loop(0, nchunks, lambda c, a: jnp.maximum(a, key_norm2(latent_k(c))), km)
        kmax_ref[...] = jnp.broadcast_to(km, kmax_ref.shape)

    qf32 = q.astype(F32)
    bound = jnp.sqrt(_dot(qf32 * qf32, sel) * kmax_ref[0:1, :]) * _BOUND_MARGIN
    shift = jnp.concatenate([bound[:, 0:1], bound[:, 1:2]], axis=0)
    use_bound = 2.0 * jnp.max(shift) <= _MAX_EXP2_SPAN

    @pl.when(use_bound)
    def _():
        lane2 = lax.broadcasted_iota(jnp.int32, (2 * tq, 128), 1)
        qf = jnp.where(lane2 == DA_VDIM, (-shift).astype(BF16), qs)

        def absorb_shifted(kblk, vtblk):
            acc_ref[...] += _dot(vtblk, jnp.exp2(_dot_nt(kblk, qf)).astype(BF16))

        acc_ref[...] = jnp.zeros(acc_ref.shape, F32)
        absorb_shifted(kc_ref[...], vct_ref[0])
        if with_latent:
            def body(c, carry):
                c0 = pl.multiple_of(c * fast_chunk, fast_chunk)
                absorb_shifted(k_ref[pl.ds(c0, fast_chunk), :], vt_ref[0, :, pl.ds(c0, fast_chunk)])
                return carry

            lax.fori_loop(0, k_ref.shape[0] // fast_chunk, body, 0, unroll=fast_unroll)

    @pl.when(jnp.logical_not(use_bound))
    def _():
        m, acc = absorb(scores(kc_ref[...]), vct_ref[0], jnp.full((1, 2 * tq), NEG_BIG, F32),
                        jnp.zeros((128, 2 * tq), F32))
        if with_latent:
            pk = min(piece, kchunk)
            npieces = kchunk // pk

            def col_max(x):
                if pk > 128:
                    x = jnp.max(x.reshape(pk // 128, 128, 2 * tq), axis=0)
                return jnp.max(x.reshape(x.shape[0] // 8, 8, 2 * tq), axis=0)

            def fused(cur, c, m_, acc_, nxt):
                part = jnp.full((8, 2 * tq), NEG_BIG, F32)
                for t in range(npieces):
                    if nxt is not None:
                        r0 = pl.multiple_of((c + 1) * kchunk + t * pk, pk)
                        s_new = scores(k_ref[pl.ds(r0, pk), :])
                        s_ref[nxt, t * pk:(t + 1) * pk, :] = s_new
                        part = jnp.maximum(part, col_max(s_new))
                    c0 = pl.multiple_of(c * kchunk + t * pk, pk)
                    p = jnp.exp2((s_ref[cur, t * pk:(t + 1) * pk, :] - m_).astype(BF16))
                    acc_ = acc_ + _dot(vt_ref[0, :, pl.ds(c0, pk)], p)
                if nxt is None:
                    return m_, acc_
                m_new = jnp.maximum(m_, jnp.max(part, axis=0, keepdims=True))
                return m_new, acc_ * jnp.exp2(m_ - m_new)

            s0 = scores(latent_k(0))
            s_ref[0] = s0
            m0 = jnp.maximum(m, jnp.max(jnp.max(s0.reshape(kchunk // 128, 128, 2 * tq), axis=0), axis=0,
                                        keepdims=True))
            acc = acc * jnp.exp2(m - m0)
            m = m0

            def pair(c, carry, last):
                m_, acc_ = fused(0, c, *carry, 1)
                return fused(1, c + 1, m_, acc_, None if last else 0)

            if nchunks > 2:
                m, acc = lax.fori_loop(0, nchunks // 2 - 1, lambda j, cr: pair(2 * j, cr, False), (m, acc))
            m, acc = pair(nchunks - 2, (m, acc), True)
        acc_ref[...] = acc

    acc = acc_ref[...]
    a1 = acc[:, :tq]
    a2 = acc[:, tq:]
    o = a1 / a1[DA_VDIM:DA_VDIM + 1, :] - lam_ref[0] * (a2 / a2[DA_VDIM:DA_VDIM + 1, :])
    row = lax.broadcasted_iota(jnp.int32, (128, tq), 0)
    o = jnp.where(row < DA_VDIM, o, 0.0)
    ms = jnp.sum(o * o, axis=0, keepdims=True) * (1.0 / DA_VDIM)
    o_ref[...] = (o * lax.rsqrt(ms + NORM_EPS) * gain_ref[...] * out_scale).T.astype(BF16)


def _diff_attention(lam, q, kc, vc, k, v, gain, batch, seq_q, ctx_len, seq_k, out_scale, q_is_ctx):
    tq = min(512, seq_q)
    nq = seq_q // tq
    with_latent = not q_is_ctx
    if with_latent:
        kchunk = min(2048, seq_k // 2)
        assert seq_k % (2 * kchunk) == 0
    else:
        k, v = kc, vc
        seq_k, kchunk = ctx_len, ctx_len
    fast_chunk = kchunk
    fast_unroll = 2 if (seq_k // fast_chunk) % 2 == 0 else 1
    kern = functools.partial(_attn_kernel, with_latent=with_latent, kchunk=kchunk, piece=512, fast_chunk=fast_chunk,
                             fast_unroll=fast_unroll, out_scale=out_scale)
    return pl.pallas_call(
        kern,
        grid=(batch, DA_HEADS, nq),
        in_specs=[
            pl.BlockSpec(memory_space=pltpu.SMEM),
            pl.BlockSpec((tq, 128), lambda b, h, i: (b * nq + i, h)),
            pl.BlockSpec((ctx_len, 128), lambda b, h, i: (b, h)),
            pl.BlockSpec((1, 128, ctx_len), lambda b, h, i: (b, h, 0)),
            pl.BlockSpec((seq_k, 128), lambda b, h, i: (b, h)),
            pl.BlockSpec((1, 128, seq_k), lambda b, h, i: (b, h, 0)),
            pl.BlockSpec((128, 1), lambda b, h, i: (h, 0)),
        ],
        out_specs=pl.BlockSpec((tq, 128), lambda b, h, i: (b * nq + i, h)),
        out_shape=jax.ShapeDtypeStruct((batch * seq_q, DA_X), BF16),
        scratch_shapes=[pltpu.VMEM((2, kchunk, 2 * tq), F32), pltpu.VMEM((128, 2 * tq), F32),
                        pltpu.VMEM((8, 128), F32)],
        compiler_params=_params("arbitrary", "arbitrary", "arbitrary"),
        name="diff_attention",
    )(lam, q, kc, vc, k, v, gain)


def _outproj_kernel(x_ref, hof_ref, hob_ref, hg_ref, rof_ref, rob_ref, rg_ref, da_ref, seg_ref, w_ref,
                    hn_ref, rn_ref, n1_ref, n2_ref, g1_ref, sc2_ref, sh2_ref, rw_ref, rb_ref,
                    x1_ref, tok_ref, idx_ref, gate_ref):
    seg = seg_ref[...]

    def gated_head_norm(o, gain, gate):
        ms = _dot((o * o).astype(BF16), seg) * (1.0 / HDIM)
        return (o * lax.rsqrt(ms + NORM_EPS) * gain * (gate * _sigmoid(gate))).astype(BF16)

    a = gated_head_norm(hof_ref[...].astype(F32) + hob_ref[...].astype(F32), hn_ref[...], hg_ref[...].astype(F32))
    b = gated_head_norm(rof_ref[...].astype(F32) + rob_ref[...].astype(F32), rn_ref[...], rg_ref[...].astype(F32))
    y = (_dot(a, w_ref[0:REC_W, :]) + _dot(b, w_ref[REC_W:2 * REC_W, :])
         + _dot(da_ref[...], w_ref[2 * REC_W:2 * REC_W + DA_X, :]))
    x1 = x_ref[...] + g1_ref[0] * _rms(y, n1_ref[...])
    x1_ref[...] = x1
    tok = _rms(x1, n2_ref[...]) * (1.0 + sc2_ref[0]) + sh2_ref[0]
    tok_ref[...] = tok.astype(BF16)
    logits = _dot_nt(rw_ref[...], tok) + rb_ref[...]
    eid = lax.broadcasted_iota(jnp.int32, logits.shape, 0).astype(F32)
    vals, ids = [], []
    for _ in range(TOP_K):
        best = jnp.max(logits, axis=0, keepdims=True)
        first = jnp.min(jnp.where(logits == best, eid, float(N_EXPERTS)), axis=0, keepdims=True)
        vals.append(best)
        ids.append(first)
        logits = jnp.where(eid == first, -jnp.inf, logits)
    ex = [jnp.exp(v - vals[0]) for v in vals]
    total = ex[0] + ex[1] + ex[2] + ex[3]
    idx_ref[...] = jnp.concatenate(ids, axis=0).astype(jnp.int32)
    gate_ref[...] = jnp.concatenate(ex, axis=0) / total


def _outproj(xf, seq, streams, seg, w_out, hn, rn, n1, n2, g1, sc2, sh2, rw, rb):
    rows, d = xf.shape
    tm = min(256, seq)
    nb = seq // tm
    row_spec = lambda w: pl.BlockSpec((tm, w), lambda i: (i, 0))
    full = lambda a: pl.BlockSpec(a.shape, lambda i: (0,) * a.ndim)
    mod_spec = pl.BlockSpec((1, 1, d), lambda i: (i // nb, 0, 0))
    hof, hob, hg, rof, rob, rg, da = streams
    return pl.pallas_call(
        _outproj_kernel,
        grid=(rows // tm,),
        in_specs=[row_spec(d)] + [row_spec(REC_W)] * 6 + [row_spec(DA_X), full(seg), full(w_out),
                  full(hn), full(rn), full(n1), full(n2), mod_spec, mod_spec, mod_spec, full(rw), full(rb)],
        out_specs=[row_spec(d), row_spec(d), pl.BlockSpec((TOP_K, tm), lambda i: (0, i)),
                   pl.BlockSpec((TOP_K, tm), lambda i: (0, i))],
        out_shape=[jax.ShapeDtypeStruct((rows, d), F32), jax.ShapeDtypeStruct((rows, d), BF16),
                   jax.ShapeDtypeStruct((TOP_K, rows), jnp.int32), jax.ShapeDtypeStruct((TOP_K, rows), F32)],
        compiler_params=_params("arbitrary"),
        name="outproj",
    )(xf, hof, hob, hg, rof, rob, rg, da, seg, w_out, hn, rn, n1, n2, g1, sc2, sh2, rw, rb)


def _ffn_kernel(be_ref, na_ref, x_ref, w1_ref, w2_ref, b1g_ref, b1l_ref, b2_ref, pe_ref, po_ref, yprev_ref, y_ref,
                w1g_s, w1l_s, w2_s, *, blk0):
    del yprev_ref
    i = pl.program_id(0)
    blk = i + blk0
    active = blk < na_ref[0]
    fresh = jnp.logical_or(i == 0, be_ref[blk] != be_ref[jnp.maximum(blk - 1, 0)])

    @pl.when(fresh)
    def _():
        n = w1_ref.shape[3]
        for c in range(n // 256):
            wb = w1_ref[0, 0, :, 256 * c:256 * (c + 1)].astype(BF16)
            w1g_s[:, 128 * c:128 * (c + 1)] = _dot(wb, pe_ref[...]).astype(BF16)
            w1l_s[:, 128 * c:128 * (c + 1)] = _dot(wb, po_ref[...]).astype(BF16)
        w2_s[...] = w2_ref[0, 0].astype(BF16)

    @pl.when(jnp.logical_not(active))
    def _():
        y_ref[...] = jnp.zeros(y_ref.shape, y_ref.dtype)

    @pl.when(active)
    def _():
        x = x_ref[...]
        glu = jnp.minimum(_dot(x, w1g_s[...]) + b1g_ref[0], SWIGLU_LIMIT)
        lin = jnp.clip(_dot(x, w1l_s[...]) + b1l_ref[0], -SWIGLU_LIMIT, SWIGLU_LIMIT)
        act = glu * _sigmoid(SWIGLU_ALPHA * glu) * (lin + 1.0)
        y_ref[...] = (_dot(act.astype(BF16), w2_s[...]) + b2_ref[0]).astype(y_ref.dtype)


def _expert_ffn(block_expert, n_active, tok, tok_sorted, w1, w2, b1g, b1l, b2, layer, bm, nparts=4):
    p = tok_sorted.shape[0]
    d = tok.shape[1]
    f = w2.shape[2]
    nblocks = p // bm
    src = jnp.arange(256)[:, None]
    dst = jnp.arange(128)[None, :]
    pe = (src == 2 * dst).astype(BF16)
    po = (src == 2 * dst + 1).astype(BF16)
    sel = pl.BlockSpec((256, 128), lambda i, be, na: (0, 0))
    bounds = [0] + [nblocks * (4 * j + 1) // (4 * nparts) for j in range(nparts)] + [nblocks]
    nparts += 1
    yg = jnp.zeros((8, 128), BF16)
    for j in range(nparts):
        b0, nb = bounds[j], bounds[j + 1] - bounds[j]
        xg = tok.at[tok_sorted[b0 * bm:(b0 + nb) * bm]].get(mode="promise_in_bounds")
        wspec = lambda s, b0=b0: pl.BlockSpec((1, 1) + s, lambda i, be, na: (layer, be[i + b0], 0, 0))
        bspec = lambda s, b0=b0: pl.BlockSpec((1,) + s, lambda i, be, na: (be[i + b0], 0, 0))
        yg = pl.pallas_call(
            functools.partial(_ffn_kernel, blk0=b0),
            grid_spec=pltpu.PrefetchScalarGridSpec(
                num_scalar_prefetch=2,
                grid=(nb,),
                in_specs=[pl.BlockSpec((bm, d), lambda i, be, na: (i, 0)),
                          wspec((d, 2 * f)), wspec((f, d)), bspec((1, f)), bspec((1, f)), bspec((1, d)), sel, sel,
                          pl.BlockSpec(memory_space=pl.ANY)],
                out_specs=pl.BlockSpec((bm, d), lambda i, be, na, b0=b0: (i + b0, 0)),
                scratch_shapes=[pltpu.VMEM((d, f), BF16), pltpu.VMEM((d, f), BF16), pltpu.VMEM((f, d), BF16)],
            ),
            out_shape=jax.ShapeDtypeStruct((p, d), BF16),
            input_output_aliases={10: 0} if j else {},
            compiler_params=_params("arbitrary"),
            name="expert_ffn",
        )(block_expert, n_active, xg, w1, w2, b1g, b1l, b2, pe, po, yg)
    return yg


def _resid_kernel(x_ref, y0_ref, y1_ref, y2_ref, y3_ref, gate_ref, n_ref, g_ref, o_ref):
    gates = gate_ref[...]
    f = y0_ref[...].astype(F32) * gates[:, 0:1]
    for k, y_ref in enumerate((y1_ref, y2_ref, y3_ref), start=1):
        f = f + y_ref[...].astype(F32) * gates[:, k:k + 1]
    o_ref[...] = x_ref[...] + g_ref[0] * _rms(f, n_ref[...])


def _ffn_residual(xf, ys, gates, seq, n3, g2):
    rows, d = xf.shape
    tm = min(512, seq)
    nb = seq // tm
    row_spec = pl.BlockSpec((tm, d), lambda i: (i, 0))
    return pl.pallas_call(
        _resid_kernel,
        grid=(rows // tm,),
        in_specs=[row_spec] * 5 + [pl.BlockSpec((tm, TOP_K), lambda i: (i, 0)), pl.BlockSpec((1, d), lambda i: (0, 0)),
                                   pl.BlockSpec((1, 1, d), lambda i: (i // nb, 0, 0))],
        out_specs=row_spec,
        out_shape=jax.ShapeDtypeStruct((rows, d), F32),
        compiler_params=_params("arbitrary"),
        name="ffn_residual",
    )(xf, *ys, gates, n3, g2)


def _rope_tables(pos, dim):
    inv = 1.0 / (ROPE_BASE ** (jnp.arange(0, dim, 2, dtype=F32) / dim))
    ang = pos.astype(F32)[:, None] * inv[None, :]
    return jnp.cos(ang), jnp.sin(ang)


def _rot_cols(w, head_dim, halves):
    d, n = w.shape
    g = head_dim // halves
    w4 = w.reshape(d, n // g, 2, g // 2)
    return jnp.concatenate([-w4[:, :, 1], w4[:, :, 0]], axis=-1).reshape(d, n)


def _prep_w_in(w):
    rq, rk = w[:, _C_RQ:_C_RQ + REC_W], w[:, _C_RK:_C_RK + REC_W]
    dq, dk = w[:, _C_DQ:_C_DQ + DA_W], w[:, _C_DK:_C_DK + DA_W]
    return jnp.concatenate([w, _rot_cols(rq, HDIM, 1), _rot_cols(rk, HDIM, 1),
                            _rot_cols(dq, DA_HEAD_DIM, 2), _rot_cols(dk, DA_HEAD_DIM, 2)], axis=1).astype(BF16)


def _routing(top_idx_t, bm):
    n = top_idx_t.shape[1]
    a = n * TOP_K
    flat_e = top_idx_t.T.reshape(a)
    sorted_e, order = lax.sort_key_val(flat_e, jnp.arange(a, dtype=jnp.int32))
    experts = jnp.arange(N_EXPERTS + 1, dtype=jnp.int32)
    bounds = jnp.sum((flat_e[None, :] < experts[:, None]).astype(jnp.int32), axis=1)
    start, counts = bounds[:-1], bounds[1:] - bounds[:-1]
    padded = (counts + bm - 1) // bm * bm
    pend = jnp.cumsum(padded)
    shift = pend - padded - start
    dest = jnp.arange(a, dtype=jnp.int32) + shift[sorted_e]
    _, pos = lax.sort_key_val(order, dest)
    nblocks = -(-(a + N_EXPERTS * (bm - 1)) // bm)
    block_row = jnp.arange(nblocks, dtype=jnp.int32) * bm
    block_expert = jnp.minimum(jnp.sum((pend[None, :] <= block_row[:, None]).astype(jnp.int32), axis=1),
                               N_EXPERTS - 1)
    row_e = jnp.repeat(block_expert, bm)
    rank = jnp.arange(nblocks * bm, dtype=jnp.int32) - shift[row_e]
    valid = rank < bounds[row_e + 1]
    tok_sorted = jnp.where(valid, order[jnp.clip(rank, 0, a - 1)] // TOP_K, 0)
    n_active = (pend[-1:] // bm).astype(jnp.int32)
    return tok_sorted, pos, block_expert, n_active


def kernel(x, c, ctx, c_ctx, mod_w, mod_b, norm_g, w_in, hgrn_lb, hgrn_norm, ret_decay, ret_norm, da_lambda,
           da_subln, w_out, router_w, router_b, w1, b1, w2, b2):
    B, S, D = x.shape
    C = ctx.shape[1]
    depth = mod_w.shape[0]
    bm = 256

    pos = jnp.arange(S)
    cr, sr = _rope_tables(pos // GRID_W, DA_HEAD_DIM // 2)
    cc, sc_ = _rope_tables(pos % GRID_W, DA_HEAD_DIM // 2)
    cs, ss = _rope_tables(pos, HDIM)
    cosr = jnp.tile(jnp.concatenate([cs, cs], -1), (1, 128 // HDIM))
    sinr = jnp.tile(jnp.concatenate([ss, ss], -1), (1, 128 // HDIM))
    cosd = jnp.tile(jnp.concatenate([cr, cr, cc, cc], -1), (1, 128 // DA_HEAD_DIM))
    sind = jnp.tile(jnp.concatenate([sr, sr, sc_, sc_], -1), (1, 128 // DA_HEAD_DIM))
    ones_r, zeros_r = jnp.ones((C, 128), F32), jnp.zeros((C, 128), F32)
    ones_d, zeros_d = ones_r, zeros_r

    lb_cum = jnp.cumsum(jax.nn.softmax(hgrn_lb.astype(F32), axis=0), axis=0)
    lower = lb_cum - lb_cum[0:1]

    cvec = jnp.zeros((8, D), F32).at[:B].set(c).at[B].set(c_ctx)
    mods = _modulation(cvec, mod_w, mod_b)

    head_id = jnp.arange(REC_W) // HDIM
    seg = (head_id[:, None] == head_id[None, :]).astype(BF16)

    xf = x.reshape(B * S, D)
    xc = ctx.reshape(B * C, D)
    zero_state = jnp.zeros((B, 2, HEADS, HDIM, HDIM), F32)

    for layer in range(depth):
        need_ctx = layer < depth - 1
        lam_init = 0.8 - 0.6 * math.exp(-0.3 * layer)
        m6 = mods[layer].reshape(8, 6, D)
        lat = lambda k: m6[:B, k][:, None, :]
        cxm = lambda k: jnp.broadcast_to(m6[B, k][None, None, :], (B, 1, D))
        ng = norm_g[layer]
        lb = lower[layer]
        lbc = jnp.zeros((8, REC_W), F32).at[0].set(jnp.log(lb)).at[1].set(jnp.log1p(-lb)).at[2].set(1.0 - lb)
        w_all = _prep_w_in(w_in[layer])
        log_gamma = jnp.log1p(-jnp.exp2(-ret_decay[layer].astype(F32)))
        lgrow = jnp.zeros((8, REC_W), F32).at[:2].set(jnp.repeat(log_gamma, HDIM, axis=1))
        lamv = da_lambda[layer].astype(F32)
        lam = (jnp.exp(jnp.sum(lamv[0] * lamv[1])) - jnp.exp(jnp.sum(lamv[2] * lamv[3])) + lam_init).reshape(1)
        hn = jnp.tile(hgrn_norm[layer], HEADS)[None, :]
        rn = jnp.tile(ret_norm[layer], HEADS)[None, :]
        dn = jnp.tile(jnp.concatenate([da_subln[layer], jnp.zeros((128 - DA_VDIM,), F32)]), DA_HEADS)[:, None]
        wo = w_out[layer]
        wo_da = jnp.pad(wo[2 * REC_W:].reshape(DA_HEADS, DA_VDIM, D), ((0, 0), (0, 128 - DA_VDIM), (0, 0)))
        w_out_b = jnp.concatenate([wo[:2 * REC_W], wo_da.reshape(DA_X, D)], axis=0).astype(BF16)
        rw = router_w[layer].T
        rb = router_b[layer][:, None]

        (hq, hv, hg, kf, lff, kb, lfb, rq, rk, rv, rg, dq, dk, dvx) = _inproj(
            xf, S, ng[0:1], lat(1), lat(0), w_all, lbc, cosr, sinr, cosd, sind)
        (hq_c, hv_c, hg_c, kf_c, lff_c, kb_c, lfb_c, rq_c, rk_c, rv_c, rg_c, dq_c, dk_c, dvx_c) = _inproj(
            xc, C, ng[0:1], cxm(1), cxm(0), w_all, lbc, ones_r, zeros_r, ones_d, zeros_d)

        hof_c, hob_c, hs = _hgrn_scan(hq_c, hv_c, kf_c, lff_c, kb_c, lfb_c,
                                      jnp.zeros((B, 2, REC_W // 128, 128, 128), F32), seg, B, C)
        hof, hob, _ = _hgrn_scan(hq, hv, kf, lff, kb, lfb, hs, seg, B, S)
        rof_c, rob_c, rs = _ret_scan(rq_c, rk_c, rv_c, zero_state, log_gamma, lgrow, B, C)
        rof, rob, _ = _ret_scan(rq, rk, rv, rs, log_gamma, lgrow, B, S)
        out_scale = 1.0 - lam_init
        da = _diff_attention(lam, dq, dk_c, dvx_c, dk, dvx, dn, B, S, C, S, out_scale, False)

        x1, tok, top_idx_t, gates_t = _outproj(xf, S, (hof, hob, hg, rof, rob, rg, da), seg, w_out_b, hn, rn,
                                               ng[1:2], ng[2:3], lat(2), lat(4), lat(3), rw, rb)
        if need_ctx:
            da_c = _diff_attention(lam, dq_c, dk_c, dvx_c, None, None, dn, B, C, C, C, out_scale, True)
            xc1, tok_c, top_idx_c, gates_c = _outproj(xc, C, (hof_c, hob_c, hg_c, rof_c, rob_c, rg_c, da_c), seg,
                                                      w_out_b, hn, rn, ng[1:2], ng[2:3], cxm(2), cxm(4), cxm(3),
                                                      rw, rb)
            tok = jnp.concatenate([tok, tok_c], axis=0)
            top_idx_t = jnp.concatenate([top_idx_t, top_idx_c], axis=1)
            gates_t = jnp.concatenate([gates_t, gates_c], axis=1)

        n_tok = tok.shape[0]
        gates = gates_t.T
        tok_sorted, posn, block_expert, n_active = _routing(top_idx_t, bm)
        b1g = b1[layer][:, None, 0::2]
        b1l = b1[layer][:, None, 1::2]
        b2l = b2[layer][:, None, :]
        yg = _expert_ffn(block_expert, n_active, tok, tok_sorted, w1, w2, b1g, b1l, b2l, layer, bm)
        pos2 = posn.reshape(n_tok, TOP_K)

        def expert_rows(lo, hi):
            return [yg.at[pos2[lo:hi, k]].get(mode="promise_in_bounds") for k in range(TOP_K)]

        xf = _ffn_residual(x1, expert_rows(0, B * S), gates[:B * S], S, ng[3:4], lat(5))
        if need_ctx:
            xc = _ffn_residual(xc1, expert_rows(B * S, n_tok), gates[B * S:], C, ng[3:4], cxm(5))
    return xf.reshape(B, S, D)
```

```python
import functools
import math

import jax
import jax.numpy as jnp
from jax import lax
from jax.experimental import pallas as pl
from jax.experimental.pallas import tpu as pltpu

F32 = jnp.float32
BF16 = jnp.bfloat16

GRID_W = 64
HEADS = 6
HDIM = 64
REC_W = HEADS * HDIM
DA_HEADS = 4
DA_HEAD_DIM = 32
DA_MAPS = 2 * DA_HEADS
DA_W = DA_MAPS * DA_HEAD_DIM
DA_VDIM = 2 * DA_HEAD_DIM
DA_X = DA_HEADS * 128
_Q_SCALE = DA_HEAD_DIM ** -0.5 * math.log2(math.e)
_BOUND_MARGIN = 1.01
_MAX_EXP2_SPAN = 100.0
ROPE_BASE = 10000.0
N_EXPERTS = 32
TOP_K = 4
SWIGLU_ALPHA = 1.702
SWIGLU_LIMIT = 7.0
NORM_EPS = 1e-6

SUB = 16
NEG_BIG = -1e30
VMEM_LIMIT = 56 * 1024 * 1024

_C_HQ, _C_HFF, _C_HFB, _C_HI, _C_HG = 0, 384, 768, 1152, 1536
_C_RQ, _C_RK, _C_RV, _C_RG = 1920, 2304, 2688, 3072
_C_DQ, _C_DK, _C_DV = 3456, 3712, 3968
_C_RQR, _C_RKR, _C_DQR, _C_DKR = 4224, 4608, 4992, 5248
_W_ALL = 5504


def _dot(a, b):
    return jnp.dot(a, b, preferred_element_type=F32)


def _dot_nt(a, b):
    return lax.dot_general(a, b, (((1,), (1,)), ((), ())), preferred_element_type=F32)


def _dot_tn(a, b):
    return lax.dot_general(a, b, (((0,), (0,)), ((), ())), preferred_element_type=F32)


def _sigmoid(x):
    return 1.0 / (1.0 + jnp.exp(-x))


def _rms(x, g):
    ms = jnp.mean(x * x, axis=-1, keepdims=True)
    return x * lax.rsqrt(ms + NORM_EPS) * g


def _params(*sem):
    return pltpu.CompilerParams(dimension_semantics=sem, vmem_limit_bytes=VMEM_LIMIT)


def _mod_kernel(c_ref, w_ref, b_ref, o_ref):
    c = c_ref[...]
    o_ref[0] = _dot(c * _sigmoid(c), w_ref[0]) + b_ref[0]


def _modulation(cvec, mod_w, mod_b):
    depth, d, n = mod_w.shape
    tn = 1536
    return pl.pallas_call(
        _mod_kernel,
        grid=(depth, n // tn),
        in_specs=[
            pl.BlockSpec((8, d), lambda l, j: (0, 0)),
            pl.BlockSpec((1, d, tn), lambda l, j: (l, 0, j)),
            pl.BlockSpec((1, 1, tn), lambda l, j: (l, 0, j)),
        ],
        out_specs=pl.BlockSpec((1, 8, tn), lambda l, j: (l, 0, j)),
        out_shape=jax.ShapeDtypeStruct((depth, 8, n), F32),
        compiler_params=_params("arbitrary", "arbitrary"),
        name="modulation",
    )(cvec, mod_w, mod_b.reshape(depth, 1, n))


def _inproj_kernel(x_ref, g_ref, sc_ref, sh_ref, w_ref, lbc_ref, cosr_ref, sinr_ref, cosd_ref, sind_ref,
                   hq_ref, hv_ref, hg_ref, kf_ref, lff_ref, kb_ref, lfb_ref,
                   rq_ref, rk_ref, rv_ref, rg_ref, dq_ref, dk_ref, dvt_ref):
    x = x_ref[...]
    h = _rms(x, g_ref[...]) * (1.0 + sc_ref[0]) + sh_ref[0]
    hb = h.astype(BF16)

    wide = {}

    def proj(c0, n):
        lo = max(b for b in (0, _C_RQ, _C_DQ) if b <= c0)
        hi = {0: _C_RQ, _C_RQ: _C_DQ, _C_DQ: _W_ALL}[lo]
        if lo not in wide:
            wide[lo] = _dot(hb, w_ref[:, lo:hi])
        return wide[lo][:, c0 - lo:c0 - lo + n]

    hq_ref[...] = proj(_C_HQ, REC_W).astype(BF16)
    hv_ref[...] = proj(_C_HI, REC_W).astype(BF16)
    hg_ref[...] = proj(_C_HG, REC_W).astype(BF16)

    log_lb = lbc_ref[0:1, :]
    log_1m = lbc_ref[1:2, :]
    one_m = lbc_ref[2:3, :]

    def gates(z):
        log_sig = jnp.minimum(z, 0.0) - jnp.log1p(jnp.exp(-jnp.abs(z)))
        t = log_1m + log_sig
        m = jnp.maximum(log_lb, t)
        logf = m + jnp.log1p(jnp.exp(-jnp.abs(log_lb - t)))
        return one_m / (1.0 + jnp.exp(z)), logf

    k, lf = gates(proj(_C_HFF, REC_W))
    kf_ref[...] = k.astype(BF16)
    lff_ref[...] = lf
    k, lf = gates(proj(_C_HFB, REC_W))
    kb_ref[...] = k.astype(BF16)
    lfb_ref[...] = lf

    cosr = jnp.concatenate([cosr_ref[...]] * (REC_W // 128), axis=1)
    sinr = jnp.concatenate([sinr_ref[...]] * (REC_W // 128), axis=1)
    rq_ref[...] = (proj(_C_RQ, REC_W) * cosr + proj(_C_RQR, REC_W) * sinr).astype(BF16)
    rk_ref[...] = ((proj(_C_RK, REC_W) * cosr + proj(_C_RKR, REC_W) * sinr) * (HDIM ** -0.5)).astype(BF16)
    rv_ref[...] = proj(_C_RV, REC_W).astype(BF16)
    rg_ref[...] = proj(_C_RG, REC_W).astype(BF16)

    cosd = jnp.concatenate([cosd_ref[...]] * (DA_W // 128), axis=1)
    sind = jnp.concatenate([sind_ref[...]] * (DA_W // 128), axis=1)
    dq = ((proj(_C_DQ, DA_W) * cosd + proj(_C_DQR, DA_W) * sind) * _Q_SCALE).astype(BF16)
    dk = (proj(_C_DK, DA_W) * cosd + proj(_C_DKR, DA_W) * sind).astype(BF16)
    dv_t = proj(_C_DV, DA_W).T.astype(BF16)
    tm = dq.shape[0]
    zero_col = jnp.zeros((tm, 64), BF16)
    ones_col = jnp.where(lax.broadcasted_iota(jnp.int32, (tm, 64), 1) == 0, 1.0, 0.0).astype(BF16)
    sub = lax.broadcasted_iota(jnp.int32, (64, tm), 0)
    ones_row = jnp.where(sub == 0, 1.0, 0.0).astype(BF16)
    for h_ in range(DA_HEADS):
        lo, mid, hi_ = 128 * h_, 128 * h_ + 64, 128 * h_ + 128
        dq_ref[:, lo:mid] = dq[:, 64 * h_:64 * h_ + 64]
        dq_ref[:, mid:hi_] = zero_col
        dk_ref[:, lo:mid] = dk[:, 64 * h_:64 * h_ + 64]
        dk_ref[:, mid:hi_] = ones_col
        dvt_ref[0, lo:mid, :] = dv_t[64 * h_:64 * h_ + 64, :]
        dvt_ref[0, mid:hi_, :] = ones_row


def _inproj(xf, seq, gnorm, sc, sh, w_all, lbc, cosr, sinr, cosd, sind):
    rows, d = xf.shape
    tm = min(256, seq)
    nb = seq // tm
    row_spec = lambda w: pl.BlockSpec((tm, w), lambda i: (i, 0))
    tab_spec = lambda w: pl.BlockSpec((tm, w), lambda i: (i % nb, 0))
    mod_spec = pl.BlockSpec((1, 1, d), lambda i: (i // nb, 0, 0))
    widths = [REC_W] * 11 + [DA_X, DA_X]
    dtypes = [BF16, BF16, BF16, BF16, F32, BF16, F32, BF16, BF16, BF16, BF16, BF16, BF16]
    out_specs = [row_spec(w) for w in widths] + [pl.BlockSpec((1, DA_X, tm), lambda i: (i // nb, 0, i % nb))]
    out_shape = ([jax.ShapeDtypeStruct((rows, w), dt) for w, dt in zip(widths, dtypes)]
                 + [jax.ShapeDtypeStruct((rows // seq, DA_X, seq), BF16)])
    return pl.pallas_call(
        _inproj_kernel,
        grid=(rows // tm,),
        in_specs=[
            row_spec(d),
            pl.BlockSpec((1, d), lambda i: (0, 0)),
            mod_spec, mod_spec,
            pl.BlockSpec((d, _W_ALL), lambda i: (0, 0)),
            pl.BlockSpec((8, REC_W), lambda i: (0, 0)),
            tab_spec(128), tab_spec(128), tab_spec(128), tab_spec(128),
        ],
        out_specs=out_specs,
        out_shape=out_shape,
        compiler_params=_params("arbitrary"),
        name="inproj",
    )(xf, gnorm, sc, sh, w_all, lbc, cosr, sinr, cosd, sind)


def _hgrn_dir(q, k, v, lf, seg, st_ref, inter_ref, intra_ref, reverse):
    tt = q.shape[0]
    nblk = tt // SUB
    row = lax.broadcasted_iota(jnp.int32, (tt, tt), 0)
    col = lax.broadcasted_iota(jnp.int32, (tt, tt), 1)
    same = (row // SUB) == (col // SUB)
    tri = (col >= row) if reverse else (col <= row)
    l_all = jnp.where(same, 1.0, 0.0)
    l_cum = jnp.where(tri, l_all, 0.0).astype(BF16)
    l_all = l_all.astype(BF16)
    hi = lf.astype(BF16)
    r1 = lf - hi.astype(F32)
    mid = r1.astype(BF16)
    lo = (r1 - mid.astype(F32)).astype(BF16)
    b = _dot(l_cum, hi) + _dot(l_cum, mid) + _dot(l_cum, lo)
    btot = _dot(l_all, hi) + _dot(l_all, mid) + _dot(l_all, lo)

    qe = (q * jnp.exp(b)).astype(BF16)
    kd = (k * jnp.exp(btot - b)).astype(BF16)
    dec = jnp.exp(btot)
    v_t = v.T
    tok_blk = lax.broadcasted_iota(jnp.int32, (128, tt), 1) // SUB
    hr = lax.broadcasted_iota(jnp.int32, (128, 128), 0) // HDIM
    hc = lax.broadcasted_iota(jnp.int32, (128, 128), 1) // HDIM
    same_head = hr == hc
    order = range(nblk - 1, -1, -1) if reverse else range(nblk)
    half = lax.broadcasted_iota(jnp.int32, (nblk, 8, 128), 1)
    seg2 = seg[0:128, 0:128]

    def carried(c, g):
        v_tg = v_t[c, :]
        quarter = max(nblk // 4, 1)
        upd = []
        for j0 in range(0, nblk, quarter):
            v_stack = jnp.concatenate([jnp.where(tok_blk == j, v_tg, 0.0).astype(BF16)
                                       for j in range(j0, j0 + quarter)], axis=0)
            part = _dot(v_stack, kd[:, c])
            upd += [part[128 * j:128 * (j + 1)] for j in range(quarter)]
        st = st_ref[g]
        for j in order:
            r0 = j * SUB
            inter_ref[r0:r0 + SUB, c] = _dot_nt(qe[r0:r0 + SUB, c], st.astype(BF16))
            st = st * dec[r0:r0 + 1, c] + jnp.where(same_head, upd[j], 0.0)
        st_ref[g] = st

    def within(c):
        def halves(a):
            a4 = a[:, c].reshape(nblk, 2, 8, 128)
            return (a4[:, 1], a4[:, 0]) if reverse else (a4[:, 0], a4[:, 1])

        (q0, q1), (k0, k1), (v0, v1), (b0, b1) = halves(q), halves(k), halves(v), halves(b)

        def pairs(qa, ba, kb_, bb_, vb_, r, masked):
            sh = ((8 - r) % 8) if reverse else r
            if sh:
                kb_, bb_, vb_ = pltpu.roll(kb_, sh, 1), pltpu.roll(bb_, sh, 1), pltpu.roll(vb_, sh, 1)
            diff = ba - bb_
            if masked and r:
                diff = jnp.where((half <= 7 - r) if reverse else (half >= r), diff, NEG_BIG)
            term = (qa * kb_ * jnp.exp(diff)).reshape(nblk * 8, 128).astype(BF16)
            return _dot(term, seg2).reshape(nblk, 8, 128) * vb_

        o0 = pairs(q0, b0, k0, b0, v0, 0, True)
        o1 = pairs(q1, b1, k1, b1, v1, 0, True) + pairs(q1, b1, k0, b0, v0, 0, False)
        for r in range(1, 8):
            o0 = o0 + pairs(q0, b0, k0, b0, v0, r, True)
            o1 = o1 + pairs(q1, b1, k1, b1, v1, r, True) + pairs(q1, b1, k0, b0, v0, r, False)
        first, second = (o1, o0) if reverse else (o0, o1)
        intra_ref[:, 0, :, c] = first
        intra_ref[:, 1, :, c] = second

    for g in range(REC_W // 128):
        c = slice(128 * g, 128 * (g + 1))
        carried(c, g)
        within(c)
    return intra_ref[...].reshape(tt, REC_W) + inter_ref[...]


def _hgrn_kernel(qf_ref, vf_ref, kf_ref, lff_ref, qb_ref, vb_ref, kb_ref, lfb_ref, s0_ref, seg_ref,
                 of_ref, ob_ref, sout_ref, st_ref, inter_ref, intra_ref):
    i = pl.program_id(1)

    @pl.when(i == 0)
    def _():
        st_ref[...] = s0_ref[0]

    seg = seg_ref[...]
    of_ref[...] = _hgrn_dir(qf_ref[...].astype(F32), kf_ref[...].astype(F32), vf_ref[...].astype(F32),
                            lff_ref[...], seg, st_ref.at[0], inter_ref, intra_ref, False).astype(BF16)
    ob_ref[...] = _hgrn_dir(qb_ref[...].astype(F32), kb_ref[...].astype(F32), vb_ref[...].astype(F32),
                            lfb_ref[...], seg, st_ref.at[1], inter_ref, intra_ref, True).astype(BF16)

    @pl.when(i == pl.num_programs(1) - 1)
    def _():
        sout_ref[0] = st_ref[...]


def _hgrn_scan(hq, hv, kf, lff, kb, lfb, s0, seg, batch, seq):
    tt = min(256, seq)
    n = seq // tt
    fwd = pl.BlockSpec((tt, REC_W), lambda b, i: (b * n + i, 0))
    bwd = pl.BlockSpec((tt, REC_W), lambda b, i: (b * n + n - 1 - i, 0))
    st_spec = pl.BlockSpec((1, 2, REC_W // 128, 128, 128), lambda b, i: (b, 0, 0, 0, 0))
    rows = batch * seq
    return pl.pallas_call(
        _hgrn_kernel,
        grid=(batch, n),
        in_specs=[fwd, fwd, fwd, fwd, bwd, bwd, bwd, bwd, st_spec,
                  pl.BlockSpec((REC_W, REC_W), lambda b, i: (0, 0))],
        out_specs=[fwd, bwd, st_spec],
        out_shape=[jax.ShapeDtypeStruct((rows, REC_W), BF16), jax.ShapeDtypeStruct((rows, REC_W), BF16),
                   jax.ShapeDtypeStruct(s0.shape, F32)],
        scratch_shapes=[pltpu.VMEM((2, REC_W // 128, 128, 128), F32), pltpu.VMEM((tt, REC_W), F32),
                        pltpu.VMEM((tt // SUB, 2, 8, REC_W), F32)],
        compiler_params=_params("arbitrary", "arbitrary"),
        name="hgrn_scan",
    )(hq, hv, kf, lff, hq, hv, kb, lfb, s0, seg)


def _ret_kernel(lg_ref, qf_ref, kf_ref, vf_ref, qb_ref, kb_ref, vb_ref, s0_ref, lgrow_ref,
                of_ref, ob_ref, sout_ref, st_ref, dmask_ref, tab_ref, o_scr):
    i = pl.program_id(1)
    tt = qf_ref.shape[0]

    @pl.when(i == 0)
    def _():
        st_ref[...] = s0_ref[0]
        row = lax.broadcasted_iota(jnp.int32, (tt, tt), 0)
        col = lax.broadcasted_iota(jnp.int32, (tt, tt), 1)
        dist = (row - col).astype(F32)
        for h in range(HEADS):
            fw = jnp.where(dist >= 0, jnp.exp(dist * lg_ref[0, h]), 0.0)
            bw = jnp.where(dist <= 0, jnp.exp(-dist * lg_ref[1, h]), 0.0)
            dmask_ref[h] = fw + bw
        pos = lax.broadcasted_iota(jnp.int32, (tt, REC_W), 0).astype(F32)
        lgf = lgrow_ref[0:1, :]
        lgb = lgrow_ref[1:2, :]
        tab_ref[0] = jnp.exp((pos + 1.0) * lgf)
        tab_ref[1] = jnp.exp((tt - 1.0 - pos) * lgf)
        tab_ref[2] = jnp.exp((tt - pos) * lgb)
        tab_ref[3] = jnp.exp(pos * lgb)

    tile_f = jnp.exp(tt * lgrow_ref[0:1, :])
    tile_b = jnp.exp(tt * lgrow_ref[1:2, :])

    q = qf_ref[...]
    k = kf_ref[...]
    v = vf_ref[...]
    qe = (q.astype(F32) * tab_ref[0]).astype(BF16)
    ke = (k.astype(F32) * tab_ref[1]).astype(BF16)
    for h in range(HEADS):
        c = slice(h * HDIM, (h + 1) * HDIM)
        s = _dot_nt(q[:, c], k[:, c]) * dmask_ref[h]
        st = st_ref[0, h]
        o_scr[:, c] = _dot(s.astype(BF16), v[:, c]) + _dot(qe[:, c], st.astype(BF16))
        st_ref[0, h] = st * tile_f[:, c] + _dot_tn(ke[:, c], v[:, c])
    of_ref[...] = o_scr[...].astype(BF16)

    q = qb_ref[...]
    k = kb_ref[...]
    v = vb_ref[...]
    qe = (q.astype(F32) * tab_ref[2]).astype(BF16)
    ke = (k.astype(F32) * tab_ref[3]).astype(BF16)
    for h in range(HEADS):
        c = slice(h * HDIM, (h + 1) * HDIM)
        st = st_ref[1, h]
        o_scr[:, c] = _dot(qe[:, c], st.astype(BF16))
        st_ref[1, h] = st * tile_b[:, c] + _dot_tn(ke[:, c], v[:, c])
    ob_ref[...] = o_scr[...].astype(BF16)

    @pl.when(i == pl.num_programs(1) - 1)
    def _():
        sout_ref[0] = st_ref[...]


def _ret_scan(rq, rk, rv, s0, lg, lgrow, batch, seq):
    tt = min(256, seq)
    n = seq // tt
    fwd = pl.BlockSpec((tt, REC_W), lambda b, i: (b * n + i, 0))
    bwd = pl.BlockSpec((tt, REC_W), lambda b, i: (b * n + n - 1 - i, 0))
    st_spec = pl.BlockSpec((1, 2, HEADS, HDIM, HDIM), lambda b, i: (b, 0, 0, 0, 0))
    rows = batch * seq
    return pl.pallas_call(
        _ret_kernel,
        grid=(batch, n),
        in_specs=[pl.BlockSpec(memory_space=pltpu.SMEM), fwd, fwd, fwd, bwd, bwd, bwd, st_spec,
                  pl.BlockSpec((8, REC_W), lambda b, i: (0, 0))],
        out_specs=[fwd, bwd, st_spec],
        out_shape=[jax.ShapeDtypeStruct((rows, REC_W), BF16), jax.ShapeDtypeStruct((rows, REC_W), BF16),
                   jax.ShapeDtypeStruct(s0.shape, F32)],
        scratch_shapes=[pltpu.VMEM((2, HEADS, HDIM, HDIM), F32), pltpu.VMEM((HEADS, tt, tt), F32),
                        pltpu.VMEM((4, tt, REC_W), F32), pltpu.VMEM((tt, REC_W), F32)],
        compiler_params=_params("arbitrary", "arbitrary"),
        name="ret_scan",
    )(lg, rq, rk, rv, rq, rk, rv, s0, lgrow)


def _attn_kernel(lam_ref, q_ref, kc_ref, vct_ref, k_ref, vt_ref, gain_ref, o_ref, s_ref, acc_ref, kmax_ref, *,
                 with_latent, kchunk, piece, fast_chunk, fast_unroll, out_scale):
    tq = q_ref.shape[0]
    q = q_ref[...]
    lane = lax.broadcasted_iota(jnp.int32, (tq, 128), 1)
    zero = jnp.zeros_like(q)
    qs = jnp.concatenate([jnp.where(lane < DA_HEAD_DIM, q, zero),
                          jnp.where(lane >= DA_HEAD_DIM, q, zero)], axis=0)
    nchunks = (k_ref.shape[0] // kchunk) if with_latent else 0

    def scores(kblk):
        return _dot_nt(kblk, qs)

    def absorb(s, vtblk, m, acc):
        n = s.shape[0]
        part = jnp.max(s.reshape(n // 128, 128, 2 * tq), axis=0) if n > 128 else s
        m_new = jnp.maximum(m, jnp.max(part, axis=0, keepdims=True))
        p = jnp.exp2((s - m_new).astype(BF16))
        return m_new, jnp.exp2(m - m_new) * acc + _dot(vtblk, p)

    def latent_k(c):
        return k_ref[pl.ds(pl.multiple_of(c * kchunk, kchunk), kchunk), :]

    def latent_vt(c):
        return vt_ref[0, :, pl.ds(pl.multiple_of(c * kchunk, kchunk), kchunk)]

    sel_r = lax.broadcasted_iota(jnp.int32, (128, 128), 0) // DA_HEAD_DIM
    sel_c = lax.broadcasted_iota(jnp.int32, (128, 128), 1)
    sel = jnp.where(sel_r == sel_c, 1.0, 0.0).astype(BF16)

    def key_norm2(kblk):
        kf = kblk.astype(F32)
        return jnp.max(_dot((kf * kf).astype(BF16), sel), axis=0, keepdims=True)

    @pl.when(pl.program_id(2) == 0)
    def _():
        km = key_norm2(kc_ref[...])
        if with_latent:
            km = lax.fori_loop(0, nchunks, lambda c, a: jnp.maximum(a, key_norm2(latent_k(c))), km)
        kmax_ref[...] = jnp.broadcast_to(km, kmax_ref.shape)

    qf32 = q.astype(F32)
    qn2 = _dot((qf32 * qf32).astype(BF16), sel)
    bound = jnp.sqrt(qn2 * kmax_ref[0:1, :]) * _BOUND_MARGIN
    shift = jnp.concatenate([bound[:, 0:1], bound[:, 1:2]], axis=0)
    use_bound = 2.0 * jnp.max(shift) <= _MAX_EXP2_SPAN

    @pl.when(use_bound)
    def _():
        lane2 = lax.broadcasted_iota(jnp.int32, (2 * tq, 128), 1)
        qf = jnp.where(lane2 == DA_VDIM, (-shift).astype(BF16), qs)

        def absorb_shifted(kblk, vtblk):
            acc_ref[...] += _dot(vtblk, jnp.exp2(_dot_nt(kblk, qf)).astype(BF16))

        acc_ref[...] = jnp.zeros(acc_ref.shape, F32)
        absorb_shifted(kc_ref[...], vct_ref[0])
        if with_latent:
            def body(c, carry):
                c0 = pl.multiple_of(c * fast_chunk, fast_chunk)
                absorb_shifted(k_ref[pl.ds(c0, fast_chunk), :], vt_ref[0, :, pl.ds(c0, fast_chunk)])
                return carry

            lax.fori_loop(0, k_ref.shape[0] // fast_chunk, body, 0, unroll=fast_unroll)

    @pl.when(jnp.logical_not(use_bound))
    def _():
        m, acc = absorb(scores(kc_ref[...]), vct_ref[0], jnp.full((1, 2 * tq), NEG_BIG, F32),
                        jnp.zeros((128, 2 * tq), F32))
        if with_latent:
            pk = min(piece, kchunk)
            npieces = kchunk // pk

            def col_max(x):
                if pk > 128:
                    x = jnp.max(x.reshape(pk // 128, 128, 2 * tq), axis=0)
                return jnp.max(x.reshape(x.shape[0] // 8, 8, 2 * tq), axis=0)

            def fused(cur, c, m_, acc_, nxt):
                part = jnp.full((8, 2 * tq), NEG_BIG, F32)
                for t in range(npieces):
                    if nxt is not None:
                        r0 = pl.multiple_of((c + 1) * kchunk + t * pk, pk)
                        s_new = scores(k_ref[pl.ds(r0, pk), :])
                        s_ref[nxt, t * pk:(t + 1) * pk, :] = s_new
                        part = jnp.maximum(part, col_max(s_new))
                    c0 = pl.multiple_of(c * kchunk + t * pk, pk)
                    p = jnp.exp2((s_ref[cur, t * pk:(t + 1) * pk, :] - m_).astype(BF16))
                    acc_ = acc_ + _dot(vt_ref[0, :, pl.ds(c0, pk)], p)
                if nxt is None:
                    return m_, acc_
                m_new = jnp.maximum(m_, jnp.max(part, axis=0, keepdims=True))
                return m_new, acc_ * jnp.exp2(m_ - m_new)

            s0 = scores(latent_k(0))
            s_ref[0] = s0
            m0 = jnp.maximum(m, jnp.max(jnp.max(s0.reshape(kchunk // 128, 128, 2 * tq), axis=0), axis=0,
                                        keepdims=True))
            acc = acc * jnp.exp2(m - m0)
            m = m0

            def pair(c, carry, last):
                m_, acc_ = fused(0, c, *carry, 1)
                return fused(1, c + 1, m_, acc_, None if last else 0)

            if nchunks > 2:
                m, acc = lax.fori_loop(0, nchunks // 2 - 1, lambda j, cr: pair(2 * j, cr, False), (m, acc))
            m, acc = pair(nchunks - 2, (m, acc), True)
        acc_ref[...] = acc

    acc = acc_ref[...]
    a1 = acc[:, :tq]
    a2 = acc[:, tq:]
    o = a1 / a1[DA_VDIM:DA_VDIM + 1, :] - lam_ref[0] * (a2 / a2[DA_VDIM:DA_VDIM + 1, :])
    row = lax.broadcasted_iota(jnp.int32, (128, tq), 0)
    o = jnp.where(row < DA_VDIM, o, 0.0)
    ms = jnp.sum(o * o, axis=0, keepdims=True) * (1.0 / DA_VDIM)
    o_ref[...] = (o * lax.rsqrt(ms + NORM_EPS) * gain_ref[...] * out_scale).T.astype(BF16)


def _diff_attention(lam, q, kc, vc, k, v, gain, batch, seq_q, ctx_len, seq_k, out_scale, q_is_ctx):
    tq = min(512, seq_q)
    nq = seq_q // tq
    with_latent = not q_is_ctx
    if with_latent:
        kchunk = min(2048, seq_k // 2)
        assert seq_k % (2 * kchunk) == 0
    else:
        k, v = kc, vc
        seq_k, kchunk = ctx_len, ctx_len
    fast_chunk = kchunk
    fast_unroll = 2 if (seq_k // fast_chunk) % 2 == 0 else 1
    kern = functools.partial(_attn_kernel, with_latent=with_latent, kchunk=kchunk, piece=512, fast_chunk=fast_chunk,
                             fast_unroll=fast_unroll, out_scale=out_scale)
    return pl.pallas_call(
        kern,
        grid=(batch, DA_HEADS, nq),
        in_specs=[
            pl.BlockSpec(memory_space=pltpu.SMEM),
            pl.BlockSpec((tq, 128), lambda b, h, i: (b * nq + i, h)),
            pl.BlockSpec((ctx_len, 128), lambda b, h, i: (b, h)),
            pl.BlockSpec((1, 128, ctx_len), lambda b, h, i: (b, h, 0)),
            pl.BlockSpec((seq_k, 128), lambda b, h, i: (b, h)),
            pl.BlockSpec((1, 128, seq_k), lambda b, h, i: (b, h, 0)),
            pl.BlockSpec((128, 1), lambda b, h, i: (h, 0)),
        ],
        out_specs=pl.BlockSpec((tq, 128), lambda b, h, i: (b * nq + i, h)),
        out_shape=jax.ShapeDtypeStruct((batch * seq_q, DA_X), BF16),
        scratch_shapes=[pltpu.VMEM((2, kchunk, 2 * tq), F32), pltpu.VMEM((128, 2 * tq), F32),
                        pltpu.VMEM((8, 128), F32)],
        compiler_params=_params("arbitrary", "arbitrary", "arbitrary"),
        name="diff_attention",
    )(lam, q, kc, vc, k, v, gain)


def _outproj_kernel(x_ref, hof_ref, hob_ref, hg_ref, rof_ref, rob_ref, rg_ref, da_ref, seg_ref, w_ref,
                    hn_ref, rn_ref, n1_ref, n2_ref, g1_ref, sc2_ref, sh2_ref, rw_ref, rb_ref,
                    x1_ref, tok_ref, idx_ref, gate_ref):
    seg = seg_ref[...]

    def gated_head_norm(o, gain, gate):
        ms = _dot((o * o).astype(BF16), seg) * (1.0 / HDIM)
        return (o * lax.rsqrt(ms + NORM_EPS) * gain * (gate * _sigmoid(gate))).astype(BF16)

    a = gated_head_norm(hof_ref[...].astype(F32) + hob_ref[...].astype(F32), hn_ref[...], hg_ref[...].astype(F32))
    b = gated_head_norm(rof_ref[...].astype(F32) + rob_ref[...].astype(F32), rn_ref[...], rg_ref[...].astype(F32))
    y = (_dot(a, w_ref[0:REC_W, :]) + _dot(b, w_ref[REC_W:2 * REC_W, :])
         + _dot(da_ref[...], w_ref[2 * REC_W:2 * REC_W + DA_X, :]))
    x1 = x_ref[...] + g1_ref[0] * _rms(y, n1_ref[...])
    x1_ref[...] = x1
    tok = _rms(x1, n2_ref[...]) * (1.0 + sc2_ref[0]) + sh2_ref[0]
    tok_ref[...] = tok.astype(BF16)
    logits = _dot_nt(rw_ref[...], tok) + rb_ref[...]
    eid = lax.broadcasted_iota(jnp.int32, logits.shape, 0).astype(F32)
    vals, ids = [], []
    for _ in range(TOP_K):
        best = jnp.max(logits, axis=0, keepdims=True)
        first = jnp.min(jnp.where(logits == best, eid, float(N_EXPERTS)), axis=0, keepdims=True)
        vals.append(best)
        ids.append(first)
        logits = jnp.where(eid == first, -jnp.inf, logits)
    ex = [jnp.exp(v - vals[0]) for v in vals]
    total = ex[0] + ex[1] + ex[2] + ex[3]
    idx_ref[...] = jnp.concatenate(ids, axis=0).astype(jnp.int32)
    gate_ref[...] = jnp.concatenate(ex, axis=0) / total


def _outproj(xf, seq, streams, seg, w_out, hn, rn, n1, n2, g1, sc2, sh2, rw, rb):
    rows, d = xf.shape
    tm = min(256, seq)
    nb = seq // tm
    row_spec = lambda w: pl.BlockSpec((tm, w), lambda i: (i, 0))
    full = lambda a: pl.BlockSpec(a.shape, lambda i: (0,) * a.ndim)
    mod_spec = pl.BlockSpec((1, 1, d), lambda i: (i // nb, 0, 0))
    hof, hob, hg, rof, rob, rg, da = streams
    return pl.pallas_call(
        _outproj_kernel,
        grid=(rows // tm,),
        in_specs=[row_spec(d)] + [row_spec(REC_W)] * 6 + [row_spec(DA_X), full(seg), full(w_out),
                  full(hn), full(rn), full(n1), full(n2), mod_spec, mod_spec, mod_spec, full(rw), full(rb)],
        out_specs=[row_spec(d), row_spec(d), pl.BlockSpec((TOP_K, tm), lambda i: (0, i)),
                   pl.BlockSpec((TOP_K, tm), lambda i: (0, i))],
        out_shape=[jax.ShapeDtypeStruct((rows, d), F32), jax.ShapeDtypeStruct((rows, d), BF16),
                   jax.ShapeDtypeStruct((TOP_K, rows), jnp.int32), jax.ShapeDtypeStruct((TOP_K, rows), F32)],
        compiler_params=_params("arbitrary"),
        name="outproj",
    )(xf, hof, hob, hg, rof, rob, rg, da, seg, w_out, hn, rn, n1, n2, g1, sc2, sh2, rw, rb)


def _ffn_kernel(be_ref, na_ref, x_ref, w1_ref, w2_ref, b1g_ref, b1l_ref, b2_ref, pe_ref, po_ref, yprev_ref, y_ref,
                w1g_s, w1l_s, w2_s, *, blk0):
    del yprev_ref
    i = pl.program_id(0)
    blk = i + blk0
    active = blk < na_ref[0]
    fresh = jnp.logical_or(i == 0, be_ref[blk] != be_ref[jnp.maximum(blk - 1, 0)])

    @pl.when(fresh)
    def _():
        n = w1_ref.shape[3]
        for c in range(n // 256):
            wb = w1_ref[0, 0, :, 256 * c:256 * (c + 1)].astype(BF16)
            w1g_s[:, 128 * c:128 * (c + 1)] = _dot(wb, pe_ref[...]).astype(BF16)
            w1l_s[:, 128 * c:128 * (c + 1)] = _dot(wb, po_ref[...]).astype(BF16)
        w2_s[...] = w2_ref[0, 0].astype(BF16)

    @pl.when(jnp.logical_not(active))
    def _():
        y_ref[...] = jnp.zeros(y_ref.shape, y_ref.dtype)

    @pl.when(active)
    def _():
        x = x_ref[...]
        glu = jnp.minimum(_dot(x, w1g_s[...]) + b1g_ref[0], SWIGLU_LIMIT)
        lin = jnp.clip(_dot(x, w1l_s[...]) + b1l_ref[0], -SWIGLU_LIMIT, SWIGLU_LIMIT)
        act = glu * _sigmoid(SWIGLU_ALPHA * glu) * (lin + 1.0)
        y_ref[...] = (_dot(act.astype(BF16), w2_s[...]) + b2_ref[0]).astype(y_ref.dtype)


def _expert_ffn(block_expert, n_active, tok, tok_sorted, w1, w2, b1g, b1l, b2, layer, bm, nparts=4):
    p = tok_sorted.shape[0]
    d = tok.shape[1]
    f = w2.shape[2]
    nblocks = p // bm
    src = jnp.arange(256)[:, None]
    dst = jnp.arange(128)[None, :]
    pe = (src == 2 * dst).astype(BF16)
    po = (src == 2 * dst + 1).astype(BF16)
    sel = pl.BlockSpec((256, 128), lambda i, be, na: (0, 0))
    bounds = [0] + [nblocks * (4 * j + 1) // (4 * nparts) for j in range(nparts)] + [nblocks]
    nparts += 1
    yg = jnp.zeros((8, 128), BF16)
    for j in range(nparts):
        b0, nb = bounds[j], bounds[j + 1] - bounds[j]
        xg = tok.at[tok_sorted[b0 * bm:(b0 + nb) * bm]].get(mode="promise_in_bounds")
        wspec = lambda s, b0=b0: pl.BlockSpec((1, 1) + s, lambda i, be, na: (layer, be[i + b0], 0, 0))
        bspec = lambda s, b0=b0: pl.BlockSpec((1,) + s, lambda i, be, na: (be[i + b0], 0, 0))
        yg = pl.pallas_call(
            functools.partial(_ffn_kernel, blk0=b0),
            grid_spec=pltpu.PrefetchScalarGridSpec(
                num_scalar_prefetch=2,
                grid=(nb,),
                in_specs=[pl.BlockSpec((bm, d), lambda i, be, na: (i, 0)),
                          wspec((d, 2 * f)), wspec((f, d)), bspec((1, f)), bspec((1, f)), bspec((1, d)), sel, sel,
                          pl.BlockSpec(memory_space=pl.ANY)],
                out_specs=pl.BlockSpec((bm, d), lambda i, be, na, b0=b0: (i + b0, 0)),
                scratch_shapes=[pltpu.VMEM((d, f), BF16), pltpu.VMEM((d, f), BF16), pltpu.VMEM((f, d), BF16)],
            ),
            out_shape=jax.ShapeDtypeStruct((p, d), BF16),
            input_output_aliases={10: 0} if j else {},
            compiler_params=_params("arbitrary"),
            name="expert_ffn",
        )(block_expert, n_active, xg, w1, w2, b1g, b1l, b2, pe, po, yg)
    return yg


def _resid_kernel(x_ref, y0_ref, y1_ref, y2_ref, y3_ref, gate_ref, n_ref, g_ref, o_ref):
    gates = gate_ref[...]
    f = y0_ref[...].astype(F32) * gates[:, 0:1]
    for k, y_ref in enumerate((y1_ref, y2_ref, y3_ref), start=1):
        f = f + y_ref[...].astype(F32) * gates[:, k:k + 1]
    o_ref[...] = x_ref[...] + g_ref[0] * _rms(f, n_ref[...])


def _ffn_residual(xf, ys, gates, seq, n3, g2):
    rows, d = xf.shape
    tm = min(512, seq)
    nb = seq // tm
    row_spec = pl.BlockSpec((tm, d), lambda i: (i, 0))
    return pl.pallas_call(
        _resid_kernel,
        grid=(rows // tm,),
        in_specs=[row_spec] * 5 + [pl.BlockSpec((tm, TOP_K), lambda i: (i, 0)), pl.BlockSpec((1, d), lambda i: (0, 0)),
                                   pl.BlockSpec((1, 1, d), lambda i: (i // nb, 0, 0))],
        out_specs=row_spec,
        out_shape=jax.ShapeDtypeStruct((rows, d), F32),
        compiler_params=_params("arbitrary"),
        name="ffn_residual",
    )(xf, *ys, gates, n3, g2)


def _rope_tables(pos, dim):
    inv = 1.0 / (ROPE_BASE ** (jnp.arange(0, dim, 2, dtype=F32) / dim))
    ang = pos.astype(F32)[:, None] * inv[None, :]
    return jnp.cos(ang), jnp.sin(ang)


def _rot_cols(w, head_dim, halves):
    d, n = w.shape
    g = head_dim // halves
    w4 = w.reshape(d, n // g, 2, g // 2)
    return jnp.concatenate([-w4[:, :, 1], w4[:, :, 0]], axis=-1).reshape(d, n)


def _prep_w_in(w):
    rq, rk = w[:, _C_RQ:_C_RQ + REC_W], w[:, _C_RK:_C_RK + REC_W]
    dq, dk = w[:, _C_DQ:_C_DQ + DA_W], w[:, _C_DK:_C_DK + DA_W]
    return jnp.concatenate([w, _rot_cols(rq, HDIM, 1), _rot_cols(rk, HDIM, 1),
                            _rot_cols(dq, DA_HEAD_DIM, 2), _rot_cols(dk, DA_HEAD_DIM, 2)], axis=1).astype(BF16)


def _routing(top_idx_t, bm):
    n = top_idx_t.shape[1]
    a = n * TOP_K
    flat_e = top_idx_t.T.reshape(a)
    sorted_e, order = lax.sort_key_val(flat_e, jnp.arange(a, dtype=jnp.int32))
    experts = jnp.arange(N_EXPERTS + 1, dtype=jnp.int32)
    bounds = jnp.sum((flat_e[None, :] < experts[:, None]).astype(jnp.int32), axis=1)
    start, counts = bounds[:-1], bounds[1:] - bounds[:-1]
    padded = (counts + bm - 1) // bm * bm
    pend = jnp.cumsum(padded)
    shift = pend - padded - start
    dest = jnp.arange(a, dtype=jnp.int32) + shift[sorted_e]
    _, pos = lax.sort_key_val(order, dest)
    nblocks = -(-(a + N_EXPERTS * (bm - 1)) // bm)
    block_row = jnp.arange(nblocks, dtype=jnp.int32) * bm
    block_expert = jnp.minimum(jnp.sum((pend[None, :] <= block_row[:, None]).astype(jnp.int32), axis=1),
                               N_EXPERTS - 1)
    row_e = jnp.repeat(block_expert, bm)
    rank = jnp.arange(nblocks * bm, dtype=jnp.int32) - shift[row_e]
    valid = rank < bounds[row_e + 1]
    tok_sorted = jnp.where(valid, order[jnp.clip(rank, 0, a - 1)] // TOP_K, 0)
    n_active = (pend[-1:] // bm).astype(jnp.int32)
    return tok_sorted, pos, block_expert, n_active


def kernel(x, c, ctx, c_ctx, mod_w, mod_b, norm_g, w_in, hgrn_lb, hgrn_norm, ret_decay, ret_norm, da_lambda,
           da_subln, w_out, router_w, router_b, w1, b1, w2, b2):
    B, S, D = x.shape
    C = ctx.shape[1]
    depth = mod_w.shape[0]
    bm = 256

    pos = jnp.arange(S)
    cr, sr = _rope_tables(pos // GRID_W, DA_HEAD_DIM // 2)
    cc, sc_ = _rope_tables(pos % GRID_W, DA_HEAD_DIM // 2)
    cs, ss = _rope_tables(pos, HDIM)
    cosr = jnp.tile(jnp.concatenate([cs, cs], -1), (1, 128 // HDIM))
    sinr = jnp.tile(jnp.concatenate([ss, ss], -1), (1, 128 // HDIM))
    cosd = jnp.tile(jnp.concatenate([cr, cr, cc, cc], -1), (1, 128 // DA_HEAD_DIM))
    sind = jnp.tile(jnp.concatenate([sr, sr, sc_, sc_], -1), (1, 128 // DA_HEAD_DIM))
    ones_r, zeros_r = jnp.ones((C, 128), F32), jnp.zeros((C, 128), F32)
    ones_d, zeros_d = ones_r, zeros_r

    lb_cum = jnp.cumsum(jax.nn.softmax(hgrn_lb.astype(F32), axis=0), axis=0)
    lower = lb_cum - lb_cum[0:1]

    cvec = jnp.zeros((8, D), F32).at[:B].set(c).at[B].set(c_ctx)
    mods = _modulation(cvec, mod_w, mod_b)

    head_id = jnp.arange(REC_W) // HDIM
    seg = (head_id[:, None] == head_id[None, :]).astype(BF16)

    xf = x.reshape(B * S, D)
    xc = ctx.reshape(B * C, D)
    zero_state = jnp.zeros((B, 2, HEADS, HDIM, HDIM), F32)

    for layer in range(depth):
        need_ctx = layer < depth - 1
        lam_init = 0.8 - 0.6 * math.exp(-0.3 * layer)
        m6 = mods[layer].reshape(8, 6, D)
        lat = lambda k: m6[:B, k][:, None, :]
        cxm = lambda k: jnp.broadcast_to(m6[B, k][None, None, :], (B, 1, D))
        ng = norm_g[layer]
        lb = lower[layer]
        lbc = jnp.zeros((8, REC_W), F32).at[0].set(jnp.log(lb)).at[1].set(jnp.log1p(-lb)).at[2].set(1.0 - lb)
        w_all = _prep_w_in(w_in[layer])
        log_gamma = jnp.log1p(-jnp.exp2(-ret_decay[layer].astype(F32)))
        lgrow = jnp.zeros((8, REC_W), F32).at[:2].set(jnp.repeat(log_gamma, HDIM, axis=1))
        lamv = da_lambda[layer].astype(F32)
        lam = (jnp.exp(jnp.sum(lamv[0] * lamv[1])) - jnp.exp(jnp.sum(lamv[2] * lamv[3])) + lam_init).reshape(1)
        hn = jnp.tile(hgrn_norm[layer], HEADS)[None, :]
        rn = jnp.tile(ret_norm[layer], HEADS)[None, :]
        dn = jnp.tile(jnp.concatenate([da_subln[layer], jnp.zeros((128 - DA_VDIM,), F32)]), DA_HEADS)[:, None]
        wo = w_out[layer]
        wo_da = jnp.pad(wo[2 * REC_W:].reshape(DA_HEADS, DA_VDIM, D), ((0, 0), (0, 128 - DA_VDIM), (0, 0)))
        w_out_b = jnp.concatenate([wo[:2 * REC_W], wo_da.reshape(DA_X, D)], axis=0).astype(BF16)
        rw = router_w[layer].T
        rb = router_b[layer][:, None]

        (hq, hv, hg, kf, lff, kb, lfb, rq, rk, rv, rg, dq, dk, dvx) = _inproj(
            xf, S, ng[0:1], lat(1), lat(0), w_all, lbc, cosr, sinr, cosd, sind)
        (hq_c, hv_c, hg_c, kf_c, lff_c, kb_c, lfb_c, rq_c, rk_c, rv_c, rg_c, dq_c, dk_c, dvx_c) = _inproj(
            xc, C, ng[0:1], cxm(1), cxm(0), w_all, lbc, ones_r, zeros_r, ones_d, zeros_d)

        hof_c, hob_c, hs = _hgrn_scan(hq_c, hv_c, kf_c, lff_c, kb_c, lfb_c,
                                      jnp.zeros((B, 2, REC_W // 128, 128, 128), F32), seg, B, C)
        hof, hob, _ = _hgrn_scan(hq, hv, kf, lff, kb, lfb, hs, seg, B, S)
        rof_c, rob_c, rs = _ret_scan(rq_c, rk_c, rv_c, zero_state, log_gamma, lgrow, B, C)
        rof, rob, _ = _ret_scan(rq, rk, rv, rs, log_gamma, lgrow, B, S)
        out_scale = 1.0 - lam_init
        da = _diff_attention(lam, dq, dk_c, dvx_c, dk, dvx, dn, B, S, C, S, out_scale, False)

        x1, tok, top_idx_t, gates_t = _outproj(xf, S, (hof, hob, hg, rof, rob, rg, da), seg, w_out_b, hn, rn,
                                               ng[1:2], ng[2:3], lat(2), lat(4), lat(3), rw, rb)
        if need_ctx:
            da_c = _diff_attention(lam, dq_c, dk_c, dvx_c, None, None, dn, B, C, C, C, out_scale, True)
            xc1, tok_c, top_idx_c, gates_c = _outproj(xc, C, (hof_c, hob_c, hg_c, rof_c, rob_c, rg_c, da_c), seg,
                                                      w_out_b, hn, rn, ng[1:2], ng[2:3], cxm(2), cxm(4), cxm(3),
                                                      rw, rb)
            tok = jnp.concatenate([tok, tok_c], axis=0)
            top_idx_t = jnp.concatenate([top_idx_t, top_idx_c], axis=1)
            gates_t = jnp.concatenate([gates_t, gates_c], axis=1)

        n_tok = tok.shape[0]
        gates = gates_t.T
        tok_sorted, posn, block_expert, n_active = _routing(top_idx_t, bm)
        b1g = b1[layer][:, None, 0::2]
        b1l = b1[layer][:, None, 1::2]
        b2l = b2[layer][:, None, :]
        yg = _expert_ffn(block_expert, n_active, tok, tok_sorted, w1, w2, b1g, b1l, b2l, layer, bm)
        pos2 = posn.reshape(n_tok, TOP_K)

        def expert_rows(lo, hi):
            return [yg.at[pos2[lo:hi, k]].get(mode="promise_in_bounds") for k in range(TOP_K)]

        xf = _ffn_residual(x1, expert_rows(0, B * S), gates[:B * S], S, ng[3:4], lat(5))
        if need_ctx:
            xc = _ffn_residual(xc1, expert_rows(B * S, n_tok), gates[B * S:], C, ng[3:4], cxm(5))
    return xf.reshape(B, S, D)
```

```python
import functools
import math

import jax
import jax.numpy as jnp
from jax import lax
from jax.experimental import pallas as pl
from jax.experimental.pallas import tpu as pltpu

F32 = jnp.float32
BF16 = jnp.bfloat16

GRID_W = 64
HEADS = 6
HDIM = 64
REC_W = HEADS * HDIM
DA_HEADS = 4
DA_HEAD_DIM = 32
DA_MAPS = 2 * DA_HEADS
DA_W = DA_MAPS * DA_HEAD_DIM
DA_VDIM = 2 * DA_HEAD_DIM
DA_X = DA_HEADS * 128
_Q_SCALE = DA_HEAD_DIM ** -0.5 * math.log2(math.e)
_BOUND_MARGIN = 1.01
_MAX_EXP2_SPAN = 100.0
ROPE_BASE = 10000.0
N_EXPERTS = 32
TOP_K = 4
SWIGLU_ALPHA = 1.702
SWIGLU_LIMIT = 7.0
NORM_EPS = 1e-6

SUB = 16
NEG_BIG = -1e30
VMEM_LIMIT = 56 * 1024 * 1024

_C_HQ, _C_HFF, _C_HFB, _C_HI, _C_HG = 0, 384, 768, 1152, 1536
_C_RQ, _C_RK, _C_RV, _C_RG = 1920, 2304, 2688, 3072
_C_DQ, _C_DK, _C_DV = 3456, 3712, 3968
_C_RQR, _C_RKR, _C_DQR, _C_DKR = 4224, 4608, 4992, 5248
_W_ALL = 5504


def _dot(a, b):
    return jnp.dot(a, b, preferred_element_type=F32)


def _dot_nt(a, b):
    return lax.dot_general(a, b, (((1,), (1,)), ((), ())), preferred_element_type=F32)


def _dot_tn(a, b):
    return lax.dot_general(a, b, (((0,), (0,)), ((), ())), preferred_element_type=F32)


def _sigmoid(x):
    return 1.0 / (1.0 + jnp.exp(-x))


def _rms(x, g):
    ms = jnp.mean(x * x, axis=-1, keepdims=True)
    return x * lax.rsqrt(ms + NORM_EPS) * g


def _params(*sem):
    return pltpu.CompilerParams(dimension_semantics=sem, vmem_limit_bytes=VMEM_LIMIT)


def _mod_kernel(c_ref, w_ref, b_ref, o_ref):
    c = c_ref[...]
    o_ref[0] = _dot(c * _sigmoid(c), w_ref[0]) + b_ref[0]


def _modulation(cvec, mod_w, mod_b):
    depth, d, n = mod_w.shape
    tn = 1536
    return pl.pallas_call(
        _mod_kernel,
        grid=(depth, n // tn),
        in_specs=[
            pl.BlockSpec((8, d), lambda l, j: (0, 0)),
            pl.BlockSpec((1, d, tn), lambda l, j: (l, 0, j)),
            pl.BlockSpec((1, 1, tn), lambda l, j: (l, 0, j)),
        ],
        out_specs=pl.BlockSpec((1, 8, tn), lambda l, j: (l, 0, j)),
        out_shape=jax.ShapeDtypeStruct((depth, 8, n), F32),
        compiler_params=_params("arbitrary", "arbitrary"),
        name="modulation",
    )(cvec, mod_w, mod_b.reshape(depth, 1, n))


def _inproj_kernel(x_ref, g_ref, sc_ref, sh_ref, w_ref, lbc_ref, cosr_ref, sinr_ref, cosd_ref, sind_ref,
                   hq_ref, hv_ref, hg_ref, kf_ref, lff_ref, kb_ref, lfb_ref,
                   rq_ref, rk_ref, rv_ref, rg_ref, dq_ref, dk_ref, dvt_ref):
    x = x_ref[...]
    h = _rms(x, g_ref[...]) * (1.0 + sc_ref[0]) + sh_ref[0]
    hb = h.astype(BF16)

    wide = {}

    def proj(c0, n):
        lo = max(b for b in (0, _C_RQ, _C_DQ) if b <= c0)
        hi = {0: _C_RQ, _C_RQ: _C_DQ, _C_DQ: _W_ALL}[lo]
        if lo not in wide:
            wide[lo] = _dot(hb, w_ref[:, lo:hi])
        return wide[lo][:, c0 - lo:c0 - lo + n]

    hq_ref[...] = proj(_C_HQ, REC_W).astype(BF16)
    hv_ref[...] = proj(_C_HI, REC_W).astype(BF16)
    hg_ref[...] = proj(_C_HG, REC_W).astype(BF16)

    log_lb = lbc_ref[0:1, :]
    log_1m = lbc_ref[1:2, :]
    one_m = lbc_ref[2:3, :]

    def gates(z):
        log_sig = jnp.minimum(z, 0.0) - jnp.log1p(jnp.exp(-jnp.abs(z)))
        t = log_1m + log_sig
        m = jnp.maximum(log_lb, t)
        logf = m + jnp.log1p(jnp.exp(-jnp.abs(log_lb - t)))
        return one_m / (1.0 + jnp.exp(z)), logf

    k, lf = gates(proj(_C_HFF, REC_W))
    kf_ref[...] = k.astype(BF16)
    lff_ref[...] = lf
    k, lf = gates(proj(_C_HFB, REC_W))
    kb_ref[...] = k.astype(BF16)
    lfb_ref[...] = lf

    cosr = jnp.concatenate([cosr_ref[...]] * (REC_W // 128), axis=1)
    sinr = jnp.concatenate([sinr_ref[...]] * (REC_W // 128), axis=1)
    rq_ref[...] = (proj(_C_RQ, REC_W) * cosr + proj(_C_RQR, REC_W) * sinr).astype(BF16)
    rk_ref[...] = ((proj(_C_RK, REC_W) * cosr + proj(_C_RKR, REC_W) * sinr) * (HDIM ** -0.5)).astype(BF16)
    rv_ref[...] = proj(_C_RV, REC_W).astype(BF16)
    rg_ref[...] = proj(_C_RG, REC_W).astype(BF16)

    cosd = jnp.concatenate([cosd_ref[...]] * (DA_W // 128), axis=1)
    sind = jnp.concatenate([sind_ref[...]] * (DA_W // 128), axis=1)
    dq = ((proj(_C_DQ, DA_W) * cosd + proj(_C_DQR, DA_W) * sind) * _Q_SCALE).astype(BF16)
    dk = (proj(_C_DK, DA_W) * cosd + proj(_C_DKR, DA_W) * sind).astype(BF16)
    dv_t = proj(_C_DV, DA_W).T.astype(BF16)
    tm = dq.shape[0]
    zero_col = jnp.zeros((tm, 64), BF16)
    ones_col = jnp.where(lax.broadcasted_iota(jnp.int32, (tm, 64), 1) == 0, 1.0, 0.0).astype(BF16)
    sub = lax.broadcasted_iota(jnp.int32, (64, tm), 0)
    ones_row = jnp.where(sub == 0, 1.0, 0.0).astype(BF16)
    for h_ in range(DA_HEADS):
        lo, mid, hi_ = 128 * h_, 128 * h_ + 64, 128 * h_ + 128
        dq_ref[:, lo:mid] = dq[:, 64 * h_:64 * h_ + 64]
        dq_ref[:, mid:hi_] = zero_col
        dk_ref[:, lo:mid] = dk[:, 64 * h_:64 * h_ + 64]
        dk_ref[:, mid:hi_] = ones_col
        dvt_ref[0, lo:mid, :] = dv_t[64 * h_:64 * h_ + 64, :]
        dvt_ref[0, mid:hi_, :] = ones_row


def _inproj(xf, seq, gnorm, sc, sh, w_all, lbc, cosr, sinr, cosd, sind):
    rows, d = xf.shape
    tm = min(256, seq)
    nb = seq // tm
    row_spec = lambda w: pl.BlockSpec((tm, w), lambda i: (i, 0))
    tab_spec = lambda w: pl.BlockSpec((tm, w), lambda i: (i % nb, 0))
    mod_spec = pl.BlockSpec((1, 1, d), lambda i: (i // nb, 0, 0))
    widths = [REC_W] * 11 + [DA_X, DA_X]
    dtypes = [BF16, BF16, BF16, BF16, F32, BF16, F32, BF16, BF16, BF16, BF16, BF16, BF16]
    out_specs = [row_spec(w) for w in widths] + [pl.BlockSpec((1, DA_X, tm), lambda i: (i // nb, 0, i % nb))]
    out_shape = ([jax.ShapeDtypeStruct((rows, w), dt) for w, dt in zip(widths, dtypes)]
                 + [jax.ShapeDtypeStruct((rows // seq, DA_X, seq), BF16)])
    return pl.pallas_call(
        _inproj_kernel,
        grid=(rows // tm,),
        in_specs=[
            row_spec(d),
            pl.BlockSpec((1, d), lambda i: (0, 0)),
            mod_spec, mod_spec,
            pl.BlockSpec((d, _W_ALL), lambda i: (0, 0)),
            pl.BlockSpec((8, REC_W), lambda i: (0, 0)),
            tab_spec(128), tab_spec(128), tab_spec(128), tab_spec(128),
        ],
        out_specs=out_specs,
        out_shape=out_shape,
        compiler_params=_params("arbitrary"),
        name="inproj",
    )(xf, gnorm, sc, sh, w_all, lbc, cosr, sinr, cosd, sind)


def _hgrn_dir(q, k, v, lf, seg, st_ref, inter_ref, intra_ref, reverse):
    tt = q.shape[0]
    nblk = tt // SUB
    row = lax.broadcasted_iota(jnp.int32, (tt, tt), 0)
    col = lax.broadcasted_iota(jnp.int32, (tt, tt), 1)
    same = (row // SUB) == (col // SUB)
    tri = (col >= row) if reverse else (col <= row)
    l_all = jnp.where(same, 1.0, 0.0)
    l_cum = jnp.where(tri, l_all, 0.0).astype(BF16)
    l_all = l_all.astype(BF16)
    hi = lf.astype(BF16)
    r1 = lf - hi.astype(F32)
    mid = r1.astype(BF16)
    lo = (r1 - mid.astype(F32)).astype(BF16)
    b = _dot(l_cum, hi) + _dot(l_cum, mid) + _dot(l_cum, lo)
    btot = _dot(l_all, hi) + _dot(l_all, mid) + _dot(l_all, lo)

    qe = (q * jnp.exp(b)).astype(BF16)
    kd = (k * jnp.exp(btot - b)).astype(BF16)
    dec = jnp.exp(btot)
    v_t = v.T
    tok_blk = lax.broadcasted_iota(jnp.int32, (128, tt), 1) // SUB
    hr = lax.broadcasted_iota(jnp.int32, (128, 128), 0) // HDIM
    hc = lax.broadcasted_iota(jnp.int32, (128, 128), 1) // HDIM
    same_head = hr == hc
    order = range(nblk - 1, -1, -1) if reverse else range(nblk)
    half = lax.broadcasted_iota(jnp.int32, (nblk, 8, 128), 1)
    seg2 = seg[0:128, 0:128]

    def carried(c, g):
        v_tg = v_t[c, :]
        quarter = max(nblk // 4, 1)
        upd = []
        for j0 in range(0, nblk, quarter):
            v_stack = jnp.concatenate([jnp.where(tok_blk == j, v_tg, 0.0).astype(BF16)
                                       for j in range(j0, j0 + quarter)], axis=0)
            part = _dot(v_stack, kd[:, c])
            upd += [part[128 * j:128 * (j + 1)] for j in range(quarter)]
        st = st_ref[g]
        for j in order:
            r0 = j * SUB
            inter_ref[r0:r0 + SUB, c] = _dot_nt(qe[r0:r0 + SUB, c], st.astype(BF16))
            st = st * dec[r0:r0 + 1, c] + jnp.where(same_head, upd[j], 0.0)
        st_ref[g] = st

    def within(c):
        def halves(a):
            a4 = a[:, c].reshape(nblk, 2, 8, 128)
            return (a4[:, 1], a4[:, 0]) if reverse else (a4[:, 0], a4[:, 1])

        (q0, q1), (k0, k1), (v0, v1), (b0, b1) = halves(q), halves(k), halves(v), halves(b)

        def pairs(qa, ba, kb_, bb_, vb_, r, masked):
            sh = ((8 - r) % 8) if reverse else r
            if sh:
                kb_, bb_, vb_ = pltpu.roll(kb_, sh, 1), pltpu.roll(bb_, sh, 1), pltpu.roll(vb_, sh, 1)
            diff = ba - bb_
            if masked and r:
                diff = jnp.where((half <= 7 - r) if reverse else (half >= r), diff, NEG_BIG)
            term = (qa * kb_ * jnp.exp(diff)).reshape(nblk * 8, 128).astype(BF16)
            return _dot(term, seg2).reshape(nblk, 8, 128) * vb_

        o0 = pairs(q0, b0, k0, b0, v0, 0, True)
        o1 = pairs(q1, b1, k1, b1, v1, 0, True) + pairs(q1, b1, k0, b0, v0, 0, False)
        for r in range(1, 8):
            o0 = o0 + pairs(q0, b0, k0, b0, v0, r, True)
            o1 = o1 + pairs(q1, b1, k1, b1, v1, r, True) + pairs(q1, b1, k0, b0, v0, r, False)
        first, second = (o1, o0) if reverse else (o0, o1)
        intra_ref[:, 0, :, c] = first
        intra_ref[:, 1, :, c] = second

    for g in range(REC_W // 128):
        c = slice(128 * g, 128 * (g + 1))
        carried(c, g)
        within(c)
    return intra_ref[...].reshape(tt, REC_W) + inter_ref[...]


def _hgrn_kernel(qf_ref, vf_ref, kf_ref, lff_ref, qb_ref, vb_ref, kb_ref, lfb_ref, s0_ref, seg_ref,
                 of_ref, ob_ref, sout_ref, st_ref, inter_ref, intra_ref):
    i = pl.program_id(1)

    @pl.when(i == 0)
    def _():
        st_ref[...] = s0_ref[0]

    seg = seg_ref[...]
    of_ref[...] = _hgrn_dir(qf_ref[...].astype(F32), kf_ref[...].astype(F32), vf_ref[...].astype(F32),
                            lff_ref[...], seg, st_ref.at[0], inter_ref, intra_ref, False).astype(BF16)
    ob_ref[...] = _hgrn_dir(qb_ref[...].astype(F32), kb_ref[...].astype(F32), vb_ref[...].astype(F32),
                            lfb_ref[...], seg, st_ref.at[1], inter_ref, intra_ref, True).astype(BF16)

    @pl.when(i == pl.num_programs(1) - 1)
    def _():
        sout_ref[0] = st_ref[...]


def _hgrn_scan(hq, hv, kf, lff, kb, lfb, s0, seg, batch, seq):
    tt = min(256, seq)
    n = seq // tt
    fwd = pl.BlockSpec((tt, REC_W), lambda b, i: (b * n + i, 0))
    bwd = pl.BlockSpec((tt, REC_W), lambda b, i: (b * n + n - 1 - i, 0))
    st_spec = pl.BlockSpec((1, 2, REC_W // 128, 128, 128), lambda b, i: (b, 0, 0, 0, 0))
    rows = batch * seq
    return pl.pallas_call(
        _hgrn_kernel,
        grid=(batch, n),
        in_specs=[fwd, fwd, fwd, fwd, bwd, bwd, bwd, bwd, st_spec,
                  pl.BlockSpec((REC_W, REC_W), lambda b, i: (0, 0))],
        out_specs=[fwd, bwd, st_spec],
        out_shape=[jax.ShapeDtypeStruct((rows, REC_W), BF16), jax.ShapeDtypeStruct((rows, REC_W), BF16),
                   jax.ShapeDtypeStruct(s0.shape, F32)],
        scratch_shapes=[pltpu.VMEM((2, REC_W // 128, 128, 128), F32), pltpu.VMEM((tt, REC_W), F32),
                        pltpu.VMEM((tt // SUB, 2, 8, REC_W), F32)],
        compiler_params=_params("arbitrary", "arbitrary"),
        name="hgrn_scan",
    )(hq, hv, kf, lff, hq, hv, kb, lfb, s0, seg)


def _ret_kernel(lg_ref, qf_ref, kf_ref, vf_ref, qb_ref, kb_ref, vb_ref, s0_ref, lgrow_ref,
                of_ref, ob_ref, sout_ref, st_ref, dmask_ref, tab_ref, o_scr):
    i = pl.program_id(1)
    tt = qf_ref.shape[0]

    @pl.when(i == 0)
    def _():
        st_ref[...] = s0_ref[0]
        row = lax.broadcasted_iota(jnp.int32, (tt, tt), 0)
        col = lax.broadcasted_iota(jnp.int32, (tt, tt), 1)
        dist = (row - col).astype(F32)
        for h in range(HEADS):
            fw = jnp.where(dist >= 0, jnp.exp(dist * lg_ref[0, h]), 0.0)
            bw = jnp.where(dist <= 0, jnp.exp(-dist * lg_ref[1, h]), 0.0)
            dmask_ref[h] = fw + bw
        pos = lax.broadcasted_iota(jnp.int32, (tt, REC_W), 0).astype(F32)
        lgf = lgrow_ref[0:1, :]
        lgb = lgrow_ref[1:2, :]
        tab_ref[0] = jnp.exp((pos + 1.0) * lgf)
        tab_ref[1] = jnp.exp((tt - 1.0 - pos) * lgf)
        tab_ref[2] = jnp.exp((tt - pos) * lgb)
        tab_ref[3] = jnp.exp(pos * lgb)

    tile_f = jnp.exp(tt * lgrow_ref[0:1, :])
    tile_b = jnp.exp(tt * lgrow_ref[1:2, :])

    q = qf_ref[...]
    k = kf_ref[...]
    v = vf_ref[...]
    qe = (q.astype(F32) * tab_ref[0]).astype(BF16)
    ke = (k.astype(F32) * tab_ref[1]).astype(BF16)
    for h in range(HEADS):
        c = slice(h * HDIM, (h + 1) * HDIM)
        s = _dot_nt(q[:, c], k[:, c]) * dmask_ref[h]
        st = st_ref[0, h]
        o_scr[:, c] = _dot(s.astype(BF16), v[:, c]) + _dot(qe[:, c], st.astype(BF16))
        st_ref[0, h] = st * tile_f[:, c] + _dot_tn(ke[:, c], v[:, c])
    of_ref[...] = o_scr[...].astype(BF16)

    q = qb_ref[...]
    k = kb_ref[...]
    v = vb_ref[...]
    qe = (q.astype(F32) * tab_ref[2]).astype(BF16)
    ke = (k.astype(F32) * tab_ref[3]).astype(BF16)
    for h in range(HEADS):
        c = slice(h * HDIM, (h + 1) * HDIM)
        st = st_ref[1, h]
        o_scr[:, c] = _dot(qe[:, c], st.astype(BF16))
        st_ref[1, h] = st * tile_b[:, c] + _dot_tn(ke[:, c], v[:, c])
    ob_ref[...] = o_scr[...].astype(BF16)

    @pl.when(i == pl.num_programs(1) - 1)
    def _():
        sout_ref[0] = st_ref[...]


def _ret_scan(rq, rk, rv, s0, lg, lgrow, batch, seq):
    tt = min(256, seq)
    n = seq // tt
    fwd = pl.BlockSpec((tt, REC_W), lambda b, i: (b * n + i, 0))
    bwd = pl.BlockSpec((tt, REC_W), lambda b, i: (b * n + n - 1 - i, 0))
    st_spec = pl.BlockSpec((1, 2, HEADS, HDIM, HDIM), lambda b, i: (b, 0, 0, 0, 0))
    rows = batch * seq
    return pl.pallas_call(
        _ret_kernel,
        grid=(batch, n),
        in_specs=[pl.BlockSpec(memory_space=pltpu.SMEM), fwd, fwd, fwd, bwd, bwd, bwd, st_spec,
                  pl.BlockSpec((8, REC_W), lambda b, i: (0, 0))],
        out_specs=[fwd, bwd, st_spec],
        out_shape=[jax.ShapeDtypeStruct((rows, REC_W), BF16), jax.ShapeDtypeStruct((rows, REC_W), BF16),
                   jax.ShapeDtypeStruct(s0.shape, F32)],
        scratch_shapes=[pltpu.VMEM((2, HEADS, HDIM, HDIM), F32), pltpu.VMEM((HEADS, tt, tt), F32),
                        pltpu.VMEM((4, tt, REC_W), F32), pltpu.VMEM((tt, REC_W), F32)],
        compiler_params=_params("arbitrary", "arbitrary"),
        name="ret_scan",
    )(lg, rq, rk, rv, rq, rk, rv, s0, lgrow)


def _attn_kernel(lam_ref, q_ref, kc_ref, vct_ref, k_ref, vt_ref, gain_ref, o_ref, s_ref, acc_ref, kmax_ref, *,
                 with_latent, kchunk, piece, fast_chunk, fast_unroll, out_scale):
    tq = q_ref.shape[0]
    q = q_ref[...]
    lane = lax.broadcasted_iota(jnp.int32, (tq, 128), 1)
    zero = jnp.zeros_like(q)
    qs = jnp.concatenate([jnp.where(lane < DA_HEAD_DIM, q, zero),
                          jnp.where(lane >= DA_HEAD_DIM, q, zero)], axis=0)
    nchunks = (k_ref.shape[0] // kchunk) if with_latent else 0

    def scores(kblk):
        return _dot_nt(kblk, qs)

    def absorb(s, vtblk, m, acc):
        n = s.shape[0]
        part = jnp.max(s.reshape(n // 128, 128, 2 * tq), axis=0) if n > 128 else s
        m_new = jnp.maximum(m, jnp.max(part, axis=0, keepdims=True))
        p = jnp.exp2((s - m_new).astype(BF16))
        return m_new, jnp.exp2(m - m_new) * acc + _dot(vtblk, p)

    def latent_k(c):
        return k_ref[pl.ds(pl.multiple_of(c * kchunk, kchunk), kchunk), :]

    def latent_vt(c):
        return vt_ref[0, :, pl.ds(pl.multiple_of(c * kchunk, kchunk), kchunk)]

    sel_r = lax.broadcasted_iota(jnp.int32, (128, 128), 0) // DA_HEAD_DIM
    sel_c = lax.broadcasted_iota(jnp.int32, (128, 128), 1)
    sel = jnp.where(sel_r == sel_c, 1.0, 0.0).astype(BF16)

    def key_norm2(kblk):
        kf = kblk.astype(F32)
        return jnp.max(_dot((kf * kf).astype(BF16), sel), axis=0, keepdims=True)

    @pl.when(pl.program_id(2) == 0)
    def _():
        km = key_norm2(kc_ref[...])
        if with_latent:
            km = lax.fori_loop(0, nchunks, lambda c, a: jnp.maximum(a, key_norm2(latent_k(c))), km)
        kmax_ref[...] = jnp.broadcast_to(km, kmax_ref.shape)

    qf32 = q.astype(F32)
    qn2 = _dot((qf32 * qf32).astype(BF16), sel)
    bound = jnp.sqrt(qn2 * kmax_ref[0:1, :]) * _BOUND_MARGIN
    shift = jnp.concatenate([bound[:, 0:1], bound[:, 1:2]], axis=0)
    use_bound = 2.0 * jnp.max(shift) <= _MAX_EXP2_SPAN

    @pl.when(use_bound)
    def _():
        lane2 = lax.broadcasted_iota(jnp.int32, (2 * tq, 128), 1)
        qf = jnp.where(lane2 == DA_VDIM, (-shift).astype(BF16), qs)

        def absorb_shifted(kblk, vtblk):
            acc_ref[...] += _dot(vtblk, jnp.exp2(_dot_nt(kblk, qf)).astype(BF16))

        acc_ref[...] = jnp.zeros(acc_ref.shape, F32)
        absorb_shifted(kc_ref[...], vct_ref[0])
        if with_latent:
            def body(c, carry):
                c0 = pl.multiple_of(c * fast_chunk, fast_chunk)
                absorb_shifted(k_ref[pl.ds(c0, fast_chunk), :], vt_ref[0, :, pl.ds(c0, fast_chunk)])
                return carry

            lax.fori_loop(0, k_ref.shape[0] // fast_chunk, body, 0, unroll=fast_unroll)

    @pl.when(jnp.logical_not(use_bound))
    def _():
        m, acc = absorb(scores(kc_ref[...]), vct_ref[0], jnp.full((1, 2 * tq), NEG_BIG, F32),
                        jnp.zeros((128, 2 * tq), F32))
        if with_latent:
            pk = min(piece, kchunk)
            npieces = kchunk // pk

            def col_max(x):
                if pk > 128:
                    x = jnp.max(x.reshape(pk // 128, 128, 2 * tq), axis=0)
                return jnp.max(x.reshape(x.shape[0] // 8, 8, 2 * tq), axis=0)

            def fused(cur, c, m_, acc_, nxt):
                part = jnp.full((8, 2 * tq), NEG_BIG, F32)
                for t in range(npieces):
                    if nxt is not None:
                        r0 = pl.multiple_of((c + 1) * kchunk + t * pk, pk)
                        s_new = scores(k_ref[pl.ds(r0, pk), :])
                        s_ref[nxt, t * pk:(t + 1) * pk, :] = s_new
                        part = jnp.maximum(part, col_max(s_new))
                    c0 = pl.multiple_of(c * kchunk + t * pk, pk)
                    p = jnp.exp2((s_ref[cur, t * pk:(t + 1) * pk, :] - m_).astype(BF16))
                    acc_ = acc_ + _dot(vt_ref[0, :, pl.ds(c0, pk)], p)
                if nxt is None:
                    return m_, acc_
                m_new = jnp.maximum(m_, jnp.max(part, axis=0, keepdims=True))
                return m_new, acc_ * jnp.exp2(m_ - m_new)

            s0 = scores(latent_k(0))
            s_ref[0] = s0
            m0 = jnp.maximum(m, jnp.max(jnp.max(s0.reshape(kchunk // 128, 128, 2 * tq), axis=0), axis=0,
                                        keepdims=True))
            acc = acc * jnp.exp2(m - m0)
            m = m0

            def pair(c, carry, last):
                m_, acc_ = fused(0, c, *carry, 1)
                return fused(1, c + 1, m_, acc_, None if last else 0)

            if nchunks > 2:
                m, acc = lax.fori_loop(0, nchunks // 2 - 1, lambda j, cr: pair(2 * j, cr, False), (m, acc))
            m, acc = pair(nchunks - 2, (m, acc), True)
        acc_ref[...] = acc

    acc = acc_ref[...]
    a1 = acc[:, :tq]
    a2 = acc[:, tq:]
    o = a1 / a1[DA_VDIM:DA_VDIM + 1, :] - lam_ref[0] * (a2 / a2[DA_VDIM:DA_VDIM + 1, :])
    row = lax.broadcasted_iota(jnp.int32, (128, tq), 0)
    o = jnp.where(row < DA_VDIM, o, 0.0)
    ms = jnp.sum(o * o, axis=0, keepdims=True) * (1.0 / DA_VDIM)
    o_ref[...] = (o * lax.rsqrt(ms + NORM_EPS) * gain_ref[...] * out_scale).T.astype(BF16)


def _diff_attention(lam, q, kc, vc, k, v, gain, batch, seq_q, ctx_len, seq_k, out_scale, q_is_ctx):
    tq = min(512, seq_q)
    nq = seq_q // tq
    with_latent = not q_is_ctx
    if with_latent:
        kchunk = min(2048, seq_k // 2)
        assert seq_k % (2 * kchunk) == 0
    else:
        k, v = kc, vc
        seq_k, kchunk = ctx_len, ctx_len
    fast_chunk = kchunk
    fast_unroll = 2 if (seq_k // fast_chunk) % 2 == 0 else 1
    kern = functools.partial(_attn_kernel, with_latent=with_latent, kchunk=kchunk, piece=512, fast_chunk=fast_chunk,
                             fast_unroll=fast_unroll, out_scale=out_scale)
    return pl.pallas_call(
        kern,
        grid=(batch, DA_HEADS, nq),
        in_specs=[
            pl.BlockSpec(memory_space=pltpu.SMEM),
            pl.BlockSpec((tq, 128), lambda b, h, i: (b * nq + i, h)),
            pl.BlockSpec((ctx_len, 128), lambda b, h, i: (b, h)),
            pl.BlockSpec((1, 128, ctx_len), lambda b, h, i: (b, h, 0)),
            pl.BlockSpec((seq_k, 128), lambda b, h, i: (b, h)),
            pl.BlockSpec((1, 128, seq_k), lambda b, h, i: (b, h, 0)),
            pl.BlockSpec((128, 1), lambda b, h, i: (h, 0)),
        ],
        out_specs=pl.BlockSpec((tq, 128), lambda b, h, i: (b * nq + i, h)),
        out_shape=jax.ShapeDtypeStruct((batch * seq_q, DA_X), BF16),
        scratch_shapes=[pltpu.VMEM((2, kchunk, 2 * tq), F32), pltpu.VMEM((128, 2 * tq), F32),
                        pltpu.VMEM((8, 128), F32)],
        compiler_params=_params("arbitrary", "arbitrary", "arbitrary"),
        name="diff_attention",
    )(lam, q, kc, vc, k, v, gain)


def _outproj_kernel(x_ref, hof_ref, hob_ref, hg_ref, rof_ref, rob_ref, rg_ref, da_ref, seg_ref, w_ref,
                    hn_ref, rn_ref, n1_ref, n2_ref, g1_ref, sc2_ref, sh2_ref, rw_ref, rb_ref,
                    x1_ref, tok_ref, idx_ref, gate_ref, cnt_ref):
    seg = seg_ref[...]

    def gated_head_norm(o, gain, gate):
        ms = _dot((o * o).astype(BF16), seg) * (1.0 / HDIM)
        return (o * lax.rsqrt(ms + NORM_EPS) * gain * (gate * _sigmoid(gate))).astype(BF16)

    a = gated_head_norm(hof_ref[...].astype(F32) + hob_ref[...].astype(F32), hn_ref[...], hg_ref[...].astype(F32))
    b = gated_head_norm(rof_ref[...].astype(F32) + rob_ref[...].astype(F32), rn_ref[...], rg_ref[...].astype(F32))
    y = (_dot(a, w_ref[0:REC_W, :]) + _dot(b, w_ref[REC_W:2 * REC_W, :])
         + _dot(da_ref[...], w_ref[2 * REC_W:2 * REC_W + DA_X, :]))
    x1 = x_ref[...] + g1_ref[0] * _rms(y, n1_ref[...])
    x1_ref[...] = x1
    tok = _rms(x1, n2_ref[...]) * (1.0 + sc2_ref[0]) + sh2_ref[0]
    tok_ref[...] = tok.astype(BF16)
    logits = _dot_nt(rw_ref[...], tok) + rb_ref[...]
    eid = lax.broadcasted_iota(jnp.int32, logits.shape, 0).astype(F32)
    vals, ids = [], []
    for _ in range(TOP_K):
        best = jnp.max(logits, axis=0, keepdims=True)
        first = jnp.min(jnp.where(logits == best, eid, float(N_EXPERTS)), axis=0, keepdims=True)
        vals.append(best)
        ids.append(first)
        logits = jnp.where(eid == first, -jnp.inf, logits)
    ex = [jnp.exp(v - vals[0]) for v in vals]
    total = ex[0] + ex[1] + ex[2] + ex[3]
    idx_ref[...] = jnp.concatenate(ids, axis=0).astype(jnp.int32)
    gate_ref[...] = jnp.concatenate(ex, axis=0) / total
    hits = sum((eid == first).astype(F32) for first in ids)
    cnt_ref[0] = jnp.broadcast_to(jnp.sum(hits, axis=1, keepdims=True), cnt_ref.shape[1:])


def _outproj(xf, seq, streams, seg, w_out, hn, rn, n1, n2, g1, sc2, sh2, rw, rb):
    rows, d = xf.shape
    tm = min(256, seq)
    nb = seq // tm
    row_spec = lambda w: pl.BlockSpec((tm, w), lambda i: (i, 0))
    full = lambda a: pl.BlockSpec(a.shape, lambda i: (0,) * a.ndim)
    mod_spec = pl.BlockSpec((1, 1, d), lambda i: (i // nb, 0, 0))
    hof, hob, hg, rof, rob, rg, da = streams
    return pl.pallas_call(
        _outproj_kernel,
        grid=(rows // tm,),
        in_specs=[row_spec(d)] + [row_spec(REC_W)] * 6 + [row_spec(DA_X), full(seg), full(w_out),
                  full(hn), full(rn), full(n1), full(n2), mod_spec, mod_spec, mod_spec, full(rw), full(rb)],
        out_specs=[row_spec(d), row_spec(d), pl.BlockSpec((TOP_K, tm), lambda i: (0, i)),
                   pl.BlockSpec((TOP_K, tm), lambda i: (0, i)),
                   pl.BlockSpec((1, N_EXPERTS, 128), lambda i: (i, 0, 0))],
        out_shape=[jax.ShapeDtypeStruct((rows, d), F32), jax.ShapeDtypeStruct((rows, d), BF16),
                   jax.ShapeDtypeStruct((TOP_K, rows), jnp.int32), jax.ShapeDtypeStruct((TOP_K, rows), F32),
                   jax.ShapeDtypeStruct((rows // tm, N_EXPERTS, 128), F32)],
        compiler_params=_params("arbitrary"),
        name="outproj",
    )(xf, hof, hob, hg, rof, rob, rg, da, seg, w_out, hn, rn, n1, n2, g1, sc2, sh2, rw, rb)


def _ffn_kernel(be_ref, na_ref, x_ref, w1_ref, w2_ref, b1g_ref, b1l_ref, b2_ref, pe_ref, po_ref, yprev_ref, y_ref,
                w1g_s, w1l_s, w2_s, *, blk0):
    del yprev_ref
    i = pl.program_id(0)
    blk = i + blk0
    active = blk < na_ref[0]
    fresh = jnp.logical_or(i == 0, be_ref[blk] != be_ref[jnp.maximum(blk - 1, 0)])

    @pl.when(fresh)
    def _():
        n = w1_ref.shape[3]
        for c in range(n // 256):
            wb = w1_ref[0, 0, :, 256 * c:256 * (c + 1)].astype(BF16)
            w1g_s[:, 128 * c:128 * (c + 1)] = _dot(wb, pe_ref[...]).astype(BF16)
            w1l_s[:, 128 * c:128 * (c + 1)] = _dot(wb, po_ref[...]).astype(BF16)
        w2_s[...] = w2_ref[0, 0].astype(BF16)

    @pl.when(jnp.logical_not(active))
    def _():
        y_ref[...] = jnp.zeros(y_ref.shape, y_ref.dtype)

    @pl.when(active)
    def _():
        x = x_ref[...]
        glu = jnp.minimum(_dot(x, w1g_s[...]) + b1g_ref[0], SWIGLU_LIMIT)
        lin = jnp.clip(_dot(x, w1l_s[...]) + b1l_ref[0], -SWIGLU_LIMIT, SWIGLU_LIMIT)
        act = glu * _sigmoid(SWIGLU_ALPHA * glu) * (lin + 1.0)
        y_ref[...] = (_dot(act.astype(BF16), w2_s[...]) + b2_ref[0]).astype(y_ref.dtype)


def _expert_ffn(block_expert, n_active, tok, tok_sorted, w1, w2, b1g, b1l, b2, layer, bm, nparts=4):
    p = tok_sorted.shape[0]
    d = tok.shape[1]
    f = w2.shape[2]
    nblocks = p // bm
    src = jnp.arange(256)[:, None]
    dst = jnp.arange(128)[None, :]
    pe = (src == 2 * dst).astype(BF16)
    po = (src == 2 * dst + 1).astype(BF16)
    sel = pl.BlockSpec((256, 128), lambda i, be, na: (0, 0))
    bounds = [nblocks * f // 16 for f in (0, 1, 3, 7, 11, 16)][:nparts + 2]
    nparts = len(bounds) - 1
    yg = jnp.zeros((8, 128), BF16)
    for j in range(nparts):
        b0, nb = bounds[j], bounds[j + 1] - bounds[j]
        xg = tok.at[tok_sorted[b0 * bm:(b0 + nb) * bm]].get(mode="promise_in_bounds")
        wspec = lambda s, b0=b0: pl.BlockSpec((1, 1) + s, lambda i, be, na: (layer, be[i + b0], 0, 0))
        bspec = lambda s, b0=b0: pl.BlockSpec((1,) + s, lambda i, be, na: (be[i + b0], 0, 0))
        yg = pl.pallas_call(
            functools.partial(_ffn_kernel, blk0=b0),
            grid_spec=pltpu.PrefetchScalarGridSpec(
                num_scalar_prefetch=2,
                grid=(nb,),
                in_specs=[pl.BlockSpec((bm, d), lambda i, be, na: (i, 0)),
                          wspec((d, 2 * f)), wspec((f, d)), bspec((1, f)), bspec((1, f)), bspec((1, d)), sel, sel,
                          pl.BlockSpec(memory_space=pl.ANY)],
                out_specs=pl.BlockSpec((bm, d), lambda i, be, na, b0=b0: (i + b0, 0)),
                scratch_shapes=[pltpu.VMEM((d, f), BF16), pltpu.VMEM((d, f), BF16), pltpu.VMEM((f, d), BF16)],
            ),
            out_shape=jax.ShapeDtypeStruct((p, d), BF16),
            input_output_aliases={10: 0} if j else {},
            compiler_params=_params("arbitrary"),
            name="expert_ffn",
        )(block_expert, n_active, xg, w1, w2, b1g, b1l, b2, pe, po, yg)
    return yg


def _resid_kernel(x_ref, y0_ref, y1_ref, y2_ref, y3_ref, gate_ref, n_ref, g_ref, o_ref):
    gates = gate_ref[...]
    f = y0_ref[...].astype(F32) * gates[:, 0:1]
    for k, y_ref in enumerate((y1_ref, y2_ref, y3_ref), start=1):
        f = f + y_ref[...].astype(F32) * gates[:, k:k + 1]
    o_ref[...] = x_ref[...] + g_ref[0] * _rms(f, n_ref[...])


def _ffn_residual(xf, ys, gates, seq, n3, g2):
    rows, d = xf.shape
    tm = min(512, seq)
    nb = seq // tm
    row_spec = pl.BlockSpec((tm, d), lambda i: (i, 0))
    return pl.pallas_call(
        _resid_kernel,
        grid=(rows // tm,),
        in_specs=[row_spec] * 5 + [pl.BlockSpec((tm, TOP_K), lambda i: (i, 0)), pl.BlockSpec((1, d), lambda i: (0, 0)),
                                   pl.BlockSpec((1, 1, d), lambda i: (i // nb, 0, 0))],
        out_specs=row_spec,
        out_shape=jax.ShapeDtypeStruct((rows, d), F32),
        compiler_params=_params("arbitrary"),
        name="ffn_residual",
    )(xf, *ys, gates, n3, g2)


def _rope_tables(pos, dim):
    inv = 1.0 / (ROPE_BASE ** (jnp.arange(0, dim, 2, dtype=F32) / dim))
    ang = pos.astype(F32)[:, None] * inv[None, :]
    return jnp.cos(ang), jnp.sin(ang)


def _rot_cols(w, head_dim, halves):
    d, n = w.shape
    g = head_dim // halves
    w4 = w.reshape(d, n // g, 2, g // 2)
    return jnp.concatenate([-w4[:, :, 1], w4[:, :, 0]], axis=-1).reshape(d, n)


def _prep_w_in(w):
    rq, rk = w[:, _C_RQ:_C_RQ + REC_W], w[:, _C_RK:_C_RK + REC_W]
    dq, dk = w[:, _C_DQ:_C_DQ + DA_W], w[:, _C_DK:_C_DK + DA_W]
    return jnp.concatenate([w, _rot_cols(rq, HDIM, 1), _rot_cols(rk, HDIM, 1),
                            _rot_cols(dq, DA_HEAD_DIM, 2), _rot_cols(dk, DA_HEAD_DIM, 2)], axis=1).astype(BF16)


def _routing(top_idx_t, counts, bm):
    n = top_idx_t.shape[1]
    a = n * TOP_K
    flat_e = top_idx_t.T.reshape(a)
    sorted_e, order = lax.sort_key_val(flat_e, jnp.arange(a, dtype=jnp.int32))
    bounds = jnp.concatenate([jnp.zeros((1,), jnp.int32), jnp.cumsum(counts)])
    start = bounds[:-1]
    padded = (counts + bm - 1) // bm * bm
    pend = jnp.cumsum(padded)
    shift = pend - padded - start
    dest = jnp.arange(a, dtype=jnp.int32) + shift[sorted_e]
    _, pos = lax.sort_key_val(order, dest)
    nblocks = -(-(a + N_EXPERTS * (bm - 1)) // bm)
    block_row = jnp.arange(nblocks, dtype=jnp.int32) * bm
    block_expert = jnp.minimum(jnp.sum((pend[None, :] <= block_row[:, None]).astype(jnp.int32), axis=1),
                               N_EXPERTS - 1)
    row_e = jnp.repeat(block_expert, bm)
    rank = jnp.arange(nblocks * bm, dtype=jnp.int32) - shift[row_e]
    valid = rank < bounds[row_e + 1]
    tok_sorted = jnp.where(valid, order[jnp.clip(rank, 0, a - 1)] // TOP_K, 0)
    n_active = (pend[-1:] // bm).astype(jnp.int32)
    return tok_sorted, pos, block_expert, n_active


def kernel(x, c, ctx, c_ctx, mod_w, mod_b, norm_g, w_in, hgrn_lb, hgrn_norm, ret_decay, ret_norm, da_lambda,
           da_subln, w_out, router_w, router_b, w1, b1, w2, b2):
    B, S, D = x.shape
    C = ctx.shape[1]
    depth = mod_w.shape[0]
    bm = 256

    pos = jnp.arange(S)
    cr, sr = _rope_tables(pos // GRID_W, DA_HEAD_DIM // 2)
    cc, sc_ = _rope_tables(pos % GRID_W, DA_HEAD_DIM // 2)
    cs, ss = _rope_tables(pos, HDIM)
    cosr = jnp.tile(jnp.concatenate([cs, cs], -1), (1, 128 // HDIM))
    sinr = jnp.tile(jnp.concatenate([ss, ss], -1), (1, 128 // HDIM))
    cosd = jnp.tile(jnp.concatenate([cr, cr, cc, cc], -1), (1, 128 // DA_HEAD_DIM))
    sind = jnp.tile(jnp.concatenate([sr, sr, sc_, sc_], -1), (1, 128 // DA_HEAD_DIM))
    ones_r, zeros_r = jnp.ones((C, 128), F32), jnp.zeros((C, 128), F32)
    ones_d, zeros_d = ones_r, zeros_r

    lb_cum = jnp.cumsum(jax.nn.softmax(hgrn_lb.astype(F32), axis=0), axis=0)
    lower = lb_cum - lb_cum[0:1]

    cvec = jnp.zeros((8, D), F32).at[:B].set(c).at[B].set(c_ctx)
    mods = _modulation(cvec, mod_w, mod_b)

    head_id = jnp.arange(REC_W) // HDIM
    seg = (head_id[:, None] == head_id[None, :]).astype(BF16)

    xf = x.reshape(B * S, D)
    xc = ctx.reshape(B * C, D)
    zero_state = jnp.zeros((B, 2, HEADS, HDIM, HDIM), F32)

    for layer in range(depth):
        need_ctx = layer < depth - 1
        lam_init = 0.8 - 0.6 * math.exp(-0.3 * layer)
        m6 = mods[layer].reshape(8, 6, D)
        lat = lambda k: m6[:B, k][:, None, :]
        cxm = lambda k: jnp.broadcast_to(m6[B, k][None, None, :], (B, 1, D))
        ng = norm_g[layer]
        lb = lower[layer]
        lbc = jnp.zeros((8, REC_W), F32).at[0].set(jnp.log(lb)).at[1].set(jnp.log1p(-lb)).at[2].set(1.0 - lb)
        w_all = _prep_w_in(w_in[layer])
        log_gamma = jnp.log1p(-jnp.exp2(-ret_decay[layer].astype(F32)))
        lgrow = jnp.zeros((8, REC_W), F32).at[:2].set(jnp.repeat(log_gamma, HDIM, axis=1))
        lamv = da_lambda[layer].astype(F32)
        lam = (jnp.exp(jnp.sum(lamv[0] * lamv[1])) - jnp.exp(jnp.sum(lamv[2] * lamv[3])) + lam_init).reshape(1)
        hn = jnp.tile(hgrn_norm[layer], HEADS)[None, :]
        rn = jnp.tile(ret_norm[layer], HEADS)[None, :]
        dn = jnp.tile(jnp.concatenate([da_subln[layer], jnp.zeros((128 - DA_VDIM,), F32)]), DA_HEADS)[:, None]
        wo = w_out[layer]
        wo_da = jnp.pad(wo[2 * REC_W:].reshape(DA_HEADS, DA_VDIM, D), ((0, 0), (0, 128 - DA_VDIM), (0, 0)))
        w_out_b = jnp.concatenate([wo[:2 * REC_W], wo_da.reshape(DA_X, D)], axis=0).astype(BF16)
        rw = router_w[layer].T
        rb = router_b[layer][:, None]

        (hq, hv, hg, kf, lff, kb, lfb, rq, rk, rv, rg, dq, dk, dvx) = _inproj(
            xf, S, ng[0:1], lat(1), lat(0), w_all, lbc, cosr, sinr, cosd, sind)
        (hq_c, hv_c, hg_c, kf_c, lff_c, kb_c, lfb_c, rq_c, rk_c, rv_c, rg_c, dq_c, dk_c, dvx_c) = _inproj(
            xc, C, ng[0:1], cxm(1), cxm(0), w_all, lbc, ones_r, zeros_r, ones_d, zeros_d)

        hof_c, hob_c, hs = _hgrn_scan(hq_c, hv_c, kf_c, lff_c, kb_c, lfb_c,
                                      jnp.zeros((B, 2, REC_W // 128, 128, 128), F32), seg, B, C)
        hof, hob, _ = _hgrn_scan(hq, hv, kf, lff, kb, lfb, hs, seg, B, S)
        rof_c, rob_c, rs = _ret_scan(rq_c, rk_c, rv_c, zero_state, log_gamma, lgrow, B, C)
        rof, rob, _ = _ret_scan(rq, rk, rv, rs, log_gamma, lgrow, B, S)
        out_scale = 1.0 - lam_init
        da = _diff_attention(lam, dq, dk_c, dvx_c, dk, dvx, dn, B, S, C, S, out_scale, False)

        x1, tok, top_idx_t, gates_t, cnt = _outproj(xf, S, (hof, hob, hg, rof, rob, rg, da), seg, w_out_b, hn, rn,
                                                    ng[1:2], ng[2:3], lat(2), lat(4), lat(3), rw, rb)
        counts = jnp.sum(cnt[:, :, 0], axis=0)
        if need_ctx:
            da_c = _diff_attention(lam, dq_c, dk_c, dvx_c, None, None, dn, B, C, C, C, out_scale, True)
            xc1, tok_c, top_idx_c, gates_c, cnt_c = _outproj(
                xc, C, (hof_c, hob_c, hg_c, rof_c, rob_c, rg_c, da_c), seg, w_out_b, hn, rn, ng[1:2], ng[2:3],
                cxm(2), cxm(4), cxm(3), rw, rb)
            tok = jnp.concatenate([tok, tok_c], axis=0)
            top_idx_t = jnp.concatenate([top_idx_t, top_idx_c], axis=1)
            gates_t = jnp.concatenate([gates_t, gates_c], axis=1)
            counts = counts + jnp.sum(cnt_c[:, :, 0], axis=0)

        n_tok = tok.shape[0]
        gates = gates_t.T
        tok_sorted, posn, block_expert, n_active = _routing(top_idx_t, counts.astype(jnp.int32), bm)
        b1g = b1[layer][:, None, 0::2]
        b1l = b1[layer][:, None, 1::2]
        b2l = b2[layer][:, None, :]
        yg = _expert_ffn(block_expert, n_active, tok, tok_sorted, w1, w2, b1g, b1l, b2l, layer, bm)
        pos2 = posn.reshape(n_tok, TOP_K)

        def expert_rows(lo, hi):
            return [yg.at[pos2[lo:hi, k]].get(mode="promise_in_bounds") for k in range(TOP_K)]

        xf = _ffn_residual(x1, expert_rows(0, B * S), gates[:B * S], S, ng[3:4], lat(5))
        if need_ctx:
            xc = _ffn_residual(xc1, expert_rows(B * S, n_tok), gates[B * S:], C, ng[3:4], cxm(5))
    return xf.reshape(B, S, D)
```

```python
import functools
import math

import jax
import jax.numpy as jnp
from jax import lax
from jax.experimental import pallas as pl
from jax.experimental.pallas import tpu as pltpu

F32 = jnp.float32
BF16 = jnp.bfloat16

GRID_W = 64
HEADS = 6
HDIM = 64
REC_W = HEADS * HDIM
DA_HEADS = 4
DA_HEAD_DIM = 32
DA_MAPS = 2 * DA_HEADS
DA_W = DA_MAPS * DA_HEAD_DIM
DA_VDIM = 2 * DA_HEAD_DIM
DA_X = DA_HEADS * 128
_Q_SCALE = DA_HEAD_DIM ** -0.5 * math.log2(math.e)
_BOUND_MARGIN = 1.01
_MAX_EXP2_SPAN = 100.0
ROPE_BASE = 10000.0
N_EXPERTS = 32
TOP_K = 4
SWIGLU_ALPHA = 1.702
SWIGLU_LIMIT = 7.0
NORM_EPS = 1e-6

SUB = 16
NEG_BIG = -1e30
VMEM_LIMIT = 56 * 1024 * 1024

_C_HQ, _C_HFF, _C_HFB, _C_HI, _C_HG = 0, 384, 768, 1152, 1536
_C_RQ, _C_RK, _C_RV, _C_RG = 1920, 2304, 2688, 3072
_C_DQ, _C_DK, _C_DV = 3456, 3712, 3968
_C_RQR, _C_RKR, _C_DQR, _C_DKR = 4224, 4608, 4992, 5248
_W_ALL = 5504


def _dot(a, b):
    return jnp.dot(a, b, preferred_element_type=F32)


def _dot_nt(a, b):
    return lax.dot_general(a, b, (((1,), (1,)), ((), ())), preferred_element_type=F32)


def _dot_tn(a, b):
    return lax.dot_general(a, b, (((0,), (0,)), ((), ())), preferred_element_type=F32)


def _sigmoid(x):
    return 1.0 / (1.0 + jnp.exp(-x))


def _rms(x, g):
    ms = jnp.mean(x * x, axis=-1, keepdims=True)
    return x * lax.rsqrt(ms + NORM_EPS) * g


def _params(*sem):
    return pltpu.CompilerParams(dimension_semantics=sem, vmem_limit_bytes=VMEM_LIMIT)


def _mod_kernel(c_ref, w_ref, b_ref, o_ref):
    c = c_ref[...]
    o_ref[0] = _dot(c * _sigmoid(c), w_ref[0]) + b_ref[0]


def _modulation(cvec, mod_w, mod_b):
    depth, d, n = mod_w.shape
    tn = 1536
    return pl.pallas_call(
        _mod_kernel,
        grid=(depth, n // tn),
        in_specs=[
            pl.BlockSpec((8, d), lambda l, j: (0, 0)),
            pl.BlockSpec((1, d, tn), lambda l, j: (l, 0, j)),
            pl.BlockSpec((1, 1, tn), lambda l, j: (l, 0, j)),
        ],
        out_specs=pl.BlockSpec((1, 8, tn), lambda l, j: (l, 0, j)),
        out_shape=jax.ShapeDtypeStruct((depth, 8, n), F32),
        compiler_params=_params("arbitrary", "arbitrary"),
        name="modulation",
    )(cvec, mod_w, mod_b.reshape(depth, 1, n))


def _inproj_kernel(x_ref, g_ref, sc_ref, sh_ref, w_ref, lbc_ref, cosr_ref, sinr_ref, cosd_ref, sind_ref,
                   hq_ref, hv_ref, hg_ref, kf_ref, lff_ref, kb_ref, lfb_ref,
                   rq_ref, rk_ref, rv_ref, rg_ref, dq_ref, dk_ref, dvt_ref):
    x = x_ref[...]
    h = _rms(x, g_ref[...]) * (1.0 + sc_ref[0]) + sh_ref[0]
    hb = h.astype(BF16)

    wide = {}

    def proj(c0, n):
        lo = max(b for b in (0, _C_RQ, _C_DQ) if b <= c0)
        hi = {0: _C_RQ, _C_RQ: _C_DQ, _C_DQ: _W_ALL}[lo]
        if lo not in wide:
            wide[lo] = _dot(hb, w_ref[:, lo:hi])
        return wide[lo][:, c0 - lo:c0 - lo + n]

    hq_ref[...] = proj(_C_HQ, REC_W).astype(BF16)
    hv_ref[...] = proj(_C_HI, REC_W).astype(BF16)
    hg_ref[...] = proj(_C_HG, REC_W).astype(BF16)

    log_lb = lbc_ref[0:1, :]
    log_1m = lbc_ref[1:2, :]
    one_m = lbc_ref[2:3, :]

    def gates(z):
        log_sig = jnp.minimum(z, 0.0) - jnp.log1p(jnp.exp(-jnp.abs(z)))
        t = log_1m + log_sig
        m = jnp.maximum(log_lb, t)
        logf = m + jnp.log1p(jnp.exp(-jnp.abs(log_lb - t)))
        return one_m / (1.0 + jnp.exp(z)), logf

    k, lf = gates(proj(_C_HFF, REC_W))
    kf_ref[...] = k.astype(BF16)
    lff_ref[...] = lf
    k, lf = gates(proj(_C_HFB, REC_W))
    kb_ref[...] = k.astype(BF16)
    lfb_ref[...] = lf

    cosr = jnp.concatenate([cosr_ref[...]] * (REC_W // 128), axis=1)
    sinr = jnp.concatenate([sinr_ref[...]] * (REC_W // 128), axis=1)
    rq_ref[...] = (proj(_C_RQ, REC_W) * cosr + proj(_C_RQR, REC_W) * sinr).astype(BF16)
    rk_ref[...] = ((proj(_C_RK, REC_W) * cosr + proj(_C_RKR, REC_W) * sinr) * (HDIM ** -0.5)).astype(BF16)
    rv_ref[...] = proj(_C_RV, REC_W).astype(BF16)
    rg_ref[...] = proj(_C_RG, REC_W).astype(BF16)

    cosd = jnp.concatenate([cosd_ref[...]] * (DA_W // 128), axis=1)
    sind = jnp.concatenate([sind_ref[...]] * (DA_W // 128), axis=1)
    dq = ((proj(_C_DQ, DA_W) * cosd + proj(_C_DQR, DA_W) * sind) * _Q_SCALE).astype(BF16)
    dk = (proj(_C_DK, DA_W) * cosd + proj(_C_DKR, DA_W) * sind).astype(BF16)
    dv_t = proj(_C_DV, DA_W).T.astype(BF16)
    tm = dq.shape[0]
    zero_col = jnp.zeros((tm, 64), BF16)
    ones_col = jnp.where(lax.broadcasted_iota(jnp.int32, (tm, 64), 1) == 0, 1.0, 0.0).astype(BF16)
    sub = lax.broadcasted_iota(jnp.int32, (64, tm), 0)
    ones_row = jnp.where(sub == 0, 1.0, 0.0).astype(BF16)
    for h_ in range(DA_HEADS):
        lo, mid, hi_ = 128 * h_, 128 * h_ + 64, 128 * h_ + 128
        dq_ref[:, lo:mid] = dq[:, 64 * h_:64 * h_ + 64]
        dq_ref[:, mid:hi_] = zero_col
        dk_ref[:, lo:mid] = dk[:, 64 * h_:64 * h_ + 64]
        dk_ref[:, mid:hi_] = ones_col
        dvt_ref[0, lo:mid, :] = dv_t[64 * h_:64 * h_ + 64, :]
        dvt_ref[0, mid:hi_, :] = ones_row


def _inproj(xf, seq, gnorm, sc, sh, w_all, lbc, cosr, sinr, cosd, sind):
    rows, d = xf.shape
    tm = min(256, seq)
    nb = seq // tm
    row_spec = lambda w: pl.BlockSpec((tm, w), lambda i: (i, 0))
    tab_spec = lambda w: pl.BlockSpec((tm, w), lambda i: (i % nb, 0))
    mod_spec = pl.BlockSpec((1, 1, d), lambda i: (i // nb, 0, 0))
    widths = [REC_W] * 11 + [DA_X, DA_X]
    dtypes = [BF16, BF16, BF16, BF16, F32, BF16, F32, BF16, BF16, BF16, BF16, BF16, BF16]
    out_specs = [row_spec(w) for w in widths] + [pl.BlockSpec((1, DA_X, tm), lambda i: (i // nb, 0, i % nb))]
    out_shape = ([jax.ShapeDtypeStruct((rows, w), dt) for w, dt in zip(widths, dtypes)]
                 + [jax.ShapeDtypeStruct((rows // seq, DA_X, seq), BF16)])
    return pl.pallas_call(
        _inproj_kernel,
        grid=(rows // tm,),
        in_specs=[
            row_spec(d),
            pl.BlockSpec((1, d), lambda i: (0, 0)),
            mod_spec, mod_spec,
            pl.BlockSpec((d, _W_ALL), lambda i: (0, 0)),
            pl.BlockSpec((8, REC_W), lambda i: (0, 0)),
            tab_spec(128), tab_spec(128), tab_spec(128), tab_spec(128),
        ],
        out_specs=out_specs,
        out_shape=out_shape,
        compiler_params=_params("arbitrary"),
        name="inproj",
    )(xf, gnorm, sc, sh, w_all, lbc, cosr, sinr, cosd, sind)


def _hgrn_dir(q, k, v, lf, seg, st_ref, inter_ref, intra_ref, reverse):
    tt = q.shape[0]
    nblk = tt // SUB
    row = lax.broadcasted_iota(jnp.int32, (tt, tt), 0)
    col = lax.broadcasted_iota(jnp.int32, (tt, tt), 1)
    same = (row // SUB) == (col // SUB)
    tri = (col >= row) if reverse else (col <= row)
    l_all = jnp.where(same, 1.0, 0.0)
    l_cum = jnp.where(tri, l_all, 0.0).astype(BF16)
    l_all = l_all.astype(BF16)
    lf = lf * math.log2(math.e)
    hi = lf.astype(BF16)
    r1 = lf - hi.astype(F32)
    mid = r1.astype(BF16)
    lo = (r1 - mid.astype(F32)).astype(BF16)
    b = _dot(l_cum, hi) + _dot(l_cum, mid) + _dot(l_cum, lo)
    btot = _dot(l_all, hi) + _dot(l_all, mid) + _dot(l_all, lo)

    qe = (q * jnp.exp2(b)).astype(BF16)
    kd = (k * jnp.exp2(btot - b)).astype(BF16)
    dec = jnp.exp2(btot)
    v_t = v.T
    tok_blk = lax.broadcasted_iota(jnp.int32, (128, tt), 1) // SUB
    hr = lax.broadcasted_iota(jnp.int32, (128, 128), 0) // HDIM
    hc = lax.broadcasted_iota(jnp.int32, (128, 128), 1) // HDIM
    same_head = hr == hc
    order = range(nblk - 1, -1, -1) if reverse else range(nblk)
    half = lax.broadcasted_iota(jnp.int32, (nblk, 8, 128), 1)
    seg2 = seg[0:128, 0:128]

    def carried(c, g):
        v_tg = v_t[c, :]
        quarter = max(nblk // 4, 1)
        upd = []
        for j0 in range(0, nblk, quarter):
            v_stack = jnp.concatenate([jnp.where(tok_blk == j, v_tg, 0.0).astype(BF16)
                                       for j in range(j0, j0 + quarter)], axis=0)
            part = _dot(v_stack, kd[:, c])
            upd += [part[128 * j:128 * (j + 1)] for j in range(quarter)]
        st = st_ref[g]
        for j in order:
            r0 = j * SUB
            inter_ref[r0:r0 + SUB, c] = _dot_nt(qe[r0:r0 + SUB, c], st.astype(BF16))
            st = st * dec[r0:r0 + 1, c] + jnp.where(same_head, upd[j], 0.0)
        st_ref[g] = st

    def within(c):
        def halves(a):
            a4 = a[:, c].reshape(nblk, 2, 8, 128)
            return (a4[:, 1], a4[:, 0]) if reverse else (a4[:, 0], a4[:, 1])

        (q0, q1), (k0, k1), (v0, v1), (b0, b1) = halves(q), halves(k), halves(v), halves(b)

        def rotated(r, *arrays):
            sh = ((8 - r) % 8) if reverse else r
            return [pltpu.roll(a, sh, 1) if sh else a for a in arrays]

        def pairs(qa, ba, kbv, r, masked):
            kb_, bb_, vb_ = kbv
            diff = ba - bb_
            if masked and r:
                diff = jnp.where((half <= 7 - r) if reverse else (half >= r), diff, NEG_BIG)
            term = (qa * kb_ * jnp.exp2(diff)).reshape(nblk * 8, 128).astype(BF16)
            return _dot(term, seg2).reshape(nblk, 8, 128) * vb_

        o0 = o1 = None
        for r in range(8):
            early, late = rotated(r, k0, b0, v0), rotated(r, k1, b1, v1)
            t0 = pairs(q0, b0, early, r, True)
            t1 = pairs(q1, b1, late, r, True) + pairs(q1, b1, early, r, False)
            o0, o1 = (t0, t1) if r == 0 else (o0 + t0, o1 + t1)
        first, second = (o1, o0) if reverse else (o0, o1)
        intra_ref[:, 0, :, c] = first
        intra_ref[:, 1, :, c] = second

    for g in range(REC_W // 128):
        c = slice(128 * g, 128 * (g + 1))
        carried(c, g)
        within(c)
    return intra_ref[...].reshape(tt, REC_W) + inter_ref[...]


def _hgrn_kernel(qf_ref, vf_ref, kf_ref, lff_ref, qb_ref, vb_ref, kb_ref, lfb_ref, s0_ref, seg_ref,
                 of_ref, ob_ref, sout_ref, st_ref, inter_ref, intra_ref):
    i = pl.program_id(1)

    @pl.when(i == 0)
    def _():
        st_ref[...] = s0_ref[0]

    seg = seg_ref[...]
    of_ref[...] = _hgrn_dir(qf_ref[...].astype(F32), kf_ref[...].astype(F32), vf_ref[...].astype(F32),
                            lff_ref[...], seg, st_ref.at[0], inter_ref, intra_ref, False).astype(BF16)
    ob_ref[...] = _hgrn_dir(qb_ref[...].astype(F32), kb_ref[...].astype(F32), vb_ref[...].astype(F32),
                            lfb_ref[...], seg, st_ref.at[1], inter_ref, intra_ref, True).astype(BF16)

    @pl.when(i == pl.num_programs(1) - 1)
    def _():
        sout_ref[0] = st_ref[...]


def _hgrn_scan(hq, hv, kf, lff, kb, lfb, s0, seg, batch, seq):
    tt = min(256, seq)
    n = seq // tt
    fwd = pl.BlockSpec((tt, REC_W), lambda b, i: (b * n + i, 0))
    bwd = pl.BlockSpec((tt, REC_W), lambda b, i: (b * n + n - 1 - i, 0))
    st_spec = pl.BlockSpec((1, 2, REC_W // 128, 128, 128), lambda b, i: (b, 0, 0, 0, 0))
    rows = batch * seq
    return pl.pallas_call(
        _hgrn_kernel,
        grid=(batch, n),
        in_specs=[fwd, fwd, fwd, fwd, bwd, bwd, bwd, bwd, st_spec,
                  pl.BlockSpec((REC_W, REC_W), lambda b, i: (0, 0))],
        out_specs=[fwd, bwd, st_spec],
        out_shape=[jax.ShapeDtypeStruct((rows, REC_W), BF16), jax.ShapeDtypeStruct((rows, REC_W), BF16),
                   jax.ShapeDtypeStruct(s0.shape, F32)],
        scratch_shapes=[pltpu.VMEM((2, REC_W // 128, 128, 128), F32), pltpu.VMEM((tt, REC_W), F32),
                        pltpu.VMEM((tt // SUB, 2, 8, REC_W), F32)],
        compiler_params=_params("arbitrary", "arbitrary"),
        name="hgrn_scan",
    )(hq, hv, kf, lff, hq, hv, kb, lfb, s0, seg)


def _ret_kernel(lg_ref, qf_ref, kf_ref, vf_ref, qb_ref, kb_ref, vb_ref, s0_ref, lgrow_ref,
                of_ref, ob_ref, sout_ref, st_ref, dmask_ref, tab_ref, o_scr):
    i = pl.program_id(1)
    tt = qf_ref.shape[0]

    @pl.when(i == 0)
    def _():
        st_ref[...] = s0_ref[0]
        row = lax.broadcasted_iota(jnp.int32, (tt, tt), 0)
        col = lax.broadcasted_iota(jnp.int32, (tt, tt), 1)
        dist = (row - col).astype(F32)
        for h in range(HEADS):
            fw = jnp.where(dist >= 0, jnp.exp(dist * lg_ref[0, h]), 0.0)
            bw = jnp.where(dist <= 0, jnp.exp(-dist * lg_ref[1, h]), 0.0)
            dmask_ref[h] = fw + bw
        pos = lax.broadcasted_iota(jnp.int32, (tt, REC_W), 0).astype(F32)
        lgf = lgrow_ref[0:1, :]
        lgb = lgrow_ref[1:2, :]
        tab_ref[0] = jnp.exp((pos + 1.0) * lgf)
        tab_ref[1] = jnp.exp((tt - 1.0 - pos) * lgf)
        tab_ref[2] = jnp.exp((tt - pos) * lgb)
        tab_ref[3] = jnp.exp(pos * lgb)

    tile_f = jnp.exp(tt * lgrow_ref[0:1, :])
    tile_b = jnp.exp(tt * lgrow_ref[1:2, :])

    q = qf_ref[...]
    k = kf_ref[...]
    v = vf_ref[...]
    qe = (q.astype(F32) * tab_ref[0]).astype(BF16)
    ke = (k.astype(F32) * tab_ref[1]).astype(BF16)
    for h in range(HEADS):
        c = slice(h * HDIM, (h + 1) * HDIM)
        s = _dot_nt(q[:, c], k[:, c]) * dmask_ref[h]
        st = st_ref[0, h]
        o_scr[:, c] = _dot(s.astype(BF16), v[:, c]) + _dot(qe[:, c], st.astype(BF16))
        st_ref[0, h] = st * tile_f[:, c] + _dot_tn(ke[:, c], v[:, c])
    of_ref[...] = o_scr[...].astype(BF16)

    q = qb_ref[...]
    k = kb_ref[...]
    v = vb_ref[...]
    qe = (q.astype(F32) * tab_ref[2]).astype(BF16)
    ke = (k.astype(F32) * tab_ref[3]).astype(BF16)
    for h in range(HEADS):
        c = slice(h * HDIM, (h + 1) * HDIM)
        st = st_ref[1, h]
        o_scr[:, c] = _dot(qe[:, c], st.astype(BF16))
        st_ref[1, h] = st * tile_b[:, c] + _dot_tn(ke[:, c], v[:, c])
    ob_ref[...] = o_scr[...].astype(BF16)

    @pl.when(i == pl.num_programs(1) - 1)
    def _():
        sout_ref[0] = st_ref[...]


def _ret_scan(rq, rk, rv, s0, lg, lgrow, batch, seq):
    tt = min(256, seq)
    n = seq // tt
    fwd = pl.BlockSpec((tt, REC_W), lambda b, i: (b * n + i, 0))
    bwd = pl.BlockSpec((tt, REC_W), lambda b, i: (b * n + n - 1 - i, 0))
    st_spec = pl.BlockSpec((1, 2, HEADS, HDIM, HDIM), lambda b, i: (b, 0, 0, 0, 0))
    rows = batch * seq
    return pl.pallas_call(
        _ret_kernel,
        grid=(batch, n),
        in_specs=[pl.BlockSpec(memory_space=pltpu.SMEM), fwd, fwd, fwd, bwd, bwd, bwd, st_spec,
                  pl.BlockSpec((8, REC_W), lambda b, i: (0, 0))],
        out_specs=[fwd, bwd, st_spec],
        out_shape=[jax.ShapeDtypeStruct((rows, REC_W), BF16), jax.ShapeDtypeStruct((rows, REC_W), BF16),
                   jax.ShapeDtypeStruct(s0.shape, F32)],
        scratch_shapes=[pltpu.VMEM((2, HEADS, HDIM, HDIM), F32), pltpu.VMEM((HEADS, tt, tt), F32),
                        pltpu.VMEM((4, tt, REC_W), F32), pltpu.VMEM((tt, REC_W), F32)],
        compiler_params=_params("arbitrary", "arbitrary"),
        name="ret_scan",
    )(lg, rq, rk, rv, rq, rk, rv, s0, lgrow)


def _attn_kernel(lam_ref, q_ref, kc_ref, vct_ref, k_ref, vt_ref, gain_ref, o_ref, s_ref, acc_ref, kmax_ref, *,
                 with_latent, kchunk, piece, fast_chunk, fast_unroll, out_scale):
    tq = q_ref.shape[0]
    q = q_ref[...]
    lane = lax.broadcasted_iota(jnp.int32, (tq, 128), 1)
    zero = jnp.zeros_like(q)
    qs = jnp.concatenate([jnp.where(lane < DA_HEAD_DIM, q, zero),
                          jnp.where(lane >= DA_HEAD_DIM, q, zero)], axis=0)
    nchunks = (k_ref.shape[0] // kchunk) if with_latent else 0

    def scores(kblk):
        return _dot_nt(kblk, qs)

    def absorb(s, vtblk, m, acc):
        n = s.shape[0]
        part = jnp.max(s.reshape(n // 128, 128, 2 * tq), axis=0) if n > 128 else s
        m_new = jnp.maximum(m, jnp.max(part, axis=0, keepdims=True))
        p = jnp.exp2((s - m_new).astype(BF16))
        return m_new, jnp.exp2(m - m_new) * acc + _dot(vtblk, p)

    def latent_k(c):
        return k_ref[pl.ds(pl.multiple_of(c * kchunk, kchunk), kchunk), :]

    def latent_vt(c):
        return vt_ref[0, :, pl.ds(pl.multiple_of(c * kchunk, kchunk), kchunk)]

    sel_r = lax.broadcasted_iota(jnp.int32, (128, 128), 0) // DA_HEAD_DIM
    sel_c = lax.broadcasted_iota(jnp.int32, (128, 128), 1)
    sel = jnp.where(sel_r == sel_c, 1.0, 0.0).astype(BF16)

    def key_norm2(kblk):
        kf = kblk.astype(F32)
        return jnp.max(_dot((kf * kf).astype(BF16), sel), axis=0, keepdims=True)

    @pl.when(pl.program_id(2) == 0)
    def _():
        km = key_norm2(kc_ref[...])
        if with_latent:
            km = lax.fori_loop(0, nchunks, lambda c, a: jnp.maximum(a, key_norm2(latent_k(c))), km)
        kmax_ref[...] = jnp.broadcast_to(km, kmax_ref.shape)

    qf32 = q.astype(F32)
    qn2 = _dot((qf32 * qf32).astype(BF16), sel)
    bound = jnp.sqrt(qn2 * kmax_ref[0:1, :]) * _BOUND_MARGIN
    shift = jnp.concatenate([bound[:, 0:1], bound[:, 1:2]], axis=0)
    use_bound = 2.0 * jnp.max(shift) <= _MAX_EXP2_SPAN

    @pl.when(use_bound)
    def _():
        lane2 = lax.broadcasted_iota(jnp.int32, (2 * tq, 128), 1)
        qf = jnp.where(lane2 == DA_VDIM, (-shift).astype(BF16), qs)

        def absorb_shifted(kblk, vtblk):
            acc_ref[...] += _dot(vtblk, jnp.exp2(_dot_nt(kblk, qf)).astype(BF16))

        acc_ref[...] = jnp.zeros(acc_ref.shape, F32)
        absorb_shifted(kc_ref[...], vct_ref[0])
        if with_latent:
            def body(c, carry):
                c0 = pl.multiple_of(c * fast_chunk, fast_chunk)
                absorb_shifted(k_ref[pl.ds(c0, fast_chunk), :], vt_ref[0, :, pl.ds(c0, fast_chunk)])
                return carry

            lax.fori_loop(0, k_ref.shape[0] // fast_chunk, body, 0, unroll=fast_unroll)

    @pl.when(jnp.logical_not(use_bound))
    def _():
        m, acc = absorb(scores(kc_ref[...]), vct_ref[0], jnp.full((1, 2 * tq), NEG_BIG, F32),
                        jnp.zeros((128, 2 * tq), F32))
        if with_latent:
            pk = min(piece, kchunk)
            npieces = kchunk // pk

            def col_max(x):
                if pk > 128:
                    x = jnp.max(x.reshape(pk // 128, 128, 2 * tq), axis=0)
                return jnp.max(x.reshape(x.shape[0] // 8, 8, 2 * tq), axis=0)

            def fused(cur, c, m_, acc_, nxt):
                part = jnp.full((8, 2 * tq), NEG_BIG, F32)
                for t in range(npieces):
                    if nxt is not None:
                        r0 = pl.multiple_of((c + 1) * kchunk + t * pk, pk)
                        s_new = scores(k_ref[pl.ds(r0, pk), :])
                        s_ref[nxt, t * pk:(t + 1) * pk, :] = s_new
                        part = jnp.maximum(part, col_max(s_new))
                    c0 = pl.multiple_of(c * kchunk + t * pk, pk)
                    p = jnp.exp2((s_ref[cur, t * pk:(t + 1) * pk, :] - m_).astype(BF16))
                    acc_ = acc_ + _dot(vt_ref[0, :, pl.ds(c0, pk)], p)
                if nxt is None:
                    return m_, acc_
                m_new = jnp.maximum(m_, jnp.max(part, axis=0, keepdims=True))
                return m_new, acc_ * jnp.exp2(m_ - m_new)

            s0 = scores(latent_k(0))
            s_ref[0] = s0
            m0 = jnp.maximum(m, jnp.max(jnp.max(s0.reshape(kchunk // 128, 128, 2 * tq), axis=0), axis=0,
                                        keepdims=True))
            acc = acc * jnp.exp2(m - m0)
            m = m0

            def pair(c, carry, last):
                m_, acc_ = fused(0, c, *carry, 1)
                return fused(1, c + 1, m_, acc_, None if last else 0)

            if nchunks > 2:
                m, acc = lax.fori_loop(0, nchunks // 2 - 1, lambda j, cr: pair(2 * j, cr, False), (m, acc))
            m, acc = pair(nchunks - 2, (m, acc), True)
        acc_ref[...] = acc

    acc = acc_ref[...]
    a1 = acc[:, :tq]
    a2 = acc[:, tq:]
    o = a1 / a1[DA_VDIM:DA_VDIM + 1, :] - lam_ref[0] * (a2 / a2[DA_VDIM:DA_VDIM + 1, :])
    row = lax.broadcasted_iota(jnp.int32, (128, tq), 0)
    o = jnp.where(row < DA_VDIM, o, 0.0)
    ms = jnp.sum(o * o, axis=0, keepdims=True) * (1.0 / DA_VDIM)
    o_ref[...] = (o * lax.rsqrt(ms + NORM_EPS) * gain_ref[...] * out_scale).T.astype(BF16)


def _diff_attention(lam, q, kc, vc, k, v, gain, batch, seq_q, ctx_len, seq_k, out_scale, q_is_ctx):
    tq = min(512, seq_q)
    nq = seq_q // tq
    with_latent = not q_is_ctx
    if with_latent:
        kchunk = min(2048, seq_k // 2)
        assert seq_k % (2 * kchunk) == 0
    else:
        k, v = kc, vc
        seq_k, kchunk = ctx_len, ctx_len
    fast_chunk = kchunk
    fast_unroll = 2 if (seq_k // fast_chunk) % 2 == 0 else 1
    kern = functools.partial(_attn_kernel, with_latent=with_latent, kchunk=kchunk, piece=512, fast_chunk=fast_chunk,
                             fast_unroll=fast_unroll, out_scale=out_scale)
    return pl.pallas_call(
        kern,
        grid=(batch, DA_HEADS, nq),
        in_specs=[
            pl.BlockSpec(memory_space=pltpu.SMEM),
            pl.BlockSpec((tq, 128), lambda b, h, i: (b * nq + i, h)),
            pl.BlockSpec((ctx_len, 128), lambda b, h, i: (b, h)),
            pl.BlockSpec((1, 128, ctx_len), lambda b, h, i: (b, h, 0)),
            pl.BlockSpec((seq_k, 128), lambda b, h, i: (b, h)),
            pl.BlockSpec((1, 128, seq_k), lambda b, h, i: (b, h, 0)),
            pl.BlockSpec((128, 1), lambda b, h, i: (h, 0)),
        ],
        out_specs=pl.BlockSpec((tq, 128), lambda b, h, i: (b * nq + i, h)),
        out_shape=jax.ShapeDtypeStruct((batch * seq_q, DA_X), BF16),
        scratch_shapes=[pltpu.VMEM((2, kchunk, 2 * tq), F32), pltpu.VMEM((128, 2 * tq), F32),
                        pltpu.VMEM((8, 128), F32)],
        compiler_params=_params("arbitrary", "arbitrary", "arbitrary"),
        name="diff_attention",
    )(lam, q, kc, vc, k, v, gain)


def _outproj_kernel(x_ref, hof_ref, hob_ref, hg_ref, rof_ref, rob_ref, rg_ref, da_ref, seg_ref, w_ref,
                    hn_ref, rn_ref, n1_ref, n2_ref, g1_ref, sc2_ref, sh2_ref, rw_ref, rb_ref,
                    x1_ref, tok_ref, idx_ref, gate_ref, cnt_ref):
    seg = seg_ref[...]

    def gated_head_norm(o, gain, gate):
        ms = _dot((o * o).astype(BF16), seg) * (1.0 / HDIM)
        return (o * lax.rsqrt(ms + NORM_EPS) * gain * (gate * _sigmoid(gate))).astype(BF16)

    a = gated_head_norm(hof_ref[...].astype(F32) + hob_ref[...].astype(F32), hn_ref[...], hg_ref[...].astype(F32))
    b = gated_head_norm(rof_ref[...].astype(F32) + rob_ref[...].astype(F32), rn_ref[...], rg_ref[...].astype(F32))
    y = (_dot(a, w_ref[0:REC_W, :]) + _dot(b, w_ref[REC_W:2 * REC_W, :])
         + _dot(da_ref[...], w_ref[2 * REC_W:2 * REC_W + DA_X, :]))
    x1 = x_ref[...] + g1_ref[0] * _rms(y, n1_ref[...])
    x1_ref[...] = x1
    tok = _rms(x1, n2_ref[...]) * (1.0 + sc2_ref[0]) + sh2_ref[0]
    tok_ref[...] = tok.astype(BF16)
    logits = _dot_nt(rw_ref[...], tok) + rb_ref[...]
    eid = lax.broadcasted_iota(jnp.int32, logits.shape, 0).astype(F32)
    vals, ids = [], []
    for _ in range(TOP_K):
        best = jnp.max(logits, axis=0, keepdims=True)
        first = jnp.min(jnp.where(logits == best, eid, float(N_EXPERTS)), axis=0, keepdims=True)
        vals.append(best)
        ids.append(first)
        logits = jnp.where(eid == first, -jnp.inf, logits)
    ex = [jnp.exp(v - vals[0]) for v in vals]
    total = ex[0] + ex[1] + ex[2] + ex[3]
    idx_ref[...] = jnp.concatenate(ids, axis=0).astype(jnp.int32)
    gate_ref[...] = jnp.concatenate(ex, axis=0) / total
    hits = sum((eid == first).astype(F32) for first in ids)
    cnt_ref[0] = jnp.broadcast_to(jnp.sum(hits, axis=1, keepdims=True), cnt_ref.shape[1:])


def _outproj(xf, seq, streams, seg, w_out, hn, rn, n1, n2, g1, sc2, sh2, rw, rb):
    rows, d = xf.shape
    tm = min(256, seq)
    nb = seq // tm
    row_spec = lambda w: pl.BlockSpec((tm, w), lambda i: (i, 0))
    full = lambda a: pl.BlockSpec(a.shape, lambda i: (0,) * a.ndim)
    mod_spec = pl.BlockSpec((1, 1, d), lambda i: (i // nb, 0, 0))
    hof, hob, hg, rof, rob, rg, da = streams
    return pl.pallas_call(
        _outproj_kernel,
        grid=(rows // tm,),
        in_specs=[row_spec(d)] + [row_spec(REC_W)] * 6 + [row_spec(DA_X), full(seg), full(w_out),
                  full(hn), full(rn), full(n1), full(n2), mod_spec, mod_spec, mod_spec, full(rw), full(rb)],
        out_specs=[row_spec(d), row_spec(d), pl.BlockSpec((TOP_K, tm), lambda i: (0, i)),
                   pl.BlockSpec((TOP_K, tm), lambda i: (0, i)),
                   pl.BlockSpec((1, N_EXPERTS, 128), lambda i: (i, 0, 0))],
        out_shape=[jax.ShapeDtypeStruct((rows, d), F32), jax.ShapeDtypeStruct((rows, d), BF16),
                   jax.ShapeDtypeStruct((TOP_K, rows), jnp.int32), jax.ShapeDtypeStruct((TOP_K, rows), F32),
                   jax.ShapeDtypeStruct((rows // tm, N_EXPERTS, 128), F32)],
        compiler_params=_params("arbitrary"),
        name="outproj",
    )(xf, hof, hob, hg, rof, rob, rg, da, seg, w_out, hn, rn, n1, n2, g1, sc2, sh2, rw, rb)


def _ffn_kernel(be_ref, na_ref, x_ref, w1_ref, w2_ref, b1g_ref, b1l_ref, b2_ref, pe_ref, po_ref, yprev_ref, y_ref,
                w1g_s, w1l_s, w2_s, *, blk0):
    del yprev_ref
    i = pl.program_id(0)
    blk = i + blk0
    active = blk < na_ref[0]
    fresh = jnp.logical_or(i == 0, be_ref[blk] != be_ref[jnp.maximum(blk - 1, 0)])

    @pl.when(fresh)
    def _():
        n = w1_ref.shape[3]
        for c in range(n // 256):
            wb = w1_ref[0, 0, :, 256 * c:256 * (c + 1)].astype(BF16)
            w1g_s[:, 128 * c:128 * (c + 1)] = _dot(wb, pe_ref[...]).astype(BF16)
            w1l_s[:, 128 * c:128 * (c + 1)] = _dot(wb, po_ref[...]).astype(BF16)
        w2_s[...] = w2_ref[0, 0].astype(BF16)

    @pl.when(jnp.logical_not(active))
    def _():
        y_ref[...] = jnp.zeros(y_ref.shape, y_ref.dtype)

    @pl.when(active)
    def _():
        x = x_ref[...]
        glu = jnp.minimum(_dot(x, w1g_s[...]) + b1g_ref[0], SWIGLU_LIMIT)
        lin = jnp.clip(_dot(x, w1l_s[...]) + b1l_ref[0], -SWIGLU_LIMIT, SWIGLU_LIMIT)
        act = glu * _sigmoid(SWIGLU_ALPHA * glu) * (lin + 1.0)
        y_ref[...] = (_dot(act.astype(BF16), w2_s[...]) + b2_ref[0]).astype(y_ref.dtype)


def _expert_ffn(block_expert, n_active, tok, tok_sorted, w1, w2, b1g, b1l, b2, layer, bm, nparts=4):
    p = tok_sorted.shape[0]
    d = tok.shape[1]
    f = w2.shape[2]
    nblocks = p // bm
    src = jnp.arange(256)[:, None]
    dst = jnp.arange(128)[None, :]
    pe = (src == 2 * dst).astype(BF16)
    po = (src == 2 * dst + 1).astype(BF16)
    sel = pl.BlockSpec((256, 128), lambda i, be, na: (0, 0))
    bounds = [nblocks * f // 16 for f in (0, 1, 3, 7, 11, 16)][:nparts + 2]
    nparts = len(bounds) - 1
    yg = jnp.zeros((8, 128), BF16)
    for j in range(nparts):
        b0, nb = bounds[j], bounds[j + 1] - bounds[j]
        xg = tok.at[tok_sorted[b0 * bm:(b0 + nb) * bm]].get(mode="promise_in_bounds")
        wspec = lambda s, b0=b0: pl.BlockSpec((1, 1) + s, lambda i, be, na: (layer, be[i + b0], 0, 0))
        bspec = lambda s, b0=b0: pl.BlockSpec((1,) + s, lambda i, be, na: (be[i + b0], 0, 0))
        yg = pl.pallas_call(
            functools.partial(_ffn_kernel, blk0=b0),
            grid_spec=pltpu.PrefetchScalarGridSpec(
                num_scalar_prefetch=2,
                grid=(nb,),
                in_specs=[pl.BlockSpec((bm, d), lambda i, be, na: (i, 0)),
                          wspec((d, 2 * f)), wspec((f, d)), bspec((1, f)), bspec((1, f)), bspec((1, d)), sel, sel,
                          pl.BlockSpec(memory_space=pl.ANY)],
                out_specs=pl.BlockSpec((bm, d), lambda i, be, na, b0=b0: (i + b0, 0)),
                scratch_shapes=[pltpu.VMEM((d, f), BF16), pltpu.VMEM((d, f), BF16), pltpu.VMEM((f, d), BF16)],
            ),
            out_shape=jax.ShapeDtypeStruct((p, d), BF16),
            input_output_aliases={10: 0} if j else {},
            compiler_params=_params("arbitrary"),
            name="expert_ffn",
        )(block_expert, n_active, xg, w1, w2, b1g, b1l, b2, pe, po, yg)
    return yg


def _resid_kernel(x_ref, y0_ref, y1_ref, y2_ref, y3_ref, gate_ref, n_ref, g_ref, o_ref):
    gates = gate_ref[...]
    f = y0_ref[...].astype(F32) * gates[:, 0:1]
    for k, y_ref in enumerate((y1_ref, y2_ref, y3_ref), start=1):
        f = f + y_ref[...].astype(F32) * gates[:, k:k + 1]
    o_ref[...] = x_ref[...] + g_ref[0] * _rms(f, n_ref[...])


def _ffn_residual(xf, ys, gates, seq, n3, g2):
    rows, d = xf.shape
    tm = min(512, seq)
    nb = seq // tm
    row_spec = pl.BlockSpec((tm, d), lambda i: (i, 0))
    return pl.pallas_call(
        _resid_kernel,
        grid=(rows // tm,),
        in_specs=[row_spec] * 5 + [pl.BlockSpec((tm, TOP_K), lambda i: (i, 0)), pl.BlockSpec((1, d), lambda i: (0, 0)),
                                   pl.BlockSpec((1, 1, d), lambda i: (i // nb, 0, 0))],
        out_specs=row_spec,
        out_shape=jax.ShapeDtypeStruct((rows, d), F32),
        compiler_params=_params("arbitrary"),
        name="ffn_residual",
    )(xf, *ys, gates, n3, g2)


def _rope_tables(pos, dim):
    inv = 1.0 / (ROPE_BASE ** (jnp.arange(0, dim, 2, dtype=F32) / dim))
    ang = pos.astype(F32)[:, None] * inv[None, :]
    return jnp.cos(ang), jnp.sin(ang)


def _rot_cols(w, head_dim, halves):
    d, n = w.shape
    g = head_dim // halves
    w4 = w.reshape(d, n // g, 2, g // 2)
    return jnp.concatenate([-w4[:, :, 1], w4[:, :, 0]], axis=-1).reshape(d, n)


def _prep_w_in(w):
    rq, rk = w[:, _C_RQ:_C_RQ + REC_W], w[:, _C_RK:_C_RK + REC_W]
    dq, dk = w[:, _C_DQ:_C_DQ + DA_W], w[:, _C_DK:_C_DK + DA_W]
    return jnp.concatenate([w, _rot_cols(rq, HDIM, 1), _rot_cols(rk, HDIM, 1),
                            _rot_cols(dq, DA_HEAD_DIM, 2), _rot_cols(dk, DA_HEAD_DIM, 2)], axis=1).astype(BF16)


def _assign_kernel(ids_ref, first_ref, upper_ref, pos_ref, seen_ref):
    @pl.when(pl.program_id(0) == 0)
    def _():
        seen_ref[...] = jnp.zeros(seen_ref.shape, F32)

    ids = ids_ref[...].astype(F32)
    tm = ids.shape[1]
    eid = lax.broadcasted_iota(jnp.int32, (N_EXPERTS, tm), 0).astype(F32)
    hit = [eid == ids[k:k + 1, :] for k in range(TOP_K)]
    hits = sum(h.astype(F32) for h in hit)
    earlier = _dot(hits.astype(BF16), upper_ref[...])
    base = earlier + seen_ref[:, 0:1] + first_ref[...]
    pos_ref[...] = jnp.concatenate([jnp.sum(jnp.where(h, base, 0.0), axis=0, keepdims=True) for h in hit],
                                   axis=0).astype(jnp.int32)
    seen_ref[...] += jnp.broadcast_to(jnp.sum(hits, axis=1, keepdims=True), seen_ref.shape)


def _assign_rows(top_idx_t, first_row):
    n = top_idx_t.shape[1]
    tm = 512 if n % 512 == 0 else 128
    upper = (jnp.arange(tm)[:, None] < jnp.arange(tm)[None, :]).astype(BF16)
    return pl.pallas_call(
        _assign_kernel,
        grid=(n // tm,),
        in_specs=[pl.BlockSpec((TOP_K, tm), lambda i: (0, i)),
                  pl.BlockSpec((N_EXPERTS, 1), lambda i: (0, 0)),
                  pl.BlockSpec((tm, tm), lambda i: (0, 0))],
        out_specs=pl.BlockSpec((TOP_K, tm), lambda i: (0, i)),
        out_shape=jax.ShapeDtypeStruct((TOP_K, n), jnp.int32),
        scratch_shapes=[pltpu.VMEM((N_EXPERTS, 128), F32)],
        compiler_params=_params("arbitrary"),
        name="assign_rows",
    )(top_idx_t, first_row.astype(F32)[:, None], upper)


def _routing(top_idx_t, counts, bm):
    n = top_idx_t.shape[1]
    a = n * TOP_K
    flat_e = top_idx_t.T.reshape(a)
    _, order = lax.sort_key_val(flat_e, jnp.arange(a, dtype=jnp.int32))
    bounds = jnp.concatenate([jnp.zeros((1,), jnp.int32), jnp.cumsum(counts)])
    start = bounds[:-1]
    padded = (counts + bm - 1) // bm * bm
    pend = jnp.cumsum(padded)
    shift = pend - padded - start
    pos = _assign_rows(top_idx_t, pend - padded)
    nblocks = -(-(a + N_EXPERTS * (bm - 1)) // bm)
    block_row = jnp.arange(nblocks, dtype=jnp.int32) * bm
    block_expert = jnp.minimum(jnp.sum((pend[None, :] <= block_row[:, None]).astype(jnp.int32), axis=1),
                               N_EXPERTS - 1)
    row_e = jnp.repeat(block_expert, bm)
    rank = jnp.arange(nblocks * bm, dtype=jnp.int32) - shift[row_e]
    valid = rank < bounds[row_e + 1]
    tok_sorted = jnp.where(valid, order[jnp.clip(rank, 0, a - 1)] // TOP_K, 0)
    n_active = (pend[-1:] // bm).astype(jnp.int32)
    return tok_sorted, pos, block_expert, n_active


def kernel(x, c, ctx, c_ctx, mod_w, mod_b, norm_g, w_in, hgrn_lb, hgrn_norm, ret_decay, ret_norm, da_lambda,
           da_subln, w_out, router_w, router_b, w1, b1, w2, b2):
    B, S, D = x.shape
    C = ctx.shape[1]
    depth = mod_w.shape[0]
    bm = 256

    pos = jnp.arange(S)
    cr, sr = _rope_tables(pos // GRID_W, DA_HEAD_DIM // 2)
    cc, sc_ = _rope_tables(pos % GRID_W, DA_HEAD_DIM // 2)
    cs, ss = _rope_tables(pos, HDIM)
    cosr = jnp.tile(jnp.concatenate([cs, cs], -1), (1, 128 // HDIM))
    sinr = jnp.tile(jnp.concatenate([ss, ss], -1), (1, 128 // HDIM))
    cosd = jnp.tile(jnp.concatenate([cr, cr, cc, cc], -1), (1, 128 // DA_HEAD_DIM))
    sind = jnp.tile(jnp.concatenate([sr, sr, sc_, sc_], -1), (1, 128 // DA_HEAD_DIM))
    ones_r, zeros_r = jnp.ones((C, 128), F32), jnp.zeros((C, 128), F32)
    ones_d, zeros_d = ones_r, zeros_r

    lb_cum = jnp.cumsum(jax.nn.softmax(hgrn_lb.astype(F32), axis=0), axis=0)
    lower = lb_cum - lb_cum[0:1]

    cvec = jnp.zeros((8, D), F32).at[:B].set(c).at[B].set(c_ctx)
    mods = _modulation(cvec, mod_w, mod_b)

    head_id = jnp.arange(REC_W) // HDIM
    seg = (head_id[:, None] == head_id[None, :]).astype(BF16)

    xf = x.reshape(B * S, D)
    xc = ctx.reshape(B * C, D)
    zero_state = jnp.zeros((B, 2, HEADS, HDIM, HDIM), F32)

    for layer in range(depth):
        need_ctx = layer < depth - 1
        lam_init = 0.8 - 0.6 * math.exp(-0.3 * layer)
        m6 = mods[layer].reshape(8, 6, D)
        lat = lambda k: m6[:B, k][:, None, :]
        cxm = lambda k: jnp.broadcast_to(m6[B, k][None, None, :], (B, 1, D))
        ng = norm_g[layer]
        lb = lower[layer]
        lbc = jnp.zeros((8, REC_W), F32).at[0].set(jnp.log(lb)).at[1].set(jnp.log1p(-lb)).at[2].set(1.0 - lb)
        w_all = _prep_w_in(w_in[layer])
        log_gamma = jnp.log1p(-jnp.exp2(-ret_decay[layer].astype(F32)))
        lgrow = jnp.zeros((8, REC_W), F32).at[:2].set(jnp.repeat(log_gamma, HDIM, axis=1))
        lamv = da_lambda[layer].astype(F32)
        lam = (jnp.exp(jnp.sum(lamv[0] * lamv[1])) - jnp.exp(jnp.sum(lamv[2] * lamv[3])) + lam_init).reshape(1)
        hn = jnp.tile(hgrn_norm[layer], HEADS)[None, :]
        rn = jnp.tile(ret_norm[layer], HEADS)[None, :]
        dn = jnp.tile(jnp.concatenate([da_subln[layer], jnp.zeros((128 - DA_VDIM,), F32)]), DA_HEADS)[:, None]
        wo = w_out[layer]
        wo_da = jnp.pad(wo[2 * REC_W:].reshape(DA_HEADS, DA_VDIM, D), ((0, 0), (0, 128 - DA_VDIM), (0, 0)))
        w_out_b = jnp.concatenate([wo[:2 * REC_W], wo_da.reshape(DA_X, D)], axis=0).astype(BF16)
        rw = router_w[layer].T
        rb = router_b[layer][:, None]

        (hq, hv, hg, kf, lff, kb, lfb, rq, rk, rv, rg, dq, dk, dvx) = _inproj(
            xf, S, ng[0:1], lat(1), lat(0), w_all, lbc, cosr, sinr, cosd, sind)
        (hq_c, hv_c, hg_c, kf_c, lff_c, kb_c, lfb_c, rq_c, rk_c, rv_c, rg_c, dq_c, dk_c, dvx_c) = _inproj(
            xc, C, ng[0:1], cxm(1), cxm(0), w_all, lbc, ones_r, zeros_r, ones_d, zeros_d)

        hof_c, hob_c, hs = _hgrn_scan(hq_c, hv_c, kf_c, lff_c, kb_c, lfb_c,
                                      jnp.zeros((B, 2, REC_W // 128, 128, 128), F32), seg, B, C)
        hof, hob, _ = _hgrn_scan(hq, hv, kf, lff, kb, lfb, hs, seg, B, S)
        rof_c, rob_c, rs = _ret_scan(rq_c, rk_c, rv_c, zero_state, log_gamma, lgrow, B, C)
        rof, rob, _ = _ret_scan(rq, rk, rv, rs, log_gamma, lgrow, B, S)
        out_scale = 1.0 - lam_init
        da = _diff_attention(lam, dq, dk_c, dvx_c, dk, dvx, dn, B, S, C, S, out_scale, False)

        x1, tok, top_idx_t, gates_t, cnt = _outproj(xf, S, (hof, hob, hg, rof, rob, rg, da), seg, w_out_b, hn, rn,
                                                    ng[1:2], ng[2:3], lat(2), lat(4), lat(3), rw, rb)
        counts = jnp.sum(cnt[:, :, 0], axis=0)
        if need_ctx:
            da_c = _diff_attention(lam, dq_c, dk_c, dvx_c, None, None, dn, B, C, C, C, out_scale, True)
            xc1, tok_c, top_idx_c, gates_c, cnt_c = _outproj(
                xc, C, (hof_c, hob_c, hg_c, rof_c, rob_c, rg_c, da_c), seg, w_out_b, hn, rn, ng[1:2], ng[2:3],
                cxm(2), cxm(4), cxm(3), rw, rb)
            tok = jnp.concatenate([tok, tok_c], axis=0)
            top_idx_t = jnp.concatenate([top_idx_t, top_idx_c], axis=1)
            gates_t = jnp.concatenate([gates_t, gates_c], axis=1)
            counts = counts + jnp.sum(cnt_c[:, :, 0], axis=0)

        n_tok = tok.shape[0]
        gates = gates_t.T
        tok_sorted, posn, block_expert, n_active = _routing(top_idx_t, counts.astype(jnp.int32), bm)
        b1g = b1[layer][:, None, 0::2]
        b1l = b1[layer][:, None, 1::2]
        b2l = b2[layer][:, None, :]
        yg = _expert_ffn(block_expert, n_active, tok, tok_sorted, w1, w2, b1g, b1l, b2l, layer, bm)

        def expert_rows(lo, hi):
            return [yg.at[posn[k, lo:hi]].get(mode="promise_in_bounds") for k in range(TOP_K)]

        xf = _ffn_residual(x1, expert_rows(0, B * S), gates[:B * S], S, ng[3:4], lat(5))
        if need_ctx:
            xc = _ffn_residual(xc1, expert_rows(B * S, n_tok), gates[B * S:], C, ng[3:4], cxm(5))
    return xf.reshape(B, S, D)
```

```python
import functools
import math

import jax
import jax.numpy as jnp
from jax import lax
from jax.experimental import pallas as pl
from jax.experimental.pallas import tpu as pltpu

F32 = jnp.float32
BF16 = jnp.bfloat16

GRID_W = 64
HEADS = 6
HDIM = 64
REC_W = HEADS * HDIM
DA_HEADS = 4
DA_HEAD_DIM = 32
DA_MAPS = 2 * DA_HEADS
DA_W = DA_MAPS * DA_HEAD_DIM
DA_VDIM = 2 * DA_HEAD_DIM
DA_X = DA_HEADS * 128
_Q_SCALE = DA_HEAD_DIM ** -0.5 * math.log2(math.e)
_BOUND_MARGIN = 1.01
_MAX_EXP2_SPAN = 100.0
ROPE_BASE = 10000.0
N_EXPERTS = 32
TOP_K = 4
SWIGLU_ALPHA = 1.702
SWIGLU_LIMIT = 7.0
NORM_EPS = 1e-6

SUB = 16
NEG_BIG = -1e30
VMEM_LIMIT = 56 * 1024 * 1024

_C_HQ, _C_HFF, _C_HFB, _C_HI, _C_HG = 0, 384, 768, 1152, 1536
_C_RQ, _C_RK, _C_RV, _C_RG = 1920, 2304, 2688, 3072
_C_DQ, _C_DK, _C_DV = 3456, 3712, 3968
_C_RQR, _C_RKR, _C_DQR, _C_DKR = 4224, 4608, 4992, 5248
_W_ALL = 5504


def _dot(a, b):
    return jnp.dot(a, b, preferred_element_type=F32)


def _dot_nt(a, b):
    return lax.dot_general(a, b, (((1,), (1,)), ((), ())), preferred_element_type=F32)


def _dot_tn(a, b):
    return lax.dot_general(a, b, (((0,), (0,)), ((), ())), preferred_element_type=F32)


def _sigmoid(x):
    return 1.0 / (1.0 + jnp.exp(-x))


def _rms(x, g):
    ms = jnp.mean(x * x, axis=-1, keepdims=True)
    return x * lax.rsqrt(ms + NORM_EPS) * g


def _params(*sem):
    return pltpu.CompilerParams(dimension_semantics=sem, vmem_limit_bytes=VMEM_LIMIT)


def _mod_kernel(c_ref, w_ref, b_ref, o_ref):
    c = c_ref[...]
    o_ref[0] = _dot(c * _sigmoid(c), w_ref[0]) + b_ref[0]


def _modulation(cvec, mod_w, mod_b):
    depth, d, n = mod_w.shape
    tn = 1536
    return pl.pallas_call(
        _mod_kernel,
        grid=(depth, n // tn),
        in_specs=[
            pl.BlockSpec((8, d), lambda l, j: (0, 0)),
            pl.BlockSpec((1, d, tn), lambda l, j: (l, 0, j)),
            pl.BlockSpec((1, 1, tn), lambda l, j: (l, 0, j)),
        ],
        out_specs=pl.BlockSpec((1, 8, tn), lambda l, j: (l, 0, j)),
        out_shape=jax.ShapeDtypeStruct((depth, 8, n), F32),
        compiler_params=_params("arbitrary", "arbitrary"),
        name="modulation",
    )(cvec, mod_w, mod_b.reshape(depth, 1, n))


def _inproj_kernel(x_ref, g_ref, sc_ref, sh_ref, w_ref, lbc_ref, cosr_ref, sinr_ref, cosd_ref, sind_ref,
                   hq_ref, hv_ref, hg_ref, kf_ref, lff_ref, kb_ref, lfb_ref,
                   rq_ref, rk_ref, rv_ref, rg_ref, dq_ref, dk_ref, dvt_ref):
    x = x_ref[...]
    h = _rms(x, g_ref[...]) * (1.0 + sc_ref[0]) + sh_ref[0]
    hb = h.astype(BF16)

    wide = {}

    def proj(c0, n):
        lo = max(b for b in (0, _C_RQ, _C_DQ) if b <= c0)
        hi = {0: _C_RQ, _C_RQ: _C_DQ, _C_DQ: _W_ALL}[lo]
        if lo not in wide:
            wide[lo] = _dot(hb, w_ref[:, lo:hi])
        return wide[lo][:, c0 - lo:c0 - lo + n]

    hq_ref[...] = proj(_C_HQ, REC_W).astype(BF16)
    hv_ref[...] = proj(_C_HI, REC_W).astype(BF16)
    hg_ref[...] = proj(_C_HG, REC_W).astype(BF16)

    log_lb = lbc_ref[0:1, :]
    log_1m = lbc_ref[1:2, :]
    one_m = lbc_ref[2:3, :]

    def gates(z):
        log_sig = jnp.minimum(z, 0.0) - jnp.log1p(jnp.exp(-jnp.abs(z)))
        t = log_1m + log_sig
        m = jnp.maximum(log_lb, t)
        logf = m + jnp.log1p(jnp.exp(-jnp.abs(log_lb - t)))
        return one_m / (1.0 + jnp.exp(z)), logf

    k, lf = gates(proj(_C_HFF, REC_W))
    kf_ref[...] = k.astype(BF16)
    lff_ref[...] = lf
    k, lf = gates(proj(_C_HFB, REC_W))
    kb_ref[...] = k.astype(BF16)
    lfb_ref[...] = lf

    cosr = jnp.concatenate([cosr_ref[...]] * (REC_W // 128), axis=1)
    sinr = jnp.concatenate([sinr_ref[...]] * (REC_W // 128), axis=1)
    rq_ref[...] = (proj(_C_RQ, REC_W) * cosr + proj(_C_RQR, REC_W) * sinr).astype(BF16)
    rk_ref[...] = ((proj(_C_RK, REC_W) * cosr + proj(_C_RKR, REC_W) * sinr) * (HDIM ** -0.5)).astype(BF16)
    rv_ref[...] = proj(_C_RV, REC_W).astype(BF16)
    rg_ref[...] = proj(_C_RG, REC_W).astype(BF16)

    cosd = jnp.concatenate([cosd_ref[...]] * (DA_W // 128), axis=1)
    sind = jnp.concatenate([sind_ref[...]] * (DA_W // 128), axis=1)
    dq = ((proj(_C_DQ, DA_W) * cosd + proj(_C_DQR, DA_W) * sind) * _Q_SCALE).astype(BF16)
    dk = (proj(_C_DK, DA_W) * cosd + proj(_C_DKR, DA_W) * sind).astype(BF16)
    dv_t = proj(_C_DV, DA_W).T.astype(BF16)
    tm = dq.shape[0]
    zero_col = jnp.zeros((tm, 64), BF16)
    ones_col = jnp.where(lax.broadcasted_iota(jnp.int32, (tm, 64), 1) == 0, 1.0, 0.0).astype(BF16)
    sub = lax.broadcasted_iota(jnp.int32, (64, tm), 0)
    ones_row = jnp.where(sub == 0, 1.0, 0.0).astype(BF16)
    for h_ in range(DA_HEADS):
        lo, mid, hi_ = 128 * h_, 128 * h_ + 64, 128 * h_ + 128
        dq_ref[:, lo:mid] = dq[:, 64 * h_:64 * h_ + 64]
        dq_ref[:, mid:hi_] = zero_col
        dk_ref[:, lo:mid] = dk[:, 64 * h_:64 * h_ + 64]
        dk_ref[:, mid:hi_] = ones_col
        dvt_ref[0, lo:mid, :] = dv_t[64 * h_:64 * h_ + 64, :]
        dvt_ref[0, mid:hi_, :] = ones_row


def _inproj(xf, seq, gnorm, sc, sh, w_all, lbc, cosr, sinr, cosd, sind):
    rows, d = xf.shape
    tm = min(256, seq)
    nb = seq // tm
    row_spec = lambda w: pl.BlockSpec((tm, w), lambda i: (i, 0))
    tab_spec = lambda w: pl.BlockSpec((tm, w), lambda i: (i % nb, 0))
    mod_spec = pl.BlockSpec((1, 1, d), lambda i: (i // nb, 0, 0))
    widths = [REC_W] * 11 + [DA_X, DA_X]
    dtypes = [BF16, BF16, BF16, BF16, F32, BF16, F32, BF16, BF16, BF16, BF16, BF16, BF16]
    out_specs = [row_spec(w) for w in widths] + [pl.BlockSpec((1, DA_X, tm), lambda i: (i // nb, 0, i % nb))]
    out_shape = ([jax.ShapeDtypeStruct((rows, w), dt) for w, dt in zip(widths, dtypes)]
                 + [jax.ShapeDtypeStruct((rows // seq, DA_X, seq), BF16)])
    return pl.pallas_call(
        _inproj_kernel,
        grid=(rows // tm,),
        in_specs=[
            row_spec(d),
            pl.BlockSpec((1, d), lambda i: (0, 0)),
            mod_spec, mod_spec,
            pl.BlockSpec((d, _W_ALL), lambda i: (0, 0)),
            pl.BlockSpec((8, REC_W), lambda i: (0, 0)),
            tab_spec(128), tab_spec(128), tab_spec(128), tab_spec(128),
        ],
        out_specs=out_specs,
        out_shape=out_shape,
        compiler_params=_params("arbitrary"),
        name="inproj",
    )(xf, gnorm, sc, sh, w_all, lbc, cosr, sinr, cosd, sind)


def _hgrn_dir(q, k, v, lf, seg, st_ref, inter_ref, intra_ref, reverse):
    tt = q.shape[0]
    nblk = tt // SUB
    row = lax.broadcasted_iota(jnp.int32, (tt, tt), 0)
    col = lax.broadcasted_iota(jnp.int32, (tt, tt), 1)
    same = (row // SUB) == (col // SUB)
    tri = (col >= row) if reverse else (col <= row)
    l_all = jnp.where(same, 1.0, 0.0)
    l_cum = jnp.where(tri, l_all, 0.0).astype(BF16)
    l_all = l_all.astype(BF16)
    lf = lf * math.log2(math.e)
    hi = lf.astype(BF16)
    r1 = lf - hi.astype(F32)
    mid = r1.astype(BF16)
    lo = (r1 - mid.astype(F32)).astype(BF16)
    b = _dot(l_cum, hi) + _dot(l_cum, mid) + _dot(l_cum, lo)
    btot = _dot(l_all, hi) + _dot(l_all, mid) + _dot(l_all, lo)

    qe = (q * jnp.exp2(b)).astype(BF16)
    kd = (k * jnp.exp2(btot - b)).astype(BF16)
    dec = jnp.exp2(btot)
    v_t = v.T
    tok_blk = lax.broadcasted_iota(jnp.int32, (128, tt), 1) // SUB
    hr = lax.broadcasted_iota(jnp.int32, (128, 128), 0) // HDIM
    hc = lax.broadcasted_iota(jnp.int32, (128, 128), 1) // HDIM
    same_head = hr == hc
    order = range(nblk - 1, -1, -1) if reverse else range(nblk)
    half = lax.broadcasted_iota(jnp.int32, (nblk, 8, 128), 1)
    seg2 = seg[0:128, 0:128]

    def carried(c, g):
        v_tg = v_t[c, :]
        quarter = max(nblk // 4, 1)
        upd = []
        for j0 in range(0, nblk, quarter):
            v_stack = jnp.concatenate([jnp.where(tok_blk == j, v_tg, 0.0).astype(BF16)
                                       for j in range(j0, j0 + quarter)], axis=0)
            part = _dot(v_stack, kd[:, c])
            upd += [part[128 * j:128 * (j + 1)] for j in range(quarter)]
        st = st_ref[g]
        for j in order:
            r0 = j * SUB
            inter_ref[r0:r0 + SUB, c] = _dot_nt(qe[r0:r0 + SUB, c], st.astype(BF16))
            st = st * dec[r0:r0 + 1, c] + jnp.where(same_head, upd[j], 0.0)
        st_ref[g] = st

    def within(c):
        def halves(a):
            a4 = a[:, c].reshape(nblk, 2, 8, 128)
            return (a4[:, 1], a4[:, 0]) if reverse else (a4[:, 0], a4[:, 1])

        (q0, q1), (k0, k1), (v0, v1), (b0, b1) = halves(q), halves(k), halves(v), halves(b)

        def rotated(r, *arrays):
            sh = ((8 - r) % 8) if reverse else r
            return [pltpu.roll(a, sh, 1) if sh else a for a in arrays]

        def pairs(qa, ba, kbv, r, masked):
            kb_, bb_, vb_ = kbv
            diff = ba - bb_
            if masked and r:
                diff = jnp.where((half <= 7 - r) if reverse else (half >= r), diff, NEG_BIG)
            term = (qa * kb_ * jnp.exp2(diff)).reshape(nblk * 8, 128).astype(BF16)
            return _dot(term, seg2).reshape(nblk, 8, 128) * vb_

        o0 = o1 = None
        for r in range(8):
            early, late = rotated(r, k0, b0, v0), rotated(r, k1, b1, v1)
            t0 = pairs(q0, b0, early, r, True)
            t1 = pairs(q1, b1, late, r, True) + pairs(q1, b1, early, r, False)
            o0, o1 = (t0, t1) if r == 0 else (o0 + t0, o1 + t1)
        first, second = (o1, o0) if reverse else (o0, o1)
        intra_ref[:, 0, :, c] = first
        intra_ref[:, 1, :, c] = second

    for g in range(REC_W // 128):
        c = slice(128 * g, 128 * (g + 1))
        carried(c, g)
        within(c)
    return intra_ref[...].reshape(tt, REC_W) + inter_ref[...]


def _hgrn_kernel(qf_ref, vf_ref, kf_ref, lff_ref, qb_ref, vb_ref, kb_ref, lfb_ref, s0_ref, seg_ref,
                 of_ref, ob_ref, sout_ref, st_ref, inter_ref, intra_ref):
    i = pl.program_id(1)

    @pl.when(i == 0)
    def _():
        st_ref[...] = s0_ref[0]

    seg = seg_ref[...]
    of_ref[...] = _hgrn_dir(qf_ref[...].astype(F32), kf_ref[...].astype(F32), vf_ref[...].astype(F32),
                            lff_ref[...], seg, st_ref.at[0], inter_ref, intra_ref, False).astype(BF16)
    ob_ref[...] = _hgrn_dir(qb_ref[...].astype(F32), kb_ref[...].astype(F32), vb_ref[...].astype(F32),
                            lfb_ref[...], seg, st_ref.at[1], inter_ref, intra_ref, True).astype(BF16)

    @pl.when(i == pl.num_programs(1) - 1)
    def _():
        sout_ref[0] = st_ref[...]


def _hgrn_scan(hq, hv, kf, lff, kb, lfb, s0, seg, batch, seq):
    tt = min(256, seq)
    n = seq // tt
    fwd = pl.BlockSpec((tt, REC_W), lambda b, i: (b * n + i, 0))
    bwd = pl.BlockSpec((tt, REC_W), lambda b, i: (b * n + n - 1 - i, 0))
    st_spec = pl.BlockSpec((1, 2, REC_W // 128, 128, 128), lambda b, i: (b, 0, 0, 0, 0))
    rows = batch * seq
    return pl.pallas_call(
        _hgrn_kernel,
        grid=(batch, n),
        in_specs=[fwd, fwd, fwd, fwd, bwd, bwd, bwd, bwd, st_spec,
                  pl.BlockSpec((REC_W, REC_W), lambda b, i: (0, 0))],
        out_specs=[fwd, bwd, st_spec],
        out_shape=[jax.ShapeDtypeStruct((rows, REC_W), BF16), jax.ShapeDtypeStruct((rows, REC_W), BF16),
                   jax.ShapeDtypeStruct(s0.shape, F32)],
        scratch_shapes=[pltpu.VMEM((2, REC_W // 128, 128, 128), F32), pltpu.VMEM((tt, REC_W), F32),
                        pltpu.VMEM((tt // SUB, 2, 8, REC_W), F32)],
        compiler_params=_params("arbitrary", "arbitrary"),
        name="hgrn_scan",
    )(hq, hv, kf, lff, hq, hv, kb, lfb, s0, seg)


def _ret_kernel(lg_ref, qf_ref, kf_ref, vf_ref, qb_ref, kb_ref, vb_ref, s0_ref, lgrow_ref,
                of_ref, ob_ref, sout_ref, st_ref, dmask_ref, tab_ref, o_scr):
    i = pl.program_id(1)
    tt = qf_ref.shape[0]

    @pl.when(i == 0)
    def _():
        st_ref[...] = s0_ref[0]
        row = lax.broadcasted_iota(jnp.int32, (tt, tt), 0)
        col = lax.broadcasted_iota(jnp.int32, (tt, tt), 1)
        dist = (row - col).astype(F32)
        for h in range(HEADS):
            fw = jnp.where(dist >= 0, jnp.exp(dist * lg_ref[0, h]), 0.0)
            bw = jnp.where(dist <= 0, jnp.exp(-dist * lg_ref[1, h]), 0.0)
            dmask_ref[h] = fw + bw
        pos = lax.broadcasted_iota(jnp.int32, (tt, REC_W), 0).astype(F32)
        lgf = lgrow_ref[0:1, :]
        lgb = lgrow_ref[1:2, :]
        tab_ref[0] = jnp.exp((pos + 1.0) * lgf)
        tab_ref[1] = jnp.exp((tt - 1.0 - pos) * lgf)
        tab_ref[2] = jnp.exp((tt - pos) * lgb)
        tab_ref[3] = jnp.exp(pos * lgb)

    tile_f = jnp.exp(tt * lgrow_ref[0:1, :])
    tile_b = jnp.exp(tt * lgrow_ref[1:2, :])

    q = qf_ref[...]
    k = kf_ref[...]
    v = vf_ref[...]
    qe = (q.astype(F32) * tab_ref[0]).astype(BF16)
    ke = (k.astype(F32) * tab_ref[1]).astype(BF16)
    for h in range(HEADS):
        c = slice(h * HDIM, (h + 1) * HDIM)
        s = _dot_nt(q[:, c], k[:, c]) * dmask_ref[h]
        st = st_ref[0, h]
        o_scr[:, c] = _dot(s.astype(BF16), v[:, c]) + _dot(qe[:, c], st.astype(BF16))
        st_ref[0, h] = st * tile_f[:, c] + _dot_tn(ke[:, c], v[:, c])
    of_ref[...] = o_scr[...].astype(BF16)

    q = qb_ref[...]
    k = kb_ref[...]
    v = vb_ref[...]
    qe = (q.astype(F32) * tab_ref[2]).astype(BF16)
    ke = (k.astype(F32) * tab_ref[3]).astype(BF16)
    for h in range(HEADS):
        c = slice(h * HDIM, (h + 1) * HDIM)
        st = st_ref[1, h]
        o_scr[:, c] = _dot(qe[:, c], st.astype(BF16))
        st_ref[1, h] = st * tile_b[:, c] + _dot_tn(ke[:, c], v[:, c])
    ob_ref[...] = o_scr[...].astype(BF16)

    @pl.when(i == pl.num_programs(1) - 1)
    def _():
        sout_ref[0] = st_ref[...]


def _ret_scan(rq, rk, rv, s0, lg, lgrow, batch, seq):
    tt = min(256, seq)
    n = seq // tt
    fwd = pl.BlockSpec((tt, REC_W), lambda b, i: (b * n + i, 0))
    bwd = pl.BlockSpec((tt, REC_W), lambda b, i: (b * n + n - 1 - i, 0))
    st_spec = pl.BlockSpec((1, 2, HEADS, HDIM, HDIM), lambda b, i: (b, 0, 0, 0, 0))
    rows = batch * seq
    return pl.pallas_call(
        _ret_kernel,
        grid=(batch, n),
        in_specs=[pl.BlockSpec(memory_space=pltpu.SMEM), fwd, fwd, fwd, bwd, bwd, bwd, st_spec,
                  pl.BlockSpec((8, REC_W), lambda b, i: (0, 0))],
        out_specs=[fwd, bwd, st_spec],
        out_shape=[jax.ShapeDtypeStruct((rows, REC_W), BF16), jax.ShapeDtypeStruct((rows, REC_W), BF16),
                   jax.ShapeDtypeStruct(s0.shape, F32)],
        scratch_shapes=[pltpu.VMEM((2, HEADS, HDIM, HDIM), F32), pltpu.VMEM((HEADS, tt, tt), F32),
                        pltpu.VMEM((4, tt, REC_W), F32), pltpu.VMEM((tt, REC_W), F32)],
        compiler_params=_params("arbitrary", "arbitrary"),
        name="ret_scan",
    )(lg, rq, rk, rv, rq, rk, rv, s0, lgrow)


def _attn_kernel(lam_ref, q_ref, kc_ref, vct_ref, k_ref, vt_ref, gain_ref, o_ref, s_ref, acc_ref, kmax_ref, *,
                 with_latent, kchunk, piece, fast_chunk, fast_unroll, out_scale):
    tq = q_ref.shape[0]
    q = q_ref[...]
    lane = lax.broadcasted_iota(jnp.int32, (tq, 128), 1)
    zero = jnp.zeros_like(q)
    qs = jnp.concatenate([jnp.where(lane < DA_HEAD_DIM, q, zero),
                          jnp.where(lane >= DA_HEAD_DIM, q, zero)], axis=0)
    nchunks = (k_ref.shape[0] // kchunk) if with_latent else 0

    def scores(kblk):
        return _dot_nt(kblk, qs)

    def absorb(s, vtblk, m, acc):
        n = s.shape[0]
        part = jnp.max(s.reshape(n // 128, 128, 2 * tq), axis=0) if n > 128 else s
        m_new = jnp.maximum(m, jnp.max(part, axis=0, keepdims=True))
        p = jnp.exp2((s - m_new).astype(BF16))
        return m_new, jnp.exp2(m - m_new) * acc + _dot(vtblk, p)

    def latent_k(c):
        return k_ref[pl.ds(pl.multiple_of(c * kchunk, kchunk), kchunk), :]

    def latent_vt(c):
        return vt_ref[0, :, pl.ds(pl.multiple_of(c * kchunk, kchunk), kchunk)]

    sel_r = lax.broadcasted_iota(jnp.int32, (128, 128), 0) // DA_HEAD_DIM
    sel_c = lax.broadcasted_iota(jnp.int32, (128, 128), 1)
    sel = jnp.where(sel_r == sel_c, 1.0, 0.0).astype(BF16)

    def key_norm2(kblk):
        kf = kblk.astype(F32)
        return jnp.max(_dot((kf * kf).astype(BF16), sel), axis=0, keepdims=True)

    @pl.when(pl.program_id(2) == 0)
    def _():
        km = key_norm2(kc_ref[...])
        if with_latent:
            km = lax.fori_loop(0, nchunks, lambda c, a: jnp.maximum(a, key_norm2(latent_k(c))), km)
        kmax_ref[...] = jnp.broadcast_to(km, kmax_ref.shape)

    qf32 = q.astype(F32)
    qn2 = _dot((qf32 * qf32).astype(BF16), sel)
    bound = jnp.sqrt(qn2 * kmax_ref[0:1, :]) * _BOUND_MARGIN
    shift = jnp.concatenate([bound[:, 0:1], bound[:, 1:2]], axis=0)
    use_bound = 2.0 * jnp.max(shift) <= _MAX_EXP2_SPAN

    @pl.when(use_bound)
    def _():
        lane2 = lax.broadcasted_iota(jnp.int32, (2 * tq, 128), 1)
        qf = jnp.where(lane2 == DA_VDIM, (-shift).astype(BF16), qs)

        def absorb_shifted(kblk, vtblk):
            acc_ref[...] += _dot(vtblk, jnp.exp2(_dot_nt(kblk, qf)).astype(BF16))

        acc_ref[...] = jnp.zeros(acc_ref.shape, F32)
        absorb_shifted(kc_ref[...], vct_ref[0])
        if with_latent:
            def body(c, carry):
                c0 = pl.multiple_of(c * fast_chunk, fast_chunk)
                absorb_shifted(k_ref[pl.ds(c0, fast_chunk), :], vt_ref[0, :, pl.ds(c0, fast_chunk)])
                return carry

            lax.fori_loop(0, k_ref.shape[0] // fast_chunk, body, 0, unroll=fast_unroll)

    @pl.when(jnp.logical_not(use_bound))
    def _():
        m, acc = absorb(scores(kc_ref[...]), vct_ref[0], jnp.full((1, 2 * tq), NEG_BIG, F32),
                        jnp.zeros((128, 2 * tq), F32))
        if with_latent:
            pk = min(piece, kchunk)
            npieces = kchunk // pk

            def col_max(x):
                if pk > 128:
                    x = jnp.max(x.reshape(pk // 128, 128, 2 * tq), axis=0)
                return jnp.max(x.reshape(x.shape[0] // 8, 8, 2 * tq), axis=0)

            def fused(cur, c, m_, acc_, nxt):
                part = jnp.full((8, 2 * tq), NEG_BIG, F32)
                for t in range(npieces):
                    if nxt is not None:
                        r0 = pl.multiple_of((c + 1) * kchunk + t * pk, pk)
                        s_new = scores(k_ref[pl.ds(r0, pk), :])
                        s_ref[nxt, t * pk:(t + 1) * pk, :] = s_new
                        part = jnp.maximum(part, col_max(s_new))
                    c0 = pl.multiple_of(c * kchunk + t * pk, pk)
                    p = jnp.exp2((s_ref[cur, t * pk:(t + 1) * pk, :] - m_).astype(BF16))
                    acc_ = acc_ + _dot(vt_ref[0, :, pl.ds(c0, pk)], p)
                if nxt is None:
                    return m_, acc_
                m_new = jnp.maximum(m_, jnp.max(part, axis=0, keepdims=True))
                return m_new, acc_ * jnp.exp2(m_ - m_new)

            s0 = scores(latent_k(0))
            s_ref[0] = s0
            m0 = jnp.maximum(m, jnp.max(jnp.max(s0.reshape(kchunk // 128, 128, 2 * tq), axis=0), axis=0,
                                        keepdims=True))
            acc = acc * jnp.exp2(m - m0)
            m = m0

            def pair(c, carry, last):
                m_, acc_ = fused(0, c, *carry, 1)
                return fused(1, c + 1, m_, acc_, None if last else 0)

            if nchunks > 2:
                m, acc = lax.fori_loop(0, nchunks // 2 - 1, lambda j, cr: pair(2 * j, cr, False), (m, acc))
            m, acc = pair(nchunks - 2, (m, acc), True)
        acc_ref[...] = acc

    acc = acc_ref[...]
    a1 = acc[:, :tq]
    a2 = acc[:, tq:]
    o = a1 / a1[DA_VDIM:DA_VDIM + 1, :] - lam_ref[0] * (a2 / a2[DA_VDIM:DA_VDIM + 1, :])
    row = lax.broadcasted_iota(jnp.int32, (128, tq), 0)
    o = jnp.where(row < DA_VDIM, o, 0.0)
    ms = jnp.sum(o * o, axis=0, keepdims=True) * (1.0 / DA_VDIM)
    o_ref[...] = (o * lax.rsqrt(ms + NORM_EPS) * gain_ref[...] * out_scale).T.astype(BF16)


def _diff_attention(lam, q, kc, vc, k, v, gain, batch, seq_q, ctx_len, seq_k, out_scale, q_is_ctx):
    tq = min(512, seq_q)
    nq = seq_q // tq
    with_latent = not q_is_ctx
    if with_latent:
        kchunk = min(2048, seq_k // 2)
        assert seq_k % (2 * kchunk) == 0
    else:
        k, v = kc, vc
        seq_k, kchunk = ctx_len, ctx_len
    fast_chunk = kchunk
    fast_unroll = 2 if (seq_k // fast_chunk) % 2 == 0 else 1
    kern = functools.partial(_attn_kernel, with_latent=with_latent, kchunk=kchunk, piece=512, fast_chunk=fast_chunk,
                             fast_unroll=fast_unroll, out_scale=out_scale)
    return pl.pallas_call(
        kern,
        grid=(batch, DA_HEADS, nq),
        in_specs=[
            pl.BlockSpec(memory_space=pltpu.SMEM),
            pl.BlockSpec((tq, 128), lambda b, h, i: (b * nq + i, h)),
            pl.BlockSpec((ctx_len, 128), lambda b, h, i: (b, h)),
            pl.BlockSpec((1, 128, ctx_len), lambda b, h, i: (b, h, 0)),
            pl.BlockSpec((seq_k, 128), lambda b, h, i: (b, h)),
            pl.BlockSpec((1, 128, seq_k), lambda b, h, i: (b, h, 0)),
            pl.BlockSpec((128, 1), lambda b, h, i: (h, 0)),
        ],
        out_specs=pl.BlockSpec((tq, 128), lambda b, h, i: (b * nq + i, h)),
        out_shape=jax.ShapeDtypeStruct((batch * seq_q, DA_X), BF16),
        scratch_shapes=[pltpu.VMEM((2, kchunk, 2 * tq), F32), pltpu.VMEM((128, 2 * tq), F32),
                        pltpu.VMEM((8, 128), F32)],
        compiler_params=_params("arbitrary", "arbitrary", "arbitrary"),
        name="diff_attention",
    )(lam, q, kc, vc, k, v, gain)


def _outproj_kernel(x_ref, hof_ref, hob_ref, hg_ref, rof_ref, rob_ref, rg_ref, da_ref, seg_ref, w_ref,
                    hn_ref, rn_ref, n1_ref, n2_ref, g1_ref, sc2_ref, sh2_ref, rw_ref, rb_ref,
                    x1_ref, tok_ref, idx_ref, gate_ref, cnt_ref):
    seg = seg_ref[...]

    def gated_head_norm(o, gain, gate):
        ms = _dot((o * o).astype(BF16), seg) * (1.0 / HDIM)
        return (o * lax.rsqrt(ms + NORM_EPS) * gain * (gate * _sigmoid(gate))).astype(BF16)

    a = gated_head_norm(hof_ref[...].astype(F32) + hob_ref[...].astype(F32), hn_ref[...], hg_ref[...].astype(F32))
    b = gated_head_norm(rof_ref[...].astype(F32) + rob_ref[...].astype(F32), rn_ref[...], rg_ref[...].astype(F32))
    y = (_dot(a, w_ref[0:REC_W, :]) + _dot(b, w_ref[REC_W:2 * REC_W, :])
         + _dot(da_ref[...], w_ref[2 * REC_W:2 * REC_W + DA_X, :]))
    x1 = x_ref[...] + g1_ref[0] * _rms(y, n1_ref[...])
    x1_ref[...] = x1
    tok = _rms(x1, n2_ref[...]) * (1.0 + sc2_ref[0]) + sh2_ref[0]
    tok_ref[...] = tok.astype(BF16)
    logits = _dot_nt(rw_ref[...], tok) + rb_ref[...]
    eid = lax.broadcasted_iota(jnp.int32, logits.shape, 0).astype(F32)
    vals, ids = [], []
    for _ in range(TOP_K):
        best = jnp.max(logits, axis=0, keepdims=True)
        first = jnp.min(jnp.where(logits == best, eid, float(N_EXPERTS)), axis=0, keepdims=True)
        vals.append(best)
        ids.append(first)
        logits = jnp.where(eid == first, -jnp.inf, logits)
    ex = [jnp.exp(v - vals[0]) for v in vals]
    total = ex[0] + ex[1] + ex[2] + ex[3]
    idx_ref[...] = jnp.concatenate(ids, axis=0).astype(jnp.int32)
    gate_ref[...] = jnp.concatenate(ex, axis=0) / total
    hits = sum((eid == first).astype(F32) for first in ids)
    cnt_ref[0] = jnp.broadcast_to(jnp.sum(hits, axis=1, keepdims=True), cnt_ref.shape[1:])


def _outproj(xf, seq, streams, seg, w_out, hn, rn, n1, n2, g1, sc2, sh2, rw, rb):
    rows, d = xf.shape
    tm = min(256, seq)
    nb = seq // tm
    row_spec = lambda w: pl.BlockSpec((tm, w), lambda i: (i, 0))
    full = lambda a: pl.BlockSpec(a.shape, lambda i: (0,) * a.ndim)
    mod_spec = pl.BlockSpec((1, 1, d), lambda i: (i // nb, 0, 0))
    hof, hob, hg, rof, rob, rg, da = streams
    return pl.pallas_call(
        _outproj_kernel,
        grid=(rows // tm,),
        in_specs=[row_spec(d)] + [row_spec(REC_W)] * 6 + [row_spec(DA_X), full(seg), full(w_out),
                  full(hn), full(rn), full(n1), full(n2), mod_spec, mod_spec, mod_spec, full(rw), full(rb)],
        out_specs=[row_spec(d), row_spec(d), pl.BlockSpec((TOP_K, tm), lambda i: (0, i)),
                   pl.BlockSpec((TOP_K, tm), lambda i: (0, i)),
                   pl.BlockSpec((1, N_EXPERTS, 128), lambda i: (i, 0, 0))],
        out_shape=[jax.ShapeDtypeStruct((rows, d), F32), jax.ShapeDtypeStruct((rows, d), BF16),
                   jax.ShapeDtypeStruct((TOP_K, rows), jnp.int32), jax.ShapeDtypeStruct((TOP_K, rows), F32),
                   jax.ShapeDtypeStruct((rows // tm, N_EXPERTS, 128), F32)],
        compiler_params=_params("arbitrary"),
        name="outproj",
    )(xf, hof, hob, hg, rof, rob, rg, da, seg, w_out, hn, rn, n1, n2, g1, sc2, sh2, rw, rb)


def _ffn_kernel(be_ref, na_ref, x_ref, w1_ref, w2_ref, b1g_ref, b1l_ref, b2_ref, pe_ref, po_ref, yprev_ref, y_ref,
                w1g_s, w1l_s, w2_s, *, blk0):
    del yprev_ref
    i = pl.program_id(0)
    blk = i + blk0
    active = blk < na_ref[0]
    fresh = jnp.logical_or(i == 0, be_ref[blk] != be_ref[jnp.maximum(blk - 1, 0)])

    @pl.when(fresh)
    def _():
        n = w1_ref.shape[3]
        for c in range(n // 256):
            wb = w1_ref[0, 0, :, 256 * c:256 * (c + 1)].astype(BF16)
            w1g_s[:, 128 * c:128 * (c + 1)] = _dot(wb, pe_ref[...]).astype(BF16)
            w1l_s[:, 128 * c:128 * (c + 1)] = _dot(wb, po_ref[...]).astype(BF16)
        w2_s[...] = w2_ref[0, 0].astype(BF16)

    @pl.when(jnp.logical_not(active))
    def _():
        y_ref[...] = jnp.zeros(y_ref.shape, y_ref.dtype)

    @pl.when(active)
    def _():
        x = x_ref[...]
        glu = jnp.minimum(_dot(x, w1g_s[...]) + b1g_ref[0], SWIGLU_LIMIT)
        lin = jnp.clip(_dot(x, w1l_s[...]) + b1l_ref[0], -SWIGLU_LIMIT, SWIGLU_LIMIT)
        act = glu * _sigmoid(SWIGLU_ALPHA * glu) * (lin + 1.0)
        y_ref[...] = (_dot(act.astype(BF16), w2_s[...]) + b2_ref[0]).astype(y_ref.dtype)


def _expert_ffn(block_expert, n_active, tok, tok_sorted, w1, w2, b1g, b1l, b2, layer, bm, nparts=4):
    p = tok_sorted.shape[0]
    d = tok.shape[1]
    f = w2.shape[2]
    nblocks = p // bm
    src = jnp.arange(256)[:, None]
    dst = jnp.arange(128)[None, :]
    pe = (src == 2 * dst).astype(BF16)
    po = (src == 2 * dst + 1).astype(BF16)
    sel = pl.BlockSpec((256, 128), lambda i, be, na: (0, 0))
    bounds = [nblocks * f // 16 for f in (0, 1, 3, 7, 11, 16)][:nparts + 2]
    nparts = len(bounds) - 1
    yg = jnp.zeros((8, 128), BF16)
    for j in range(nparts):
        b0, nb = bounds[j], bounds[j + 1] - bounds[j]
        xg = tok.at[tok_sorted[b0 * bm:(b0 + nb) * bm]].get(mode="promise_in_bounds")
        wspec = lambda s, b0=b0: pl.BlockSpec((1, 1) + s, lambda i, be, na: (layer, be[i + b0], 0, 0))
        bspec = lambda s, b0=b0: pl.BlockSpec((1,) + s, lambda i, be, na: (be[i + b0], 0, 0))
        yg = pl.pallas_call(
            functools.partial(_ffn_kernel, blk0=b0),
            grid_spec=pltpu.PrefetchScalarGridSpec(
                num_scalar_prefetch=2,
                grid=(nb,),
                in_specs=[pl.BlockSpec((bm, d), lambda i, be, na: (i, 0)),
                          wspec((d, 2 * f)), wspec((f, d)), bspec((1, f)), bspec((1, f)), bspec((1, d)), sel, sel,
                          pl.BlockSpec(memory_space=pl.ANY)],
                out_specs=pl.BlockSpec((bm, d), lambda i, be, na, b0=b0: (i + b0, 0)),
                scratch_shapes=[pltpu.VMEM((d, f), BF16), pltpu.VMEM((d, f), BF16), pltpu.VMEM((f, d), BF16)],
            ),
            out_shape=jax.ShapeDtypeStruct((p, d), BF16),
            input_output_aliases={10: 0} if j else {},
            compiler_params=_params("arbitrary"),
            name="expert_ffn",
        )(block_expert, n_active, xg, w1, w2, b1g, b1l, b2, pe, po, yg)
    return yg


def _resid_kernel(x_ref, y0_ref, y1_ref, y2_ref, y3_ref, gate_ref, n_ref, g_ref, o_ref):
    gates = gate_ref[...]
    f = y0_ref[...].astype(F32) * gates[:, 0:1]
    for k, y_ref in enumerate((y1_ref, y2_ref, y3_ref), start=1):
        f = f + y_ref[...].astype(F32) * gates[:, k:k + 1]
    o_ref[...] = x_ref[...] + g_ref[0] * _rms(f, n_ref[...])


def _ffn_residual(xf, ys, gates, seq, n3, g2):
    rows, d = xf.shape
    tm = min(512, seq)
    nb = seq // tm
    row_spec = pl.BlockSpec((tm, d), lambda i: (i, 0))
    return pl.pallas_call(
        _resid_kernel,
        grid=(rows // tm,),
        in_specs=[row_spec] * 5 + [pl.BlockSpec((tm, TOP_K), lambda i: (i, 0)), pl.BlockSpec((1, d), lambda i: (0, 0)),
                                   pl.BlockSpec((1, 1, d), lambda i: (i // nb, 0, 0))],
        out_specs=row_spec,
        out_shape=jax.ShapeDtypeStruct((rows, d), F32),
        compiler_params=_params("arbitrary"),
        name="ffn_residual",
    )(xf, *ys, gates, n3, g2)


def _rope_tables(pos, dim):
    inv = 1.0 / (ROPE_BASE ** (jnp.arange(0, dim, 2, dtype=F32) / dim))
    ang = pos.astype(F32)[:, None] * inv[None, :]
    return jnp.cos(ang), jnp.sin(ang)


def _rot_cols(w, head_dim, halves):
    d, n = w.shape
    g = head_dim // halves
    w4 = w.reshape(d, n // g, 2, g // 2)
    return jnp.concatenate([-w4[:, :, 1], w4[:, :, 0]], axis=-1).reshape(d, n)


def _prep_w_in(w):
    rq, rk = w[:, _C_RQ:_C_RQ + REC_W], w[:, _C_RK:_C_RK + REC_W]
    dq, dk = w[:, _C_DQ:_C_DQ + DA_W], w[:, _C_DK:_C_DK + DA_W]
    return jnp.concatenate([w, _rot_cols(rq, HDIM, 1), _rot_cols(rk, HDIM, 1),
                            _rot_cols(dq, DA_HEAD_DIM, 2), _rot_cols(dk, DA_HEAD_DIM, 2)], axis=1).astype(BF16)


def _assign_kernel(ids_ref, first_ref, upper_ref, pos_ref, seen_ref):
    @pl.when(pl.program_id(0) == 0)
    def _():
        seen_ref[...] = jnp.zeros(seen_ref.shape, F32)

    ids = ids_ref[...].astype(F32)
    tm = ids.shape[1]
    eid = lax.broadcasted_iota(jnp.int32, (N_EXPERTS, tm), 0).astype(F32)
    hit = [eid == ids[k:k + 1, :] for k in range(TOP_K)]
    hits = sum(h.astype(F32) for h in hit)
    earlier = _dot(hits.astype(BF16), upper_ref[...])
    base = earlier + seen_ref[:, 0:1] + first_ref[...]
    pos_ref[...] = jnp.concatenate([jnp.sum(jnp.where(h, base, 0.0), axis=0, keepdims=True) for h in hit],
                                   axis=0).astype(jnp.int32)
    seen_ref[...] += jnp.broadcast_to(jnp.sum(hits, axis=1, keepdims=True), seen_ref.shape)


def _assign_rows(top_idx_t, first_row):
    n = top_idx_t.shape[1]
    tm = 512 if n % 512 == 0 else 128
    upper = (jnp.arange(tm)[:, None] < jnp.arange(tm)[None, :]).astype(BF16)
    return pl.pallas_call(
        _assign_kernel,
        grid=(n // tm,),
        in_specs=[pl.BlockSpec((TOP_K, tm), lambda i: (0, i)),
                  pl.BlockSpec((N_EXPERTS, 1), lambda i: (0, 0)),
                  pl.BlockSpec((tm, tm), lambda i: (0, 0))],
        out_specs=pl.BlockSpec((TOP_K, tm), lambda i: (0, i)),
        out_shape=jax.ShapeDtypeStruct((TOP_K, n), jnp.int32),
        scratch_shapes=[pltpu.VMEM((N_EXPERTS, 128), F32)],
        compiler_params=_params("arbitrary"),
        name="assign_rows",
    )(top_idx_t, first_row.astype(F32)[:, None], upper)


def _routing(top_idx_t, counts, bm):
    n = top_idx_t.shape[1]
    a = n * TOP_K
    flat_e = top_idx_t.T.reshape(a)
    _, order = lax.sort_key_val(flat_e, jnp.arange(a, dtype=jnp.int32))
    bounds = jnp.concatenate([jnp.zeros((1,), jnp.int32), jnp.cumsum(counts)])
    start = bounds[:-1]
    padded = (counts + bm - 1) // bm * bm
    pend = jnp.cumsum(padded)
    shift = pend - padded - start
    pos = _assign_rows(top_idx_t, pend - padded)
    nblocks = -(-(a + N_EXPERTS * (bm - 1)) // bm)
    block_row = jnp.arange(nblocks, dtype=jnp.int32) * bm
    block_expert = jnp.minimum(jnp.sum((pend[None, :] <= block_row[:, None]).astype(jnp.int32), axis=1),
                               N_EXPERTS - 1)
    row_e = jnp.repeat(block_expert, bm)
    rank = jnp.arange(nblocks * bm, dtype=jnp.int32) - shift[row_e]
    valid = rank < bounds[row_e + 1]
    tok_sorted = jnp.where(valid, order[jnp.clip(rank, 0, a - 1)] // TOP_K, 0)
    n_active = (pend[-1:] // bm).astype(jnp.int32)
    return tok_sorted, pos, block_expert, n_active


def kernel(x, c, ctx, c_ctx, mod_w, mod_b, norm_g, w_in, hgrn_lb, hgrn_norm, ret_decay, ret_norm, da_lambda,
           da_subln, w_out, router_w, router_b, w1, b1, w2, b2):
    B, S, D = x.shape
    C = ctx.shape[1]
    depth = mod_w.shape[0]
    bm = 256

    pos = jnp.arange(S)
    cr, sr = _rope_tables(pos // GRID_W, DA_HEAD_DIM // 2)
    cc, sc_ = _rope_tables(pos % GRID_W, DA_HEAD_DIM // 2)
    cs, ss = _rope_tables(pos, HDIM)
    cosr = jnp.tile(jnp.concatenate([cs, cs], -1), (1, 128 // HDIM))
    sinr = jnp.tile(jnp.concatenate([ss, ss], -1), (1, 128 // HDIM))
    cosd = jnp.tile(jnp.concatenate([cr, cr, cc, cc], -1), (1, 128 // DA_HEAD_DIM))
    sind = jnp.tile(jnp.concatenate([sr, sr, sc_, sc_], -1), (1, 128 // DA_HEAD_DIM))
    ones_r, zeros_r = jnp.ones((C, 128), F32), jnp.zeros((C, 128), F32)
    ones_d, zeros_d = ones_r, zeros_r

    lb_cum = jnp.cumsum(jax.nn.softmax(hgrn_lb.astype(F32), axis=0), axis=0)
    lower = lb_cum - lb_cum[0:1]

    cvec = jnp.zeros((8, D), F32).at[:B].set(c).at[B].set(c_ctx)
    mods = _modulation(cvec, mod_w, mod_b)

    head_id = jnp.arange(REC_W) // HDIM
    seg = (head_id[:, None] == head_id[None, :]).astype(BF16)

    xf = x.reshape(B * S, D)
    xc = ctx.reshape(B * C, D)
    zero_state = jnp.zeros((B, 2, HEADS, HDIM, HDIM), F32)

    for layer in range(depth):
        need_ctx = layer < depth - 1
        lam_init = 0.8 - 0.6 * math.exp(-0.3 * layer)
        m6 = mods[layer].reshape(8, 6, D)
        lat = lambda k: m6[:B, k][:, None, :]
        cxm = lambda k: jnp.broadcast_to(m6[B, k][None, None, :], (B, 1, D))
        ng = norm_g[layer]
        lb = lower[layer]
        lbc = jnp.zeros((8, REC_W), F32).at[0].set(jnp.log(lb)).at[1].set(jnp.log1p(-lb)).at[2].set(1.0 - lb)
        w_all = _prep_w_in(w_in[layer])
        log_gamma = jnp.log1p(-jnp.exp2(-ret_decay[layer].astype(F32)))
        lgrow = jnp.zeros((8, REC_W), F32).at[:2].set(jnp.repeat(log_gamma, HDIM, axis=1))
        lamv = da_lambda[layer].astype(F32)
        lam = (jnp.exp(jnp.sum(lamv[0] * lamv[1])) - jnp.exp(jnp.sum(lamv[2] * lamv[3])) + lam_init).reshape(1)
        hn = jnp.tile(hgrn_norm[layer], HEADS)[None, :]
        rn = jnp.tile(ret_norm[layer], HEADS)[None, :]
        dn = jnp.tile(jnp.concatenate([da_subln[layer], jnp.zeros((128 - DA_VDIM,), F32)]), DA_HEADS)[:, None]
        wo = w_out[layer]
        wo_da = jnp.pad(wo[2 * REC_W:].reshape(DA_HEADS, DA_VDIM, D), ((0, 0), (0, 128 - DA_VDIM), (0, 0)))
        w_out_b = jnp.concatenate([wo[:2 * REC_W], wo_da.reshape(DA_X, D)], axis=0).astype(BF16)
        rw = router_w[layer].T
        rb = router_b[layer][:, None]

        (hq, hv, hg, kf, lff, kb, lfb, rq, rk, rv, rg, dq, dk, dvt) = _inproj(
            xf, S, ng[0:1], lat(1), lat(0), w_all, lbc, cosr, sinr, cosd, sind)
        (hq_c, hv_c, hg_c, kf_c, lff_c, kb_c, lfb_c, rq_c, rk_c, rv_c, rg_c, dq_c, dk_c, dvt_c) = _inproj(
            xc, C, ng[0:1], cxm(1), cxm(0), w_all, lbc, ones_r, zeros_r, ones_d, zeros_d)

        hof_c, hob_c, hs = _hgrn_scan(hq_c, hv_c, kf_c, lff_c, kb_c, lfb_c,
                                      jnp.zeros((B, 2, REC_W // 128, 128, 128), F32), seg, B, C)
        hof, hob, _ = _hgrn_scan(hq, hv, kf, lff, kb, lfb, hs, seg, B, S)
        rof_c, rob_c, rs = _ret_scan(rq_c, rk_c, rv_c, zero_state, log_gamma, lgrow, B, C)
        rof, rob, _ = _ret_scan(rq, rk, rv, rs, log_gamma, lgrow, B, S)
        out_scale = 1.0 - lam_init
        da = _diff_attention(lam, dq, dk_c, dvt_c, dk, dvt, dn, B, S, C, S, out_scale, False)

        x1, tok, top_idx_t, gates_t, cnt = _outproj(xf, S, (hof, hob, hg, rof, rob, rg, da), seg, w_out_b, hn, rn,
                                                    ng[1:2], ng[2:3], lat(2), lat(4), lat(3), rw, rb)
        counts = jnp.sum(cnt[:, :, 0], axis=0)
        if need_ctx:
            da_c = _diff_attention(lam, dq_c, dk_c, dvt_c, None, None, dn, B, C, C, C, out_scale, True)
            xc1, tok_c, top_idx_c, gates_c, cnt_c = _outproj(
                xc, C, (hof_c, hob_c, hg_c, rof_c, rob_c, rg_c, da_c), seg, w_out_b, hn, rn, ng[1:2], ng[2:3],
                cxm(2), cxm(4), cxm(3), rw, rb)
            tok = jnp.concatenate([tok, tok_c], axis=0)
            top_idx_t = jnp.concatenate([top_idx_t, top_idx_c], axis=1)
            gates_t = jnp.concatenate([gates_t, gates_c], axis=1)
            counts = counts + jnp.sum(cnt_c[:, :, 0], axis=0)

        n_tok = tok.shape[0]
        gates = gates_t.T
        tok_sorted, posn, block_expert, n_active = _routing(top_idx_t, counts.astype(jnp.int32), bm)
        b1g = b1[layer][:, None, 0::2]
        b1l = b1[layer][:, None, 1::2]
        b2l = b2[layer][:, None, :]
        yg = _expert_ffn(block_expert, n_active, tok, tok_sorted, w1, w2, b1g, b1l, b2l, layer, bm)

        def expert_rows(lo, hi):
            return [yg.at[posn[k, lo:hi]].get(mode="promise_in_bounds") for k in range(TOP_K)]

        xf = _ffn_residual(x1, expert_rows(0, B * S), gates[:B * S], S, ng[3:4], lat(5))
        if need_ctx:
            xc = _ffn_residual(xc1, expert_rows(B * S, n_tok), gates[B * S:], C, ng[3:4], cxm(5))
    return xf.reshape(B, S, D)
```

```python
import functools
import math

import jax
import jax.numpy as jnp
from jax import lax
from jax.experimental import pallas as pl
from jax.experimental.pallas import tpu as pltpu

F32 = jnp.float32
BF16 = jnp.bfloat16

GRID_W = 64
HEADS = 6
HDIM = 64
REC_W = HEADS * HDIM
DA_HEADS = 4
DA_HEAD_DIM = 32
DA_MAPS = 2 * DA_HEADS
DA_W = DA_MAPS * DA_HEAD_DIM
DA_VDIM = 2 * DA_HEAD_DIM
DA_X = DA_HEADS * 128
_Q_SCALE = DA_HEAD_DIM ** -0.5 * math.log2(math.e)
_BOUND_MARGIN = 1.01
_MAX_EXP2_SPAN = 100.0
ROPE_BASE = 10000.0
N_EXPERTS = 32
TOP_K = 4
SWIGLU_ALPHA = 1.702
SWIGLU_LIMIT = 7.0
NORM_EPS = 1e-6

SUB = 16
NEG_BIG = -1e30
VMEM_LIMIT = 56 * 1024 * 1024

_C_HQ, _C_HFF, _C_HFB, _C_HI, _C_HG = 0, 384, 768, 1152, 1536
_C_RQ, _C_RK, _C_RV, _C_RG = 1920, 2304, 2688, 3072
_C_DQ, _C_DK, _C_DV = 3456, 3712, 3968
_C_RQR, _C_RKR, _C_DQR, _C_DKR = 4224, 4608, 4992, 5248
_W_ALL = 5504


def _dot(a, b):
    return jnp.dot(a, b, preferred_element_type=F32)


def _dot_nt(a, b):
    return lax.dot_general(a, b, (((1,), (1,)), ((), ())), preferred_element_type=F32)


def _dot_tn(a, b):
    return lax.dot_general(a, b, (((0,), (0,)), ((), ())), preferred_element_type=F32)


def _sigmoid(x):
    return 1.0 / (1.0 + jnp.exp(-x))


def _rms(x, g):
    ms = jnp.mean(x * x, axis=-1, keepdims=True)
    return x * lax.rsqrt(ms + NORM_EPS) * g


def _params(*sem):
    return pltpu.CompilerParams(dimension_semantics=sem, vmem_limit_bytes=VMEM_LIMIT)


def _mod_kernel(c_ref, w_ref, b_ref, o_ref):
    c = c_ref[...]
    o_ref[0] = _dot(c * _sigmoid(c), w_ref[0]) + b_ref[0]


def _modulation(cvec, mod_w, mod_b):
    depth, d, n = mod_w.shape
    tn = 1536
    return pl.pallas_call(
        _mod_kernel,
        grid=(depth, n // tn),
        in_specs=[
            pl.BlockSpec((8, d), lambda l, j: (0, 0)),
            pl.BlockSpec((1, d, tn), lambda l, j: (l, 0, j)),
            pl.BlockSpec((1, 1, tn), lambda l, j: (l, 0, j)),
        ],
        out_specs=pl.BlockSpec((1, 8, tn), lambda l, j: (l, 0, j)),
        out_shape=jax.ShapeDtypeStruct((depth, 8, n), F32),
        compiler_params=_params("arbitrary", "arbitrary"),
        name="modulation",
    )(cvec, mod_w, mod_b.reshape(depth, 1, n))


def _inproj_kernel(x_ref, g_ref, sc_ref, sh_ref, w_ref, lbc_ref, cosr_ref, sinr_ref, cosd_ref, sind_ref,
                   hq_ref, hv_ref, hg_ref, kf_ref, lff_ref, kb_ref, lfb_ref,
                   rq_ref, rk_ref, rv_ref, rg_ref, dq_ref, dk_ref, dvt_ref):
    x = x_ref[...]
    h = _rms(x, g_ref[...]) * (1.0 + sc_ref[0]) + sh_ref[0]
    hb = h.astype(BF16)

    wide = {}

    def proj(c0, n):
        lo = max(b for b in (0, _C_RQ, _C_DQ) if b <= c0)
        hi = {0: _C_RQ, _C_RQ: _C_DQ, _C_DQ: _W_ALL}[lo]
        if lo not in wide:
            wide[lo] = _dot(hb, w_ref[:, lo:hi])
        return wide[lo][:, c0 - lo:c0 - lo + n]

    hq_ref[...] = proj(_C_HQ, REC_W).astype(BF16)
    hv_ref[...] = proj(_C_HI, REC_W).astype(BF16)
    hg_ref[...] = proj(_C_HG, REC_W).astype(BF16)

    log_lb = lbc_ref[0:1, :]
    log_1m = lbc_ref[1:2, :]
    one_m = lbc_ref[2:3, :]

    def gates(z):
        log_sig = jnp.minimum(z, 0.0) - jnp.log1p(jnp.exp(-jnp.abs(z)))
        t = log_1m + log_sig
        m = jnp.maximum(log_lb, t)
        logf = m + jnp.log1p(jnp.exp(-jnp.abs(log_lb - t)))
        return one_m / (1.0 + jnp.exp(z)), logf

    k, lf = gates(proj(_C_HFF, REC_W))
    kf_ref[...] = k.astype(BF16)
    lff_ref[...] = lf
    k, lf = gates(proj(_C_HFB, REC_W))
    kb_ref[...] = k.astype(BF16)
    lfb_ref[...] = lf

    cosr = jnp.concatenate([cosr_ref[...]] * (REC_W // 128), axis=1)
    sinr = jnp.concatenate([sinr_ref[...]] * (REC_W // 128), axis=1)
    rq_ref[...] = (proj(_C_RQ, REC_W) * cosr + proj(_C_RQR, REC_W) * sinr).astype(BF16)
    rk_ref[...] = ((proj(_C_RK, REC_W) * cosr + proj(_C_RKR, REC_W) * sinr) * (HDIM ** -0.5)).astype(BF16)
    rv_ref[...] = proj(_C_RV, REC_W).astype(BF16)
    rg_ref[...] = proj(_C_RG, REC_W).astype(BF16)

    cosd = jnp.concatenate([cosd_ref[...]] * (DA_W // 128), axis=1)
    sind = jnp.concatenate([sind_ref[...]] * (DA_W // 128), axis=1)
    dq = ((proj(_C_DQ, DA_W) * cosd + proj(_C_DQR, DA_W) * sind) * _Q_SCALE).astype(BF16)
    dk = (proj(_C_DK, DA_W) * cosd + proj(_C_DKR, DA_W) * sind).astype(BF16)
    dv_t = proj(_C_DV, DA_W).T.astype(BF16)
    tm = dq.shape[0]
    zero_col = jnp.zeros((tm, 64), BF16)
    ones_col = jnp.where(lax.broadcasted_iota(jnp.int32, (tm, 64), 1) == 0, 1.0, 0.0).astype(BF16)
    sub = lax.broadcasted_iota(jnp.int32, (64, tm), 0)
    ones_row = jnp.where(sub == 0, 1.0, 0.0).astype(BF16)
    for h_ in range(DA_HEADS):
        lo, mid, hi_ = 128 * h_, 128 * h_ + 64, 128 * h_ + 128
        dq_ref[:, lo:mid] = dq[:, 64 * h_:64 * h_ + 64]
        dq_ref[:, mid:hi_] = zero_col
        dk_ref[:, lo:mid] = dk[:, 64 * h_:64 * h_ + 64]
        dk_ref[:, mid:hi_] = ones_col
        dvt_ref[0, lo:mid, :] = dv_t[64 * h_:64 * h_ + 64, :]
        dvt_ref[0, mid:hi_, :] = ones_row


def _inproj(xf, seq, gnorm, sc, sh, w_all, lbc, cosr, sinr, cosd, sind):
    rows, d = xf.shape
    tm = min(256, seq)
    nb = seq // tm
    row_spec = lambda w: pl.BlockSpec((tm, w), lambda i: (i, 0))
    tab_spec = lambda w: pl.BlockSpec((tm, w), lambda i: (i % nb, 0))
    mod_spec = pl.BlockSpec((1, 1, d), lambda i: (i // nb, 0, 0))
    widths = [REC_W] * 11 + [DA_X, DA_X]
    dtypes = [BF16, BF16, BF16, BF16, F32, BF16, F32, BF16, BF16, BF16, BF16, BF16, BF16]
    out_specs = [row_spec(w) for w in widths] + [pl.BlockSpec((1, DA_X, tm), lambda i: (i // nb, 0, i % nb))]
    out_shape = ([jax.ShapeDtypeStruct((rows, w), dt) for w, dt in zip(widths, dtypes)]
                 + [jax.ShapeDtypeStruct((rows // seq, DA_X, seq), BF16)])
    return pl.pallas_call(
        _inproj_kernel,
        grid=(rows // tm,),
        in_specs=[
            row_spec(d),
            pl.BlockSpec((1, d), lambda i: (0, 0)),
            mod_spec, mod_spec,
            pl.BlockSpec((d, _W_ALL), lambda i: (0, 0)),
            pl.BlockSpec((8, REC_W), lambda i: (0, 0)),
            tab_spec(128), tab_spec(128), tab_spec(128), tab_spec(128),
        ],
        out_specs=out_specs,
        out_shape=out_shape,
        compiler_params=_params("arbitrary"),
        name="inproj",
    )(xf, gnorm, sc, sh, w_all, lbc, cosr, sinr, cosd, sind)


def _hgrn_dir(q, k, v, lf, seg, st_ref, inter_ref, intra_ref, reverse):
    tt = q.shape[0]
    nblk = tt // SUB
    row = lax.broadcasted_iota(jnp.int32, (tt, tt), 0)
    col = lax.broadcasted_iota(jnp.int32, (tt, tt), 1)
    same = (row // SUB) == (col // SUB)
    tri = (col >= row) if reverse else (col <= row)
    l_all = jnp.where(same, 1.0, 0.0)
    l_cum = jnp.where(tri, l_all, 0.0).astype(BF16)
    l_all = l_all.astype(BF16)
    lf = lf * math.log2(math.e)
    hi = lf.astype(BF16)
    r1 = lf - hi.astype(F32)
    mid = r1.astype(BF16)
    lo = (r1 - mid.astype(F32)).astype(BF16)
    b = _dot(l_cum, hi) + _dot(l_cum, mid) + _dot(l_cum, lo)
    btot = _dot(l_all, hi) + _dot(l_all, mid) + _dot(l_all, lo)

    qe = (q * jnp.exp2(b)).astype(BF16)
    kd = (k * jnp.exp2(btot - b)).astype(BF16)
    dec = jnp.exp2(btot)
    v_t = v.T
    tok_blk = lax.broadcasted_iota(jnp.int32, (128, tt), 1) // SUB
    hr = lax.broadcasted_iota(jnp.int32, (128, 128), 0) // HDIM
    hc = lax.broadcasted_iota(jnp.int32, (128, 128), 1) // HDIM
    same_head = hr == hc
    order = range(nblk - 1, -1, -1) if reverse else range(nblk)
    half = lax.broadcasted_iota(jnp.int32, (nblk, 8, 128), 1)
    seg2 = seg[0:128, 0:128]

    def carried(c, g):
        v_tg = v_t[c, :]
        quarter = max(nblk // 4, 1)
        upd = []
        for j0 in range(0, nblk, quarter):
            v_stack = jnp.concatenate([jnp.where(tok_blk == j, v_tg, 0.0).astype(BF16)
                                       for j in range(j0, j0 + quarter)], axis=0)
            part = _dot(v_stack, kd[:, c])
            upd += [part[128 * j:128 * (j + 1)] for j in range(quarter)]
        st = st_ref[g]
        for j in order:
            r0 = j * SUB
            inter_ref[r0:r0 + SUB, c] = _dot_nt(qe[r0:r0 + SUB, c], st.astype(BF16))
            st = st * dec[r0:r0 + 1, c] + jnp.where(same_head, upd[j], 0.0)
        st_ref[g] = st

    def within(c):
        def halves(a):
            a4 = a[:, c].reshape(nblk, 2, 8, 128)
            return (a4[:, 1], a4[:, 0]) if reverse else (a4[:, 0], a4[:, 1])

        (q0, q1), (k0, k1), (v0, v1), (b0, b1) = halves(q), halves(k), halves(v), halves(b)

        def rotated(r, *arrays):
            sh = ((8 - r) % 8) if reverse else r
            return [pltpu.roll(a, sh, 1) if sh else a for a in arrays]

        def pairs(qa, ba, kbv, r, masked):
            kb_, bb_, vb_ = kbv
            diff = ba - bb_
            if masked and r:
                diff = jnp.where((half <= 7 - r) if reverse else (half >= r), diff, NEG_BIG)
            term = (qa * kb_ * jnp.exp2(diff)).reshape(nblk * 8, 128).astype(BF16)
            return _dot(term, seg2).reshape(nblk, 8, 128) * vb_

        o0 = o1 = None
        for r in range(8):
            early, late = rotated(r, k0, b0, v0), rotated(r, k1, b1, v1)
            t0 = pairs(q0, b0, early, r, True)
            t1 = pairs(q1, b1, late, r, True) + pairs(q1, b1, early, r, False)
            o0, o1 = (t0, t1) if r == 0 else (o0 + t0, o1 + t1)
        first, second = (o1, o0) if reverse else (o0, o1)
        intra_ref[:, 0, :, c] = first
        intra_ref[:, 1, :, c] = second

    for g in range(REC_W // 128):
        c = slice(128 * g, 128 * (g + 1))
        carried(c, g)
        within(c)
    return intra_ref[...].reshape(tt, REC_W) + inter_ref[...]


def _hgrn_kernel(qf_ref, vf_ref, kf_ref, lff_ref, qb_ref, vb_ref, kb_ref, lfb_ref, s0_ref, seg_ref,
                 of_ref, ob_ref, sout_ref, st_ref, inter_ref, intra_ref):
    i = pl.program_id(1)

    @pl.when(i == 0)
    def _():
        st_ref[...] = s0_ref[0]

    seg = seg_ref[...]
    of_ref[...] = _hgrn_dir(qf_ref[...].astype(F32), kf_ref[...].astype(F32), vf_ref[...].astype(F32),
                            lff_ref[...], seg, st_ref.at[0], inter_ref, intra_ref, False).astype(BF16)
    ob_ref[...] = _hgrn_dir(qb_ref[...].astype(F32), kb_ref[...].astype(F32), vb_ref[...].astype(F32),
                            lfb_ref[...], seg, st_ref.at[1], inter_ref, intra_ref, True).astype(BF16)

    @pl.when(i == pl.num_programs(1) - 1)
    def _():
        sout_ref[0] = st_ref[...]


def _hgrn_scan(hq, hv, kf, lff, kb, lfb, s0, seg, batch, seq):
    tt = min(256, seq)
    n = seq // tt
    fwd = pl.BlockSpec((tt, REC_W), lambda b, i: (b * n + i, 0))
    bwd = pl.BlockSpec((tt, REC_W), lambda b, i: (b * n + n - 1 - i, 0))
    st_spec = pl.BlockSpec((1, 2, REC_W // 128, 128, 128), lambda b, i: (b, 0, 0, 0, 0))
    rows = batch * seq
    return pl.pallas_call(
        _hgrn_kernel,
        grid=(batch, n),
        in_specs=[fwd, fwd, fwd, fwd, bwd, bwd, bwd, bwd, st_spec,
                  pl.BlockSpec((REC_W, REC_W), lambda b, i: (0, 0))],
        out_specs=[fwd, bwd, st_spec],
        out_shape=[jax.ShapeDtypeStruct((rows, REC_W), BF16), jax.ShapeDtypeStruct((rows, REC_W), BF16),
                   jax.ShapeDtypeStruct(s0.shape, F32)],
        scratch_shapes=[pltpu.VMEM((2, REC_W // 128, 128, 128), F32), pltpu.VMEM((tt, REC_W), F32),
                        pltpu.VMEM((tt // SUB, 2, 8, REC_W), F32)],
        compiler_params=_params("arbitrary", "arbitrary"),
        name="hgrn_scan",
    )(hq, hv, kf, lff, hq, hv, kb, lfb, s0, seg)


def _ret_kernel(lg_ref, qf_ref, kf_ref, vf_ref, qb_ref, kb_ref, vb_ref, s0_ref, lgrow_ref,
                of_ref, ob_ref, sout_ref, st_ref, dmask_ref, tab_ref, o_scr):
    i = pl.program_id(1)
    tt = qf_ref.shape[0]

    @pl.when(i == 0)
    def _():
        st_ref[...] = s0_ref[0]
        row = lax.broadcasted_iota(jnp.int32, (tt, tt), 0)
        col = lax.broadcasted_iota(jnp.int32, (tt, tt), 1)
        dist = (row - col).astype(F32)
        for h in range(HEADS):
            fw = jnp.where(dist >= 0, jnp.exp(dist * lg_ref[0, h]), 0.0)
            bw = jnp.where(dist <= 0, jnp.exp(-dist * lg_ref[1, h]), 0.0)
            dmask_ref[h] = fw + bw
        pos = lax.broadcasted_iota(jnp.int32, (tt, REC_W), 0).astype(F32)
        lgf = lgrow_ref[0:1, :]
        lgb = lgrow_ref[1:2, :]
        tab_ref[0] = jnp.exp((pos + 1.0) * lgf)
        tab_ref[1] = jnp.exp((tt - 1.0 - pos) * lgf)
        tab_ref[2] = jnp.exp((tt - pos) * lgb)
        tab_ref[3] = jnp.exp(pos * lgb)

    tile_f = jnp.exp(tt * lgrow_ref[0:1, :])
    tile_b = jnp.exp(tt * lgrow_ref[1:2, :])

    q = qf_ref[...]
    k = kf_ref[...]
    v = vf_ref[...]
    qe = (q.astype(F32) * tab_ref[0]).astype(BF16)
    ke = (k.astype(F32) * tab_ref[1]).astype(BF16)
    for h in range(HEADS):
        c = slice(h * HDIM, (h + 1) * HDIM)
        s = _dot_nt(q[:, c], k[:, c]) * dmask_ref[h]
        st = st_ref[0, h]
        o_scr[:, c] = _dot(s.astype(BF16), v[:, c]) + _dot(qe[:, c], st.astype(BF16))
        st_ref[0, h] = st * tile_f[:, c] + _dot_tn(ke[:, c], v[:, c])
    of_ref[...] = o_scr[...].astype(BF16)

    q = qb_ref[...]
    k = kb_ref[...]
    v = vb_ref[...]
    qe = (q.astype(F32) * tab_ref[2]).astype(BF16)
    ke = (k.astype(F32) * tab_ref[3]).astype(BF16)
    for h in range(HEADS):
        c = slice(h * HDIM, (h + 1) * HDIM)
        st = st_ref[1, h]
        o_scr[:, c] = _dot(qe[:, c], st.astype(BF16))
        st_ref[1, h] = st * tile_b[:, c] + _dot_tn(ke[:, c], v[:, c])
    ob_ref[...] = o_scr[...].astype(BF16)

    @pl.when(i == pl.num_programs(1) - 1)
    def _():
        sout_ref[0] = st_ref[...]


def _ret_scan(rq, rk, rv, s0, lg, lgrow, batch, seq):
    tt = min(256, seq)
    n = seq // tt
    fwd = pl.BlockSpec((tt, REC_W), lambda b, i: (b * n + i, 0))
    bwd = pl.BlockSpec((tt, REC_W), lambda b, i: (b * n + n - 1 - i, 0))
    st_spec = pl.BlockSpec((1, 2, HEADS, HDIM, HDIM), lambda b, i: (b, 0, 0, 0, 0))
    rows = batch * seq
    return pl.pallas_call(
        _ret_kernel,
        grid=(batch, n),
        in_specs=[pl.BlockSpec(memory_space=pltpu.SMEM), fwd, fwd, fwd, bwd, bwd, bwd, st_spec,
                  pl.BlockSpec((8, REC_W), lambda b, i: (0, 0))],
        out_specs=[fwd, bwd, st_spec],
        out_shape=[jax.ShapeDtypeStruct((rows, REC_W), BF16), jax.ShapeDtypeStruct((rows, REC_W), BF16),
                   jax.ShapeDtypeStruct(s0.shape, F32)],
        scratch_shapes=[pltpu.VMEM((2, HEADS, HDIM, HDIM), F32), pltpu.VMEM((HEADS, tt, tt), F32),
                        pltpu.VMEM((4, tt, REC_W), F32), pltpu.VMEM((tt, REC_W), F32)],
        compiler_params=_params("arbitrary", "arbitrary"),
        name="ret_scan",
    )(lg, rq, rk, rv, rq, rk, rv, s0, lgrow)


def _attn_kernel(lam_ref, q_ref, kc_ref, vct_ref, k_ref, vt_ref, gain_ref, o_ref, s_ref, acc_ref, kmax_ref, *,
                 with_latent, kchunk, piece, fast_chunk, fast_unroll, out_scale):
    tq = q_ref.shape[0]
    q = q_ref[...]
    lane = lax.broadcasted_iota(jnp.int32, (tq, 128), 1)
    zero = jnp.zeros_like(q)
    qs = jnp.concatenate([jnp.where(lane < DA_HEAD_DIM, q, zero),
                          jnp.where(lane >= DA_HEAD_DIM, q, zero)], axis=0)
    nchunks = (k_ref.shape[0] // kchunk) if with_latent else 0

    def scores(kblk):
        return _dot_nt(kblk, qs)

    def absorb(s, vtblk, m, acc):
        n = s.shape[0]
        part = jnp.max(s.reshape(n // 128, 128, 2 * tq), axis=0) if n > 128 else s
        m_new = jnp.maximum(m, jnp.max(part, axis=0, keepdims=True))
        p = jnp.exp2((s - m_new).astype(BF16))
        return m_new, jnp.exp2(m - m_new) * acc + _dot(vtblk, p)

    def latent_k(c):
        return k_ref[pl.ds(pl.multiple_of(c * kchunk, kchunk), kchunk), :]

    def latent_vt(c):
        return vt_ref[0, :, pl.ds(pl.multiple_of(c * kchunk, kchunk), kchunk)]

    sel_r = lax.broadcasted_iota(jnp.int32, (128, 128), 0) // DA_HEAD_DIM
    sel_c = lax.broadcasted_iota(jnp.int32, (128, 128), 1)
    sel = jnp.where(sel_r == sel_c, 1.0, 0.0).astype(BF16)

    def key_norm2(kblk):
        kf = kblk.astype(F32)
        return jnp.max(_dot((kf * kf).astype(BF16), sel), axis=0, keepdims=True)

    @pl.when(pl.program_id(2) == 0)
    def _():
        km = key_norm2(kc_ref[...])
        if with_latent:
            km = lax.fori_loop(0, nchunks, lambda c, a: jnp.maximum(a, key_norm2(latent_k(c))), km)
        kmax_ref[...] = jnp.broadcast_to(km, kmax_ref.shape)

    qf32 = q.astype(F32)
    qn2 = _dot((qf32 * qf32).astype(BF16), sel)
    bound = jnp.sqrt(qn2 * kmax_ref[0:1, :]) * _BOUND_MARGIN
    shift = jnp.concatenate([bound[:, 0:1], bound[:, 1:2]], axis=0)
    use_bound = 2.0 * jnp.max(shift) <= _MAX_EXP2_SPAN

    @pl.when(use_bound)
    def _():
        lane2 = lax.broadcasted_iota(jnp.int32, (2 * tq, 128), 1)
        qf = jnp.where(lane2 == DA_VDIM, (-shift).astype(BF16), qs)

        def absorb_shifted(kblk, vtblk):
            acc_ref[...] += _dot(vtblk, jnp.exp2(_dot_nt(kblk, qf)).astype(BF16))

        acc_ref[...] = jnp.zeros(acc_ref.shape, F32)
        absorb_shifted(kc_ref[...], vct_ref[0])
        if with_latent:
            def body(c, carry):
                c0 = pl.multiple_of(c * fast_chunk, fast_chunk)
                absorb_shifted(k_ref[pl.ds(c0, fast_chunk), :], vt_ref[0, :, pl.ds(c0, fast_chunk)])
                return carry

            lax.fori_loop(0, k_ref.shape[0] // fast_chunk, body, 0, unroll=fast_unroll)

    @pl.when(jnp.logical_not(use_bound))
    def _():
        m, acc = absorb(scores(kc_ref[...]), vct_ref[0], jnp.full((1, 2 * tq), NEG_BIG, F32),
                        jnp.zeros((128, 2 * tq), F32))
        if with_latent:
            pk = min(piece, kchunk)
            npieces = kchunk // pk

            def col_max(x):
                if pk > 128:
                    x = jnp.max(x.reshape(pk // 128, 128, 2 * tq), axis=0)
                return jnp.max(x.reshape(x.shape[0] // 8, 8, 2 * tq), axis=0)

            def fused(cur, c, m_, acc_, nxt):
                part = jnp.full((8, 2 * tq), NEG_BIG, F32)
                for t in range(npieces):
                    if nxt is not None:
                        r0 = pl.multiple_of((c + 1) * kchunk + t * pk, pk)
                        s_new = scores(k_ref[pl.ds(r0, pk), :])
                        s_ref[nxt, t * pk:(t + 1) * pk, :] = s_new
                        part = jnp.maximum(part, col_max(s_new))
                    c0 = pl.multiple_of(c * kchunk + t * pk, pk)
                    p = jnp.exp2((s_ref[cur, t * pk:(t + 1) * pk, :] - m_).astype(BF16))
                    acc_ = acc_ + _dot(vt_ref[0, :, pl.ds(c0, pk)], p)
                if nxt is None:
                    return m_, acc_
                m_new = jnp.maximum(m_, jnp.max(part, axis=0, keepdims=True))
                return m_new, acc_ * jnp.exp2(m_ - m_new)

            s0 = scores(latent_k(0))
            s_ref[0] = s0
            m0 = jnp.maximum(m, jnp.max(jnp.max(s0.reshape(kchunk // 128, 128, 2 * tq), axis=0), axis=0,
                                        keepdims=True))
            acc = acc * jnp.exp2(m - m0)
            m = m0

            def pair(c, carry, last):
                m_, acc_ = fused(0, c, *carry, 1)
                return fused(1, c + 1, m_, acc_, None if last else 0)

            if nchunks > 2:
                m, acc = lax.fori_loop(0, nchunks // 2 - 1, lambda j, cr: pair(2 * j, cr, False), (m, acc))
            m, acc = pair(nchunks - 2, (m, acc), True)
        acc_ref[...] = acc

    acc = acc_ref[...]
    a1 = acc[:, :tq]
    a2 = acc[:, tq:]
    o = a1 / a1[DA_VDIM:DA_VDIM + 1, :] - lam_ref[0] * (a2 / a2[DA_VDIM:DA_VDIM + 1, :])
    row = lax.broadcasted_iota(jnp.int32, (128, tq), 0)
    o = jnp.where(row < DA_VDIM, o, 0.0)
    ms = jnp.sum(o * o, axis=0, keepdims=True) * (1.0 / DA_VDIM)
    o_ref[...] = (o * lax.rsqrt(ms + NORM_EPS) * gain_ref[...] * out_scale).T.astype(BF16)


def _diff_attention(lam, q, kc, vc, k, v, gain, batch, seq_q, ctx_len, seq_k, out_scale, q_is_ctx):
    tq = min(512, seq_q)
    nq = seq_q // tq
    with_latent = not q_is_ctx
    if with_latent:
        kchunk = min(2048, seq_k // 2)
        assert seq_k % (2 * kchunk) == 0
    else:
        k, v = kc, vc
        seq_k, kchunk = ctx_len, ctx_len
    fast_chunk = kchunk
    fast_unroll = 2 if (seq_k // fast_chunk) % 2 == 0 else 1
    kern = functools.partial(_attn_kernel, with_latent=with_latent, kchunk=kchunk, piece=512, fast_chunk=fast_chunk,
                             fast_unroll=fast_unroll, out_scale=out_scale)
    return pl.pallas_call(
        kern,
        grid=(batch, DA_HEADS, nq),
        in_specs=[
            pl.BlockSpec(memory_space=pltpu.SMEM),
            pl.BlockSpec((tq, 128), lambda b, h, i: (b * nq + i, h)),
            pl.BlockSpec((ctx_len, 128), lambda b, h, i: (b, h)),
            pl.BlockSpec((1, 128, ctx_len), lambda b, h, i: (b, h, 0)),
            pl.BlockSpec((seq_k, 128), lambda b, h, i: (b, h)),
            pl.BlockSpec((1, 128, seq_k), lambda b, h, i: (b, h, 0)),
            pl.BlockSpec((128, 1), lambda b, h, i: (h, 0)),
        ],
        out_specs=pl.BlockSpec((tq, 128), lambda b, h, i: (b * nq + i, h)),
        out_shape=jax.ShapeDtypeStruct((batch * seq_q, DA_X), BF16),
        scratch_shapes=[pltpu.VMEM((2, kchunk, 2 * tq), F32), pltpu.VMEM((128, 2 * tq), F32),
                        pltpu.VMEM((8, 128), F32)],
        compiler_params=_params("arbitrary", "arbitrary", "arbitrary"),
        name="diff_attention",
    )(lam, q, kc, vc, k, v, gain)


def _outproj_kernel(x_ref, hof_ref, hob_ref, hg_ref, rof_ref, rob_ref, rg_ref, da_ref, seg_ref, w_ref,
                    hn_ref, rn_ref, n1_ref, n2_ref, g1_ref, sc2_ref, sh2_ref, rw_ref, rb_ref,
                    x1_ref, tok_ref, idx_ref, gate_ref, cnt_ref):
    seg = seg_ref[...]

    def gated_head_norm(o, gain, gate):
        ms = _dot((o * o).astype(BF16), seg) * (1.0 / HDIM)
        return (o * lax.rsqrt(ms + NORM_EPS) * gain * (gate * _sigmoid(gate))).astype(BF16)

    a = gated_head_norm(hof_ref[...].astype(F32) + hob_ref[...].astype(F32), hn_ref[...], hg_ref[...].astype(F32))
    b = gated_head_norm(rof_ref[...].astype(F32) + rob_ref[...].astype(F32), rn_ref[...], rg_ref[...].astype(F32))
    y = (_dot(a, w_ref[0:REC_W, :]) + _dot(b, w_ref[REC_W:2 * REC_W, :])
         + _dot(da_ref[...], w_ref[2 * REC_W:2 * REC_W + DA_X, :]))
    x1 = x_ref[...] + g1_ref[0] * _rms(y, n1_ref[...])
    x1_ref[...] = x1
    tok = _rms(x1, n2_ref[...]) * (1.0 + sc2_ref[0]) + sh2_ref[0]
    tok_ref[...] = tok.astype(BF16)
    logits = _dot_nt(rw_ref[...], tok) + rb_ref[...]
    eid = lax.broadcasted_iota(jnp.int32, logits.shape, 0).astype(F32)
    vals, ids = [], []
    for _ in range(TOP_K):
        best = jnp.max(logits, axis=0, keepdims=True)
        first = jnp.min(jnp.where(logits == best, eid, float(N_EXPERTS)), axis=0, keepdims=True)
        vals.append(best)
        ids.append(first)
        logits = jnp.where(eid == first, -jnp.inf, logits)
    ex = [jnp.exp(v - vals[0]) for v in vals]
    total = ex[0] + ex[1] + ex[2] + ex[3]
    idx_ref[...] = jnp.concatenate(ids, axis=0).astype(jnp.int32)
    gate_ref[...] = jnp.concatenate(ex, axis=0) / total
    hits = sum((eid == first).astype(F32) for first in ids)
    cnt_ref[0] = jnp.broadcast_to(jnp.sum(hits, axis=1, keepdims=True), cnt_ref.shape[1:])


def _outproj(xf, seq, streams, seg, w_out, hn, rn, n1, n2, g1, sc2, sh2, rw, rb):
    rows, d = xf.shape
    tm = min(256, seq)
    nb = seq // tm
    row_spec = lambda w: pl.BlockSpec((tm, w), lambda i: (i, 0))
    full = lambda a: pl.BlockSpec(a.shape, lambda i: (0,) * a.ndim)
    mod_spec = pl.BlockSpec((1, 1, d), lambda i: (i // nb, 0, 0))
    hof, hob, hg, rof, rob, rg, da = streams
    return pl.pallas_call(
        _outproj_kernel,
        grid=(rows // tm,),
        in_specs=[row_spec(d)] + [row_spec(REC_W)] * 6 + [row_spec(DA_X), full(seg), full(w_out),
                  full(hn), full(rn), full(n1), full(n2), mod_spec, mod_spec, mod_spec, full(rw), full(rb)],
        out_specs=[row_spec(d), row_spec(d), pl.BlockSpec((TOP_K, tm), lambda i: (0, i)),
                   pl.BlockSpec((TOP_K, tm), lambda i: (0, i)),
                   pl.BlockSpec((1, N_EXPERTS, 128), lambda i: (i, 0, 0))],
        out_shape=[jax.ShapeDtypeStruct((rows, d), F32), jax.ShapeDtypeStruct((rows, d), BF16),
                   jax.ShapeDtypeStruct((TOP_K, rows), jnp.int32), jax.ShapeDtypeStruct((TOP_K, rows), F32),
                   jax.ShapeDtypeStruct((rows // tm, N_EXPERTS, 128), F32)],
        compiler_params=_params("arbitrary"),
        name="outproj",
    )(xf, hof, hob, hg, rof, rob, rg, da, seg, w_out, hn, rn, n1, n2, g1, sc2, sh2, rw, rb)


def _ffn_kernel(be_ref, na_ref, x_ref, w1_ref, w2_ref, b1g_ref, b1l_ref, b2_ref, pe_ref, po_ref, yprev_ref, y_ref,
                w1g_s, w1l_s, w2_s, *, blk0):
    del yprev_ref
    i = pl.program_id(0)
    blk = i + blk0
    active = blk < na_ref[0]
    fresh = jnp.logical_or(i == 0, be_ref[blk] != be_ref[jnp.maximum(blk - 1, 0)])

    @pl.when(fresh)
    def _():
        n = w1_ref.shape[3]
        for c in range(n // 256):
            wb = w1_ref[0, 0, :, 256 * c:256 * (c + 1)].astype(BF16)
            w1g_s[:, 128 * c:128 * (c + 1)] = _dot(wb, pe_ref[...]).astype(BF16)
            w1l_s[:, 128 * c:128 * (c + 1)] = _dot(wb, po_ref[...]).astype(BF16)
        w2_s[...] = w2_ref[0, 0].astype(BF16)

    @pl.when(jnp.logical_not(active))
    def _():
        y_ref[...] = jnp.zeros(y_ref.shape, y_ref.dtype)

    @pl.when(active)
    def _():
        x = x_ref[...]
        glu = jnp.minimum(_dot(x, w1g_s[...]) + b1g_ref[0], SWIGLU_LIMIT)
        lin = jnp.clip(_dot(x, w1l_s[...]) + b1l_ref[0], -SWIGLU_LIMIT, SWIGLU_LIMIT)
        act = glu * _sigmoid(SWIGLU_ALPHA * glu) * (lin + 1.0)
        y_ref[...] = (_dot(act.astype(BF16), w2_s[...]) + b2_ref[0]).astype(y_ref.dtype)


def _expert_ffn(block_expert, n_active, tok, tok_sorted, w1, w2, b1g, b1l, b2, layer, bm, nparts=4):
    p = tok_sorted.shape[0]
    d = tok.shape[1]
    f = w2.shape[2]
    nblocks = p // bm
    src = jnp.arange(256)[:, None]
    dst = jnp.arange(128)[None, :]
    pe = (src == 2 * dst).astype(BF16)
    po = (src == 2 * dst + 1).astype(BF16)
    sel = pl.BlockSpec((256, 128), lambda i, be, na: (0, 0))
    bounds = [nblocks * f // 16 for f in (0, 1, 3, 7, 11, 16)][:nparts + 2]
    nparts = len(bounds) - 1
    yg = jnp.zeros((8, 128), BF16)
    for j in range(nparts):
        b0, nb = bounds[j], bounds[j + 1] - bounds[j]
        xg = tok.at[tok_sorted[b0 * bm:(b0 + nb) * bm]].get(mode="promise_in_bounds")
        wspec = lambda s, b0=b0: pl.BlockSpec((1, 1) + s, lambda i, be, na: (layer, be[i + b0], 0, 0))
        bspec = lambda s, b0=b0: pl.BlockSpec((1,) + s, lambda i, be, na: (be[i + b0], 0, 0))
        yg = pl.pallas_call(
            functools.partial(_ffn_kernel, blk0=b0),
            grid_spec=pltpu.PrefetchScalarGridSpec(
                num_scalar_prefetch=2,
                grid=(nb,),
                in_specs=[pl.BlockSpec((bm, d), lambda i, be, na: (i, 0)),
                          wspec((d, 2 * f)), wspec((f, d)), bspec((1, f)), bspec((1, f)), bspec((1, d)), sel, sel,
                          pl.BlockSpec(memory_space=pl.ANY)],
                out_specs=pl.BlockSpec((bm, d), lambda i, be, na, b0=b0: (i + b0, 0)),
                scratch_shapes=[pltpu.VMEM((d, f), BF16), pltpu.VMEM((d, f), BF16), pltpu.VMEM((f, d), BF16)],
            ),
            out_shape=jax.ShapeDtypeStruct((p, d), BF16),
            input_output_aliases={10: 0} if j else {},
            compiler_params=_params("arbitrary"),
            name="expert_ffn",
        )(block_expert, n_active, xg, w1, w2, b1g, b1l, b2, pe, po, yg)
    return yg


def _resid_kernel(x_ref, y0_ref, y1_ref, y2_ref, y3_ref, gate_ref, n_ref, g_ref, o_ref):
    gates = gate_ref[...]
    f = y0_ref[...].astype(F32) * gates[:, 0:1]
    for k, y_ref in enumerate((y1_ref, y2_ref, y3_ref), start=1):
        f = f + y_ref[...].astype(F32) * gates[:, k:k + 1]
    o_ref[...] = x_ref[...] + g_ref[0] * _rms(f, n_ref[...])


def _ffn_residual(xf, ys, gates, seq, n3, g2):
    rows, d = xf.shape
    tm = min(512, seq)
    nb = seq // tm
    row_spec = pl.BlockSpec((tm, d), lambda i: (i, 0))
    return pl.pallas_call(
        _resid_kernel,
        grid=(rows // tm,),
        in_specs=[row_spec] * 5 + [pl.BlockSpec((tm, TOP_K), lambda i: (i, 0)), pl.BlockSpec((1, d), lambda i: (0, 0)),
                                   pl.BlockSpec((1, 1, d), lambda i: (i // nb, 0, 0))],
        out_specs=row_spec,
        out_shape=jax.ShapeDtypeStruct((rows, d), F32),
        compiler_params=_params("arbitrary"),
        name="ffn_residual",
    )(xf, *ys, gates, n3, g2)


def _rope_tables(pos, dim):
    inv = 1.0 / (ROPE_BASE ** (jnp.arange(0, dim, 2, dtype=F32) / dim))
    ang = pos.astype(F32)[:, None] * inv[None, :]
    return jnp.cos(ang), jnp.sin(ang)


def _rot_cols(w, head_dim, halves):
    d, n = w.shape
    g = head_dim // halves
    w4 = w.reshape(d, n // g, 2, g // 2)
    return jnp.concatenate([-w4[:, :, 1], w4[:, :, 0]], axis=-1).reshape(d, n)


def _prep_w_in(w):
    rq, rk = w[:, _C_RQ:_C_RQ + REC_W], w[:, _C_RK:_C_RK + REC_W]
    dq, dk = w[:, _C_DQ:_C_DQ + DA_W], w[:, _C_DK:_C_DK + DA_W]
    return jnp.concatenate([w, _rot_cols(rq, HDIM, 1), _rot_cols(rk, HDIM, 1),
                            _rot_cols(dq, DA_HEAD_DIM, 2), _rot_cols(dk, DA_HEAD_DIM, 2)], axis=1).astype(BF16)


def _assign_kernel(ids_ref, first_ref, upper_ref, pos_ref, seen_ref):
    @pl.when(pl.program_id(0) == 0)
    def _():
        seen_ref[...] = jnp.zeros(seen_ref.shape, F32)

    ids = ids_ref[...].astype(F32)
    tm = ids.shape[1]
    eid = lax.broadcasted_iota(jnp.int32, (N_EXPERTS, tm), 0).astype(F32)
    hit = [eid == ids[k:k + 1, :] for k in range(TOP_K)]
    hits = sum(h.astype(F32) for h in hit)
    earlier = _dot(hits.astype(BF16), upper_ref[...])
    base = earlier + seen_ref[:, 0:1] + first_ref[...]
    pos_ref[...] = jnp.concatenate([jnp.sum(jnp.where(h, base, 0.0), axis=0, keepdims=True) for h in hit],
                                   axis=0).astype(jnp.int32)
    seen_ref[...] += jnp.broadcast_to(jnp.sum(hits, axis=1, keepdims=True), seen_ref.shape)


def _assign_rows(top_idx_t, first_row):
    n = top_idx_t.shape[1]
    tm = 512 if n % 512 == 0 else 128
    upper = (jnp.arange(tm)[:, None] < jnp.arange(tm)[None, :]).astype(BF16)
    return pl.pallas_call(
        _assign_kernel,
        grid=(n // tm,),
        in_specs=[pl.BlockSpec((TOP_K, tm), lambda i: (0, i)),
                  pl.BlockSpec((N_EXPERTS, 1), lambda i: (0, 0)),
                  pl.BlockSpec((tm, tm), lambda i: (0, 0))],
        out_specs=pl.BlockSpec((TOP_K, tm), lambda i: (0, i)),
        out_shape=jax.ShapeDtypeStruct((TOP_K, n), jnp.int32),
        scratch_shapes=[pltpu.VMEM((N_EXPERTS, 128), F32)],
        compiler_params=_params("arbitrary"),
        name="assign_rows",
    )(top_idx_t, first_row.astype(F32)[:, None], upper)


def _routing(top_idx_t, counts, bm):
    n = top_idx_t.shape[1]
    a = n * TOP_K
    flat_e = top_idx_t.T.reshape(a)
    _, order = lax.sort_key_val(flat_e, jnp.arange(a, dtype=jnp.int32))
    bounds = jnp.concatenate([jnp.zeros((1,), jnp.int32), jnp.cumsum(counts)])
    start = bounds[:-1]
    padded = (counts + bm - 1) // bm * bm
    pend = jnp.cumsum(padded)
    shift = pend - padded - start
    pos = _assign_rows(top_idx_t, pend - padded)
    nblocks = -(-(a + N_EXPERTS * (bm - 1)) // bm)
    block_row = jnp.arange(nblocks, dtype=jnp.int32) * bm
    block_expert = jnp.minimum(jnp.sum((pend[None, :] <= block_row[:, None]).astype(jnp.int32), axis=1),
                               N_EXPERTS - 1)
    rank = (block_row - shift[block_expert])[:, None] + jnp.arange(bm, dtype=jnp.int32)[None, :]
    valid = rank < bounds[block_expert + 1][:, None]
    tok_sorted = jnp.where(valid, order[jnp.clip(rank, 0, a - 1)] // TOP_K, 0).reshape(nblocks * bm)
    n_active = (pend[-1:] // bm).astype(jnp.int32)
    return tok_sorted, pos, block_expert, n_active


def kernel(x, c, ctx, c_ctx, mod_w, mod_b, norm_g, w_in, hgrn_lb, hgrn_norm, ret_decay, ret_norm, da_lambda,
           da_subln, w_out, router_w, router_b, w1, b1, w2, b2):
    B, S, D = x.shape
    C = ctx.shape[1]
    depth = mod_w.shape[0]
    bm = 256

    pos = jnp.arange(S)
    cr, sr = _rope_tables(pos // GRID_W, DA_HEAD_DIM // 2)
    cc, sc_ = _rope_tables(pos % GRID_W, DA_HEAD_DIM // 2)
    cs, ss = _rope_tables(pos, HDIM)
    cosr = jnp.tile(jnp.concatenate([cs, cs], -1), (1, 128 // HDIM))
    sinr = jnp.tile(jnp.concatenate([ss, ss], -1), (1, 128 // HDIM))
    cosd = jnp.tile(jnp.concatenate([cr, cr, cc, cc], -1), (1, 128 // DA_HEAD_DIM))
    sind = jnp.tile(jnp.concatenate([sr, sr, sc_, sc_], -1), (1, 128 // DA_HEAD_DIM))
    ones_r, zeros_r = jnp.ones((C, 128), F32), jnp.zeros((C, 128), F32)
    ones_d, zeros_d = ones_r, zeros_r

    lb_cum = jnp.cumsum(jax.nn.softmax(hgrn_lb.astype(F32), axis=0), axis=0)
    lower = lb_cum - lb_cum[0:1]

    cvec = jnp.zeros((8, D), F32).at[:B].set(c).at[B].set(c_ctx)
    mods = _modulation(cvec, mod_w, mod_b)

    head_id = jnp.arange(REC_W) // HDIM
    seg = (head_id[:, None] == head_id[None, :]).astype(BF16)

    xf = x.reshape(B * S, D)
    xc = ctx.reshape(B * C, D)
    zero_state = jnp.zeros((B, 2, HEADS, HDIM, HDIM), F32)

    for layer in range(depth):
        need_ctx = layer < depth - 1
        lam_init = 0.8 - 0.6 * math.exp(-0.3 * layer)
        m6 = mods[layer].reshape(8, 6, D)
        lat = lambda k: m6[:B, k][:, None, :]
        cxm = lambda k: jnp.broadcast_to(m6[B, k][None, None, :], (B, 1, D))
        ng = norm_g[layer]
        lb = lower[layer]
        lbc = jnp.zeros((8, REC_W), F32).at[0].set(jnp.log(lb)).at[1].set(jnp.log1p(-lb)).at[2].set(1.0 - lb)
        w_all = _prep_w_in(w_in[layer])
        log_gamma = jnp.log1p(-jnp.exp2(-ret_decay[layer].astype(F32)))
        lgrow = jnp.zeros((8, REC_W), F32).at[:2].set(jnp.repeat(log_gamma, HDIM, axis=1))
        lamv = da_lambda[layer].astype(F32)
        lam = (jnp.exp(jnp.sum(lamv[0] * lamv[1])) - jnp.exp(jnp.sum(lamv[2] * lamv[3])) + lam_init).reshape(1)
        hn = jnp.tile(hgrn_norm[layer], HEADS)[None, :]
        rn = jnp.tile(ret_norm[layer], HEADS)[None, :]
        dn = jnp.tile(jnp.concatenate([da_subln[layer], jnp.zeros((128 - DA_VDIM,), F32)]), DA_HEADS)[:, None]
        wo = w_out[layer]
        wo_da = jnp.pad(wo[2 * REC_W:].reshape(DA_HEADS, DA_VDIM, D), ((0, 0), (0, 128 - DA_VDIM), (0, 0)))
        w_out_b = jnp.concatenate([wo[:2 * REC_W], wo_da.reshape(DA_X, D)], axis=0).astype(BF16)
        rw = router_w[layer].T
        rb = router_b[layer][:, None]

        (hq, hv, hg, kf, lff, kb, lfb, rq, rk, rv, rg, dq, dk, dvt) = _inproj(
            xf, S, ng[0:1], lat(1), lat(0), w_all, lbc, cosr, sinr, cosd, sind)
        (hq_c, hv_c, hg_c, kf_c, lff_c, kb_c, lfb_c, rq_c, rk_c, rv_c, rg_c, dq_c, dk_c, dvt_c) = _inproj(
            xc, C, ng[0:1], cxm(1), cxm(0), w_all, lbc, ones_r, zeros_r, ones_d, zeros_d)

        hof_c, hob_c, hs = _hgrn_scan(hq_c, hv_c, kf_c, lff_c, kb_c, lfb_c,
                                      jnp.zeros((B, 2, REC_W // 128, 128, 128), F32), seg, B, C)
        hof, hob, _ = _hgrn_scan(hq, hv, kf, lff, kb, lfb, hs, seg, B, S)
        rof_c, rob_c, rs = _ret_scan(rq_c, rk_c, rv_c, zero_state, log_gamma, lgrow, B, C)
        rof, rob, _ = _ret_scan(rq, rk, rv, rs, log_gamma, lgrow, B, S)
        out_scale = 1.0 - lam_init
        da = _diff_attention(lam, dq, dk_c, dvt_c, dk, dvt, dn, B, S, C, S, out_scale, False)

        x1, tok, top_idx_t, gates_t, cnt = _outproj(xf, S, (hof, hob, hg, rof, rob, rg, da), seg, w_out_b, hn, rn,
                                                    ng[1:2], ng[2:3], lat(2), lat(4), lat(3), rw, rb)
        counts = jnp.sum(cnt[:, :, 0], axis=0)
        if need_ctx:
            da_c = _diff_attention(lam, dq_c, dk_c, dvt_c, None, None, dn, B, C, C, C, out_scale, True)
            xc1, tok_c, top_idx_c, gates_c, cnt_c = _outproj(
                xc, C, (hof_c, hob_c, hg_c, rof_c, rob_c, rg_c, da_c), seg, w_out_b, hn, rn, ng[1:2], ng[2:3],
                cxm(2), cxm(4), cxm(3), rw, rb)
            tok = jnp.concatenate([tok, tok_c], axis=0)
            top_idx_t = jnp.concatenate([top_idx_t, top_idx_c], axis=1)
            gates_t = jnp.concatenate([gates_t, gates_c], axis=1)
            counts = counts + jnp.sum(cnt_c[:, :, 0], axis=0)

        n_tok = tok.shape[0]
        gates = gates_t.T
        tok_sorted, posn, block_expert, n_active = _routing(top_idx_t, counts.astype(jnp.int32), bm)
        b1g = b1[layer][:, None, 0::2]
        b1l = b1[layer][:, None, 1::2]
        b2l = b2[layer][:, None, :]
        yg = _expert_ffn(block_expert, n_active, tok, tok_sorted, w1, w2, b1g, b1l, b2l, layer, bm)

        def expert_rows(lo, hi):
            return [yg.at[posn[k, lo:hi]].get(mode="promise_in_bounds") for k in range(TOP_K)]

        xf = _ffn_residual(x1, expert_rows(0, B * S), gates[:B * S], S, ng[3:4], lat(5))
        if need_ctx:
            xc = _ffn_residual(xc1, expert_rows(B * S, n_tok), gates[B * S:], C, ng[3:4], cxm(5))
    return xf.reshape(B, S, D)
```
